```python
import math
import jax, jax.numpy as jnp
from jax import lax
import numpy as np

D_MODEL = 4096
BATCH = 2
SEQ = 4096
DEPTH = 1
DEC_BATCH = 128
DEC_SEQ = 8
PAST_LEN = 2048
PAGE_SIZE = 128

HEAD_DIM = 128
NSA_HEADS = 16
NSA_KV_HEADS = 2
NSA_GROUP = NSA_HEADS // NSA_KV_HEADS
CMP_BLOCK = 32
SEL_BLOCK = 64
SEL_RATIO = SEL_BLOCK // CMP_BLOCK
SEL_TOPK = 16
WINDOW = 512
DIFF_HEADS = 8
DIFF_KV_HEADS = 4
DIFF_GROUP = DIFF_HEADS // DIFF_KV_HEADS
DIFF_DV = 2 * HEAD_DIM
REL_BUCKETS = 32
REL_MAX_DIST = 128
N_HEADS_TOTAL = NSA_HEADS + DIFF_HEADS
N_EXPERT_GROUPS = 4
EXPERTS_PER_GROUP = 8
N_EXPERTS = N_EXPERT_GROUPS * EXPERTS_PER_GROUP
EXPERT_TOPK = 2
EXPERT_FF = 1024
PLE_DIM = 256
Q_BLOCK = 128
EPS = 1e-6
NEG_INF = -1e30
SEL_FORCE_SCORE = 1e4
IN_SPLITS = (NSA_HEADS * HEAD_DIM, 6 * NSA_KV_HEADS * HEAD_DIM, 3 * NSA_HEADS,
             DIFF_HEADS * 2 * HEAD_DIM, DIFF_KV_HEADS * 2 * HEAD_DIM, DIFF_KV_HEADS * DIFF_DV)
IN_COLS = sum(IN_SPLITS)
MIX_WIDTH = NSA_HEADS * HEAD_DIM + DIFF_HEADS * DIFF_DV

kernel_name = 'hybrid_nsa_diffattn_hmoe_decode_step'


def rmsnorm(x, g):
    xf = x.astype(jnp.float32)
    y = xf * lax.rsqrt(jnp.mean(xf * xf, axis=-1, keepdims=True) + EPS)
    return (y * g.astype(jnp.float32)).astype(x.dtype)


def rel_bucket(dist):
    n = jnp.maximum(dist, 0)
    max_exact = REL_BUCKETS // 2
    nf = jnp.maximum(n, 1).astype(jnp.float32)
    log_b = max_exact + (jnp.log(nf / max_exact) / math.log(REL_MAX_DIST / max_exact)
                         * (REL_BUCKETS - max_exact)).astype(jnp.int32)
    return jnp.where(n < max_exact, n, jnp.minimum(log_b, REL_BUCKETS - 1))


def masked_softmax(s, mask):
    p = jax.nn.softmax(jnp.where(mask, s, NEG_INF), axis=-1)
    return jnp.where(mask, p, 0.0)


def in_proj(xn, w_in):
    b, t, _ = xn.shape
    cuts = [int(c) for c in np.cumsum(IN_SPLITS)[:-1]]
    qn, kvn, gn, qd, kd, vd = jnp.split(xn @ w_in, cuts, axis=-1)
    qn = qn.reshape(b, t, NSA_KV_HEADS, NSA_GROUP, HEAD_DIM)
    kvn = kvn.reshape(b, t, 6, NSA_KV_HEADS, HEAD_DIM)
    gn = jax.nn.sigmoid(gn).reshape(b, t, 3, NSA_KV_HEADS, NSA_GROUP)
    qd = qd.reshape(b, t, DIFF_KV_HEADS, DIFF_GROUP, 2, HEAD_DIM)
    dkv = jnp.concatenate([kd.reshape(b, t, DIFF_KV_HEADS, 2 * HEAD_DIM),
                           vd.reshape(b, t, DIFF_KV_HEADS, DIFF_DV)], axis=-1)
    return qn, gn, kvn[:, :, :4], kvn[:, :, 4:], qd, dkv


def compress(k, w):
    b, l = k.shape[:2]
    nc = l // CMP_BLOCK
    blk = k[:, :nc * CMP_BLOCK].reshape(b, nc, CMP_BLOCK, NSA_KV_HEADS, HEAD_DIM)
    return jnp.einsum('bnjhd,jde->bnhe', blk, w)


def sel_blocks(k):
    b, l = k.shape[:2]
    ns = -(-l // SEL_BLOCK)
    kp = jnp.pad(k, ((0, 0), (0, ns * SEL_BLOCK - l), (0, 0), (0, 0)))
    return kp.reshape(b, ns, SEL_BLOCK, NSA_KV_HEADS, HEAD_DIM).transpose(0, 3, 1, 2, 4)


def gather_blocks(blk, idx):
    return jax.vmap(jax.vmap(lambda bb, ii: bb[ii]))(blk, idx)


def gather_pages(pool, layer, page_table):
    g = pool[layer, page_table]
    return g.reshape((page_table.shape[0], page_table.shape[1] * pool.shape[2]) + pool.shape[3:])


def nsa_block(q, gates, qpos, kc, vc, ks_blk, vs_blk, kw, vw, kpos_w, tbl):
    b, nq = q.shape[:2]
    scale = HEAD_DIM ** -0.5
    nc = kc.shape[1]
    ns = ks_blk.shape[2]
    cpos_end = jnp.arange(nc) * CMP_BLOCK + (CMP_BLOCK - 1)
    dist_c = qpos[:, None] - cpos_end[None, :]
    s_c = jnp.einsum('bqhgd,bnhd->bhgqn', q, kc, preferred_element_type=jnp.float32) * scale
    s_c = s_c + tbl[rel_bucket(dist_c)].transpose(2, 3, 0, 1)
    p_c = masked_softmax(s_c, dist_c >= 0)
    o_c = jnp.einsum('bhgqn,bnhd->bqhgd', p_c.astype(vc.dtype), vc)
    imp = jnp.pad(p_c.sum(axis=2), ((0, 0), (0, 0), (0, 0), (0, ns * SEL_RATIO - nc)))
    imp = imp.reshape(b, NSA_KV_HEADS, nq, ns, SEL_RATIO).sum(-1)
    blk_id = jnp.arange(ns)[None, :]
    cur = (qpos // SEL_BLOCK)[:, None]
    valid = blk_id * SEL_BLOCK <= qpos[:, None]
    forced = (blk_id == 0) | (blk_id == cur) | (blk_id == cur - 1)
    score = jnp.where(valid, imp + jnp.where(forced, SEL_FORCE_SCORE, 0.0), NEG_INF)
    _, idx = lax.top_k(score, min(SEL_TOPK, ns))
    n_sel = idx.shape[-1] * SEL_BLOCK
    ks = gather_blocks(ks_blk, idx).reshape(b, NSA_KV_HEADS, nq, n_sel, HEAD_DIM)
    vs = gather_blocks(vs_blk, idx).reshape(b, NSA_KV_HEADS, nq, n_sel, HEAD_DIM)
    kpos_s = (idx[..., None] * SEL_BLOCK + jnp.arange(SEL_BLOCK)).reshape(b, NSA_KV_HEADS, nq, n_sel)
    dist_s = qpos[:, None] - kpos_s
    bias_s = jax.vmap(lambda t, d: t[rel_bucket(d)], in_axes=(1, 1), out_axes=1)(tbl, dist_s)
    s_s = jnp.einsum('bqhgd,bhqsd->bhgqs', q, ks, preferred_element_type=jnp.float32) * scale
    s_s = s_s + jnp.moveaxis(bias_s, -1, 2)
    p_s = masked_softmax(s_s, (dist_s >= 0)[:, :, None])
    o_s = jnp.einsum('bhgqs,bhqsd->bqhgd', p_s.astype(vs.dtype), vs)
    dist_w = qpos[:, None] - kpos_w[None, :]
    mask_w = (dist_w >= 0) & (dist_w <= WINDOW) & (kpos_w >= 0)[None, :]
    s_w = jnp.einsum('bqhgd,bkhd->bhgqk', q, kw, preferred_element_type=jnp.float32) * scale
    s_w = s_w + tbl[rel_bucket(dist_w)].transpose(2, 3, 0, 1)
    p_w = masked_softmax(s_w, mask_w)
    o_w = jnp.einsum('bhgqk,bkhd->bqhgd', p_w.astype(vw.dtype), vw)
    g = gates[..., None]
    o = g[:, :, 0] * o_c + g[:, :, 1] * o_s + g[:, :, 2] * o_w
    return o.reshape(b, nq, NSA_HEADS * HEAD_DIM)


def nsa_prompt(q, gates, kv, kvw, w_cmp, tbl):
    b, l = q.shape[:2]
    kc = compress(kv[:, :, 0], w_cmp[0])
    vc = compress(kv[:, :, 1], w_cmp[1])
    ks_blk = sel_blocks(kv[:, :, 2])
    vs_blk = sel_blocks(kv[:, :, 3])
    pad = ((0, 0), (WINDOW, 0), (0, 0), (0, 0))
    kw_pad = jnp.pad(kvw[:, :, 0], pad)
    vw_pad = jnp.pad(kvw[:, :, 1], pad)
    qb = Q_BLOCK if l % Q_BLOCK == 0 else l
    band = WINDOW + qb

    def one_block(i):
        s0 = i * qb
        return nsa_block(lax.dynamic_slice_in_dim(q, s0, qb, 1),
                         lax.dynamic_slice_in_dim(gates, s0, qb, 1),
                         s0 + jnp.arange(qb), kc, vc, ks_blk, vs_blk,
                         lax.dynamic_slice_in_dim(kw_pad, s0, band, 1),
                         lax.dynamic_slice_in_dim(vw_pad, s0, band, 1),
                         s0 - WINDOW + jnp.arange(band), tbl)

    out = lax.map(one_block, jnp.arange(l // qb))
    return out.transpose(1, 0, 2, 3).reshape(b, l, -1)


def nsa_sample(q, gates, kv_new, kvw_new, past, win_buf, w_cmp, tbl):
    p_len, t, wb = past.shape[1], q.shape[1], win_buf.shape[1]
    kv = jnp.concatenate([past, kv_new], axis=1)
    kc = compress(kv[:, :, 0], w_cmp[0])
    vc = compress(kv[:, :, 1], w_cmp[1])
    ks_blk = sel_blocks(kv[:, :, 2])
    vs_blk = sel_blocks(kv[:, :, 3])
    win = jnp.concatenate([win_buf, kvw_new], axis=1)
    o = nsa_block(q, gates, p_len + jnp.arange(t), kc, vc, ks_blk, vs_blk,
                  win[:, :, 0], win[:, :, 1], (p_len - wb) + jnp.arange(wb + t), tbl)
    return o, win[:, t:]


def split_diff_kv(dkv):
    b, l = dkv.shape[:2]
    k = dkv[..., :2 * HEAD_DIM].reshape(b, l, DIFF_KV_HEADS, 2, HEAD_DIM)
    return k, dkv[..., 2 * HEAD_DIM:]


def diff_block(q, k, v, qpos, kpos, tbl, lam, g, lam_init):
    b, nq = q.shape[:2]
    dist = qpos[:, None] - kpos[None, :]
    s = jnp.einsum('bqhgmd,bkhmd->bhgmqk', q, k, preferred_element_type=jnp.float32) * (HEAD_DIM ** -0.5)
    s = s + tbl[rel_bucket(dist)].transpose(2, 3, 0, 1)[:, :, None]
    p = masked_softmax(s, dist >= 0)
    a = p[:, :, :, 0] - lam * p[:, :, :, 1]
    o = jnp.einsum('bhgqk,bkhd->bqhgd', a.astype(v.dtype), v)
    o = rmsnorm(o, g) * (1.0 - lam_init)
    return o.reshape(b, nq, DIFF_HEADS * DIFF_DV)


def diff_prompt(qd, dkv, tbl, lam, g, lam_init):
    b, l = qd.shape[:2]
    k, v = split_diff_kv(dkv)
    kpos = jnp.arange(l)
    qb = Q_BLOCK if l % Q_BLOCK == 0 else l

    def one_block(i):
        s0 = i * qb
        return diff_block(lax.dynamic_slice_in_dim(qd, s0, qb, 1), k, v, s0 + jnp.arange(qb),
                          kpos, tbl, lam, g, lam_init)

    out = lax.map(one_block, jnp.arange(l // qb))
    return out.transpose(1, 0, 2, 3).reshape(b, l, -1)


def diff_sample(qd, dkv_new, past, tbl, lam, g, lam_init):
    p_len, t = past.shape[1], qd.shape[1]
    k, v = split_diff_kv(jnp.concatenate([past, dkv_new], axis=1))
    return diff_block(qd, k, v, p_len + jnp.arange(t), jnp.arange(p_len + t), tbl, lam, g, lam_init)


def hier_moe(x, w_rg, b_rg, w_re, b_re, w_g, w_u, w_d):
    shp = x.shape
    xf = x.reshape(-1, shp[-1])
    n = xf.shape[0]
    lg = jnp.einsum('nd,dg->ng', xf, w_rg, preferred_element_type=jnp.float32) + b_rg
    pg = jax.nn.softmax(lg, axis=-1)
    gsel = jnp.argmax(lg, axis=-1)
    gw = jnp.take_along_axis(pg, gsel[:, None], axis=1)
    le = (jnp.einsum('nd,de->ne', xf, w_re, preferred_element_type=jnp.float32) + b_re)
    le = le.reshape(n, N_EXPERT_GROUPS, EXPERTS_PER_GROUP)
    le_sel = jnp.take_along_axis(le, gsel[:, None, None], axis=1)[:, 0]
    tv, ti = lax.top_k(jax.nn.softmax(le_sel, axis=-1), EXPERT_TOPK)
    tv = tv / jnp.sum(tv, axis=-1, keepdims=True) * gw
    eid = gsel[:, None] * EXPERTS_PER_GROUP + ti
    combine = jnp.sum(jax.nn.one_hot(eid, N_EXPERTS, dtype=jnp.float32) * tv[..., None], axis=1).astype(x.dtype)
    out = jnp.zeros_like(xf)
    for gi in range(N_EXPERT_GROUPS):
        sl = slice(gi * EXPERTS_PER_GROUP, (gi + 1) * EXPERTS_PER_GROUP)
        a = jnp.einsum('nd,edf->nef', xf, w_g[sl])
        u = jnp.einsum('nd,edf->nef', xf, w_u[sl])
        hdn = jax.nn.silu(a) * u * combine[:, sl, None]
        out = out + jnp.einsum('nef,efd->nd', hdn, w_d[sl])
    return out.reshape(shp)


def setup_inputs(seed: int = 0) -> dict:
    key = jax.random.key(seed)
    keys = iter(jax.random.split(key, 40))
    n_pages = PAST_LEN // PAGE_SIZE
    n_phys = (5 * DEC_BATCH * n_pages + 3) // 4
    win_buf = min(WINDOW, PAST_LEN)

    def nrm(shape, scale=1.0):
        return jax.random.normal(next(keys), shape, jnp.float32) * scale

    x_prompt = nrm((BATCH, SEQ, D_MODEL))
    x_sample = nrm((DEC_BATCH, DEC_SEQ, D_MODEL))
    cache_nsa_kv = nrm((DEPTH, n_phys, PAGE_SIZE, 4, NSA_KV_HEADS, HEAD_DIM))
    cache_diff_kv = nrm((DEPTH, n_phys, PAGE_SIZE, DIFF_KV_HEADS, 4 * HEAD_DIM))
    state_nsa_win = nrm((DEPTH, DEC_BATCH, win_buf, 2, NSA_KV_HEADS, HEAD_DIM))
    perm = jax.random.permutation(next(keys), n_phys)
    page_table = perm[:DEC_BATCH * n_pages].reshape(DEC_BATCH, n_pages).astype(jnp.int32)
    p_prompt = nrm((DEPTH, BATCH, SEQ, PLE_DIM))
    p_sample = nrm((DEPTH, DEC_BATCH, DEC_SEQ, PLE_DIM))
    return {
        'x_prompt': x_prompt,
        'x_sample': x_sample,
        'cache_nsa_kv': cache_nsa_kv,
        'cache_diff_kv': cache_diff_kv,
        'state_nsa_win': state_nsa_win,
        'page_table': page_table,
        'p_prompt': p_prompt,
        'p_sample': p_sample,
        'rel_bias_table': nrm((REL_BUCKETS, N_HEADS_TOTAL), 0.5),
        'norm_mix': 1.0 + nrm((DEPTH, D_MODEL), 0.02),
        'w_in': nrm((DEPTH, D_MODEL, IN_COLS), D_MODEL ** -0.5),
        'w_cmp': nrm((DEPTH, 2, CMP_BLOCK, HEAD_DIM, HEAD_DIM), (CMP_BLOCK * HEAD_DIM) ** -0.5),
        'diff_lambda': nrm((DEPTH, 4, HEAD_DIM), 0.1),
        'diff_subln': 1.0 + nrm((DEPTH, DIFF_DV), 0.02),
        'w_out': nrm((DEPTH, MIX_WIDTH, D_MODEL), MIX_WIDTH ** -0.5),
        'norm_ffn': 1.0 + nrm((DEPTH, D_MODEL), 0.02),
        'w_router_group': nrm((DEPTH, D_MODEL, N_EXPERT_GROUPS), D_MODEL ** -0.5),
        'b_router_group': nrm((DEPTH, N_EXPERT_GROUPS), 0.01),
        'w_router_expert': nrm((DEPTH, D_MODEL, N_EXPERTS), D_MODEL ** -0.5),
        'b_router_expert': nrm((DEPTH, N_EXPERTS), 0.01),
        'w_exp_gate': nrm((DEPTH, N_EXPERTS, D_MODEL, EXPERT_FF), D_MODEL ** -0.5),
        'w_exp_up': nrm((DEPTH, N_EXPERTS, D_MODEL, EXPERT_FF), D_MODEL ** -0.5),
        'w_exp_down': nrm((DEPTH, N_EXPERTS, EXPERT_FF, D_MODEL), EXPERT_FF ** -0.5),
        'norm_ple': 1.0 + nrm((DEPTH, D_MODEL), 0.02),
        'w_ple_gate': nrm((DEPTH, D_MODEL, D_MODEL), D_MODEL ** -0.5),
        'w_ple_proj': nrm((DEPTH, PLE_DIM, D_MODEL), PLE_DIM ** -0.5),
        'final_norm': 1.0 + nrm((D_MODEL,), 0.02),
    }


def reference(x_prompt, x_sample, cache_nsa_kv, cache_diff_kv, state_nsa_win, page_table,
              p_prompt, p_sample, rel_bias_table, norm_mix, w_in, w_cmp, diff_lambda, diff_subln,
              w_out, norm_ffn, w_router_group, b_router_group, w_router_expert, b_router_expert,
              w_exp_gate, w_exp_up, w_exp_down, norm_ple, w_ple_gate, w_ple_proj, final_norm):
    tbl_nsa = rel_bias_table[:, :NSA_HEADS].reshape(REL_BUCKETS, NSA_KV_HEADS, NSA_GROUP)
    tbl_diff = rel_bias_table[:, NSA_HEADS:].reshape(REL_BUCKETS, DIFF_KV_HEADS, DIFF_GROUP)
    win_keep_prompt = min(WINDOW, x_prompt.shape[1])

    def channel_and_ple(h, p_i, i):
        h = h + hier_moe(rmsnorm(h, norm_ffn[i]), w_router_group[i], b_router_group[i],
                         w_router_expert[i], b_router_expert[i],
                         w_exp_gate[i], w_exp_up[i], w_exp_down[i])
        gate = jax.nn.sigmoid(rmsnorm(h, norm_ple[i]) @ w_ple_gate[i])
        return h + gate * (p_i @ w_ple_proj[i])

    hp, hs = x_prompt, x_sample
    nsa_p, nsa_s, diff_p, diff_s, win_p, win_s = [], [], [], [], [], []
    for i in range(DEPTH):
        lam_init = 0.8 - 0.6 * math.exp(-0.3 * i)
        dl = diff_lambda[i].astype(jnp.float32)
        lam = jnp.exp(jnp.sum(dl[0] * dl[1])) - jnp.exp(jnp.sum(dl[2] * dl[3])) + lam_init

        qn, gn, kvp, kvw, qd, dkv = in_proj(rmsnorm(hp, norm_mix[i]), w_in[i])
        o_n = nsa_prompt(qn, gn, kvp, kvw, w_cmp[i], tbl_nsa)
        o_d = diff_prompt(qd, dkv, tbl_diff, lam, diff_subln[i], lam_init)
        hp = hp + jnp.concatenate([o_n, o_d], axis=-1) @ w_out[i]
        hp = channel_and_ple(hp, p_prompt[i], i)
        nsa_p.append(kvp)
        diff_p.append(dkv)
        win_p.append(kvw[:, kvw.shape[1] - win_keep_prompt:])

        past_nsa = gather_pages(cache_nsa_kv, i, page_table)
        past_diff = gather_pages(cache_diff_kv, i, page_table)
        qn, gn, kvp, kvw, qd, dkv = in_proj(rmsnorm(hs, norm_mix[i]), w_in[i])
        o_n, new_win = nsa_sample(qn, gn, kvp, kvw, past_nsa, state_nsa_win[i], w_cmp[i], tbl_nsa)
        o_d = diff_sample(qd, dkv, past_diff, tbl_diff, lam, diff_subln[i], lam_init)
        hs = hs + jnp.concatenate([o_n, o_d], axis=-1) @ w_out[i]
        hs = channel_and_ple(hs, p_sample[i], i)
        nsa_s.append(kvp)
        diff_s.append(dkv)
        win_s.append(new_win)

    y_prompt = rmsnorm(hp, final_norm)
    y_sample = rmsnorm(hs, final_norm)
    return (y_prompt, y_sample, jnp.stack(nsa_p), jnp.stack(nsa_s), jnp.stack(diff_p),
            jnp.stack(diff_s), jnp.stack(win_p), jnp.stack(win_s))
```

```python
import functools
import math

import numpy as np
import jax
import jax.numpy as jnp
from jax import lax
from jax.experimental import pallas as pl
from jax.experimental.pallas import tpu as pltpu

BF = jnp.bfloat16
F32 = jnp.float32

HD = 128
NSA_H = 16
NSA_KV = 2
NSA_G = NSA_H // NSA_KV
CMP = 32
SELB = 64
TOPK = 16
WIN = 512
DF_H = 8
DF_KV = 4
DF_G = DF_H // DF_KV
DF_DV = 2 * HD
REL_BUCKETS = 32
REL_MAX_DIST = 128
N_GROUPS = 4
EPG = 8
N_EXP = N_GROUPS * EPG
EPS = 1e-6
NEG = -1e30
FORCE = 1e4
LANE = 128
VMEM_LIMIT = 56 * 1024 * 1024
MOE_TM = 256

QB_NSA = 128
PAD_SEL = 384
PAD_WIN = WIN
NEAR_SEL = 512
BAND_WIN = WIN + QB_NSA
QB_DF = 256
PAD_DF = 256
NEAR_DF = 512


def _dot(a, b):
    return jnp.dot(a, b, preferred_element_type=F32)


def _dot_nt(a, b):
    return lax.dot_general(a, b, (((1,), (1,)), ((), ())), preferred_element_type=F32)


def _cparams(sem):
    return pltpu.CompilerParams(dimension_semantics=sem, vmem_limit_bytes=VMEM_LIMIT)


def _pick(n, cands):
    for c in cands:
        if n % c == 0:
            return c
    raise ValueError(f"no tile in {cands} divides {n}")


def _rms2_kernel(xp_ref, xs_ref, g_ref, o_ref, *, np_tiles):
    i = pl.program_id(0)

    def go(x_ref):
        x = x_ref[...]
        ms = jnp.mean(x * x, axis=-1, keepdims=True)
        o_ref[...] = (x * lax.rsqrt(ms + EPS) * g_ref[...]).astype(o_ref.dtype)

    @pl.when(i < np_tiles)
    def _():
        go(xp_ref)

    @pl.when(i >= np_tiles)
    def _():
        go(xs_ref)


def _rms2(xp, xs, g, tm):
    mp, d = xp.shape
    ms = xs.shape[0]
    npt, nst = mp // tm, ms // tm
    return pl.pallas_call(
        functools.partial(_rms2_kernel, np_tiles=npt),
        grid=(npt + nst,),
        in_specs=[pl.BlockSpec((tm, d), lambda i: (jnp.minimum(i, npt - 1), 0)),
                  pl.BlockSpec((tm, d), lambda i: (jnp.maximum(i - npt, 0), 0)),
                  pl.BlockSpec((1, d), lambda i: (0, 0))],
        out_specs=pl.BlockSpec((tm, d), lambda i: (i, 0)),
        out_shape=jax.ShapeDtypeStruct((mp + ms, d), BF),
        compiler_params=_cparams(("arbitrary",)),
        name="rms2",
    )(xp, xs, g.reshape(1, d))


def _rms_kernel(x_ref, g_ref, o_ref):
    x = x_ref[...]
    ms = jnp.mean(x * x, axis=-1, keepdims=True)
    o_ref[...] = (x * lax.rsqrt(ms + EPS) * g_ref[...]).astype(o_ref.dtype)


def _rms(x, g, tm, out_dtype, row0=0, rows=None):
    m, d = x.shape
    rows = m if rows is None else rows
    t0 = row0 // tm
    return pl.pallas_call(
        _rms_kernel,
        grid=(rows // tm,),
        in_specs=[pl.BlockSpec((tm, d), lambda i: (i + t0, 0)),
                  pl.BlockSpec((1, d), lambda i: (0, 0))],
        out_specs=pl.BlockSpec((tm, d), lambda i: (i, 0)),
        out_shape=jax.ShapeDtypeStruct((rows, d), out_dtype),
        compiler_params=_cparams(("arbitrary",)),
        name="rms",
    )(x, g.reshape(1, d))


def _cast_rows(src_ref, dst_ref):
    k = src_ref.shape[0]
    ch = 256 if k % 256 == 0 else k

    def body(c, carry):
        r = pl.multiple_of(c * ch, ch)
        dst_ref[pl.ds(r, ch), :] = src_ref[pl.ds(r, ch), :].astype(BF)
        return carry

    lax.fori_loop(0, k // ch, body, 0)


def _mm_kernel(*refs, n_a, cast, epi, scale, np_tiles):
    a = refs[:n_a]
    w = refs[n_a:2 * n_a]
    idx = 2 * n_a
    if epi == "res2":
        rp_ref, rs_ref = refs[idx:idx + 2]
        idx += 2
    elif epi == "ple":
        h_ref, p_ref, wp_ref = refs[idx:idx + 3]
        idx += 3
    o_ref = refs[idx]
    idx += 1
    wb = refs[idx:idx + n_a] if cast else w
    i = pl.program_id(1)

    if cast:
        @pl.when(i == 0)
        def _():
            for k in range(n_a):
                _cast_rows(w[k], wb[k])

    acc = _dot(a[0][...], wb[0][...])
    for k in range(1, n_a):
        acc = acc + _dot(a[k][...], wb[k][...])
    if scale is not None:
        acc = acc * scale
    if epi is None:
        o_ref[...] = acc.astype(o_ref.dtype)
    elif epi == "res2":
        @pl.when(i < np_tiles)
        def _():
            o_ref[...] = (acc + rp_ref[...]).astype(o_ref.dtype)

        @pl.when(i >= np_tiles)
        def _():
            o_ref[...] = (acc + rs_ref[...]).astype(o_ref.dtype)
    elif epi == "ple":
        gate = jax.nn.sigmoid(acc)
        proj = _dot(p_ref[...], wp_ref[...].astype(BF))
        o_ref[...] = (h_ref[...] + gate * proj).astype(o_ref.dtype)


def _matmul(a_list, w_list, *, rows, n_cols, tm, tn, out_dtype, row0=0, col0=0,
            scale=None, epi=None, epi_args=(), name="mm"):
    n_a = len(a_list)
    cast = w_list[0][0].dtype != BF
    t0, c0 = row0 // tm, col0 // tn
    gm, gn = rows // tm, n_cols // tn
    in_specs, args = [], []
    for a in a_list:
        in_specs.append(pl.BlockSpec((tm, a.shape[1]), lambda j, i: (i + t0, 0)))
        args.append(a)
    for (w, kb), a in zip(w_list, a_list):
        in_specs.append(pl.BlockSpec((a.shape[1], tn), lambda j, i, kb=kb: (kb, j + c0)))
        args.append(w)
    np_tiles = 0
    if epi == "res2":
        xp, xs = epi_args
        np_tiles = xp.shape[0] // tm
        in_specs.append(pl.BlockSpec((tm, tn), lambda j, i: (jnp.minimum(i, np_tiles - 1), j)))
        in_specs.append(pl.BlockSpec((tm, tn), lambda j, i: (jnp.maximum(i - np_tiles, 0), j)))
        args += [xp, xs]
    elif epi == "ple":
        h, p, wp = epi_args
        in_specs.append(pl.BlockSpec((tm, tn), lambda j, i: (i, j)))
        in_specs.append(pl.BlockSpec((tm, p.shape[1]), lambda j, i: (i, 0)))
        in_specs.append(pl.BlockSpec((wp.shape[0], tn), lambda j, i: (0, j)))
        args += [h, p, wp]
    scratch = [pltpu.VMEM((a.shape[1], tn), BF) for a in a_list] if cast else []
    return pl.pallas_call(
        functools.partial(_mm_kernel, n_a=n_a, cast=cast, epi=epi, scale=scale, np_tiles=np_tiles),
        grid=(gn, gm),
        in_specs=in_specs,
        out_specs=pl.BlockSpec((tm, tn), lambda j, i: (i, j)),
        out_shape=jax.ShapeDtypeStruct((rows, n_cols), out_dtype),
        scratch_shapes=scratch,
        compiler_params=_cparams(("arbitrary", "arbitrary")),
        name=name,
    )(*args)


def _bucket_np(dist):
    n = np.maximum(dist, 0)
    max_exact = REL_BUCKETS // 2
    nf = np.maximum(n, 1).astype(np.float32)
    log_b = max_exact + (np.log(nf / np.float32(max_exact)) / np.float32(math.log(REL_MAX_DIST / max_exact))
                         * np.float32(REL_BUCKETS - max_exact)).astype(np.int32)
    return np.where(n < max_exact, n, np.minimum(log_b, REL_BUCKETS - 1)).astype(np.int32)


def _bias(tbl_t, dist, valid, sub_far):
    b = jnp.take(tbl_t, jnp.asarray(_bucket_np(dist)), axis=1)
    if sub_far:
        b = b - tbl_t[:, REL_BUCKETS - 1].reshape((-1,) + (1,) * dist.ndim)
    return jnp.where(jnp.asarray(valid)[None], b, NEG).astype(F32)


def _pair_sum_matrix(nc):
    n = np.arange(nc)[:, None]
    b = np.arange(LANE)[None, :]
    return jnp.asarray((n // (SELB // CMP) == b).astype(np.float32), dtype=BF)


def _expand_matrix(pad, n_keys):
    l = np.arange(LANE)[:, None]
    c = np.arange(pad + n_keys)[None, :]
    return jnp.asarray(((c >= pad) & ((c - pad) // SELB == l)).astype(np.float32), dtype=BF)


def _split3(x):
    hi = x.astype(BF)
    r = x - hi.astype(F32)
    mid = r.astype(BF)
    lo = (r - mid.astype(F32)).astype(BF)
    return hi, mid, lo


def _select_blocks(psum, s_mat, qpos, ns):
    hi, mid, lo = _split3(psum)
    imp = _dot(hi, s_mat) + _dot(mid, s_mat) + _dot(lo, s_mat)
    shape = imp.shape
    lane = lax.broadcasted_iota(jnp.int32, shape, 1)
    valid = lane * SELB <= qpos
    cur = jnp.right_shift(qpos, 6)
    forced = (lane == 0) | (lane == cur) | (lane == cur - 1)
    score = jnp.where(valid, imp + jnp.where(forced, FORCE, 0.0), NEG)
    score = jnp.where(lane < ns, score, -3e38)
    cnt = jnp.zeros(shape, F32)
    for i in range(ns):
        ci = score[:, i:i + 1]
        cnt = cnt + jnp.where(lane > i, jnp.where(ci >= score, 1.0, 0.0), jnp.where(ci > score, 1.0, 0.0))
    sel = (cnt < float(min(TOPK, ns))) & (lane < ns)
    return jnp.where(sel, 1.0, 0.0).astype(BF)


def _softmax_rows(s, valid=None):
    m = jnp.max(s, axis=-1, keepdims=True)
    p = jnp.exp(s - m)
    if valid is not None:
        p = jnp.where(valid, p, 0.0)
    l = jnp.sum(p, axis=-1, keepdims=True)
    return p / jnp.where(l > 0.0, l, 1.0)


def _online(carry, s, vt):
    m, l, acc = carry
    m_new = jnp.maximum(m, jnp.max(s, axis=-1, keepdims=True))
    p = jnp.exp(s - m_new)
    alpha = jnp.exp(m - m_new)
    l = alpha * l + jnp.sum(p, axis=-1, keepdims=True)
    acc = alpha * acc + _dot(p.astype(BF), vt)
    return m_new, l, acc


def _nsa_prompt_kernel(q_ref, gate_ref, kc_ref, vc_ref, ks_ref, vs_ref, kw_ref, vw_ref,
                       bc_ref, pn_ref, pw_ref, e_ref, s_ref, o_ref,
                       ks_s, vs_s, kw_s, vw_s, *, seq, nc, ns):
    i = pl.program_id(2)
    s0 = i * QB_NSA
    rows = NSA_G * QB_NSA

    @pl.when(i == 0)
    def _():
        ks_s[0:PAD_SEL, :] = jnp.zeros((PAD_SEL, HD), BF)
        vs_s[0:PAD_SEL, :] = jnp.zeros((PAD_SEL, HD), BF)
        kw_s[0:PAD_WIN, :] = jnp.zeros((PAD_WIN, HD), BF)
        vw_s[0:PAD_WIN, :] = jnp.zeros((PAD_WIN, HD), BF)
        ch = 512

        def cp(c, carry):
            r = pl.multiple_of(c * ch, ch)
            ks_s[pl.ds(PAD_SEL + r, ch), :] = ks_ref[pl.ds(r, ch), :].astype(BF)
            vs_s[pl.ds(PAD_SEL + r, ch), :] = vs_ref[pl.ds(r, ch), :].astype(BF)
            kw_s[pl.ds(PAD_WIN + r, ch), :] = kw_ref[pl.ds(r, ch), :].astype(BF)
            vw_s[pl.ds(PAD_WIN + r, ch), :] = vw_ref[pl.ds(r, ch), :].astype(BF)
            return carry

        lax.fori_loop(0, seq // ch, cp, 0)

    q = q_ref[...]
    qs = jnp.concatenate([q[:, g * HD:(g + 1) * HD] for g in range(NSA_G)], axis=0)

    bc = bc_ref[...].reshape(rows, nc)
    pc = _softmax_rows(_dot_nt(qs, kc_ref[...]) + bc, bc > 0.5 * NEG)
    oc = _dot(pc.astype(BF), vc_ref[...])
    psum = pc[0:QB_NSA]
    for g in range(1, NSA_G):
        psum = psum + pc[g * QB_NSA:(g + 1) * QB_NSA]
    qpos = s0 + lax.broadcasted_iota(jnp.int32, (QB_NSA, LANE), 0)
    selb = _select_blocks(psum, s_ref[...], qpos, ns)

    nch = jnp.maximum(i - 1, 0) // 2
    far_keys = nch * 256

    def far(c, carry):
        r = pl.multiple_of(PAD_SEL + c * 256, LANE)
        s = _dot_nt(qs, ks_s[pl.ds(r, 256), :])
        madd = (_dot(selb, e_ref[:, pl.ds(r, 256)]) - 1.0) * (-NEG)
        s = (s.reshape(NSA_G, QB_NSA, 256) + madd[None]).reshape(rows, 256)
        return _online(carry, s, vs_s[pl.ds(r, 256), :])

    init = (jnp.full((rows, 1), NEG, F32), jnp.zeros((rows, 1), F32), jnp.zeros((rows, HD), F32))
    carry = lax.fori_loop(0, nch, far, init)
    rn = pl.multiple_of(s0, LANE)
    s = _dot_nt(qs, ks_s[pl.ds(rn, NEAR_SEL), :])
    madd = (_dot(selb, e_ref[:, pl.ds(rn, NEAR_SEL)]) - 1.0) * (-NEG)
    col = lax.broadcasted_iota(jnp.int32, (QB_NSA, NEAR_SEL), 1)
    madd = jnp.where(col < far_keys - s0 + PAD_SEL, NEG, madd)
    s = (s.reshape(NSA_G, QB_NSA, NEAR_SEL) + pn_ref[...] + madd[None]).reshape(rows, NEAR_SEL)
    _, l_s, acc_s = _online(carry, s, vs_s[pl.ds(rn, NEAR_SEL), :])
    osel = acc_s / l_s

    s = _dot_nt(qs, kw_s[pl.ds(rn, BAND_WIN), :])
    colw = lax.broadcasted_iota(jnp.int32, (QB_NSA, BAND_WIN), 1)
    maddw = jnp.where(colw < PAD_WIN - s0, NEG, 0.0)
    s = (s.reshape(NSA_G, QB_NSA, BAND_WIN) + pw_ref[...] + maddw[None]).reshape(rows, BAND_WIN)
    ow = _dot(_softmax_rows(s).astype(BF), vw_s[pl.ds(rn, BAND_WIN), :])

    gt = jax.nn.sigmoid(gate_ref[...])
    for g in range(NSA_G):
        sl = slice(g * QB_NSA, (g + 1) * QB_NSA)
        o = (gt[:, g:g + 1] * oc[sl] + gt[:, NSA_G + g:NSA_G + g + 1] * osel[sl]
             + gt[:, 2 * NSA_G + g:2 * NSA_G + g + 1] * ow[sl])
        o_ref[:, g * HD:(g + 1) * HD] = o.astype(o_ref.dtype)


def _nsa_prompt(q_nsa, gates, kc, vc, kv4_p, win_p, tbl_t, batch, seq, m_total):
    nc = seq // CMP
    ns = -(-seq // SELB)
    nqb = seq // QB_NSA
    qi = np.arange(QB_NSA)[:, None]
    c = np.arange(NEAR_SEL)[None, :]
    d = qi + PAD_SEL - c
    pn = _bias(tbl_t[:NSA_H], d, d >= 0, True)
    c = np.arange(BAND_WIN)[None, :]
    d = qi + PAD_WIN - c
    pw = _bias(tbl_t[:NSA_H], d, (d >= 0) & (d <= WIN), False)
    qpos = np.arange(seq)[:, None]
    d = qpos - (np.arange(nc)[None, :] * CMP + CMP - 1)
    bc = _bias(tbl_t[:NSA_H], d, d >= 0, False)
    e_mat = _expand_matrix(PAD_SEL, seq)
    s_mat = _pair_sum_matrix(nc)
    kv_spec = lambda col: pl.BlockSpec((seq, HD), lambda b, h, i, col=col: (b, col + h))
    return pl.pallas_call(
        functools.partial(_nsa_prompt_kernel, seq=seq, nc=nc, ns=ns),
        grid=(batch, NSA_KV, nqb),
        in_specs=[
            pl.BlockSpec((QB_NSA, NSA_G * HD), lambda b, h, i: (b * nqb + i, h)),
            pl.BlockSpec((QB_NSA, LANE), lambda b, h, i: (b * nqb + i, h)),
            pl.BlockSpec((None, None, nc, HD), lambda b, h, i: (h, b, 0, 0)),
            pl.BlockSpec((None, None, nc, HD), lambda b, h, i: (h, b, 0, 0)),
            kv_spec(2 * NSA_KV), kv_spec(3 * NSA_KV),
            pl.BlockSpec((seq, HD), lambda b, h, i: (b, h)),
            pl.BlockSpec((seq, HD), lambda b, h, i: (b, NSA_KV + h)),
            pl.BlockSpec((NSA_G, QB_NSA, nc), lambda b, h, i: (h, i, 0)),
            pl.BlockSpec((NSA_G, QB_NSA, NEAR_SEL), lambda b, h, i: (h, 0, 0)),
            pl.BlockSpec((NSA_G, QB_NSA, BAND_WIN), lambda b, h, i: (h, 0, 0)),
            pl.BlockSpec((LANE, PAD_SEL + seq), lambda b, h, i: (0, 0)),
            pl.BlockSpec((nc, LANE), lambda b, h, i: (0, 0)),
        ],
        out_specs=pl.BlockSpec((QB_NSA, NSA_G * HD), lambda b, h, i: (b * nqb + i, h)),
        out_shape=jax.ShapeDtypeStruct((m_total, NSA_H * HD), BF),
        scratch_shapes=[pltpu.VMEM((PAD_SEL + seq, HD), BF), pltpu.VMEM((PAD_SEL + seq, HD), BF),
                        pltpu.VMEM((PAD_WIN + seq, HD), BF), pltpu.VMEM((PAD_WIN + seq, HD), BF)],
        compiler_params=_cparams(("arbitrary", "arbitrary", "arbitrary")),
        name="nsa_prompt",
    )(q_nsa, gates, kc, vc, kv4_p, kv4_p, win_p, win_p, bc, pn, pw, e_mat, s_mat)


def _diff_lambda(dl, lam_init):
    a = jnp.sum(dl[0:1] * dl[1:2], axis=-1, keepdims=True)
    b = jnp.sum(dl[2:3] * dl[3:4], axis=-1, keepdims=True)
    return jnp.exp(a) - jnp.exp(b) + lam_init


def _diff_finish(a, sub, lam_init):
    ms = jnp.mean(a * a, axis=-1, keepdims=True)
    return a * lax.rsqrt(ms + EPS) * sub * (1.0 - lam_init)


def _diff_prompt_kernel(q_ref, kv_ref, pn_ref, dl_ref, sub_ref, o_ref, kv_s, *, seq, lam_init):
    i = pl.program_id(2)
    s0 = i * QB_DF
    rows = DF_G * QB_DF

    @pl.when(i == 0)
    def _():
        kv_s[0:PAD_DF, :] = jnp.zeros((PAD_DF, 4 * HD), BF)
        ch = 256

        def cp(c, carry):
            r = pl.multiple_of(c * ch, ch)
            kv_s[pl.ds(PAD_DF + r, ch), :] = kv_ref[pl.ds(r, ch), :].astype(BF)
            return carry

        lax.fori_loop(0, seq // ch, cp, 0)

    lam = _diff_lambda(dl_ref[...], lam_init)
    q = q_ref[...]
    q1 = jnp.concatenate([q[:, g * 2 * HD:g * 2 * HD + HD] for g in range(DF_G)], axis=0)
    q2 = jnp.concatenate([q[:, g * 2 * HD + HD:(g + 1) * 2 * HD] for g in range(DF_G)], axis=0)
    nfar = jnp.maximum(i - 1, 0)

    def far(c, carry):
        c1, c2 = carry
        r = pl.multiple_of(PAD_DF + c * 256, 256)
        vt = kv_s[pl.ds(r, 256), 2 * HD:4 * HD]
        c1 = _online(c1, _dot_nt(q1, kv_s[pl.ds(r, 256), 0:HD]), vt)
        c2 = _online(c2, _dot_nt(q2, kv_s[pl.ds(r, 256), HD:2 * HD]), vt)
        return c1, c2

    init = (jnp.full((rows, 1), NEG, F32), jnp.zeros((rows, 1), F32), jnp.zeros((rows, DF_DV), F32))
    c1, c2 = lax.fori_loop(0, nfar, far, (init, init))
    rn = pl.multiple_of(s0, 256)
    col = lax.broadcasted_iota(jnp.int32, (QB_DF, NEAR_DF), 1)
    madd = jnp.where(col < nfar * 256 - s0 + PAD_DF, NEG, 0.0)
    bias = (pn_ref[...] + madd[None]).reshape(rows, NEAR_DF)
    vt = kv_s[pl.ds(rn, NEAR_DF), 2 * HD:4 * HD]
    _, l1, a1 = _online(c1, _dot_nt(q1, kv_s[pl.ds(rn, NEAR_DF), 0:HD]) + bias, vt)
    _, l2, a2 = _online(c2, _dot_nt(q2, kv_s[pl.ds(rn, NEAR_DF), HD:2 * HD]) + bias, vt)
    out = _diff_finish(a1 / l1 - lam * (a2 / l2), sub_ref[...], lam_init)
    for g in range(DF_G):
        o_ref[:, g * DF_DV:(g + 1) * DF_DV] = out[g * QB_DF:(g + 1) * QB_DF].astype(o_ref.dtype)


def _diff_prompt(qd, dkv_p, tbl_t, dl, sub, batch, seq, m_total, lam_init):
    nqb = seq // QB_DF
    qi = np.arange(QB_DF)[:, None]
    c = np.arange(NEAR_DF)[None, :]
    d = qi + PAD_DF - c
    pn = _bias(tbl_t[NSA_H:], d, d >= 0, True)
    width = DF_G * 2 * HD
    return pl.pallas_call(
        functools.partial(_diff_prompt_kernel, seq=seq, lam_init=lam_init),
        grid=(batch, DF_KV, nqb),
        in_specs=[
            pl.BlockSpec((QB_DF, width), lambda b, h, i: (b * nqb + i, h)),
            pl.BlockSpec((seq, 4 * HD), lambda b, h, i: (b, h)),
            pl.BlockSpec((DF_G, QB_DF, NEAR_DF), lambda b, h, i: (h, 0, 0)),
            pl.BlockSpec((4, HD), lambda b, h, i: (0, 0)),
            pl.BlockSpec((1, DF_DV), lambda b, h, i: (0, 0)),
        ],
        out_specs=pl.BlockSpec((QB_DF, DF_G * DF_DV), lambda b, h, i: (b * nqb + i, h)),
        out_shape=jax.ShapeDtypeStruct((m_total, DF_H * DF_DV), BF),
        scratch_shapes=[pltpu.VMEM((PAD_DF + seq, 4 * HD), BF)],
        compiler_params=_cparams(("arbitrary", "arbitrary", "arbitrary")),
        name="diff_prompt",
    )(qd, dkv_p, pn, dl, sub.reshape(1, DF_DV))


def _tail_tile(new, width):
    t = new.shape[0]
    return jnp.concatenate([new, jnp.zeros((LANE - t, width), F32)], axis=0).astype(BF)


def _nsa_sample_kernel(pt_ref, *refs, n_pages, page, past, t_new, ncs, ns, wb):
    pages = refs[:n_pages]
    (q_ref, gate_ref, kvn_ref, wn_ref, st_ref, wc_ref, bc_ref, bs_ref, bw_ref, e_ref, s_ref,
     o_ref, kcmp_s, ksel_s, kwin_s) = refs[n_pages:]
    del pt_ref
    rows = NSA_G * t_new
    n_kinds = 4

    for p in range(n_pages):
        for kind in range(n_kinds):
            for h in range(NSA_KV):
                blk = pages[p][pl.ds(kind * NSA_KV + h, page, stride=n_kinds * NSA_KV), :]
                if kind < 2:
                    kcmp_s[kind, h, p * page:(p + 1) * page, :] = blk
                else:
                    ksel_s[kind - 2, h, p * page:(p + 1) * page, :] = blk.astype(BF)
    kvn = kvn_ref[...]
    wn = wn_ref[...]
    for kind in range(2):
        for h in range(NSA_KV):
            c0 = ((kind + 2) * NSA_KV + h) * HD
            ksel_s[kind, h, past:past + LANE, :] = _tail_tile(kvn[:, c0:c0 + HD], HD)
            kwin_s[kind, h, 0:wb, :] = st_ref[pl.ds(kind * NSA_KV + h, wb, stride=2 * NSA_KV), :].astype(BF)
            c0 = (kind * NSA_KV + h) * HD
            kwin_s[kind, h, wb:wb + LANE, :] = _tail_tile(wn[:, c0:c0 + HD], HD)

    q = q_ref[...]
    gt = jax.nn.sigmoid(gate_ref[...])
    qpos = past + lax.broadcasted_iota(jnp.int32, (t_new, LANE), 0)
    for h in range(NSA_KV):
        cmp = []
        for kind in range(2):
            acc = jnp.zeros((ncs, HD), F32)
            for j in range(CMP):
                kj = kcmp_s[kind, h, pl.ds(j, ncs, stride=CMP), :]
                acc = acc + _dot(kj.astype(BF), wc_ref[kind, j * HD:(j + 1) * HD, :])
            cmp.append(acc.astype(BF))
        kc, vc = cmp
        qs = jnp.concatenate([q[:, (h * NSA_G + g) * HD:(h * NSA_G + g + 1) * HD] for g in range(NSA_G)],
                             axis=0).astype(BF)
        bc = bc_ref[h * NSA_G:(h + 1) * NSA_G].reshape(rows, ncs)
        pc = _softmax_rows(_dot_nt(qs, kc) + bc, bc > 0.5 * NEG)
        oc = _dot(pc.astype(BF), vc)
        psum = pc[0:t_new]
        for g in range(1, NSA_G):
            psum = psum + pc[g * t_new:(g + 1) * t_new]
        selb = _select_blocks(psum, s_ref[...], qpos, ns)
        lk = past + LANE
        madd = (_dot(selb, e_ref[...]) - 1.0) * (-NEG)
        s = _dot_nt(qs, ksel_s[0, h])
        s = (s.reshape(NSA_G, t_new, lk) + bs_ref[h * NSA_G:(h + 1) * NSA_G] + madd[None]).reshape(rows, lk)
        osel = _dot(_softmax_rows(s).astype(BF), ksel_s[1, h])
        lw = wb + LANE
        s = _dot_nt(qs, kwin_s[0, h]) + bw_ref[h * NSA_G:(h + 1) * NSA_G].reshape(rows, lw)
        ow = _dot(_softmax_rows(s).astype(BF), kwin_s[1, h])
        for g in range(NSA_G):
            sl = slice(g * t_new, (g + 1) * t_new)
            gl = h * LANE + g
            o = (gt[:, gl:gl + 1] * oc[sl] + gt[:, gl + NSA_G:gl + NSA_G + 1] * osel[sl]
                 + gt[:, gl + 2 * NSA_G:gl + 2 * NSA_G + 1] * ow[sl])
            o_ref[:, (h * NSA_G + g) * HD:(h * NSA_G + g + 1) * HD] = o


def _nsa_sample(q_s, gates_s, kv4_s, win_s, cache, state, page_table, w_cmp, tbl_t, t_new):
    db, n_pages = page_table.shape
    n_phys, page = cache.shape[1], cache.shape[2]
    past = n_pages * page
    wb = state.shape[2]
    assert (past + t_new) // CMP * CMP <= past and past % SELB == 0 and wb == min(WIN, past)
    ncs = (past + t_new) // CMP
    ns = -(-(past + t_new) // SELB)
    lk, lw = past + LANE, wb + LANE
    rows_pp = page * 4 * NSA_KV
    cache2 = cache.reshape(cache.shape[0], n_phys, rows_pp, HD)
    state2 = state.reshape(state.shape[0], db, wb * 2 * NSA_KV, HD)
    wc = w_cmp.reshape(2, CMP * HD, HD).astype(BF)
    qpos = past + np.arange(t_new)[:, None]
    d = qpos - (np.arange(ncs)[None, :] * CMP + CMP - 1)
    bc = _bias(tbl_t[:NSA_H], d, d >= 0, False)
    c = np.arange(lk)[None, :]
    d = qpos - c
    bs = _bias(tbl_t[:NSA_H], d, (d >= 0) & (c < past + t_new), False)
    c = np.arange(lw)[None, :]
    d = qpos - (past - wb + c)
    bw = _bias(tbl_t[:NSA_H], d, (d >= 0) & (d <= WIN) & (c < wb + t_new), False)
    e_mat = _expand_matrix(0, lk)
    s_mat = _pair_sum_matrix(ncs)
    full = lambda shape: pl.BlockSpec(shape, lambda b, pt: (0,) * len(shape))
    page_specs = [pl.BlockSpec((None, None, rows_pp, HD), lambda b, pt, p=p: (0, pt[b, p], 0, 0))
                  for p in range(n_pages)]
    in_specs = page_specs + [
        pl.BlockSpec((t_new, NSA_H * HD), lambda b, pt: (b, 0)),
        pl.BlockSpec((t_new, NSA_KV * LANE), lambda b, pt: (b, 0)),
        pl.BlockSpec((t_new, 4 * NSA_KV * HD), lambda b, pt: (b, 0)),
        pl.BlockSpec((t_new, 2 * NSA_KV * HD), lambda b, pt: (b, 0)),
        pl.BlockSpec((None, None, wb * 2 * NSA_KV, HD), lambda b, pt: (0, b, 0, 0)),
        full((2, CMP * HD, HD)), full((NSA_H, t_new, ncs)), full((NSA_H, t_new, lk)),
        full((NSA_H, t_new, lw)), full((LANE, lk)), full((ncs, LANE)),
    ]
    return pl.pallas_call(
        functools.partial(_nsa_sample_kernel, n_pages=n_pages, page=page, past=past, t_new=t_new,
                          ncs=ncs, ns=ns, wb=wb),
        grid_spec=pltpu.PrefetchScalarGridSpec(
            num_scalar_prefetch=1, grid=(db,), in_specs=in_specs,
            out_specs=pl.BlockSpec((t_new, NSA_H * HD), lambda b, pt: (b, 0)),
            scratch_shapes=[pltpu.VMEM((2, NSA_KV, past, HD), F32),
                            pltpu.VMEM((2, NSA_KV, lk, HD), BF),
                            pltpu.VMEM((2, NSA_KV, lw, HD), BF)]),
        out_shape=jax.ShapeDtypeStruct((db * t_new, NSA_H * HD), F32),
        compiler_params=_cparams(("arbitrary",)),
        name="nsa_sample",
    )(page_table, *([cache2] * n_pages), q_s, gates_s, kv4_s, win_s, state2, wc, bc, bs, bw, e_mat, s_mat)


def _diff_sample_kernel(pt_ref, *refs, n_pages, page, past, t_new, lam_init):
    pages = refs[:n_pages]
    q_ref, kvn_ref, b_ref, dl_ref, sub_ref, o_ref, kv_s = refs[n_pages:]
    del pt_ref
    rows = DF_G * t_new
    lk = past + LANE
    for p in range(n_pages):
        for h in range(DF_KV):
            kv_s[h, p * page:(p + 1) * page, :] = pages[p][:, h, :].astype(BF)
    kvn = kvn_ref[...]
    for h in range(DF_KV):
        kv_s[h, past:past + LANE, :] = _tail_tile(kvn[:, h * 4 * HD:(h + 1) * 4 * HD], 4 * HD)
    lam = _diff_lambda(dl_ref[...], lam_init)
    q = q_ref[...]
    for h in range(DF_KV):
        bias = b_ref[h * DF_G:(h + 1) * DF_G].reshape(rows, lk)
        ps = []
        for m in range(2):
            qm = jnp.concatenate(
                [q[:, ((h * DF_G + g) * 2 + m) * HD:((h * DF_G + g) * 2 + m + 1) * HD] for g in range(DF_G)],
                axis=0).astype(BF)
            ps.append(_softmax_rows(_dot_nt(qm, kv_s[h, :, m * HD:(m + 1) * HD]) + bias))
        a = ps[0] - lam * ps[1]
        out = _diff_finish(_dot(a.astype(BF), kv_s[h, :, 2 * HD:4 * HD]), sub_ref[...], lam_init)
        for g in range(DF_G):
            o_ref[:, (h * DF_G + g) * DF_DV:(h * DF_G + g + 1) * DF_DV] = out[g * t_new:(g + 1) * t_new]


def _diff_sample(qd_s, dkv_s, cache, page_table, tbl_t, dl, sub, t_new, lam_init):
    db, n_pages = page_table.shape
    page = cache.shape[2]
    past = n_pages * page
    lk = past + LANE
    qpos = past + np.arange(t_new)[:, None]
    c = np.arange(lk)[None, :]
    d = qpos - c
    bias = _bias(tbl_t[NSA_H:], d, (d >= 0) & (c < past + t_new), False)
    full = lambda shape: pl.BlockSpec(shape, lambda b, pt: (0,) * len(shape))
    page_specs = [pl.BlockSpec((None, None, page, DF_KV, 4 * HD), lambda b, pt, p=p: (0, pt[b, p], 0, 0, 0))
                  for p in range(n_pages)]
    in_specs = page_specs + [
        pl.BlockSpec((t_new, DF_H * 2 * HD), lambda b, pt: (b, 0)),
        pl.BlockSpec((t_new, DF_KV * 4 * HD), lambda b, pt: (b, 0)),
        full((DF_H, t_new, lk)), full((4, HD)), full((1, DF_DV)),
    ]
    return pl.pallas_call(
        functools.partial(_diff_sample_kernel, n_pages=n_pages, page=page, past=past, t_new=t_new,
                          lam_init=lam_init),
        grid_spec=pltpu.PrefetchScalarGridSpec(
            num_scalar_prefetch=1, grid=(db,), in_specs=in_specs,
            out_specs=pl.BlockSpec((t_new, DF_H * DF_DV), lambda b, pt: (b, 0)),
            scratch_shapes=[pltpu.VMEM((DF_KV, lk, 4 * HD), BF)]),
        out_shape=jax.ShapeDtypeStruct((db * t_new, DF_H * DF_DV), F32),
        compiler_params=_cparams(("arbitrary",)),
        name="diff_sample",
    )(page_table, *([cache] * n_pages), qd_s, dkv_s, bias, dl, sub.reshape(1, DF_DV))


def _router_kernel(h_ref, g_ref, wr_ref, br_ref, xn_ref, ids_ref, wts_ref):
    x = h_ref[...]
    ms = jnp.mean(x * x, axis=-1, keepdims=True)
    xn = x * lax.rsqrt(ms + EPS) * g_ref[...]
    xh = xn.astype(BF)
    xn_ref[...] = xh
    xl = (xn - xh.astype(F32)).astype(BF)
    wr = wr_ref[...]
    wh = wr.astype(BF)
    wl = (wr - wh.astype(F32)).astype(BF)
    lg = _dot(xh, wh) + _dot(xl, wh) + _dot(xh, wl) + br_ref[...]
    lane_i = lax.broadcasted_iota(jnp.int32, lg.shape, 1)
    lane = lane_i.astype(F32)
    big = 1000.0
    isg = lane_i < N_GROUPS
    gmax = jnp.max(jnp.where(isg, lg, -3e38), axis=-1, keepdims=True)
    gsel = jnp.min(jnp.where(isg & (lg == gmax), lane, big), axis=-1, keepdims=True)
    gw = 1.0 / jnp.sum(jnp.where(isg, jnp.exp(lg - gmax), 0.0), axis=-1, keepdims=True)
    lo = N_GROUPS + gsel * EPG
    ing = (lane >= lo) & (lane < lo + EPG)
    emax = jnp.max(jnp.where(ing, lg, -3e38), axis=-1, keepdims=True)
    pe = jnp.where(ing, jnp.exp(lg - emax), 0.0)
    pr = jnp.where(ing, pe / jnp.sum(pe, axis=-1, keepdims=True), -1.0)
    v1 = jnp.max(pr, axis=-1, keepdims=True)
    i1 = jnp.min(jnp.where(pr == v1, lane, big), axis=-1, keepdims=True)
    pr2 = jnp.where(lane == i1, -1.0, pr)
    v2 = jnp.max(pr2, axis=-1, keepdims=True)
    i2 = jnp.min(jnp.where(pr2 == v2, lane, big), axis=-1, keepdims=True)
    den = v1 + v2
    e12 = jnp.where(lane_i == 0, i1 - N_GROUPS, jnp.where(lane_i == 1, i2 - N_GROUPS, 0.0))
    ids_ref[...] = e12.astype(jnp.int32)
    wts_ref[...] = jnp.where(lane_i == 0, v1 / den * gw, jnp.where(lane_i == 1, v2 / den * gw, 0.0))


def _router(h, g, wr, br, tm):
    m, d = h.shape
    return pl.pallas_call(
        _router_kernel,
        grid=(m // tm,),
        in_specs=[pl.BlockSpec((tm, d), lambda i: (i, 0)), pl.BlockSpec((1, d), lambda i: (0, 0)),
                  pl.BlockSpec((d, LANE), lambda i: (0, 0)), pl.BlockSpec((1, LANE), lambda i: (0, 0))],
        out_specs=[pl.BlockSpec((tm, d), lambda i: (i, 0)), pl.BlockSpec((tm, LANE), lambda i: (i, 0)),
                   pl.BlockSpec((tm, LANE), lambda i: (i, 0))],
        out_shape=[jax.ShapeDtypeStruct((m, d), BF), jax.ShapeDtypeStruct((m, LANE), jnp.int32),
                   jax.ShapeDtypeStruct((m, LANE), F32)],
        compiler_params=_cparams(("arbitrary",)),
        name="router",
    )(h, g.reshape(1, d), wr, br)


def _moe_up_kernel(te_ref, tf_ref, nt_ref, x_ref, wg_ref, wu_ref, cw_ref, o_ref, wgb, wub):
    t = pl.program_id(1)

    @pl.when(t < nt_ref[0])
    def _():
        @pl.when(tf_ref[t] == 1)
        def _():
            _cast_rows(wg_ref, wgb)
            _cast_rows(wu_ref, wub)

        x = x_ref[...]
        a = _dot(x, wgb[...])
        u = _dot(x, wub[...])
        o_ref[...] = (a * jax.nn.sigmoid(a) * u * cw_ref[:, 0:1]).astype(o_ref.dtype)


def _moe_down_kernel(te_ref, tf_ref, nt_ref, h_ref, wd_ref, o_ref, wdb):
    t = pl.program_id(1)

    @pl.when(t < nt_ref[0])
    def _():
        @pl.when(tf_ref[t] == 1)
        def _():
            _cast_rows(wd_ref, wdb)

        o_ref[...] = _dot(h_ref[...], wdb[...])


def _moe(xn, ids, wts, w_gate, w_up, w_down):
    m, d = xn.shape
    n_exp, _, ff = w_gate.shape
    tm = MOE_TM
    n_pairs = 2 * m
    n_tiles = -(-(n_pairs + n_exp * (tm - 1)) // tm)
    flat_e = ids[:, :2].reshape(-1)
    flat_w = wts[:, :2].reshape(-1)
    order = jnp.argsort(flat_e, stable=True)
    counts = jnp.sum(flat_e[:, None] == jnp.arange(n_exp)[None, :], axis=0).astype(jnp.int32)
    tiles_per = (counts + tm - 1) // tm
    tile_end = jnp.cumsum(tiles_per)
    pad_start = (tile_end - tiles_per) * tm
    start = jnp.cumsum(counts) - counts
    sorted_e = flat_e[order]
    dest = pad_start[sorted_e] + jnp.arange(n_pairs, dtype=jnp.int32) - start[sorted_e]
    n_rows = n_tiles * tm
    row_tok = jnp.zeros((n_rows,), jnp.int32).at[dest].set((order // 2).astype(jnp.int32))
    row_w = jnp.zeros((n_rows,), F32).at[dest].set(flat_w[order])
    pos = jnp.zeros((n_pairs,), jnp.int32).at[order].set(dest).reshape(m, 2)
    nt = tile_end[-1]
    tix = jnp.arange(n_tiles, dtype=jnp.int32)
    tile_e = jnp.minimum(jnp.searchsorted(tile_end, jnp.minimum(tix, nt - 1), side="right"),
                         n_exp - 1).astype(jnp.int32)
    tile_first = ((tix == (tile_end - tiles_per)[tile_e]) & (tix < nt)).astype(jnp.int32)
    nt_arr = nt.reshape(1).astype(jnp.int32)

    xs = jnp.take(xn, row_tok, axis=0)
    cw = jnp.broadcast_to(row_w[:, None], (n_rows, LANE))

    tf_ = _pick(ff, (512, 256, 128))
    clamp = lambda t, nt_ref: jnp.minimum(t, nt_ref[0] - 1)
    hdn = pl.pallas_call(
        _moe_up_kernel,
        grid_spec=pltpu.PrefetchScalarGridSpec(
            num_scalar_prefetch=3, grid=(ff // tf_, n_tiles),
            in_specs=[
                pl.BlockSpec((tm, d), lambda f, t, te, tf, ntr: (clamp(t, ntr), 0)),
                pl.BlockSpec((None, d, tf_), lambda f, t, te, tf, ntr: (te[t], 0, f)),
                pl.BlockSpec((None, d, tf_), lambda f, t, te, tf, ntr: (te[t], 0, f)),
                pl.BlockSpec((tm, LANE), lambda f, t, te, tf, ntr: (clamp(t, ntr), 0)),
            ],
            out_specs=pl.BlockSpec((tm, tf_), lambda f, t, te, tf, ntr: (clamp(t, ntr), f)),
            scratch_shapes=[pltpu.VMEM((d, tf_), BF), pltpu.VMEM((d, tf_), BF)]),
        out_shape=jax.ShapeDtypeStruct((n_rows, ff), BF),
        compiler_params=_cparams(("arbitrary", "arbitrary")),
        name="moe_up",
    )(tile_e, tile_first, nt_arr, xs, w_gate, w_up, cw)

    tn = _pick(d, (1024, 512, 256, 128))
    y = pl.pallas_call(
        _moe_down_kernel,
        grid_spec=pltpu.PrefetchScalarGridSpec(
            num_scalar_prefetch=3, grid=(d // tn, n_tiles),
            in_specs=[
                pl.BlockSpec((tm, ff), lambda j, t, te, tf, ntr: (clamp(t, ntr), 0)),
                pl.BlockSpec((None, ff, tn), lambda j, t, te, tf, ntr: (te[t], 0, j)),
            ],
            out_specs=pl.BlockSpec((tm, tn), lambda j, t, te, tf, ntr: (clamp(t, ntr), j)),
            scratch_shapes=[pltpu.VMEM((ff, tn), BF)]),
        out_shape=jax.ShapeDtypeStruct((n_rows, d), F32),
        compiler_params=_cparams(("arbitrary", "arbitrary")),
        name="moe_down",
    )(tile_e, tile_first, nt_arr, hdn, w_down)
    return y, pos


def _permuted_w_in(w):
    d = w.shape[0]
    o_q = 0
    o_kv = o_q + NSA_H * HD
    o_g = o_kv + 6 * NSA_KV * HD
    o_qd = o_g + 3 * NSA_H
    o_kd = o_qd + DF_H * 2 * HD
    o_vd = o_kd + DF_KV * 2 * HD
    parts = [w[:, o_q:o_kv], w[:, o_kv:o_g], w[:, o_qd:o_kd]]
    for h in range(DF_KV):
        parts.append(w[:, o_kd + h * 2 * HD:o_kd + (h + 1) * 2 * HD])
        parts.append(w[:, o_vd + h * DF_DV:o_vd + (h + 1) * DF_DV])
    for h in range(NSA_KV):
        for r in range(3):
            c = o_g + r * NSA_H + h * NSA_G
            parts.append(w[:, c:c + NSA_G])
        parts.append(jnp.zeros((d, LANE - 3 * NSA_G), w.dtype))
    return jnp.concatenate(parts, axis=1).astype(BF)


def kernel(x_prompt, x_sample, cache_nsa_kv, cache_diff_kv, state_nsa_win, page_table, p_prompt, p_sample,
           rel_bias_table, norm_mix, w_in, w_cmp, diff_lambda, diff_subln, w_out, norm_ffn, w_router_group,
           b_router_group, w_router_expert, b_router_expert, w_exp_gate, w_exp_up, w_exp_down, norm_ple,
           w_ple_gate, w_ple_proj, final_norm):
    assert norm_mix.shape[0] == 1, "single-layer trunk"
    batch, seq, d = x_prompt.shape
    db, t_new, _ = x_sample.shape
    mp, ms = batch * seq, db * t_new
    m = mp + ms
    lam_init = 0.8 - 0.6 * math.exp(-0.3 * 0)
    tm = _pick(math.gcd(mp, ms), (1024, 512, 256, 128))
    tm_s = _pick(math.gcd(mp, ms), (256, 128))
    xp = x_prompt.reshape(mp, d)
    xs = x_sample.reshape(ms, d)
    tbl_t = rel_bias_table.T

    xn = _rms2(xp, xs, norm_mix[0], tm_s)
    wp = _permuted_w_in(w_in[0])
    c_q, c_kv, c_win, c_qd, c_dkv, c_g = 0, 2048, 3072, 3584, 5632, 7680
    scale = HD ** -0.5
    proj = functools.partial(_matmul, [xn], [(wp, 0)], tm=tm)
    q_nsa = proj(rows=m, n_cols=2048, col0=c_q, tn=512, out_dtype=BF, scale=scale, name="proj_qn")
    qd = proj(rows=m, n_cols=2048, col0=c_qd, tn=512, out_dtype=BF, scale=scale, name="proj_qd")
    gates = proj(rows=m, n_cols=256, col0=c_g, tn=256, out_dtype=F32, name="proj_gate")
    kv4_p = proj(rows=mp, n_cols=1024, col0=c_kv, tn=512, out_dtype=F32, name="proj_kv_p")
    kv4_s = proj(rows=ms, row0=mp, n_cols=1024, col0=c_kv, tn=512, out_dtype=F32, name="proj_kv_s")
    win_p = proj(rows=mp, n_cols=512, col0=c_win, tn=512, out_dtype=F32, name="proj_win_p")
    win_s = proj(rows=ms, row0=mp, n_cols=512, col0=c_win, tn=512, out_dtype=F32, name="proj_win_s")
    dkv_p = proj(rows=mp, n_cols=2048, col0=c_dkv, tn=512, out_dtype=F32, name="proj_dkv_p")
    dkv_s = proj(rows=ms, row0=mp, n_cols=2048, col0=c_dkv, tn=512, out_dtype=F32, name="proj_dkv_s")

    nc = seq // CMP
    kvr = kv4_p.reshape(batch, seq, 4, NSA_KV, HD)
    cmp_out = []
    for kind in range(2):
        a = kvr[:, :, kind].transpose(2, 0, 1, 3).reshape(NSA_KV * batch * nc, CMP * HD).astype(BF)
        r = a.shape[0]
        cmp_out.append(_matmul([a], [(w_cmp[0, kind].reshape(CMP * HD, HD), 0)], rows=r, n_cols=HD,
                               tm=_pick(r, (512, 256, 128, 64, 32, 16)), tn=HD, out_dtype=BF,
                               name="compress").reshape(NSA_KV, batch, nc, HD))
    o_n = _nsa_prompt(q_nsa, gates, cmp_out[0], cmp_out[1], kv4_p, win_p, tbl_t, batch, seq, m)
    o_d = _diff_prompt(qd, dkv_p, tbl_t, diff_lambda[0], diff_subln[0], batch, seq, m, lam_init)

    o_n_s = _nsa_sample(q_nsa[mp:].astype(F32), gates[mp:], kv4_s, win_s, cache_nsa_kv, state_nsa_win,
                        page_table, w_cmp[0], tbl_t, t_new)
    o_d_s = _diff_sample(qd[mp:].astype(F32), dkv_s, cache_diff_kv, page_table, tbl_t, diff_lambda[0],
                         diff_subln[0], t_new, lam_init)
    o_n = lax.dynamic_update_slice(o_n, o_n_s.astype(BF), (mp, 0))
    o_d = lax.dynamic_update_slice(o_d, o_d_s.astype(BF), (mp, 0))

    h1 = _matmul([o_n, o_d], [(w_out[0], 0), (w_out[0], 1)], rows=m, n_cols=d, tm=tm,
                 tn=_pick(d, (512, 256, 128)), out_dtype=F32, epi="res2", epi_args=(xp, xs), name="out_proj")

    wr = jnp.concatenate([w_router_group[0], w_router_expert[0],
                          jnp.zeros((d, LANE - N_GROUPS - N_EXP), F32)], axis=1)
    br = jnp.concatenate([b_router_group[0], b_router_expert[0],
                          jnp.zeros((LANE - N_GROUPS - N_EXP,), F32)]).reshape(1, LANE)
    xn2, ids, wts = _router(h1, norm_ffn[0], wr, br, tm_s)
    y, pos = _moe(xn2, ids, wts, w_exp_gate[0], w_exp_up[0], w_exp_down[0])
    h2 = h1 + jnp.take(y, pos[:, 0], axis=0) + jnp.take(y, pos[:, 1], axis=0)

    xn3 = _rms(h2, norm_ple[0], tm_s, BF)
    p_all = jnp.concatenate([p_prompt[0].reshape(mp, -1), p_sample[0].reshape(ms, -1)], axis=0).astype(BF)
    h3 = _matmul([xn3], [(w_ple_gate[0], 0)], rows=m, n_cols=d, tm=tm, tn=_pick(d, (512, 256, 128)),
                 out_dtype=F32, epi="ple", epi_args=(h2, p_all, w_ple_proj[0]), name="ple")

    y_p = _rms(h3, final_norm, tm_s, F32, row0=0, rows=mp).reshape(batch, seq, d)
    y_s = _rms(h3, final_norm, tm_s, F32, row0=mp, rows=ms).reshape(db, t_new, d)
    wk = min(WIN, seq)
    win_p_out = win_p.reshape(batch, seq, 2, NSA_KV, HD)[:, seq - wk:]
    new_win = jnp.concatenate([state_nsa_win[0], win_s.reshape(db, t_new, 2, NSA_KV, HD)], axis=1)[:, t_new:]
    return (y_p, y_s,
            kv4_p.reshape(1, batch, seq, 4, NSA_KV, HD), kv4_s.reshape(1, db, t_new, 4, NSA_KV, HD),
            dkv_p.reshape(1, batch, seq, DF_KV, 4 * HD), dkv_s.reshape(1, db, t_new, DF_KV, 4 * HD),
            win_p_out[None], new_win[None])
```

```python
import functools
import math

import numpy as np
import jax
import jax.numpy as jnp
from jax import lax
from jax.experimental import pallas as pl
from jax.experimental.pallas import tpu as pltpu

BF = jnp.bfloat16
F32 = jnp.float32

HD = 128
NSA_H = 16
NSA_KV = 2
NSA_G = NSA_H // NSA_KV
CMP = 32
SELB = 64
TOPK = 16
WIN = 512
DF_H = 8
DF_KV = 4
DF_G = DF_H // DF_KV
DF_DV = 2 * HD
REL_BUCKETS = 32
REL_MAX_DIST = 128
N_GROUPS = 4
EPG = 8
N_EXP = N_GROUPS * EPG
EPS = 1e-6
NEG = -1e30
FORCE = 1e4
LANE = 128
VMEM_LIMIT = 56 * 1024 * 1024
MOE_TM = 640

QB_NSA = 128
PAD_SEL = 384
PAD_WIN = WIN
NEAR_SEL = 512
BAND_WIN = WIN + QB_NSA
QB_DF = 256
PAD_DF = 256
NEAR_DF = 512
TK = 256


def _dot(a, b):
    return jnp.dot(a, b, preferred_element_type=F32)


def _dot_nt(a, b):
    return lax.dot_general(a, b, (((1,), (1,)), ((), ())), preferred_element_type=F32)


def _cparams(sem):
    return pltpu.CompilerParams(dimension_semantics=sem, vmem_limit_bytes=VMEM_LIMIT)


def _pick(n, cands):
    for c in cands:
        if n % c == 0:
            return c
    raise ValueError(f"no tile in {cands} divides {n}")


def _rms2_kernel(xp_ref, xs_ref, g_ref, o_ref, *, np_tiles):
    i = pl.program_id(0)

    def go(x_ref):
        x = x_ref[...]
        ms = jnp.mean(x * x, axis=-1, keepdims=True)
        o_ref[...] = (x * lax.rsqrt(ms + EPS) * g_ref[...]).astype(o_ref.dtype)

    @pl.when(i < np_tiles)
    def _():
        go(xp_ref)

    @pl.when(i >= np_tiles)
    def _():
        go(xs_ref)


def _rms2(xp, xs, g, tm):
    mp, d = xp.shape
    ms = xs.shape[0]
    npt, nst = mp // tm, ms // tm
    return pl.pallas_call(
        functools.partial(_rms2_kernel, np_tiles=npt),
        grid=(npt + nst,),
        in_specs=[pl.BlockSpec((tm, d), lambda i: (jnp.minimum(i, npt - 1), 0)),
                  pl.BlockSpec((tm, d), lambda i: (jnp.maximum(i - npt, 0), 0)),
                  pl.BlockSpec((1, d), lambda i: (0, 0))],
        out_specs=pl.BlockSpec((tm, d), lambda i: (i, 0)),
        out_shape=jax.ShapeDtypeStruct((mp + ms, d), BF),
        compiler_params=_cparams(("arbitrary",)),
        name="rms2",
    )(xp, xs, g.reshape(1, d))


def _rms_kernel(x_ref, g_ref, o_ref):
    x = x_ref[...]
    ms = jnp.mean(x * x, axis=-1, keepdims=True)
    o_ref[...] = (x * lax.rsqrt(ms + EPS) * g_ref[...]).astype(o_ref.dtype)


def _rms(x, g, tm, out_dtype, row0=0, rows=None):
    m, d = x.shape
    rows = m if rows is None else rows
    t0 = row0 // tm
    return pl.pallas_call(
        _rms_kernel,
        grid=(rows // tm,),
        in_specs=[pl.BlockSpec((tm, d), lambda i: (i + t0, 0)),
                  pl.BlockSpec((1, d), lambda i: (0, 0))],
        out_specs=pl.BlockSpec((tm, d), lambda i: (i, 0)),
        out_shape=jax.ShapeDtypeStruct((rows, d), out_dtype),
        compiler_params=_cparams(("arbitrary",)),
        name="rms",
    )(x, g.reshape(1, d))


def _cast_rows(src_ref, dst_ref):
    k = src_ref.shape[0]
    ch = 256 if k % 256 == 0 else k

    def body(c, carry):
        r = pl.multiple_of(c * ch, ch)
        dst_ref[pl.ds(r, ch), :] = src_ref[pl.ds(r, ch), :].astype(BF)
        return carry

    lax.fori_loop(0, k // ch, body, 0)


def _mm_kernel(*refs, n_a, cast, epi, scale, np_tiles):
    a = refs[:n_a]
    w = refs[n_a:2 * n_a]
    idx = 2 * n_a
    if epi == "res2":
        rp_ref, rs_ref = refs[idx:idx + 2]
        idx += 2
    elif epi == "ple":
        h_ref, p_ref, wp_ref = refs[idx:idx + 3]
        idx += 3
    o_ref = refs[idx]
    idx += 1
    wb = refs[idx:idx + n_a] if cast else w
    i = pl.program_id(1)

    if cast:
        @pl.when(i == 0)
        def _():
            for k in range(n_a):
                _cast_rows(w[k], wb[k])

    acc = _dot(a[0][...], wb[0][...])
    for k in range(1, n_a):
        acc = acc + _dot(a[k][...], wb[k][...])
    if scale is not None:
        acc = acc * scale
    if epi is None:
        o_ref[...] = acc.astype(o_ref.dtype)
    elif epi == "res2":
        @pl.when(i < np_tiles)
        def _():
            o_ref[...] = (acc + rp_ref[...]).astype(o_ref.dtype)

        @pl.when(i >= np_tiles)
        def _():
            o_ref[...] = (acc + rs_ref[...]).astype(o_ref.dtype)
    elif epi == "ple":
        gate = jax.nn.sigmoid(acc)
        proj = _dot(p_ref[...], wp_ref[...].astype(BF))
        o_ref[...] = (h_ref[...] + gate * proj).astype(o_ref.dtype)


def _matmul(a_list, w_list, *, rows, n_cols, tm, tn, out_dtype, row0=0, col0=0,
            scale=None, epi=None, epi_args=(), name="mm"):
    n_a = len(a_list)
    cast = w_list[0][0].dtype != BF
    t0, c0 = row0 // tm, col0 // tn
    gm, gn = rows // tm, n_cols // tn
    in_specs, args = [], []
    for a in a_list:
        in_specs.append(pl.BlockSpec((tm, a.shape[1]), lambda j, i: (i + t0, 0)))
        args.append(a)
    for ent, a in zip(w_list, a_list):
        w, kb = ent[0], ent[1]
        colfn = ent[2] if len(ent) > 2 else (lambda j: j + c0)
        in_specs.append(pl.BlockSpec((a.shape[1], tn), lambda j, i, kb=kb, colfn=colfn: (kb, colfn(j))))
        args.append(w)
    np_tiles = 0
    if epi == "res2":
        xp, xs = epi_args
        np_tiles = xp.shape[0] // tm
        in_specs.append(pl.BlockSpec((tm, tn), lambda j, i: (jnp.minimum(i, np_tiles - 1), j)))
        in_specs.append(pl.BlockSpec((tm, tn), lambda j, i: (jnp.maximum(i - np_tiles, 0), j)))
        args += [xp, xs]
    elif epi == "ple":
        h, p, wp = epi_args
        in_specs.append(pl.BlockSpec((tm, tn), lambda j, i: (i, j)))
        in_specs.append(pl.BlockSpec((tm, p.shape[1]), lambda j, i: (i, 0)))
        in_specs.append(pl.BlockSpec((wp.shape[0], tn), lambda j, i: (0, j)))
        args += [h, p, wp]
    scratch = [pltpu.VMEM((a.shape[1], tn), BF) for a in a_list] if cast else []
    return pl.pallas_call(
        functools.partial(_mm_kernel, n_a=n_a, cast=cast, epi=epi, scale=scale, np_tiles=np_tiles),
        grid=(gn, gm),
        in_specs=in_specs,
        out_specs=pl.BlockSpec((tm, tn), lambda j, i: (i, j)),
        out_shape=jax.ShapeDtypeStruct((rows, n_cols), out_dtype),
        scratch_shapes=scratch,
        compiler_params=_cparams(("arbitrary", "arbitrary")),
        name=name,
    )(*args)


def _bucket_np(dist):
    n = np.maximum(dist, 0)
    max_exact = REL_BUCKETS // 2
    nf = np.maximum(n, 1).astype(np.float32)
    log_b = max_exact + (np.log(nf / np.float32(max_exact)) / np.float32(math.log(REL_MAX_DIST / max_exact))
                         * np.float32(REL_BUCKETS - max_exact)).astype(np.int32)
    return np.where(n < max_exact, n, np.minimum(log_b, REL_BUCKETS - 1)).astype(np.int32)


def _bias_vec(tbl_t, d_lo, d_hi, v_lo, v_hi, sub_far):
    dist = np.arange(d_lo, d_hi)
    b = jnp.take(tbl_t, jnp.asarray(_bucket_np(dist)), axis=1)
    if sub_far:
        b = b - tbl_t[:, REL_BUCKETS - 1:REL_BUCKETS]
    return jnp.where(jnp.asarray((dist >= v_lo) & (dist <= v_hi))[None], b, NEG).astype(F32)


def _rel_pattern(tbl_t, nrows, ncols, base, v_lo, v_hi, sub_far, col_step=1):
    d_lo = base - col_step * (ncols - 1)
    vec = _bias_vec(tbl_t, d_lo, base + nrows, v_lo, v_hi, sub_far)
    if col_step == 1:
        rv = vec[:, ::-1]
        rows = [lax.slice_in_dim(rv, nrows - 1 - r, nrows - 1 - r + ncols, axis=1) for r in range(nrows)]
        return jnp.stack(rows, axis=1)
    cols = [lax.slice_in_dim(vec, col_step * (ncols - 1 - c), col_step * (ncols - 1 - c) + nrows, axis=1)
            for c in range(ncols)]
    return jnp.stack(cols, axis=1).transpose(0, 2, 1)


def _pair_sum_matrix(nc):
    n = np.arange(nc)[:, None]
    b = np.arange(LANE)[None, :]
    return jnp.asarray((n // (SELB // CMP) == b).astype(np.float32), dtype=BF)


def _expand_matrix(pad, n_keys):
    l = np.arange(LANE)[:, None]
    c = np.arange(pad + n_keys)[None, :]
    return jnp.asarray(((c >= pad) & ((c - pad) // SELB == l)).astype(np.float32), dtype=BF)


def _split3(x):
    hi = x.astype(BF)
    r = x - hi.astype(F32)
    mid = r.astype(BF)
    lo = (r - mid.astype(F32)).astype(BF)
    return hi, mid, lo


def _select_blocks(psum, s_mat, qpos, ns):
    hi, mid, lo = _split3(psum)
    imp = _dot(hi, s_mat) + _dot(mid, s_mat) + _dot(lo, s_mat)
    shape = imp.shape
    lane = lax.broadcasted_iota(jnp.int32, shape, 1)
    valid = lane * SELB <= qpos
    cur = jnp.right_shift(qpos, 6)
    forced = (lane == 0) | (lane == cur) | (lane == cur - 1)
    score = jnp.where(valid, imp + jnp.where(forced, FORCE, 0.0), NEG)
    score = jnp.where(lane < ns, score, -3e38)
    cnt = jnp.zeros(shape, F32)
    for i in range(ns):
        ci = score[:, i:i + 1]
        cnt = cnt + jnp.where(lane > i, jnp.where(ci >= score, 1.0, 0.0), jnp.where(ci > score, 1.0, 0.0))
    sel = (cnt < float(min(TOPK, ns))) & (lane < ns)
    return jnp.where(sel, 1.0, 0.0).astype(BF)


def _softmax_rows(s, valid=None):
    m = jnp.max(s, axis=-1, keepdims=True)
    p = jnp.exp(s - m)
    if valid is not None:
        p = jnp.where(valid, p, 0.0)
    l = jnp.sum(p, axis=-1, keepdims=True)
    return p / jnp.where(l > 0.0, l, 1.0)


def _lanes(x, n):
    return x if n == LANE else jnp.concatenate([x] * (n // LANE), axis=1)


def _online(carry, s, vt, ones_in_v=False):
    m, l, acc = carry
    dv = acc.shape[1]
    m_new = jnp.maximum(m, jnp.max(s, axis=-1, keepdims=True))
    p = jnp.exp(s - _lanes(m_new, s.shape[1]))
    alpha = jnp.exp(m - m_new)
    pv = _dot(p.astype(BF), vt)
    if ones_in_v:
        l = alpha * l + pv[:, dv:dv + LANE]
        pv = pv[:, 0:dv]
    else:
        l = alpha * l + jnp.sum(p, axis=-1, keepdims=True)
    return m_new, l, _lanes(alpha, dv) * acc + pv


def _online_init(rows, dv):
    return jnp.full((rows, LANE), NEG, F32), jnp.zeros((rows, LANE), F32), jnp.zeros((rows, dv), F32)


def _flash_step(q, kt, vt, bias, m_ref, l_ref, acc_ref, rows, ones_in_v=False):
    s = _dot_nt(q, kt)
    if bias is not None:
        s = s + bias
    m_new, l, acc = _online((m_ref[rows, :], l_ref[rows, :], acc_ref[rows, :]), s, vt, ones_in_v)
    m_ref[rows, :] = m_new
    l_ref[rows, :] = l
    acc_ref[rows, :] = acc


def _nsa_prompt_kernel(q_ref, gate_ref, kc_ref, vc_ref, ks_ref, vs_ref, kw_ref, vw_ref,
                       bc_ref, pn_ref, pw_ref, e_ref, s_ref, o_ref,
                       ks_s, vs_s, kw_s, vw_s, m_s, l_s, acc_s, o_s, *, seq, nc, ns):
    i = pl.program_id(2)
    s0 = i * QB_NSA
    rows = NSA_G * QB_NSA

    @pl.when(i == 0)
    def _():
        ks_s[0:PAD_SEL, :] = jnp.zeros((PAD_SEL, HD), BF)
        vs_s[0:PAD_SEL, :] = jnp.zeros((PAD_SEL, 2 * HD), BF)
        kw_s[0:PAD_WIN, :] = jnp.zeros((PAD_WIN, HD), BF)
        vw_s[0:PAD_WIN, :] = jnp.zeros((PAD_WIN, 2 * HD), BF)
        ch = 512
        ones = jnp.ones((ch, HD), BF)

        def cp(c, carry):
            r = pl.multiple_of(c * ch, ch)
            ks_s[pl.ds(PAD_SEL + r, ch), :] = ks_ref[pl.ds(r, ch), :].astype(BF)
            vs_s[pl.ds(PAD_SEL + r, ch), 0:HD] = vs_ref[pl.ds(r, ch), :].astype(BF)
            vs_s[pl.ds(PAD_SEL + r, ch), HD:2 * HD] = ones
            kw_s[pl.ds(PAD_WIN + r, ch), :] = kw_ref[pl.ds(r, ch), :].astype(BF)
            vw_s[pl.ds(PAD_WIN + r, ch), 0:HD] = vw_ref[pl.ds(r, ch), :].astype(BF)
            vw_s[pl.ds(PAD_WIN + r, ch), HD:2 * HD] = ones
            return carry

        lax.fori_loop(0, seq // ch, cp, 0)

    gt = jax.nn.sigmoid(gate_ref[...])
    head = lambda g: slice(g * HD, (g + 1) * HD)
    hrows = lambda g: slice(g * QB_NSA, (g + 1) * QB_NSA)

    kc = kc_ref[...]
    vc = vc_ref[...]
    psum = jnp.zeros((QB_NSA, nc), F32)
    for g in range(NSA_G):
        bc = bc_ref[g]
        pc = _softmax_rows(_dot_nt(q_ref[:, head(g)], kc) + bc, bc > 0.5 * NEG)
        psum = psum + pc
        o_s[:, head(g)] = gt[:, g:g + 1] * _dot(pc.astype(BF), vc)
    qpos = s0 + lax.broadcasted_iota(jnp.int32, (QB_NSA, LANE), 0)
    selb = _select_blocks(psum, s_ref[...], qpos, ns)

    nch = jnp.maximum(i - 1, 0) // 2
    far_keys = nch * TK
    m_s[...] = jnp.full((rows, LANE), NEG, F32)
    l_s[...] = jnp.zeros((rows, LANE), F32)
    acc_s[...] = jnp.zeros((rows, HD), F32)

    def sel_chunk(r, bias_of):
        kt = ks_s[pl.ds(r, TK), :]
        vt = vs_s[pl.ds(r, TK), :]
        madd = (_dot(selb, e_ref[:, pl.ds(r, TK)]) - 1.0) * (-NEG)
        for g in range(NSA_G):
            _flash_step(q_ref[:, head(g)], kt, vt, bias_of(g, madd), m_s, l_s, acc_s, hrows(g), ones_in_v=True)

    def far(c, carry):
        sel_chunk(pl.multiple_of(PAD_SEL + c * TK, LANE), lambda g, madd: madd)
        return carry

    lax.fori_loop(0, nch, far, 0)
    for kh in range(NEAR_SEL // TK):
        col = lax.broadcasted_iota(jnp.int32, (QB_NSA, TK), 1) + kh * TK
        cut = jnp.where(col < far_keys - s0 + PAD_SEL, NEG, 0.0)
        sel_chunk(pl.multiple_of(s0 + kh * TK, LANE),
                  lambda g, madd, kh=kh, cut=cut: pn_ref[g, :, kh * TK:(kh + 1) * TK] + madd + cut)
    for g in range(NSA_G):
        osel = acc_s[hrows(g), :] / l_s[hrows(g), :]
        o_s[:, head(g)] = o_s[:, head(g)] + gt[:, NSA_G + g:NSA_G + g + 1] * osel

    for g in range(NSA_G):
        st = _online_init(QB_NSA, HD)
        for c0 in range(0, BAND_WIN, TK):
            w = min(TK, BAND_WIN - c0)
            r = pl.multiple_of(s0 + c0, LANE)
            colw = lax.broadcasted_iota(jnp.int32, (QB_NSA, w), 1) + c0
            bias = pw_ref[g, :, c0:c0 + w] + jnp.where(colw < PAD_WIN - s0, NEG, 0.0)
            st = _online(st, _dot_nt(q_ref[:, head(g)], kw_s[pl.ds(r, w), :]) + bias, vw_s[pl.ds(r, w), :],
                         ones_in_v=True)
        ow = st[2] / st[1]
        o_ref[:, head(g)] = (o_s[:, head(g)] + gt[:, 2 * NSA_G + g:2 * NSA_G + g + 1] * ow).astype(o_ref.dtype)


def _nsa_prompt(q_nsa, gates, kc, vc, kv4_p, win_p, tbl_t, batch, seq, m_total):
    nc = seq // CMP
    ns = -(-seq // SELB)
    nqb = seq // QB_NSA
    big = 1 << 30
    pn = _rel_pattern(tbl_t[:NSA_H], QB_NSA, NEAR_SEL, PAD_SEL, 0, big, True)
    pw = _rel_pattern(tbl_t[:NSA_H], QB_NSA, BAND_WIN, PAD_WIN, 0, WIN, False)
    bc = _rel_pattern(tbl_t[:NSA_H], seq, nc, -(CMP - 1), 0, big, False, col_step=CMP)
    e_mat = _expand_matrix(PAD_SEL, seq)
    s_mat = _pair_sum_matrix(nc)
    kv_spec = lambda col: pl.BlockSpec((seq, HD), lambda b, h, i, col=col: (b, col + h))
    return pl.pallas_call(
        functools.partial(_nsa_prompt_kernel, seq=seq, nc=nc, ns=ns),
        grid=(batch, NSA_KV, nqb),
        in_specs=[
            pl.BlockSpec((QB_NSA, NSA_G * HD), lambda b, h, i: (b * nqb + i, h)),
            pl.BlockSpec((QB_NSA, LANE), lambda b, h, i: (b * nqb + i, h)),
            pl.BlockSpec((None, None, nc, HD), lambda b, h, i: (h, b, 0, 0)),
            pl.BlockSpec((None, None, nc, HD), lambda b, h, i: (h, b, 0, 0)),
            kv_spec(2 * NSA_KV), kv_spec(3 * NSA_KV),
            pl.BlockSpec((seq, HD), lambda b, h, i: (b, h)),
            pl.BlockSpec((seq, HD), lambda b, h, i: (b, NSA_KV + h)),
            pl.BlockSpec((NSA_G, QB_NSA, nc), lambda b, h, i: (h, i, 0)),
            pl.BlockSpec((NSA_G, QB_NSA, NEAR_SEL), lambda b, h, i: (h, 0, 0)),
            pl.BlockSpec((NSA_G, QB_NSA, BAND_WIN), lambda b, h, i: (h, 0, 0)),
            pl.BlockSpec((LANE, PAD_SEL + seq), lambda b, h, i: (0, 0)),
            pl.BlockSpec((nc, LANE), lambda b, h, i: (0, 0)),
        ],
        out_specs=pl.BlockSpec((QB_NSA, NSA_G * HD), lambda b, h, i: (b * nqb + i, h)),
        out_shape=jax.ShapeDtypeStruct((m_total, NSA_H * HD), BF),
        scratch_shapes=[pltpu.VMEM((PAD_SEL + seq, HD), BF), pltpu.VMEM((PAD_SEL + seq, 2 * HD), BF),
                        pltpu.VMEM((PAD_WIN + seq, HD), BF), pltpu.VMEM((PAD_WIN + seq, 2 * HD), BF),
                        pltpu.VMEM((NSA_G * QB_NSA, LANE), F32), pltpu.VMEM((NSA_G * QB_NSA, LANE), F32),
                        pltpu.VMEM((NSA_G * QB_NSA, HD), F32), pltpu.VMEM((QB_NSA, NSA_G * HD), F32)],
        compiler_params=_cparams(("arbitrary", "arbitrary", "arbitrary")),
        name="nsa_prompt",
    )(q_nsa, gates, kc, vc, kv4_p, kv4_p, win_p, win_p, bc, pn, pw, e_mat, s_mat)


def _diff_lambda(dl, lam_init):
    a = jnp.sum(dl[0:1] * dl[1:2], axis=-1, keepdims=True)
    b = jnp.sum(dl[2:3] * dl[3:4], axis=-1, keepdims=True)
    return jnp.exp(a) - jnp.exp(b) + lam_init


def _diff_finish(a, sub, lam_init):
    ms = jnp.mean(a * a, axis=-1, keepdims=True)
    return a * lax.rsqrt(ms + EPS) * sub * (1.0 - lam_init)


def _diff_prompt_kernel(q_ref, kv_ref, pn_ref, dl_ref, sub_ref, o_ref, kv_s, m_s, l_s, acc_s, *, seq, lam_init):
    i = pl.program_id(2)
    s0 = i * QB_DF
    rows = DF_G * QB_DF

    @pl.when(i == 0)
    def _():
        kv_s[0:PAD_DF, :] = jnp.zeros((PAD_DF, 4 * HD), BF)
        ch = 256

        def cp(c, carry):
            r = pl.multiple_of(c * ch, ch)
            kv_s[pl.ds(PAD_DF + r, ch), :] = kv_ref[pl.ds(r, ch), :].astype(BF)
            return carry

        lax.fori_loop(0, seq // ch, cp, 0)

    lam = _diff_lambda(dl_ref[...], lam_init)
    nfar = jnp.maximum(i - 1, 0)
    sub_rows = 128
    n_sub = QB_DF // sub_rows
    streams = [(m, g, j) for m in range(2) for g in range(DF_G) for j in range(n_sub)]
    srows = lambda k: slice(k * sub_rows, (k + 1) * sub_rows)
    m_s[...] = jnp.full((2 * rows, LANE), NEG, F32)
    l_s[...] = jnp.zeros((2 * rows, LANE), F32)
    acc_s[...] = jnp.zeros((2 * rows, DF_DV), F32)

    def chunk(r, bias_of):
        vt = kv_s[pl.ds(r, TK), 2 * HD:4 * HD]
        for k, (m, g, j) in enumerate(streams):
            q = q_ref[j * sub_rows:(j + 1) * sub_rows, (g * 2 + m) * HD:(g * 2 + m + 1) * HD]
            kt = kv_s[pl.ds(r, TK), m * HD:(m + 1) * HD]
            _flash_step(q, kt, vt, bias_of(g, j), m_s, l_s, acc_s, srows(k))

    def far(c, carry):
        chunk(pl.multiple_of(PAD_DF + c * TK, TK), lambda g, j: None)
        return carry

    lax.fori_loop(0, nfar, far, 0)
    for kh in range(NEAR_DF // TK):
        col = lax.broadcasted_iota(jnp.int32, (sub_rows, TK), 1) + kh * TK
        cut = jnp.where(col < nfar * TK - s0 + PAD_DF, NEG, 0.0)
        chunk(pl.multiple_of(s0 + kh * TK, TK),
              lambda g, j, kh=kh, cut=cut: pn_ref[g, j * sub_rows:(j + 1) * sub_rows, kh * TK:(kh + 1) * TK] + cut)
    half = len(streams) // 2
    for k, (_, g, j) in enumerate(streams[:half]):
        o1 = acc_s[srows(k), :] / _lanes(l_s[srows(k), :], DF_DV)
        o2 = acc_s[srows(half + k), :] / _lanes(l_s[srows(half + k), :], DF_DV)
        out = _diff_finish(o1 - lam * o2, sub_ref[...], lam_init)
        o_ref[j * sub_rows:(j + 1) * sub_rows, g * DF_DV:(g + 1) * DF_DV] = out.astype(o_ref.dtype)


def _diff_prompt(qd, dkv_p, tbl_t, dl, sub, batch, seq, m_total, lam_init):
    nqb = seq // QB_DF
    pn = _rel_pattern(tbl_t[NSA_H:], QB_DF, NEAR_DF, PAD_DF, 0, 1 << 30, True)
    width = DF_G * 2 * HD
    return pl.pallas_call(
        functools.partial(_diff_prompt_kernel, seq=seq, lam_init=lam_init),
        grid=(batch, DF_KV, nqb),
        in_specs=[
            pl.BlockSpec((QB_DF, width), lambda b, h, i: (b * nqb + i, h)),
            pl.BlockSpec((seq, 4 * HD), lambda b, h, i: (b, h)),
            pl.BlockSpec((DF_G, QB_DF, NEAR_DF), lambda b, h, i: (h, 0, 0)),
            pl.BlockSpec((4, HD), lambda b, h, i: (0, 0)),
            pl.BlockSpec((1, DF_DV), lambda b, h, i: (0, 0)),
        ],
        out_specs=pl.BlockSpec((QB_DF, DF_G * DF_DV), lambda b, h, i: (b * nqb + i, h)),
        out_shape=jax.ShapeDtypeStruct((m_total, DF_H * DF_DV), BF),
        scratch_shapes=[pltpu.VMEM((PAD_DF + seq, 4 * HD), BF),
                        pltpu.VMEM((2 * DF_G * QB_DF, LANE), F32), pltpu.VMEM((2 * DF_G * QB_DF, LANE), F32),
                        pltpu.VMEM((2 * DF_G * QB_DF, DF_DV), F32)],
        compiler_params=_cparams(("arbitrary", "arbitrary", "arbitrary")),
        name="diff_prompt",
    )(qd, dkv_p, pn, dl, sub.reshape(1, DF_DV))


def _tail_tile(new, width):
    t = new.shape[0]
    return jnp.concatenate([new, jnp.zeros((LANE - t, width), F32)], axis=0).astype(BF)


def _nsa_sample_kernel(pt_ref, *refs, n_pages, page, past, t_new, ncs, ns, wb):
    pages = refs[:n_pages]
    (q_ref, gate_ref, kvn_ref, wn_ref, st_ref, wc_ref, bc_ref, bs_ref, bw_ref, e_ref, s_ref,
     o_ref, kcmp_s, ksel_s, kwin_s) = refs[n_pages:]
    del pt_ref
    rows = NSA_G * t_new
    n_kinds = 4

    for p in range(n_pages):
        for kind in range(n_kinds):
            for h in range(NSA_KV):
                blk = pages[p][pl.ds(kind * NSA_KV + h, page, stride=n_kinds * NSA_KV), :]
                if kind < 2:
                    kcmp_s[kind, h, p * page:(p + 1) * page, :] = blk
                else:
                    ksel_s[kind - 2, h, p * page:(p + 1) * page, :] = blk.astype(BF)
    kvn = kvn_ref[...]
    wn = wn_ref[...]
    for kind in range(2):
        for h in range(NSA_KV):
            c0 = ((kind + 2) * NSA_KV + h) * HD
            ksel_s[kind, h, past:past + LANE, :] = _tail_tile(kvn[:, c0:c0 + HD], HD)
            kwin_s[kind, h, 0:wb, :] = st_ref[pl.ds(kind * NSA_KV + h, wb, stride=2 * NSA_KV), :].astype(BF)
            c0 = (kind * NSA_KV + h) * HD
            kwin_s[kind, h, wb:wb + LANE, :] = _tail_tile(wn[:, c0:c0 + HD], HD)

    q = q_ref[...]
    gt = jax.nn.sigmoid(gate_ref[...])
    qpos = past + lax.broadcasted_iota(jnp.int32, (t_new, LANE), 0)
    for h in range(NSA_KV):
        cmp = []
        for kind in range(2):
            acc = jnp.zeros((ncs, HD), F32)
            for j in range(CMP):
                kj = kcmp_s[kind, h, pl.ds(j, ncs, stride=CMP), :]
                acc = acc + _dot(kj.astype(BF), wc_ref[kind, j * HD:(j + 1) * HD, :])
            cmp.append(acc.astype(BF))
        kc, vc = cmp
        qs = jnp.concatenate([q[:, (h * NSA_G + g) * HD:(h * NSA_G + g + 1) * HD] for g in range(NSA_G)],
                             axis=0).astype(BF)
        bc = bc_ref[h * NSA_G:(h + 1) * NSA_G].reshape(rows, ncs)
        pc = _softmax_rows(_dot_nt(qs, kc) + bc, bc > 0.5 * NEG)
        oc = _dot(pc.astype(BF), vc)
        psum = pc[0:t_new]
        for g in range(1, NSA_G):
            psum = psum + pc[g * t_new:(g + 1) * t_new]
        selb = _select_blocks(psum, s_ref[...], qpos, ns)
        lk = past + LANE
        madd = (_dot(selb, e_ref[...]) - 1.0) * (-NEG)
        s = _dot_nt(qs, ksel_s[0, h])
        s = (s.reshape(NSA_G, t_new, lk) + bs_ref[h * NSA_G:(h + 1) * NSA_G] + madd[None]).reshape(rows, lk)
        osel = _dot(_softmax_rows(s).astype(BF), ksel_s[1, h])
        lw = wb + LANE
        s = _dot_nt(qs, kwin_s[0, h]) + bw_ref[h * NSA_G:(h + 1) * NSA_G].reshape(rows, lw)
        ow = _dot(_softmax_rows(s).astype(BF), kwin_s[1, h])
        for g in range(NSA_G):
            sl = slice(g * t_new, (g + 1) * t_new)
            gl = h * LANE + g
            o = (gt[:, gl:gl + 1] * oc[sl] + gt[:, gl + NSA_G:gl + NSA_G + 1] * osel[sl]
                 + gt[:, gl + 2 * NSA_G:gl + 2 * NSA_G + 1] * ow[sl])
            o_ref[:, (h * NSA_G + g) * HD:(h * NSA_G + g + 1) * HD] = o


def _nsa_sample(q_s, gates_s, kv4_s, win_s, cache, state, page_table, w_cmp, tbl_t, t_new):
    db, n_pages = page_table.shape
    n_phys, page = cache.shape[1], cache.shape[2]
    past = n_pages * page
    wb = state.shape[2]
    assert (past + t_new) // CMP * CMP <= past and past % SELB == 0 and wb == min(WIN, past)
    ncs = (past + t_new) // CMP
    ns = -(-(past + t_new) // SELB)
    lk, lw = past + LANE, wb + LANE
    rows_pp = page * 4 * NSA_KV
    cache2 = cache.reshape(cache.shape[0], n_phys, rows_pp, HD)
    state2 = state.reshape(state.shape[0], db, wb * 2 * NSA_KV, HD)
    wc = w_cmp.reshape(2, CMP * HD, HD).astype(BF)
    big = 1 << 30
    bc = _rel_pattern(tbl_t[:NSA_H], t_new, ncs, past - (CMP - 1), 0, big, False, col_step=CMP)
    bs = _rel_pattern(tbl_t[:NSA_H], t_new, lk, past, 0, big, False)
    bs = jnp.where(jnp.asarray(np.arange(lk) < past + t_new)[None, None], bs, NEG)
    bw = _rel_pattern(tbl_t[:NSA_H], t_new, lw, wb, 0, WIN, False)
    bw = jnp.where(jnp.asarray(np.arange(lw) < wb + t_new)[None, None], bw, NEG)
    e_mat = _expand_matrix(0, lk)
    s_mat = _pair_sum_matrix(ncs)
    full = lambda shape: pl.BlockSpec(shape, lambda b, pt: (0,) * len(shape))
    page_specs = [pl.BlockSpec((None, None, rows_pp, HD), lambda b, pt, p=p: (0, pt[b, p], 0, 0))
                  for p in range(n_pages)]
    in_specs = page_specs + [
        pl.BlockSpec((t_new, NSA_H * HD), lambda b, pt: (b, 0)),
        pl.BlockSpec((t_new, NSA_KV * LANE), lambda b, pt: (b, 0)),
        pl.BlockSpec((t_new, 4 * NSA_KV * HD), lambda b, pt: (b, 0)),
        pl.BlockSpec((t_new, 2 * NSA_KV * HD), lambda b, pt: (b, 0)),
        pl.BlockSpec((None, None, wb * 2 * NSA_KV, HD), lambda b, pt: (0, b, 0, 0)),
        full((2, CMP * HD, HD)), full((NSA_H, t_new, ncs)), full((NSA_H, t_new, lk)),
        full((NSA_H, t_new, lw)), full((LANE, lk)), full((ncs, LANE)),
    ]
    return pl.pallas_call(
        functools.partial(_nsa_sample_kernel, n_pages=n_pages, page=page, past=past, t_new=t_new,
                          ncs=ncs, ns=ns, wb=wb),
        grid_spec=pltpu.PrefetchScalarGridSpec(
            num_scalar_prefetch=1, grid=(db,), in_specs=in_specs,
            out_specs=pl.BlockSpec((t_new, NSA_H * HD), lambda b, pt: (b, 0)),
            scratch_shapes=[pltpu.VMEM((2, NSA_KV, past, HD), F32),
                            pltpu.VMEM((2, NSA_KV, lk, HD), BF),
                            pltpu.VMEM((2, NSA_KV, lw, HD), BF)]),
        out_shape=jax.ShapeDtypeStruct((db * t_new, NSA_H * HD), F32),
        compiler_params=_cparams(("arbitrary",)),
        name="nsa_sample",
    )(page_table, *([cache2] * n_pages), q_s, gates_s, kv4_s, win_s, state2, wc, bc, bs, bw, e_mat, s_mat)


def _diff_sample_kernel(pt_ref, *refs, n_pages, page, past, t_new, lam_init):
    pages = refs[:n_pages]
    q_ref, kvn_ref, b_ref, dl_ref, sub_ref, o_ref, kv_s, stage_s = refs[n_pages:]
    del pt_ref
    rows = DF_G * t_new
    lk = past + LANE
    for p in range(n_pages):
        for h in range(DF_KV):
            stage_s[h] = pages[p][:, h, :]
            kv_s[h, p * page:(p + 1) * page, :] = stage_s[h].astype(BF)
    kvn = kvn_ref[...]
    for h in range(DF_KV):
        kv_s[h, past:past + LANE, :] = _tail_tile(kvn[:, h * 4 * HD:(h + 1) * 4 * HD], 4 * HD)
    lam = _diff_lambda(dl_ref[...], lam_init)
    q = q_ref[...]
    for h in range(DF_KV):
        bias = b_ref[h * DF_G:(h + 1) * DF_G].reshape(rows, lk)
        ps = []
        for m in range(2):
            qm = jnp.concatenate(
                [q[:, ((h * DF_G + g) * 2 + m) * HD:((h * DF_G + g) * 2 + m + 1) * HD] for g in range(DF_G)],
                axis=0).astype(BF)
            ps.append(_softmax_rows(_dot_nt(qm, kv_s[h, :, m * HD:(m + 1) * HD]) + bias))
        a = ps[0] - lam * ps[1]
        out = _diff_finish(_dot(a.astype(BF), kv_s[h, :, 2 * HD:4 * HD]), sub_ref[...], lam_init)
        for g in range(DF_G):
            o_ref[:, (h * DF_G + g) * DF_DV:(h * DF_G + g + 1) * DF_DV] = out[g * t_new:(g + 1) * t_new]


def _diff_sample(qd_s, dkv_s, cache, page_table, tbl_t, dl, sub, t_new, lam_init):
    db, n_pages = page_table.shape
    page = cache.shape[2]
    past = n_pages * page
    lk = past + LANE
    bias = _rel_pattern(tbl_t[NSA_H:], t_new, lk, past, 0, 1 << 30, False)
    bias = jnp.where(jnp.asarray(np.arange(lk) < past + t_new)[None, None], bias, NEG)
    full = lambda shape: pl.BlockSpec(shape, lambda b, pt: (0,) * len(shape))
    page_specs = [pl.BlockSpec((None, None, page, DF_KV, 4 * HD), lambda b, pt, p=p: (0, pt[b, p], 0, 0, 0))
                  for p in range(n_pages)]
    in_specs = page_specs + [
        pl.BlockSpec((t_new, DF_H * 2 * HD), lambda b, pt: (b, 0)),
        pl.BlockSpec((t_new, DF_KV * 4 * HD), lambda b, pt: (b, 0)),
        full((DF_H, t_new, lk)), full((4, HD)), full((1, DF_DV)),
    ]
    return pl.pallas_call(
        functools.partial(_diff_sample_kernel, n_pages=n_pages, page=page, past=past, t_new=t_new,
                          lam_init=lam_init),
        grid_spec=pltpu.PrefetchScalarGridSpec(
            num_scalar_prefetch=1, grid=(db,), in_specs=in_specs,
            out_specs=pl.BlockSpec((t_new, DF_H * DF_DV), lambda b, pt: (b, 0)),
            scratch_shapes=[pltpu.VMEM((DF_KV, lk, 4 * HD), BF), pltpu.VMEM((DF_KV, page, 4 * HD), F32)]),
        out_shape=jax.ShapeDtypeStruct((db * t_new, DF_H * DF_DV), F32),
        compiler_params=_cparams(("arbitrary",)),
        name="diff_sample",
    )(page_table, *([cache] * n_pages), qd_s, dkv_s, bias, dl, sub.reshape(1, DF_DV))


def _router_kernel(h_ref, g_ref, wr_ref, br_ref, xn_ref, ids_ref, wts_ref):
    x = h_ref[...]
    ms = jnp.mean(x * x, axis=-1, keepdims=True)
    xn = x * lax.rsqrt(ms + EPS) * g_ref[...]
    xh = xn.astype(BF)
    xn_ref[...] = xh
    xl = (xn - xh.astype(F32)).astype(BF)
    wr = wr_ref[...]
    wh = wr.astype(BF)
    wl = (wr - wh.astype(F32)).astype(BF)
    lg = _dot(xh, wh) + _dot(xl, wh) + _dot(xh, wl) + br_ref[...]
    lane_i = lax.broadcasted_iota(jnp.int32, lg.shape, 1)
    lane = lane_i.astype(F32)
    big = 1000.0
    isg = lane_i < N_GROUPS
    gmax = jnp.max(jnp.where(isg, lg, -3e38), axis=-1, keepdims=True)
    gsel = jnp.min(jnp.where(isg & (lg == gmax), lane, big), axis=-1, keepdims=True)
    gw = 1.0 / jnp.sum(jnp.where(isg, jnp.exp(lg - gmax), 0.0), axis=-1, keepdims=True)
    lo = N_GROUPS + gsel * EPG
    ing = (lane >= lo) & (lane < lo + EPG)
    emax = jnp.max(jnp.where(ing, lg, -3e38), axis=-1, keepdims=True)
    pe = jnp.where(ing, jnp.exp(lg - emax), 0.0)
    pr = jnp.where(ing, pe / jnp.sum(pe, axis=-1, keepdims=True), -1.0)
    v1 = jnp.max(pr, axis=-1, keepdims=True)
    i1 = jnp.min(jnp.where(pr == v1, lane, big), axis=-1, keepdims=True)
    pr2 = jnp.where(lane == i1, -1.0, pr)
    v2 = jnp.max(pr2, axis=-1, keepdims=True)
    i2 = jnp.min(jnp.where(pr2 == v2, lane, big), axis=-1, keepdims=True)
    den = v1 + v2
    e12 = jnp.where(lane_i == 0, i1 - N_GROUPS, jnp.where(lane_i == 1, i2 - N_GROUPS, 0.0))
    ids_ref[...] = e12.astype(jnp.int32)
    wts_ref[...] = jnp.where(lane_i == 0, v1 / den * gw, jnp.where(lane_i == 1, v2 / den * gw, 0.0))


def _router(h, g, wr, br, tm):
    m, d = h.shape
    return pl.pallas_call(
        _router_kernel,
        grid=(m // tm,),
        in_specs=[pl.BlockSpec((tm, d), lambda i: (i, 0)), pl.BlockSpec((1, d), lambda i: (0, 0)),
                  pl.BlockSpec((d, LANE), lambda i: (0, 0)), pl.BlockSpec((1, LANE), lambda i: (0, 0))],
        out_specs=[pl.BlockSpec((tm, d), lambda i: (i, 0)), pl.BlockSpec((tm, LANE), lambda i: (i, 0)),
                   pl.BlockSpec((tm, LANE), lambda i: (i, 0))],
        out_shape=[jax.ShapeDtypeStruct((m, d), BF), jax.ShapeDtypeStruct((m, LANE), jnp.int32),
                   jax.ShapeDtypeStruct((m, LANE), F32)],
        compiler_params=_cparams(("arbitrary",)),
        name="router",
    )(h, g.reshape(1, d), wr, br)


def _moe_up_kernel(te_ref, nt_ref, x_ref, wg_ref, wu_ref, o_ref):
    @pl.when(pl.program_id(0) < nt_ref[0])
    def _():
        x = x_ref[...]
        a = _dot(x, wg_ref[...].astype(BF))
        u = _dot(x, wu_ref[...].astype(BF))
        o_ref[...] = (a * jax.nn.sigmoid(a) * u).astype(o_ref.dtype)


def _moe_down_kernel(te_ref, nt_ref, h_ref, wd_ref, o_ref):
    @pl.when(pl.program_id(0) < nt_ref[0])
    def _():
        o_ref[...] = _dot(h_ref[...], wd_ref[...].astype(BF))


def _moe(xn, ids, w_gate, w_up, w_down):
    m, d = xn.shape
    n_exp, _, ff = w_gate.shape
    tm = MOE_TM
    n_pairs = 2 * m
    n_tiles = n_exp + n_pairs // tm
    n_rows = n_tiles * tm
    flat_e = ids[:, :2].reshape(-1)
    onehot = (flat_e[:, None] == jnp.arange(n_exp, dtype=jnp.int32)[None, :]).astype(jnp.int32)
    counts = jnp.sum(onehot, axis=0)
    tiles_per = (counts + tm - 1) // tm
    tile_end = jnp.cumsum(tiles_per)
    pad_start = (tile_end - tiles_per) * tm
    start = jnp.cumsum(counts) - counts
    rank = jnp.sum(onehot * (jnp.cumsum(onehot, axis=0) - 1), axis=1)
    pos = (jnp.sum(onehot * pad_start[None, :], axis=1) + rank).reshape(m, 2)
    order = jnp.argsort(flat_e, stable=True).astype(jnp.int32)
    nt = tile_end[-1]
    tix = jnp.arange(n_tiles, dtype=jnp.int32)
    tile_e = jnp.minimum(jnp.searchsorted(tile_end, jnp.minimum(tix, nt - 1), side="right"),
                         n_exp - 1).astype(jnp.int32)
    row_e = jnp.repeat(tile_e, tm)
    r = jnp.arange(n_rows, dtype=jnp.int32)
    rr = r - pad_start[row_e]
    valid = (rr < counts[row_e]) & (r < nt * tm)
    src = jnp.where(valid, start[row_e] + rr, 0)
    row_tok = jnp.where(valid, order[src] // 2, 0)
    xs = xn.at[row_tok].get(mode="promise_in_bounds")
    nt_arr = nt.reshape(1).astype(jnp.int32)

    tf_ = _pick(ff, (256, 128))
    nf = ff // tf_
    live = lambda t, ntr: t < ntr[0]
    clamp = lambda t, ntr: jnp.minimum(t, ntr[0] - 1)
    wspec = pl.BlockSpec((None, d, tf_), lambda t, f, te, ntr: (te[t], 0, jnp.where(live(t, ntr), f, nf - 1)))
    hdn = pl.pallas_call(
        _moe_up_kernel,
        grid_spec=pltpu.PrefetchScalarGridSpec(
            num_scalar_prefetch=2, grid=(n_tiles, nf),
            in_specs=[pl.BlockSpec((tm, d), lambda t, f, te, ntr: (clamp(t, ntr), 0)), wspec, wspec],
            out_specs=pl.BlockSpec((tm, tf_), lambda t, f, te, ntr: (clamp(t, ntr),
                                                                     jnp.where(live(t, ntr), f, nf - 1)))),
        out_shape=jax.ShapeDtypeStruct((n_rows, ff), BF),
        compiler_params=_cparams(("arbitrary", "arbitrary")),
        name="moe_up",
    )(tile_e, nt_arr, xs, w_gate, w_up)

    tn = _pick(d, (2048, 1024, 512, 256, 128))
    nn = d // tn
    y = pl.pallas_call(
        _moe_down_kernel,
        grid_spec=pltpu.PrefetchScalarGridSpec(
            num_scalar_prefetch=2, grid=(n_tiles, nn),
            in_specs=[
                pl.BlockSpec((tm, ff), lambda t, j, te, ntr: (clamp(t, ntr), 0)),
                pl.BlockSpec((None, ff, tn), lambda t, j, te, ntr: (te[t], 0, jnp.where(live(t, ntr), j, nn - 1))),
            ],
            out_specs=pl.BlockSpec((tm, tn), lambda t, j, te, ntr: (clamp(t, ntr),
                                                                    jnp.where(live(t, ntr), j, nn - 1)))),
        out_shape=jax.ShapeDtypeStruct((n_rows, d), F32),
        compiler_params=_cparams(("arbitrary", "arbitrary")),
        name="moe_down",
    )(tile_e, nt_arr, hdn, w_down)
    return y, pos


O_KV = NSA_H * HD
O_GATE = O_KV + 6 * NSA_KV * HD
O_QD = O_GATE + 3 * NSA_H
W_QD = DF_H * 2 * HD
W_KD = DF_KV * 2 * HD


def _gate_weights(w):
    d = w.shape[0]
    wg = w[:, O_GATE:O_QD].reshape(d, 3, NSA_KV, NSA_G).transpose(0, 2, 1, 3).reshape(d, NSA_KV, 3 * NSA_G)
    return jnp.pad(wg, ((0, 0), (0, 0), (0, LANE - 3 * NSA_G))).reshape(d, NSA_KV * LANE)


def kernel(x_prompt, x_sample, cache_nsa_kv, cache_diff_kv, state_nsa_win, page_table, p_prompt, p_sample,
           rel_bias_table, norm_mix, w_in, w_cmp, diff_lambda, diff_subln, w_out, norm_ffn, w_router_group,
           b_router_group, w_router_expert, b_router_expert, w_exp_gate, w_exp_up, w_exp_down, norm_ple,
           w_ple_gate, w_ple_proj, final_norm):
    assert norm_mix.shape[0] == 1, "single-layer trunk"
    batch, seq, d = x_prompt.shape
    db, t_new, _ = x_sample.shape
    mp, ms = batch * seq, db * t_new
    m = mp + ms
    lam_init = 0.8 - 0.6 * math.exp(-0.3 * 0)
    tm = _pick(math.gcd(mp, ms), (1024, 512, 256, 128))
    tm_s = _pick(math.gcd(mp, ms), (256, 128))
    xp = x_prompt.reshape(mp, d)
    xs = x_sample.reshape(ms, d)
    tbl_t = rel_bias_table.T

    xn = _rms2(xp, xs, norm_mix[0], tm_s)
    w0 = w_in[0]
    w_diff = w0[:, O_QD:]
    w_gate = _gate_weights(w0)
    scale = HD ** -0.5
    tn = 512
    wide = functools.partial(_matmul, [xn], tm=tm, tn=tn)
    q_nsa = wide([(w0, 0)], rows=m, n_cols=O_KV, out_dtype=BF, scale=scale, name="proj_qn")
    qd = wide([(w_diff, 0)], rows=m, n_cols=W_QD, out_dtype=BF, scale=scale, name="proj_qd")
    gates = _matmul([xn], [(w_gate, 0)], rows=m, n_cols=NSA_KV * LANE, tm=tm, tn=NSA_KV * LANE, out_dtype=F32,
                    name="proj_gate")
    kv_w = [(w0, 0, lambda j: j + O_KV // tn)]
    win_w = [(w0, 0, lambda j: j + (O_KV + 4 * NSA_KV * HD) // tn)]
    tn_d = DF_DV
    dkv_w = [(w_diff, 0, lambda j: jnp.where(j % 2 == 0, W_QD // tn_d + j // 2, (W_QD + W_KD) // tn_d + j // 2))]
    dkv = functools.partial(_matmul, [xn], dkv_w, tm=tm, tn=tn_d, n_cols=DF_KV * 4 * HD, out_dtype=F32)
    kv4_p = wide(kv_w, rows=mp, n_cols=4 * NSA_KV * HD, out_dtype=F32, name="proj_kv_p")
    kv4_s = wide(kv_w, rows=ms, row0=mp, n_cols=4 * NSA_KV * HD, out_dtype=F32, name="proj_kv_s")
    win_p = wide(win_w, rows=mp, n_cols=2 * NSA_KV * HD, out_dtype=F32, name="proj_win_p")
    win_s = wide(win_w, rows=ms, row0=mp, n_cols=2 * NSA_KV * HD, out_dtype=F32, name="proj_win_s")
    dkv_p = dkv(rows=mp, name="proj_dkv_p")
    dkv_s = dkv(rows=ms, row0=mp, name="proj_dkv_s")

    nc = seq // CMP
    kvr = kv4_p.reshape(batch, seq, 4, NSA_KV, HD)
    cmp_out = []
    for kind in range(2):
        a = kvr[:, :, kind].transpose(2, 0, 1, 3).reshape(NSA_KV * batch * nc, CMP * HD).astype(BF)
        r = a.shape[0]
        cmp_out.append(_matmul([a], [(w_cmp[0, kind].reshape(CMP * HD, HD), 0)], rows=r, n_cols=HD,
                               tm=_pick(r, (512, 256, 128, 64, 32, 16)), tn=HD, out_dtype=BF,
                               name="compress").reshape(NSA_KV, batch, nc, HD))
    o_n = _nsa_prompt(q_nsa, gates, cmp_out[0], cmp_out[1], kv4_p, win_p, tbl_t, batch, seq, m)
    o_d = _diff_prompt(qd, dkv_p, tbl_t, diff_lambda[0], diff_subln[0], batch, seq, m, lam_init)

    o_n_s = _nsa_sample(q_nsa[mp:].astype(F32), gates[mp:], kv4_s, win_s, cache_nsa_kv, state_nsa_win,
                        page_table, w_cmp[0], tbl_t, t_new)
    o_d_s = _diff_sample(qd[mp:].astype(F32), dkv_s, cache_diff_kv, page_table, tbl_t, diff_lambda[0],
                         diff_subln[0], t_new, lam_init)
    o_n = lax.dynamic_update_slice(o_n, o_n_s.astype(BF), (mp, 0))
    o_d = lax.dynamic_update_slice(o_d, o_d_s.astype(BF), (mp, 0))

    h1 = _matmul([o_n, o_d], [(w_out[0], 0), (w_out[0], 1)], rows=m, n_cols=d, tm=tm,
                 tn=_pick(d, (512, 256, 128)), out_dtype=F32, epi="res2", epi_args=(xp, xs), name="out_proj")

    wr = jnp.concatenate([w_router_group[0], w_router_expert[0],
                          jnp.zeros((d, LANE - N_GROUPS - N_EXP), F32)], axis=1)
    br = jnp.concatenate([b_router_group[0], b_router_expert[0],
                          jnp.zeros((LANE - N_GROUPS - N_EXP,), F32)]).reshape(1, LANE)
    xn2, ids, wts = _router(h1, norm_ffn[0], wr, br, tm_s)
    y, pos = _moe(xn2, ids, w_exp_gate[0], w_exp_up[0], w_exp_down[0])
    h2 = (h1 + wts[:, 0:1] * y.at[pos[:, 0]].get(mode="promise_in_bounds")
          + wts[:, 1:2] * y.at[pos[:, 1]].get(mode="promise_in_bounds"))

    xn3 = _rms(h2, norm_ple[0], tm_s, BF)
    p_all = jnp.concatenate([p_prompt[0].reshape(mp, -1), p_sample[0].reshape(ms, -1)], axis=0).astype(BF)
    h3 = _matmul([xn3], [(w_ple_gate[0], 0)], rows=m, n_cols=d, tm=tm, tn=_pick(d, (512, 256, 128)),
                 out_dtype=F32, epi="ple", epi_args=(h2, p_all, w_ple_proj[0]), name="ple")

    y_p = _rms(h3, final_norm, tm_s, F32, row0=0, rows=mp).reshape(batch, seq, d)
    y_s = _rms(h3, final_norm, tm_s, F32, row0=mp, rows=ms).reshape(db, t_new, d)
    wk = min(WIN, seq)
    win_p_out = win_p.reshape(batch, seq, 2, NSA_KV, HD)[:, seq - wk:]
    new_win = jnp.concatenate([state_nsa_win[0], win_s.reshape(db, t_new, 2, NSA_KV, HD)], axis=1)[:, t_new:]
    return (y_p, y_s,
            kv4_p.reshape(1, batch, seq, 4, NSA_KV, HD), kv4_s.reshape(1, db, t_new, 4, NSA_KV, HD),
            dkv_p.reshape(1, batch, seq, DF_KV, 4 * HD), dkv_s.reshape(1, db, t_new, DF_KV, 4 * HD),
            win_p_out[None], new_win[None])
```

```python
import functools
import math

import numpy as np
import jax
import jax.numpy as jnp
from jax import lax
from jax.experimental import pallas as pl
from jax.experimental.pallas import tpu as pltpu

BF = jnp.bfloat16
F32 = jnp.float32

HD = 128
NSA_H = 16
NSA_KV = 2
NSA_G = NSA_H // NSA_KV
CMP = 32
SELB = 64
TOPK = 16
WIN = 512
DF_H = 8
DF_KV = 4
DF_G = DF_H // DF_KV
DF_DV = 2 * HD
REL_BUCKETS = 32
REL_MAX_DIST = 128
N_GROUPS = 4
EPG = 8
N_EXP = N_GROUPS * EPG
EPS = 1e-6
NEG = -1e30
FORCE = 1e4
LANE = 128
VMEM_LIMIT = 56 * 1024 * 1024
MOE_TM = 640

QB_NSA = 128
PAD_SEL = 384
PAD_WIN = WIN
NEAR_SEL = 512
BAND_WIN = WIN + QB_NSA
QB_DF = 256
PAD_DF = 256
NEAR_DF = 512
TK = 256
CMP_PITCH = 40


def _dot(a, b):
    return jnp.dot(a, b, preferred_element_type=F32)


def _dot_nt(a, b):
    return lax.dot_general(a, b, (((1,), (1,)), ((), ())), preferred_element_type=F32)


def _cparams(sem):
    return pltpu.CompilerParams(dimension_semantics=sem, vmem_limit_bytes=VMEM_LIMIT)


def _pick(n, cands):
    for c in cands:
        if n % c == 0:
            return c
    raise ValueError(f"no tile in {cands} divides {n}")


def _rms2_kernel(xp_ref, xs_ref, g_ref, o_ref, *, np_tiles):
    i = pl.program_id(0)

    def go(x_ref):
        x = x_ref[...]
        ms = jnp.mean(x * x, axis=-1, keepdims=True)
        o_ref[...] = (x * lax.rsqrt(ms + EPS) * g_ref[...]).astype(o_ref.dtype)

    @pl.when(i < np_tiles)
    def _():
        go(xp_ref)

    @pl.when(i >= np_tiles)
    def _():
        go(xs_ref)


def _rms2(xp, xs, g, tm):
    mp, d = xp.shape
    ms = xs.shape[0]
    npt, nst = mp // tm, ms // tm
    return pl.pallas_call(
        functools.partial(_rms2_kernel, np_tiles=npt),
        grid=(npt + nst,),
        in_specs=[pl.BlockSpec((tm, d), lambda i: (jnp.minimum(i, npt - 1), 0)),
                  pl.BlockSpec((tm, d), lambda i: (jnp.maximum(i - npt, 0), 0)),
                  pl.BlockSpec((1, d), lambda i: (0, 0))],
        out_specs=pl.BlockSpec((tm, d), lambda i: (i, 0)),
        out_shape=jax.ShapeDtypeStruct((mp + ms, d), BF),
        compiler_params=_cparams(("arbitrary",)),
        name="rms2",
    )(xp, xs, g.reshape(1, d))


def _rms_kernel(x_ref, g_ref, o_ref):
    x = x_ref[...]
    ms = jnp.mean(x * x, axis=-1, keepdims=True)
    o_ref[...] = (x * lax.rsqrt(ms + EPS) * g_ref[...]).astype(o_ref.dtype)


def _rms(x, g, tm, out_dtype, row0=0, rows=None):
    m, d = x.shape
    rows = m if rows is None else rows
    t0 = row0 // tm
    return pl.pallas_call(
        _rms_kernel,
        grid=(rows // tm,),
        in_specs=[pl.BlockSpec((tm, d), lambda i: (i + t0, 0)),
                  pl.BlockSpec((1, d), lambda i: (0, 0))],
        out_specs=pl.BlockSpec((tm, d), lambda i: (i, 0)),
        out_shape=jax.ShapeDtypeStruct((rows, d), out_dtype),
        compiler_params=_cparams(("arbitrary",)),
        name="rms",
    )(x, g.reshape(1, d))


def _cast_rows(src_ref, dst_ref):
    k = src_ref.shape[0]
    ch = 256 if k % 256 == 0 else k

    def body(c, carry):
        r = pl.multiple_of(c * ch, ch)
        dst_ref[pl.ds(r, ch), :] = src_ref[pl.ds(r, ch), :].astype(BF)
        return carry

    lax.fori_loop(0, k // ch, body, 0)


def _mm_kernel(*refs, n_a, cast, epi, scale, np_tiles):
    a = refs[:n_a]
    w = refs[n_a:2 * n_a]
    idx = 2 * n_a
    if epi == "res2":
        rp_ref, rs_ref = refs[idx:idx + 2]
        idx += 2
    elif epi == "ple":
        h_ref, p_ref, wp_ref = refs[idx:idx + 3]
        idx += 3
    o_ref = refs[idx]
    idx += 1
    wb = refs[idx:idx + n_a] if cast else w
    i = pl.program_id(1)

    if cast:
        @pl.when(i == 0)
        def _():
            for k in range(n_a):
                _cast_rows(w[k], wb[k])

    acc = _dot(a[0][...], wb[0][...])
    for k in range(1, n_a):
        acc = acc + _dot(a[k][...], wb[k][...])
    if scale is not None:
        acc = acc * scale
    if epi is None:
        o_ref[...] = acc.astype(o_ref.dtype)
    elif epi == "res2":
        @pl.when(i < np_tiles)
        def _():
            o_ref[...] = (acc + rp_ref[...]).astype(o_ref.dtype)

        @pl.when(i >= np_tiles)
        def _():
            o_ref[...] = (acc + rs_ref[...]).astype(o_ref.dtype)
    elif epi == "ple":
        gate = jax.nn.sigmoid(acc)
        proj = _dot(p_ref[...], wp_ref[...].astype(BF))
        o_ref[...] = (h_ref[...] + gate * proj).astype(o_ref.dtype)


def _matmul(a_list, w_list, *, rows, n_cols, tm, tn, out_dtype, row0=0, col0=0,
            scale=None, epi=None, epi_args=(), name="mm"):
    n_a = len(a_list)
    cast = w_list[0][0].dtype != BF
    t0, c0 = row0 // tm, col0 // tn
    gm, gn = rows // tm, n_cols // tn
    in_specs, args = [], []
    for a in a_list:
        in_specs.append(pl.BlockSpec((tm, a.shape[1]), lambda j, i: (i + t0, 0)))
        args.append(a)
    for ent, a in zip(w_list, a_list):
        w, kb = ent[0], ent[1]
        colfn = ent[2] if len(ent) > 2 else (lambda j: j + c0)
        in_specs.append(pl.BlockSpec((a.shape[1], tn), lambda j, i, kb=kb, colfn=colfn: (kb, colfn(j))))
        args.append(w)
    np_tiles = 0
    if epi == "res2":
        xp, xs = epi_args
        np_tiles = xp.shape[0] // tm
        in_specs.append(pl.BlockSpec((tm, tn), lambda j, i: (jnp.minimum(i, np_tiles - 1), j)))
        in_specs.append(pl.BlockSpec((tm, tn), lambda j, i: (jnp.maximum(i - np_tiles, 0), j)))
        args += [xp, xs]
    elif epi == "ple":
        h, p, wp = epi_args
        in_specs.append(pl.BlockSpec((tm, tn), lambda j, i: (i, j)))
        in_specs.append(pl.BlockSpec((tm, p.shape[1]), lambda j, i: (i, 0)))
        in_specs.append(pl.BlockSpec((wp.shape[0], tn), lambda j, i: (0, j)))
        args += [h, p, wp]
    scratch = [pltpu.VMEM((a.shape[1], tn), BF) for a in a_list] if cast else []
    return pl.pallas_call(
        functools.partial(_mm_kernel, n_a=n_a, cast=cast, epi=epi, scale=scale, np_tiles=np_tiles),
        grid=(gn, gm),
        in_specs=in_specs,
        out_specs=pl.BlockSpec((tm, tn), lambda j, i: (i, j)),
        out_shape=jax.ShapeDtypeStruct((rows, n_cols), out_dtype),
        scratch_shapes=scratch,
        compiler_params=_cparams(("arbitrary", "arbitrary")),
        name=name,
    )(*args)


def _bucket_np(dist):
    n = np.maximum(dist, 0)
    max_exact = REL_BUCKETS // 2
    nf = np.maximum(n, 1).astype(np.float32)
    log_b = max_exact + (np.log(nf / np.float32(max_exact)) / np.float32(math.log(REL_MAX_DIST / max_exact))
                         * np.float32(REL_BUCKETS - max_exact)).astype(np.int32)
    return np.where(n < max_exact, n, np.minimum(log_b, REL_BUCKETS - 1)).astype(np.int32)


def _bucket_edges():
    b = _bucket_np(np.arange(0, 4 * REL_MAX_DIST))
    return [(k, int(np.nonzero(b == k)[0].max())) for k in range(REL_BUCKETS - 1) if (b == k).any()]


def _pattern_kernel(tbl_ref, o_ref, *, h0, base, col_step, v_lo, v_hi, c_lim, sub_far, edges):
    h = pl.program_id(0) + h0
    shape = o_ref.shape
    row = lax.broadcasted_iota(jnp.int32, shape, 0) + pl.program_id(1) * shape[0]
    col = lax.broadcasted_iota(jnp.int32, shape, 1)
    dist = row + base - col_step * col
    far = tbl_ref[REL_BUCKETS - 1, h]
    b = jnp.full(shape, far, F32)
    for k, hi in reversed(edges):
        b = jnp.where(dist <= hi, tbl_ref[k, h], b)
    if sub_far:
        b = b - far
    valid = (dist >= v_lo) & (dist <= v_hi) & (col < c_lim)
    o_ref[...] = jnp.where(valid, b, NEG)


def _rel_pattern(tbl, h0, nh, nrows, ncols, base, v_lo, v_hi, sub_far, col_step=1, c_lim=None):
    tr = _pick(nrows, (512, 256, 128, 64, 32, 16, 8))
    return pl.pallas_call(
        functools.partial(_pattern_kernel, h0=h0, base=base, col_step=col_step, v_lo=v_lo, v_hi=v_hi,
                          c_lim=ncols if c_lim is None else c_lim, sub_far=sub_far, edges=_bucket_edges()),
        grid=(nh, nrows // tr),
        in_specs=[pl.BlockSpec(memory_space=pltpu.SMEM)],
        out_specs=pl.BlockSpec((None, tr, ncols), lambda h, r: (h, r, 0)),
        out_shape=jax.ShapeDtypeStruct((nh, nrows, ncols), F32),
        compiler_params=_cparams(("arbitrary", "arbitrary")),
        name="rel_bias",
    )(tbl)


def _pair_sum_matrix(nc):
    n = np.arange(nc)[:, None]
    b = np.arange(LANE)[None, :]
    return jnp.asarray((n // (SELB // CMP) == b).astype(np.float32), dtype=BF)


def _expand_matrix(pad, n_keys):
    l = np.arange(LANE)[:, None]
    c = np.arange(pad + n_keys)[None, :]
    return jnp.asarray(((c >= pad) & ((c - pad) // SELB == l)).astype(np.float32), dtype=BF)


def _split3(x):
    hi = x.astype(BF)
    r = x - hi.astype(F32)
    mid = r.astype(BF)
    lo = (r - mid.astype(F32)).astype(BF)
    return hi, mid, lo


def _select_blocks(psum, s_mat, qpos, ns):
    hi, mid, lo = _split3(psum)
    imp = _dot(hi, s_mat) + _dot(mid, s_mat) + _dot(lo, s_mat)
    shape = imp.shape
    lane = lax.broadcasted_iota(jnp.int32, shape, 1)
    valid = lane * SELB <= qpos
    cur = jnp.right_shift(qpos, 6)
    forced = (lane == 0) | (lane == cur) | (lane == cur - 1)
    score = jnp.where(valid, imp + jnp.where(forced, FORCE, 0.0), NEG)
    score = jnp.where(lane < ns, score, -3e38)
    cnt = jnp.zeros(shape, F32)
    for i in range(ns):
        ci = score[:, i:i + 1]
        cnt = cnt + jnp.where(lane > i, jnp.where(ci >= score, 1.0, 0.0), jnp.where(ci > score, 1.0, 0.0))
    sel = (cnt < float(min(TOPK, ns))) & (lane < ns)
    return jnp.where(sel, 1.0, 0.0).astype(BF)


def _softmax_rows(s, valid=None):
    m = jnp.max(s, axis=-1, keepdims=True)
    p = jnp.exp(s - m)
    if valid is not None:
        p = jnp.where(valid, p, 0.0)
    l = jnp.sum(p, axis=-1, keepdims=True)
    return p / jnp.where(l > 0.0, l, 1.0)


def _lanes(x, n):
    return x if n == LANE else jnp.concatenate([x] * (n // LANE), axis=1)


def _online(carry, s, vt, ones_in_v=False):
    m, l, acc = carry
    dv = acc.shape[1]
    m_new = jnp.maximum(m, jnp.max(s, axis=-1, keepdims=True))
    p = jnp.exp(s - _lanes(m_new, s.shape[1]))
    alpha = jnp.exp(m - m_new)
    pv = _dot(p.astype(BF), vt)
    if ones_in_v:
        l = alpha * l + pv[:, dv:dv + LANE]
        pv = pv[:, 0:dv]
    else:
        l = alpha * l + jnp.sum(p, axis=-1, keepdims=True)
    return m_new, l, _lanes(alpha, dv) * acc + pv


def _online_init(rows, dv):
    return jnp.full((rows, LANE), NEG, F32), jnp.zeros((rows, LANE), F32), jnp.zeros((rows, dv), F32)


def _flash_step(q, kt, vt, bias, m_ref, l_ref, acc_ref, rows, ones_in_v=False):
    s = _dot_nt(q, kt)
    if bias is not None:
        s = s + bias
    m_new, l, acc = _online((m_ref[rows, :], l_ref[rows, :], acc_ref[rows, :]), s, vt, ones_in_v)
    m_ref[rows, :] = m_new
    l_ref[rows, :] = l
    acc_ref[rows, :] = acc


def _nsa_prompt_kernel(q_ref, gate_ref, kc_ref, vc_ref, ks_ref, vs_ref, kw_ref, vw_ref,
                       bc_ref, pn_ref, pw_ref, e_ref, s_ref, o_ref,
                       ks_s, vs_s, kw_s, vw_s, m_s, l_s, acc_s, o_s, *, seq, nc, ns):
    i = pl.program_id(2)
    s0 = i * QB_NSA
    rows = NSA_G * QB_NSA

    @pl.when(i == 0)
    def _():
        ks_s[0:PAD_SEL, :] = jnp.zeros((PAD_SEL, HD), BF)
        vs_s[0:PAD_SEL, :] = jnp.zeros((PAD_SEL, 2 * HD), BF)
        kw_s[0:PAD_WIN, :] = jnp.zeros((PAD_WIN, HD), BF)
        vw_s[0:PAD_WIN, :] = jnp.zeros((PAD_WIN, 2 * HD), BF)
        ch = 512
        ones = jnp.ones((ch, HD), BF)

        def cp(c, carry):
            r = pl.multiple_of(c * ch, ch)
            ks_s[pl.ds(PAD_SEL + r, ch), :] = ks_ref[pl.ds(r, ch), :].astype(BF)
            vs_s[pl.ds(PAD_SEL + r, ch), 0:HD] = vs_ref[pl.ds(r, ch), :].astype(BF)
            vs_s[pl.ds(PAD_SEL + r, ch), HD:2 * HD] = ones
            kw_s[pl.ds(PAD_WIN + r, ch), :] = kw_ref[pl.ds(r, ch), :].astype(BF)
            vw_s[pl.ds(PAD_WIN + r, ch), 0:HD] = vw_ref[pl.ds(r, ch), :].astype(BF)
            vw_s[pl.ds(PAD_WIN + r, ch), HD:2 * HD] = ones
            return carry

        lax.fori_loop(0, seq // ch, cp, 0)

    gt = jax.nn.sigmoid(gate_ref[...])
    head = lambda g: slice(g * HD, (g + 1) * HD)
    hrows = lambda g: slice(g * QB_NSA, (g + 1) * QB_NSA)

    kc = kc_ref[...]
    vc = vc_ref[...]
    psum = jnp.zeros((QB_NSA, nc), F32)
    for g in range(NSA_G):
        bc = bc_ref[g]
        pc = _softmax_rows(_dot_nt(q_ref[:, head(g)], kc) + bc, bc > 0.5 * NEG)
        psum = psum + pc
        o_s[:, head(g)] = gt[:, g:g + 1] * _dot(pc.astype(BF), vc)
    qpos = s0 + lax.broadcasted_iota(jnp.int32, (QB_NSA, LANE), 0)
    selb = _select_blocks(psum, s_ref[...], qpos, ns)

    nch = jnp.maximum(i - 1, 0) // 2
    far_keys = nch * TK
    m_s[...] = jnp.full((rows, LANE), NEG, F32)
    l_s[...] = jnp.zeros((rows, LANE), F32)
    acc_s[...] = jnp.zeros((rows, HD), F32)

    def sel_chunk(r, bias_of):
        kt = ks_s[pl.ds(r, TK), :]
        vt = vs_s[pl.ds(r, TK), :]
        madd = (_dot(selb, e_ref[:, pl.ds(r, TK)]) - 1.0) * (-NEG)
        for g in range(NSA_G):
            _flash_step(q_ref[:, head(g)], kt, vt, bias_of(g, madd), m_s, l_s, acc_s, hrows(g), ones_in_v=True)

    def far(c, carry):
        sel_chunk(pl.multiple_of(PAD_SEL + c * TK, LANE), lambda g, madd: madd)
        return carry

    lax.fori_loop(0, nch, far, 0)
    for kh in range(NEAR_SEL // TK):
        col = lax.broadcasted_iota(jnp.int32, (QB_NSA, TK), 1) + kh * TK
        cut = jnp.where(col < far_keys - s0 + PAD_SEL, NEG, 0.0)
        sel_chunk(pl.multiple_of(s0 + kh * TK, LANE),
                  lambda g, madd, kh=kh, cut=cut: pn_ref[g, :, kh * TK:(kh + 1) * TK] + madd + cut)
    for g in range(NSA_G):
        osel = acc_s[hrows(g), :] / l_s[hrows(g), :]
        o_s[:, head(g)] = o_s[:, head(g)] + gt[:, NSA_G + g:NSA_G + g + 1] * osel

    for g in range(NSA_G):
        st = _online_init(QB_NSA, HD)
        for c0 in range(0, BAND_WIN, TK):
            w = min(TK, BAND_WIN - c0)
            r = pl.multiple_of(s0 + c0, LANE)
            colw = lax.broadcasted_iota(jnp.int32, (QB_NSA, w), 1) + c0
            bias = pw_ref[g, :, c0:c0 + w] + jnp.where(colw < PAD_WIN - s0, NEG, 0.0)
            st = _online(st, _dot_nt(q_ref[:, head(g)], kw_s[pl.ds(r, w), :]) + bias, vw_s[pl.ds(r, w), :],
                         ones_in_v=True)
        ow = st[2] / st[1]
        o_ref[:, head(g)] = (o_s[:, head(g)] + gt[:, 2 * NSA_G + g:2 * NSA_G + g + 1] * ow).astype(o_ref.dtype)


def _nsa_prompt(q_nsa, gates, kc, vc, kv4_p, win_p, tbl_t, batch, seq, m_total):
    nc = seq // CMP
    ns = -(-seq // SELB)
    nqb = seq // QB_NSA
    big = 1 << 30
    pn = _rel_pattern(tbl_t, 0, NSA_H, QB_NSA, NEAR_SEL, PAD_SEL, 0, big, True)
    pw = _rel_pattern(tbl_t, 0, NSA_H, QB_NSA, BAND_WIN, PAD_WIN, 0, WIN, False)
    bc = _rel_pattern(tbl_t, 0, NSA_H, seq, nc, -(CMP - 1), 0, big, False, col_step=CMP)
    e_mat = _expand_matrix(PAD_SEL, seq)
    s_mat = _pair_sum_matrix(nc)
    kv_spec = lambda col: pl.BlockSpec((seq, HD), lambda b, h, i, col=col: (b, col + h))
    return pl.pallas_call(
        functools.partial(_nsa_prompt_kernel, seq=seq, nc=nc, ns=ns),
        grid=(batch, NSA_KV, nqb),
        in_specs=[
            pl.BlockSpec((QB_NSA, NSA_G * HD), lambda b, h, i: (b * nqb + i, h)),
            pl.BlockSpec((QB_NSA, LANE), lambda b, h, i: (b * nqb + i, h)),
            pl.BlockSpec((None, None, nc, HD), lambda b, h, i: (h, b, 0, 0)),
            pl.BlockSpec((None, None, nc, HD), lambda b, h, i: (h, b, 0, 0)),
            kv_spec(2 * NSA_KV), kv_spec(3 * NSA_KV),
            pl.BlockSpec((seq, HD), lambda b, h, i: (b, h)),
            pl.BlockSpec((seq, HD), lambda b, h, i: (b, NSA_KV + h)),
            pl.BlockSpec((NSA_G, QB_NSA, nc), lambda b, h, i: (h, i, 0)),
            pl.BlockSpec((NSA_G, QB_NSA, NEAR_SEL), lambda b, h, i: (h, 0, 0)),
            pl.BlockSpec((NSA_G, QB_NSA, BAND_WIN), lambda b, h, i: (h, 0, 0)),
            pl.BlockSpec((LANE, PAD_SEL + seq), lambda b, h, i: (0, 0)),
            pl.BlockSpec((nc, LANE), lambda b, h, i: (0, 0)),
        ],
        out_specs=pl.BlockSpec((QB_NSA, NSA_G * HD), lambda b, h, i: (b * nqb + i, h)),
        out_shape=jax.ShapeDtypeStruct((m_total, NSA_H * HD), BF),
        scratch_shapes=[pltpu.VMEM((PAD_SEL + seq, HD), BF), pltpu.VMEM((PAD_SEL + seq, 2 * HD), BF),
                        pltpu.VMEM((PAD_WIN + seq, HD), BF), pltpu.VMEM((PAD_WIN + seq, 2 * HD), BF),
                        pltpu.VMEM((NSA_G * QB_NSA, LANE), F32), pltpu.VMEM((NSA_G * QB_NSA, LANE), F32),
                        pltpu.VMEM((NSA_G * QB_NSA, HD), F32), pltpu.VMEM((QB_NSA, NSA_G * HD), F32)],
        compiler_params=_cparams(("arbitrary", "arbitrary", "arbitrary")),
        name="nsa_prompt",
    )(q_nsa, gates, kc, vc, kv4_p, kv4_p, win_p, win_p, bc, pn, pw, e_mat, s_mat)


def _diff_lambda(dl, lam_init):
    a = jnp.sum(dl[0:1] * dl[1:2], axis=-1, keepdims=True)
    b = jnp.sum(dl[2:3] * dl[3:4], axis=-1, keepdims=True)
    return jnp.exp(a) - jnp.exp(b) + lam_init


def _diff_finish(a, sub, lam_init):
    ms = jnp.mean(a * a, axis=-1, keepdims=True)
    return a * lax.rsqrt(ms + EPS) * sub * (1.0 - lam_init)


def _diff_prompt_kernel(q_ref, kv_ref, pn_ref, dl_ref, sub_ref, o_ref, kv_s, m_s, l_s, acc_s, *, seq, lam_init):
    i = pl.program_id(2)
    s0 = i * QB_DF
    rows = DF_G * QB_DF

    @pl.when(i == 0)
    def _():
        kv_s[0:PAD_DF, :] = jnp.zeros((PAD_DF, 4 * HD), BF)
        ch = 256

        def cp(c, carry):
            r = pl.multiple_of(c * ch, ch)
            kv_s[pl.ds(PAD_DF + r, ch), :] = kv_ref[pl.ds(r, ch), :].astype(BF)
            return carry

        lax.fori_loop(0, seq // ch, cp, 0)

    lam = _diff_lambda(dl_ref[...], lam_init)
    nfar = jnp.maximum(i - 1, 0)
    sub_rows = 128
    n_sub = QB_DF // sub_rows
    streams = [(m, g, j) for m in range(2) for g in range(DF_G) for j in range(n_sub)]
    srows = lambda k: slice(k * sub_rows, (k + 1) * sub_rows)
    m_s[...] = jnp.full((2 * rows, LANE), NEG, F32)
    l_s[...] = jnp.zeros((2 * rows, LANE), F32)
    acc_s[...] = jnp.zeros((2 * rows, DF_DV), F32)

    def chunk(r, bias_of):
        vt = kv_s[pl.ds(r, TK), 2 * HD:4 * HD]
        for k, (m, g, j) in enumerate(streams):
            q = q_ref[j * sub_rows:(j + 1) * sub_rows, (g * 2 + m) * HD:(g * 2 + m + 1) * HD]
            kt = kv_s[pl.ds(r, TK), m * HD:(m + 1) * HD]
            _flash_step(q, kt, vt, bias_of(g, j), m_s, l_s, acc_s, srows(k))

    def far(c, carry):
        chunk(pl.multiple_of(PAD_DF + c * TK, TK), lambda g, j: None)
        return carry

    lax.fori_loop(0, nfar, far, 0)
    for kh in range(NEAR_DF // TK):
        col = lax.broadcasted_iota(jnp.int32, (sub_rows, TK), 1) + kh * TK
        cut = jnp.where(col < nfar * TK - s0 + PAD_DF, NEG, 0.0)
        chunk(pl.multiple_of(s0 + kh * TK, TK),
              lambda g, j, kh=kh, cut=cut: pn_ref[g, j * sub_rows:(j + 1) * sub_rows, kh * TK:(kh + 1) * TK] + cut)
    half = len(streams) // 2
    for k, (_, g, j) in enumerate(streams[:half]):
        o1 = acc_s[srows(k), :] / _lanes(l_s[srows(k), :], DF_DV)
        o2 = acc_s[srows(half + k), :] / _lanes(l_s[srows(half + k), :], DF_DV)
        out = _diff_finish(o1 - lam * o2, sub_ref[...], lam_init)
        o_ref[j * sub_rows:(j + 1) * sub_rows, g * DF_DV:(g + 1) * DF_DV] = out.astype(o_ref.dtype)


def _diff_prompt(qd, dkv_p, tbl_t, dl, sub, batch, seq, m_total, lam_init):
    nqb = seq // QB_DF
    pn = _rel_pattern(tbl_t, NSA_H, DF_H, QB_DF, NEAR_DF, PAD_DF, 0, 1 << 30, True)
    width = DF_G * 2 * HD
    return pl.pallas_call(
        functools.partial(_diff_prompt_kernel, seq=seq, lam_init=lam_init),
        grid=(batch, DF_KV, nqb),
        in_specs=[
            pl.BlockSpec((QB_DF, width), lambda b, h, i: (b * nqb + i, h)),
            pl.BlockSpec((seq, 4 * HD), lambda b, h, i: (b, h)),
            pl.BlockSpec((DF_G, QB_DF, NEAR_DF), lambda b, h, i: (h, 0, 0)),
            pl.BlockSpec((4, HD), lambda b, h, i: (0, 0)),
            pl.BlockSpec((1, DF_DV), lambda b, h, i: (0, 0)),
        ],
        out_specs=pl.BlockSpec((QB_DF, DF_G * DF_DV), lambda b, h, i: (b * nqb + i, h)),
        out_shape=jax.ShapeDtypeStruct((m_total, DF_H * DF_DV), BF),
        scratch_shapes=[pltpu.VMEM((PAD_DF + seq, 4 * HD), BF),
                        pltpu.VMEM((2 * DF_G * QB_DF, LANE), F32), pltpu.VMEM((2 * DF_G * QB_DF, LANE), F32),
                        pltpu.VMEM((2 * DF_G * QB_DF, DF_DV), F32)],
        compiler_params=_cparams(("arbitrary", "arbitrary", "arbitrary")),
        name="diff_prompt",
    )(qd, dkv_p, pn, dl, sub.reshape(1, DF_DV))


def _tail_tile(new, width):
    t = new.shape[0]
    return jnp.concatenate([new, jnp.zeros((LANE - t, width), F32)], axis=0).astype(BF)


def _nsa_sample_kernel(pt_ref, *refs, n_pages, page, past, t_new, ncs, ns, wb):
    pages = refs[:n_pages]
    (q_ref, gate_ref, kvn_ref, wn_ref, st_ref, wc_ref, bc_ref, bs_ref, bw_ref, e_ref, s_ref,
     o_ref, kcmp_s, ksel_s, kwin_s) = refs[n_pages:]
    del pt_ref
    rows = NSA_G * t_new
    n_kinds = 4

    for p in range(n_pages):
        for kind in range(n_kinds):
            for h in range(NSA_KV):
                blk = pages[p][pl.ds(kind * NSA_KV + h, page, stride=n_kinds * NSA_KV), :]
                if kind < 2:
                    for nb in range(page // CMP):
                        r0 = (p * (page // CMP) + nb) * CMP_PITCH
                        kcmp_s[kind, h, r0:r0 + CMP, :] = blk[nb * CMP:(nb + 1) * CMP]
                else:
                    ksel_s[kind - 2, h, p * page:(p + 1) * page, :] = blk.astype(BF)
    kvn = kvn_ref[...]
    wn = wn_ref[...]
    for kind in range(2):
        for h in range(NSA_KV):
            c0 = ((kind + 2) * NSA_KV + h) * HD
            ksel_s[kind, h, past:past + LANE, :] = _tail_tile(kvn[:, c0:c0 + HD], HD)
            kwin_s[kind, h, 0:wb, :] = st_ref[pl.ds(kind * NSA_KV + h, wb, stride=2 * NSA_KV), :].astype(BF)
            c0 = (kind * NSA_KV + h) * HD
            kwin_s[kind, h, wb:wb + LANE, :] = _tail_tile(wn[:, c0:c0 + HD], HD)

    q = q_ref[...]
    gt = jax.nn.sigmoid(gate_ref[...])
    qpos = past + lax.broadcasted_iota(jnp.int32, (t_new, LANE), 0)
    for h in range(NSA_KV):
        cmp = []
        for kind in range(2):
            acc = jnp.zeros((ncs, HD), F32)
            for j in range(CMP):
                kj = kcmp_s[kind, h, pl.ds(j, ncs, stride=CMP_PITCH), :]
                acc = acc + _dot(kj.astype(BF), wc_ref[kind, j * HD:(j + 1) * HD, :])
            cmp.append(acc.astype(BF))
        kc, vc = cmp
        qs = jnp.concatenate([q[:, (h * NSA_G + g) * HD:(h * NSA_G + g + 1) * HD] for g in range(NSA_G)],
                             axis=0).astype(BF)
        bc = bc_ref[h * NSA_G:(h + 1) * NSA_G].reshape(rows, ncs)
        pc = _softmax_rows(_dot_nt(qs, kc) + bc, bc > 0.5 * NEG)
        oc = _dot(pc.astype(BF), vc)
        psum = pc[0:t_new]
        for g in range(1, NSA_G):
            psum = psum + pc[g * t_new:(g + 1) * t_new]
        selb = _select_blocks(psum, s_ref[...], qpos, ns)
        lk = past + LANE
        madd = (_dot(selb, e_ref[...]) - 1.0) * (-NEG)
        s = _dot_nt(qs, ksel_s[0, h])
        s = (s.reshape(NSA_G, t_new, lk) + bs_ref[h * NSA_G:(h + 1) * NSA_G] + madd[None]).reshape(rows, lk)
        osel = _dot(_softmax_rows(s).astype(BF), ksel_s[1, h])
        lw = wb + LANE
        s = _dot_nt(qs, kwin_s[0, h]) + bw_ref[h * NSA_G:(h + 1) * NSA_G].reshape(rows, lw)
        ow = _dot(_softmax_rows(s).astype(BF), kwin_s[1, h])
        for g in range(NSA_G):
            sl = slice(g * t_new, (g + 1) * t_new)
            gl = h * LANE + g
            o = (gt[:, gl:gl + 1] * oc[sl] + gt[:, gl + NSA_G:gl + NSA_G + 1] * osel[sl]
                 + gt[:, gl + 2 * NSA_G:gl + 2 * NSA_G + 1] * ow[sl])
            o_ref[:, (h * NSA_G + g) * HD:(h * NSA_G + g + 1) * HD] = o


def _nsa_sample(q_s, gates_s, kv4_s, win_s, cache, state, page_table, w_cmp, tbl_t, t_new):
    db, n_pages = page_table.shape
    n_phys, page = cache.shape[1], cache.shape[2]
    past = n_pages * page
    wb = state.shape[2]
    assert (past + t_new) // CMP * CMP <= past and past % SELB == 0 and wb == min(WIN, past)
    ncs = (past + t_new) // CMP
    ns = -(-(past + t_new) // SELB)
    lk, lw = past + LANE, wb + LANE
    rows_pp = page * 4 * NSA_KV
    cache2 = cache.reshape(cache.shape[0], n_phys, rows_pp, HD)
    state2 = state.reshape(state.shape[0], db, wb * 2 * NSA_KV, HD)
    wc = w_cmp.reshape(2, CMP * HD, HD).astype(BF)
    big = 1 << 30
    bc = _rel_pattern(tbl_t, 0, NSA_H, t_new, ncs, past - (CMP - 1), 0, big, False, col_step=CMP)
    bs = _rel_pattern(tbl_t, 0, NSA_H, t_new, lk, past, 0, big, False, c_lim=past + t_new)
    bw = _rel_pattern(tbl_t, 0, NSA_H, t_new, lw, wb, 0, WIN, False, c_lim=wb + t_new)
    e_mat = _expand_matrix(0, lk)
    s_mat = _pair_sum_matrix(ncs)
    full = lambda shape: pl.BlockSpec(shape, lambda b, pt: (0,) * len(shape))
    page_specs = [pl.BlockSpec((None, None, rows_pp, HD), lambda b, pt, p=p: (0, pt[b, p], 0, 0))
                  for p in range(n_pages)]
    in_specs = page_specs + [
        pl.BlockSpec((t_new, NSA_H * HD), lambda b, pt: (b, 0)),
        pl.BlockSpec((t_new, NSA_KV * LANE), lambda b, pt: (b, 0)),
        pl.BlockSpec((t_new, 4 * NSA_KV * HD), lambda b, pt: (b, 0)),
        pl.BlockSpec((t_new, 2 * NSA_KV * HD), lambda b, pt: (b, 0)),
        pl.BlockSpec((None, None, wb * 2 * NSA_KV, HD), lambda b, pt: (0, b, 0, 0)),
        full((2, CMP * HD, HD)), full((NSA_H, t_new, ncs)), full((NSA_H, t_new, lk)),
        full((NSA_H, t_new, lw)), full((LANE, lk)), full((ncs, LANE)),
    ]
    return pl.pallas_call(
        functools.partial(_nsa_sample_kernel, n_pages=n_pages, page=page, past=past, t_new=t_new,
                          ncs=ncs, ns=ns, wb=wb),
        grid_spec=pltpu.PrefetchScalarGridSpec(
            num_scalar_prefetch=1, grid=(db,), in_specs=in_specs,
            out_specs=pl.BlockSpec((t_new, NSA_H * HD), lambda b, pt: (b, 0)),
            scratch_shapes=[pltpu.VMEM((2, NSA_KV, ncs * CMP_PITCH, HD), F32),
                            pltpu.VMEM((2, NSA_KV, lk, HD), BF),
                            pltpu.VMEM((2, NSA_KV, lw, HD), BF)]),
        out_shape=jax.ShapeDtypeStruct((db * t_new, NSA_H * HD), F32),
        compiler_params=_cparams(("arbitrary",)),
        name="nsa_sample",
    )(page_table, *([cache2] * n_pages), q_s, gates_s, kv4_s, win_s, state2, wc, bc, bs, bw, e_mat, s_mat)


def _diff_sample_kernel(pt_ref, *refs, n_pages, page, past, t_new, lam_init):
    pages = refs[:n_pages]
    q_ref, kvn_ref, b_ref, dl_ref, sub_ref, o_ref, kv_s, stage_s = refs[n_pages:]
    del pt_ref
    rows = DF_G * t_new
    lk = past + LANE
    for p in range(n_pages):
        for h in range(DF_KV):
            stage_s[h] = pages[p][:, h, :]
            kv_s[h, p * page:(p + 1) * page, :] = stage_s[h].astype(BF)
    kvn = kvn_ref[...]
    for h in range(DF_KV):
        kv_s[h, past:past + LANE, :] = _tail_tile(kvn[:, h * 4 * HD:(h + 1) * 4 * HD], 4 * HD)
    lam = _diff_lambda(dl_ref[...], lam_init)
    q = q_ref[...]
    for h in range(DF_KV):
        bias = b_ref[h * DF_G:(h + 1) * DF_G].reshape(rows, lk)
        ps = []
        for m in range(2):
            qm = jnp.concatenate(
                [q[:, ((h * DF_G + g) * 2 + m) * HD:((h * DF_G + g) * 2 + m + 1) * HD] for g in range(DF_G)],
                axis=0).astype(BF)
            ps.append(_softmax_rows(_dot_nt(qm, kv_s[h, :, m * HD:(m + 1) * HD]) + bias))
        a = ps[0] - lam * ps[1]
        out = _diff_finish(_dot(a.astype(BF), kv_s[h, :, 2 * HD:4 * HD]), sub_ref[...], lam_init)
        for g in range(DF_G):
            o_ref[:, (h * DF_G + g) * DF_DV:(h * DF_G + g + 1) * DF_DV] = out[g * t_new:(g + 1) * t_new]


def _diff_sample(qd_s, dkv_s, cache, page_table, tbl_t, dl, sub, t_new, lam_init):
    db, n_pages = page_table.shape
    page = cache.shape[2]
    past = n_pages * page
    lk = past + LANE
    bias = _rel_pattern(tbl_t, NSA_H, DF_H, t_new, lk, past, 0, 1 << 30, False, c_lim=past + t_new)
    full = lambda shape: pl.BlockSpec(shape, lambda b, pt: (0,) * len(shape))
    page_specs = [pl.BlockSpec((None, None, page, DF_KV, 4 * HD), lambda b, pt, p=p: (0, pt[b, p], 0, 0, 0))
                  for p in range(n_pages)]
    in_specs = page_specs + [
        pl.BlockSpec((t_new, DF_H * 2 * HD), lambda b, pt: (b, 0)),
        pl.BlockSpec((t_new, DF_KV * 4 * HD), lambda b, pt: (b, 0)),
        full((DF_H, t_new, lk)), full((4, HD)), full((1, DF_DV)),
    ]
    return pl.pallas_call(
        functools.partial(_diff_sample_kernel, n_pages=n_pages, page=page, past=past, t_new=t_new,
                          lam_init=lam_init),
        grid_spec=pltpu.PrefetchScalarGridSpec(
            num_scalar_prefetch=1, grid=(db,), in_specs=in_specs,
            out_specs=pl.BlockSpec((t_new, DF_H * DF_DV), lambda b, pt: (b, 0)),
            scratch_shapes=[pltpu.VMEM((DF_KV, lk, 4 * HD), BF), pltpu.VMEM((DF_KV, page, 4 * HD), F32)]),
        out_shape=jax.ShapeDtypeStruct((db * t_new, DF_H * DF_DV), F32),
        compiler_params=_cparams(("arbitrary",)),
        name="diff_sample",
    )(page_table, *([cache] * n_pages), qd_s, dkv_s, bias, dl, sub.reshape(1, DF_DV))


def _router_kernel(h_ref, g_ref, wr_ref, br_ref, xn_ref, ids_ref, wts_ref):
    x = h_ref[...]
    ms = jnp.mean(x * x, axis=-1, keepdims=True)
    xn = x * lax.rsqrt(ms + EPS) * g_ref[...]
    xh = xn.astype(BF)
    xn_ref[...] = xh
    xl = (xn - xh.astype(F32)).astype(BF)
    wr = wr_ref[...]
    wh = wr.astype(BF)
    wl = (wr - wh.astype(F32)).astype(BF)
    lg = _dot(xh, wh) + _dot(xl, wh) + _dot(xh, wl) + br_ref[...]
    lane_i = lax.broadcasted_iota(jnp.int32, lg.shape, 1)
    lane = lane_i.astype(F32)
    big = 1000.0
    isg = lane_i < N_GROUPS
    gmax = jnp.max(jnp.where(isg, lg, -3e38), axis=-1, keepdims=True)
    gsel = jnp.min(jnp.where(isg & (lg == gmax), lane, big), axis=-1, keepdims=True)
    gw = 1.0 / jnp.sum(jnp.where(isg, jnp.exp(lg - gmax), 0.0), axis=-1, keepdims=True)
    lo = N_GROUPS + gsel * EPG
    ing = (lane >= lo) & (lane < lo + EPG)
    emax = jnp.max(jnp.where(ing, lg, -3e38), axis=-1, keepdims=True)
    pe = jnp.where(ing, jnp.exp(lg - emax), 0.0)
    pr = jnp.where(ing, pe / jnp.sum(pe, axis=-1, keepdims=True), -1.0)
    v1 = jnp.max(pr, axis=-1, keepdims=True)
    i1 = jnp.min(jnp.where(pr == v1, lane, big), axis=-1, keepdims=True)
    pr2 = jnp.where(lane == i1, -1.0, pr)
    v2 = jnp.max(pr2, axis=-1, keepdims=True)
    i2 = jnp.min(jnp.where(pr2 == v2, lane, big), axis=-1, keepdims=True)
    den = v1 + v2
    e12 = jnp.where(lane_i == 0, i1 - N_GROUPS, jnp.where(lane_i == 1, i2 - N_GROUPS, 0.0))
    ids_ref[...] = e12.astype(jnp.int32)
    wts_ref[...] = jnp.where(lane_i == 0, v1 / den * gw, jnp.where(lane_i == 1, v2 / den * gw, 0.0))


def _router(h, g, wr, br, tm):
    m, d = h.shape
    return pl.pallas_call(
        _router_kernel,
        grid=(m // tm,),
        in_specs=[pl.BlockSpec((tm, d), lambda i: (i, 0)), pl.BlockSpec((1, d), lambda i: (0, 0)),
                  pl.BlockSpec((d, LANE), lambda i: (0, 0)), pl.BlockSpec((1, LANE), lambda i: (0, 0))],
        out_specs=[pl.BlockSpec((tm, d), lambda i: (i, 0)), pl.BlockSpec((tm, LANE), lambda i: (i, 0)),
                   pl.BlockSpec((tm, LANE), lambda i: (i, 0))],
        out_shape=[jax.ShapeDtypeStruct((m, d), BF), jax.ShapeDtypeStruct((m, LANE), jnp.int32),
                   jax.ShapeDtypeStruct((m, LANE), F32)],
        compiler_params=_cparams(("arbitrary",)),
        name="router",
    )(h, g.reshape(1, d), wr, br)


def _moe_up_kernel(te_ref, nt_ref, x_ref, wg_ref, wu_ref, o_ref, a_s, u_s, *, nk):
    k = pl.program_id(1)

    @pl.when(pl.program_id(0) < nt_ref[0])
    def _():
        @pl.when(k == 0)
        def _():
            a_s[...] = jnp.zeros(a_s.shape, F32)
            u_s[...] = jnp.zeros(u_s.shape, F32)

        x = x_ref[...]
        a_s[...] += _dot(x, wg_ref[...].astype(BF))
        u_s[...] += _dot(x, wu_ref[...].astype(BF))

        @pl.when(k == nk - 1)
        def _():
            g = a_s[...]
            o_ref[...] = (g * jax.nn.sigmoid(g) * u_s[...]).astype(o_ref.dtype)


def _moe_down_kernel(te_ref, nt_ref, h_ref, wd_ref, o_ref):
    @pl.when(pl.program_id(0) < nt_ref[0])
    def _():
        o_ref[...] = _dot(h_ref[...], wd_ref[...].astype(BF))


def _moe(xn, ids, w_gate, w_up, w_down):
    m, d = xn.shape
    n_exp, _, ff = w_gate.shape
    tm = MOE_TM
    n_pairs = 2 * m
    n_tiles = n_exp + n_pairs // tm
    n_rows = n_tiles * tm
    flat_e = ids[:, :2].reshape(-1)
    onehot = (flat_e[:, None] == jnp.arange(n_exp, dtype=jnp.int32)[None, :]).astype(jnp.int32)
    counts = jnp.sum(onehot, axis=0)
    tiles_per = (counts + tm - 1) // tm
    tile_end = jnp.cumsum(tiles_per)
    pad_start = (tile_end - tiles_per) * tm
    start = jnp.cumsum(counts) - counts
    rank = jnp.sum(onehot * (jnp.cumsum(onehot, axis=0) - 1), axis=1)
    pos = (jnp.sum(onehot * pad_start[None, :], axis=1) + rank).reshape(m, 2)
    order = jnp.argsort(flat_e, stable=True).astype(jnp.int32)
    nt = tile_end[-1]
    tix = jnp.arange(n_tiles, dtype=jnp.int32)
    tile_e = jnp.minimum(jnp.searchsorted(tile_end, jnp.minimum(tix, nt - 1), side="right"),
                         n_exp - 1).astype(jnp.int32)
    row_e = jnp.repeat(tile_e, tm)
    r = jnp.arange(n_rows, dtype=jnp.int32)
    rr = r - pad_start[row_e]
    valid = (rr < counts[row_e]) & (r < nt * tm)
    src = jnp.where(valid, start[row_e] + rr, 0)
    row_tok = jnp.where(valid, order[src] // 2, r % m)
    xs = xn.at[row_tok].get(mode="promise_in_bounds")
    nt_arr = nt.reshape(1).astype(jnp.int32)

    tk = _pick(d, (1024, 512, 256, 128))
    nk = d // tk
    live = lambda t, ntr: t < ntr[0]
    clamp = lambda t, ntr: jnp.minimum(t, ntr[0] - 1)
    kidx = lambda t, k, ntr: jnp.where(live(t, ntr), k, nk - 1)
    wspec = pl.BlockSpec((None, tk, ff), lambda t, k, te, ntr: (te[t], kidx(t, k, ntr), 0))
    hdn = pl.pallas_call(
        functools.partial(_moe_up_kernel, nk=nk),
        grid_spec=pltpu.PrefetchScalarGridSpec(
            num_scalar_prefetch=2, grid=(n_tiles, nk),
            in_specs=[pl.BlockSpec((tm, tk), lambda t, k, te, ntr: (clamp(t, ntr), kidx(t, k, ntr))), wspec, wspec],
            out_specs=pl.BlockSpec((tm, ff), lambda t, k, te, ntr: (clamp(t, ntr), 0)),
            scratch_shapes=[pltpu.VMEM((tm, ff), F32), pltpu.VMEM((tm, ff), F32)]),
        out_shape=jax.ShapeDtypeStruct((n_rows, ff), BF),
        compiler_params=_cparams(("arbitrary", "arbitrary")),
        name="moe_up",
    )(tile_e, nt_arr, xs, w_gate, w_up)

    tn = _pick(d, (2048, 1024, 512, 256, 128))
    nn = d // tn
    y = pl.pallas_call(
        _moe_down_kernel,
        grid_spec=pltpu.PrefetchScalarGridSpec(
            num_scalar_prefetch=2, grid=(n_tiles, nn),
            in_specs=[
                pl.BlockSpec((tm, ff), lambda t, j, te, ntr: (clamp(t, ntr), 0)),
                pl.BlockSpec((None, ff, tn), lambda t, j, te, ntr: (te[t], 0, jnp.where(live(t, ntr), j, nn - 1))),
            ],
            out_specs=pl.BlockSpec((tm, tn), lambda t, j, te, ntr: (clamp(t, ntr),
                                                                    jnp.where(live(t, ntr), j, nn - 1)))),
        out_shape=jax.ShapeDtypeStruct((n_rows, d), F32),
        compiler_params=_cparams(("arbitrary", "arbitrary")),
        name="moe_down",
    )(tile_e, nt_arr, hdn, w_down)
    return y, pos


O_KV = NSA_H * HD
O_GATE = O_KV + 6 * NSA_KV * HD
O_QD = O_GATE + 3 * NSA_H
W_QD = DF_H * 2 * HD
W_KD = DF_KV * 2 * HD


def _gate_weights(w):
    d = w.shape[0]
    wg = w[:, O_GATE:O_QD].reshape(d, 3, NSA_KV, NSA_G).transpose(0, 2, 1, 3).reshape(d, NSA_KV, 3 * NSA_G)
    return jnp.pad(wg, ((0, 0), (0, 0), (0, LANE - 3 * NSA_G))).reshape(d, NSA_KV * LANE)


def kernel(x_prompt, x_sample, cache_nsa_kv, cache_diff_kv, state_nsa_win, page_table, p_prompt, p_sample,
           rel_bias_table, norm_mix, w_in, w_cmp, diff_lambda, diff_subln, w_out, norm_ffn, w_router_group,
           b_router_group, w_router_expert, b_router_expert, w_exp_gate, w_exp_up, w_exp_down, norm_ple,
           w_ple_gate, w_ple_proj, final_norm):
    assert norm_mix.shape[0] == 1, "single-layer trunk"
    batch, seq, d = x_prompt.shape
    db, t_new, _ = x_sample.shape
    mp, ms = batch * seq, db * t_new
    m = mp + ms
    lam_init = 0.8 - 0.6 * math.exp(-0.3 * 0)
    tm = _pick(math.gcd(mp, ms), (1024, 512, 256, 128))
    tm_s = _pick(math.gcd(mp, ms), (256, 128))
    xp = x_prompt.reshape(mp, d)
    xs = x_sample.reshape(ms, d)
    tbl_t = rel_bias_table

    xn = _rms2(xp, xs, norm_mix[0], tm_s)
    w0 = w_in[0]
    w_diff = w0[:, O_QD:]
    w_gate = _gate_weights(w0)
    scale = HD ** -0.5
    tn = 512
    wide = functools.partial(_matmul, [xn], tm=tm, tn=tn)
    q_nsa = wide([(w0, 0)], rows=m, n_cols=O_KV, out_dtype=BF, scale=scale, name="proj_qn")
    qd = wide([(w_diff, 0)], rows=m, n_cols=W_QD, out_dtype=BF, scale=scale, name="proj_qd")
    gates = _matmul([xn], [(w_gate, 0)], rows=m, n_cols=NSA_KV * LANE, tm=tm, tn=NSA_KV * LANE, out_dtype=F32,
                    name="proj_gate")
    kv_w = [(w0, 0, lambda j: j + O_KV // tn)]
    win_w = [(w0, 0, lambda j: j + (O_KV + 4 * NSA_KV * HD) // tn)]
    tn_d = DF_DV
    dkv_w = [(w_diff, 0, lambda j: jnp.where(j % 2 == 0, W_QD // tn_d + j // 2, (W_QD + W_KD) // tn_d + j // 2))]
    dkv = functools.partial(_matmul, [xn], dkv_w, tm=tm, tn=tn_d, n_cols=DF_KV * 4 * HD, out_dtype=F32)
    kv4_p = wide(kv_w, rows=mp, n_cols=4 * NSA_KV * HD, out_dtype=F32, name="proj_kv_p")
    kv4_s = wide(kv_w, rows=ms, row0=mp, n_cols=4 * NSA_KV * HD, out_dtype=F32, name="proj_kv_s")
    win_p = wide(win_w, rows=mp, n_cols=2 * NSA_KV * HD, out_dtype=F32, name="proj_win_p")
    win_s = wide(win_w, rows=ms, row0=mp, n_cols=2 * NSA_KV * HD, out_dtype=F32, name="proj_win_s")
    dkv_p = dkv(rows=mp, name="proj_dkv_p")
    dkv_s = dkv(rows=ms, row0=mp, name="proj_dkv_s")

    nc = seq // CMP
    kvr = kv4_p.reshape(batch, seq, 4, NSA_KV, HD)
    cmp_out = []
    for kind in range(2):
        a = kvr[:, :, kind].transpose(2, 0, 1, 3).reshape(NSA_KV * batch * nc, CMP * HD).astype(BF)
        r = a.shape[0]
        cmp_out.append(_matmul([a], [(w_cmp[0, kind].reshape(CMP * HD, HD), 0)], rows=r, n_cols=HD,
                               tm=_pick(r, (512, 256, 128, 64, 32, 16)), tn=HD, out_dtype=BF,
                               name="compress").reshape(NSA_KV, batch, nc, HD))
    o_n = _nsa_prompt(q_nsa, gates, cmp_out[0], cmp_out[1], kv4_p, win_p, tbl_t, batch, seq, m)
    o_d = _diff_prompt(qd, dkv_p, tbl_t, diff_lambda[0], diff_subln[0], batch, seq, m, lam_init)

    o_n_s = _nsa_sample(q_nsa[mp:].astype(F32), gates[mp:], kv4_s, win_s, cache_nsa_kv, state_nsa_win,
                        page_table, w_cmp[0], tbl_t, t_new)
    o_d_s = _diff_sample(qd[mp:].astype(F32), dkv_s, cache_diff_kv, page_table, tbl_t, diff_lambda[0],
                         diff_subln[0], t_new, lam_init)
    o_n = lax.dynamic_update_slice(o_n, o_n_s.astype(BF), (mp, 0))
    o_d = lax.dynamic_update_slice(o_d, o_d_s.astype(BF), (mp, 0))

    h1 = _matmul([o_n, o_d], [(w_out[0], 0), (w_out[0], 1)], rows=m, n_cols=d, tm=tm,
                 tn=_pick(d, (512, 256, 128)), out_dtype=F32, epi="res2", epi_args=(xp, xs), name="out_proj")

    wr = jnp.concatenate([w_router_group[0], w_router_expert[0],
                          jnp.zeros((d, LANE - N_GROUPS - N_EXP), F32)], axis=1)
    br = jnp.concatenate([b_router_group[0], b_router_expert[0],
                          jnp.zeros((LANE - N_GROUPS - N_EXP,), F32)]).reshape(1, LANE)
    xn2, ids, wts = _router(h1, norm_ffn[0], wr, br, tm_s)
    y, pos = _moe(xn2, ids, w_exp_gate[0], w_exp_up[0], w_exp_down[0])
    h2 = (h1 + wts[:, 0:1] * y.at[pos[:, 0]].get(mode="promise_in_bounds")
          + wts[:, 1:2] * y.at[pos[:, 1]].get(mode="promise_in_bounds"))

    xn3 = _rms(h2, norm_ple[0], tm_s, BF)
    p_all = jnp.concatenate([p_prompt[0].reshape(mp, -1), p_sample[0].reshape(ms, -1)], axis=0).astype(BF)
    h3 = _matmul([xn3], [(w_ple_gate[0], 0)], rows=m, n_cols=d, tm=tm, tn=_pick(d, (512, 256, 128)),
                 out_dtype=F32, epi="ple", epi_args=(h2, p_all, w_ple_proj[0]), name="ple")

    y_p = _rms(h3, final_norm, tm_s, F32, row0=0, rows=mp).reshape(batch, seq, d)
    y_s = _rms(h3, final_norm, tm_s, F32, row0=mp, rows=ms).reshape(db, t_new, d)
    wk = min(WIN, seq)
    win_p_out = win_p.reshape(batch, seq, 2, NSA_KV, HD)[:, seq - wk:]
    new_win = jnp.concatenate([state_nsa_win[0], win_s.reshape(db, t_new, 2, NSA_KV, HD)], axis=1)[:, t_new:]
    return (y_p, y_s,
            kv4_p.reshape(1, batch, seq, 4, NSA_KV, HD), kv4_s.reshape(1, db, t_new, 4, NSA_KV, HD),
            dkv_p.reshape(1, batch, seq, DF_KV, 4 * HD), dkv_s.reshape(1, db, t_new, DF_KV, 4 * HD),
            win_p_out[None], new_win[None])
```

```python
import functools
import math

import numpy as np
import jax
import jax.numpy as jnp
from jax import lax
from jax.experimental import pallas as pl
from jax.experimental.pallas import tpu as pltpu

BF = jnp.bfloat16
F32 = jnp.float32

HD = 128
NSA_H = 16
NSA_KV = 2
NSA_G = NSA_H // NSA_KV
CMP = 32
SELB = 64
TOPK = 16
WIN = 512
DF_H = 8
DF_KV = 4
DF_G = DF_H // DF_KV
DF_DV = 2 * HD
REL_BUCKETS = 32
REL_MAX_DIST = 128
N_GROUPS = 4
EPG = 8
N_EXP = N_GROUPS * EPG
EPS = 1e-6
NEG = -1e30
FORCE = 1e4
LANE = 128
VMEM_LIMIT = 56 * 1024 * 1024
MOE_TM = 640
MOE_DMA_SPLIT = 4

QB_NSA = 128
PAD_SEL = 384
PAD_WIN = WIN
NEAR_SEL = 512
BAND_WIN = WIN + QB_NSA
QB_DF = 256
PAD_DF = 256
NEAR_DF = 512
TK = 256
CMP_PITCH = 40


def _dot(a, b):
    return jnp.dot(a, b, preferred_element_type=F32)


def _dot_nt(a, b):
    return lax.dot_general(a, b, (((1,), (1,)), ((), ())), preferred_element_type=F32)


def _cparams(sem):
    return pltpu.CompilerParams(dimension_semantics=sem, vmem_limit_bytes=VMEM_LIMIT)


def _pick(n, cands):
    for c in cands:
        if n % c == 0:
            return c
    raise ValueError(f"no tile in {cands} divides {n}")


def _rms2_kernel(xp_ref, xs_ref, g_ref, o_ref, *, np_tiles):
    i = pl.program_id(0)

    def go(x_ref):
        x = x_ref[...]
        ms = jnp.mean(x * x, axis=-1, keepdims=True)
        o_ref[...] = (x * lax.rsqrt(ms + EPS) * g_ref[...]).astype(o_ref.dtype)

    @pl.when(i < np_tiles)
    def _():
        go(xp_ref)

    @pl.when(i >= np_tiles)
    def _():
        go(xs_ref)


def _rms2(xp, xs, g, tm):
    mp, d = xp.shape
    ms = xs.shape[0]
    npt, nst = mp // tm, ms // tm
    return pl.pallas_call(
        functools.partial(_rms2_kernel, np_tiles=npt),
        grid=(npt + nst,),
        in_specs=[pl.BlockSpec((tm, d), lambda i: (jnp.minimum(i, npt - 1), 0)),
                  pl.BlockSpec((tm, d), lambda i: (jnp.maximum(i - npt, 0), 0)),
                  pl.BlockSpec((1, d), lambda i: (0, 0))],
        out_specs=pl.BlockSpec((tm, d), lambda i: (i, 0)),
        out_shape=jax.ShapeDtypeStruct((mp + ms, d), BF),
        compiler_params=_cparams(("arbitrary",)),
        name="rms2",
    )(xp, xs, g.reshape(1, d))


def _rms_kernel(x_ref, g_ref, o_ref):
    x = x_ref[...]
    ms = jnp.mean(x * x, axis=-1, keepdims=True)
    o_ref[...] = (x * lax.rsqrt(ms + EPS) * g_ref[...]).astype(o_ref.dtype)


def _rms(x, g, tm, out_dtype, row0=0, rows=None):
    m, d = x.shape
    rows = m if rows is None else rows
    t0 = row0 // tm
    return pl.pallas_call(
        _rms_kernel,
        grid=(rows // tm,),
        in_specs=[pl.BlockSpec((tm, d), lambda i: (i + t0, 0)),
                  pl.BlockSpec((1, d), lambda i: (0, 0))],
        out_specs=pl.BlockSpec((tm, d), lambda i: (i, 0)),
        out_shape=jax.ShapeDtypeStruct((rows, d), out_dtype),
        compiler_params=_cparams(("arbitrary",)),
        name="rms",
    )(x, g.reshape(1, d))


def _cast_rows(src_ref, dst_ref):
    k = src_ref.shape[0]
    ch = 256 if k % 256 == 0 else k

    def body(c, carry):
        r = pl.multiple_of(c * ch, ch)
        dst_ref[pl.ds(r, ch), :] = src_ref[pl.ds(r, ch), :].astype(BF)
        return carry

    lax.fori_loop(0, k // ch, body, 0)


def _mm_kernel(*refs, n_a, cast, epi, scale, np_tiles):
    a = refs[:n_a]
    w = refs[n_a:2 * n_a]
    idx = 2 * n_a
    if epi == "res2":
        rp_ref, rs_ref = refs[idx:idx + 2]
        idx += 2
    elif epi == "ple":
        h_ref, p_ref, wp_ref = refs[idx:idx + 3]
        idx += 3
    o_ref = refs[idx]
    idx += 1
    wb = refs[idx:idx + n_a] if cast else w
    i = pl.program_id(1)

    if cast:
        @pl.when(i == 0)
        def _():
            for k in range(n_a):
                _cast_rows(w[k], wb[k])

    acc = _dot(a[0][...], wb[0][...])
    for k in range(1, n_a):
        acc = acc + _dot(a[k][...], wb[k][...])
    if scale is not None:
        acc = acc * scale
    if epi is None:
        o_ref[...] = acc.astype(o_ref.dtype)
    elif epi == "res2":
        @pl.when(i < np_tiles)
        def _():
            o_ref[...] = (acc + rp_ref[...]).astype(o_ref.dtype)

        @pl.when(i >= np_tiles)
        def _():
            o_ref[...] = (acc + rs_ref[...]).astype(o_ref.dtype)
    elif epi == "ple":
        gate = jax.nn.sigmoid(acc)
        proj = _dot(p_ref[...], wp_ref[...].astype(BF))
        o_ref[...] = (h_ref[...] + gate * proj).astype(o_ref.dtype)


def _matmul(a_list, w_list, *, rows, n_cols, tm, tn, out_dtype, row0=0, col0=0,
            scale=None, epi=None, epi_args=(), name="mm"):
    n_a = len(a_list)
    cast = w_list[0][0].dtype != BF
    t0, c0 = row0 // tm, col0 // tn
    gm, gn = rows // tm, n_cols // tn
    in_specs, args = [], []
    for a in a_list:
        in_specs.append(pl.BlockSpec((tm, a.shape[1]), lambda j, i: (i + t0, 0)))
        args.append(a)
    for ent, a in zip(w_list, a_list):
        w, kb = ent[0], ent[1]
        colfn = ent[2] if len(ent) > 2 else (lambda j: j + c0)
        in_specs.append(pl.BlockSpec((a.shape[1], tn), lambda j, i, kb=kb, colfn=colfn: (kb, colfn(j))))
        args.append(w)
    np_tiles = 0
    if epi == "res2":
        xp, xs = epi_args
        np_tiles = xp.shape[0] // tm
        in_specs.append(pl.BlockSpec((tm, tn), lambda j, i: (jnp.minimum(i, np_tiles - 1), j)))
        in_specs.append(pl.BlockSpec((tm, tn), lambda j, i: (jnp.maximum(i - np_tiles, 0), j)))
        args += [xp, xs]
    elif epi == "ple":
        h, p, wp = epi_args
        in_specs.append(pl.BlockSpec((tm, tn), lambda j, i: (i, j)))
        in_specs.append(pl.BlockSpec((tm, p.shape[1]), lambda j, i: (i, 0)))
        in_specs.append(pl.BlockSpec((wp.shape[0], tn), lambda j, i: (0, j)))
        args += [h, p, wp]
    scratch = [pltpu.VMEM((a.shape[1], tn), BF) for a in a_list] if cast else []
    return pl.pallas_call(
        functools.partial(_mm_kernel, n_a=n_a, cast=cast, epi=epi, scale=scale, np_tiles=np_tiles),
        grid=(gn, gm),
        in_specs=in_specs,
        out_specs=pl.BlockSpec((tm, tn), lambda j, i: (i, j)),
        out_shape=jax.ShapeDtypeStruct((rows, n_cols), out_dtype),
        scratch_shapes=scratch,
        compiler_params=_cparams(("arbitrary", "arbitrary")),
        name=name,
    )(*args)


def _bucket_np(dist):
    n = np.maximum(dist, 0)
    max_exact = REL_BUCKETS // 2
    nf = np.maximum(n, 1).astype(np.float32)
    log_b = max_exact + (np.log(nf / np.float32(max_exact)) / np.float32(math.log(REL_MAX_DIST / max_exact))
                         * np.float32(REL_BUCKETS - max_exact)).astype(np.int32)
    return np.where(n < max_exact, n, np.minimum(log_b, REL_BUCKETS - 1)).astype(np.int32)


def _bucket_edges():
    b = _bucket_np(np.arange(0, 4 * REL_MAX_DIST))
    return [(k, int(np.nonzero(b == k)[0].max())) for k in range(REL_BUCKETS - 1) if (b == k).any()]


def _pattern_kernel(tbl_ref, o_ref, *, h0, base, col_step, v_lo, v_hi, c_lim, sub_far, edges):
    h = pl.program_id(0) + h0
    shape = o_ref.shape
    row = lax.broadcasted_iota(jnp.int32, shape, 0) + pl.program_id(1) * shape[0]
    col = lax.broadcasted_iota(jnp.int32, shape, 1)
    dist = row + base - col_step * col
    far = tbl_ref[REL_BUCKETS - 1, h]
    b = jnp.full(shape, far, F32)
    for k, hi in reversed(edges):
        b = jnp.where(dist <= hi, tbl_ref[k, h], b)
    if sub_far:
        b = b - far
    valid = (dist >= v_lo) & (dist <= v_hi) & (col < c_lim)
    o_ref[...] = jnp.where(valid, b, NEG)


def _rel_pattern(tbl, h0, nh, nrows, ncols, base, v_lo, v_hi, sub_far, col_step=1, c_lim=None):
    tr = _pick(nrows, (512, 256, 128, 64, 32, 16, 8))
    return pl.pallas_call(
        functools.partial(_pattern_kernel, h0=h0, base=base, col_step=col_step, v_lo=v_lo, v_hi=v_hi,
                          c_lim=ncols if c_lim is None else c_lim, sub_far=sub_far, edges=_bucket_edges()),
        grid=(nh, nrows // tr),
        in_specs=[pl.BlockSpec(memory_space=pltpu.SMEM)],
        out_specs=pl.BlockSpec((None, tr, ncols), lambda h, r: (h, r, 0)),
        out_shape=jax.ShapeDtypeStruct((nh, nrows, ncols), F32),
        compiler_params=_cparams(("arbitrary", "arbitrary")),
        name="rel_bias",
    )(tbl)


def _pair_sum_matrix(nc):
    n = np.arange(nc)[:, None]
    b = np.arange(LANE)[None, :]
    return jnp.asarray((n // (SELB // CMP) == b).astype(np.float32), dtype=BF)


def _expand_matrix(pad, n_keys):
    l = np.arange(LANE)[:, None]
    c = np.arange(pad + n_keys)[None, :]
    return jnp.asarray(((c >= pad) & ((c - pad) // SELB == l)).astype(np.float32), dtype=BF)


def _split3(x):
    hi = x.astype(BF)
    r = x - hi.astype(F32)
    mid = r.astype(BF)
    lo = (r - mid.astype(F32)).astype(BF)
    return hi, mid, lo


def _select_blocks(psum, s_mat, qpos, ns):
    hi, mid, lo = _split3(psum)
    imp = _dot(hi, s_mat) + _dot(mid, s_mat) + _dot(lo, s_mat)
    shape = imp.shape
    lane = lax.broadcasted_iota(jnp.int32, shape, 1)
    valid = lane * SELB <= qpos
    cur = jnp.right_shift(qpos, 6)
    forced = (lane == 0) | (lane == cur) | (lane == cur - 1)
    score = jnp.where(valid, imp + jnp.where(forced, FORCE, 0.0), NEG)
    score = jnp.where(lane < ns, score, -3e38)
    cnt = jnp.zeros(shape, F32)
    for i in range(ns):
        ci = score[:, i:i + 1]
        cnt = cnt + jnp.where(lane > i, jnp.where(ci >= score, 1.0, 0.0), jnp.where(ci > score, 1.0, 0.0))
    sel = (cnt < float(min(TOPK, ns))) & (lane < ns)
    return jnp.where(sel, 1.0, 0.0).astype(BF)


def _softmax_rows(s, valid=None):
    m = jnp.max(s, axis=-1, keepdims=True)
    p = jnp.exp(s - m)
    if valid is not None:
        p = jnp.where(valid, p, 0.0)
    l = jnp.sum(p, axis=-1, keepdims=True)
    return p / jnp.where(l > 0.0, l, 1.0)


def _lanes(x, n):
    return x if n == LANE else jnp.concatenate([x] * (n // LANE), axis=1)


def _online(carry, s, vt, ones_in_v=False):
    m, l, acc = carry
    dv = acc.shape[1]
    m_new = jnp.maximum(m, jnp.max(s, axis=-1, keepdims=True))
    p = jnp.exp(s - _lanes(m_new, s.shape[1]))
    alpha = jnp.exp(m - m_new)
    pv = _dot(p.astype(BF), vt)
    if ones_in_v:
        l = alpha * l + pv[:, dv:dv + LANE]
        pv = pv[:, 0:dv]
    else:
        l = alpha * l + jnp.sum(p, axis=-1, keepdims=True)
    return m_new, l, _lanes(alpha, dv) * acc + pv


def _online_init(rows, dv):
    return jnp.full((rows, LANE), NEG, F32), jnp.zeros((rows, LANE), F32), jnp.zeros((rows, dv), F32)


def _flash_step(q, kt, vt, bias, m_ref, l_ref, acc_ref, rows, ones_in_v=False):
    s = _dot_nt(q, kt)
    if bias is not None:
        s = s + bias
    m_new, l, acc = _online((m_ref[rows, :], l_ref[rows, :], acc_ref[rows, :]), s, vt, ones_in_v)
    m_ref[rows, :] = m_new
    l_ref[rows, :] = l
    acc_ref[rows, :] = acc


def _nsa_prompt_kernel(q_ref, gate_ref, kc_ref, vc_ref, ks_ref, vs_ref, kw_ref, vw_ref,
                       bc_ref, pn_ref, pw_ref, e_ref, s_ref, o_ref,
                       ks_s, vs_s, kw_s, vw_s, m_s, l_s, acc_s, o_s, *, seq, nc, ns):
    i = pl.program_id(2)
    s0 = i * QB_NSA
    rows = NSA_G * QB_NSA

    @pl.when(i == 0)
    def _():
        ks_s[0:PAD_SEL, :] = jnp.zeros((PAD_SEL, HD), BF)
        vs_s[0:PAD_SEL, :] = jnp.zeros((PAD_SEL, 2 * HD), BF)
        kw_s[0:PAD_WIN, :] = jnp.zeros((PAD_WIN, HD), BF)
        vw_s[0:PAD_WIN, :] = jnp.zeros((PAD_WIN, 2 * HD), BF)
        ch = 512
        ones = jnp.ones((ch, HD), BF)

        def cp(c, carry):
            r = pl.multiple_of(c * ch, ch)
            ks_s[pl.ds(PAD_SEL + r, ch), :] = ks_ref[pl.ds(r, ch), :].astype(BF)
            vs_s[pl.ds(PAD_SEL + r, ch), 0:HD] = vs_ref[pl.ds(r, ch), :].astype(BF)
            vs_s[pl.ds(PAD_SEL + r, ch), HD:2 * HD] = ones
            kw_s[pl.ds(PAD_WIN + r, ch), :] = kw_ref[pl.ds(r, ch), :].astype(BF)
            vw_s[pl.ds(PAD_WIN + r, ch), 0:HD] = vw_ref[pl.ds(r, ch), :].astype(BF)
            vw_s[pl.ds(PAD_WIN + r, ch), HD:2 * HD] = ones
            return carry

        lax.fori_loop(0, seq // ch, cp, 0)

    gt = jax.nn.sigmoid(gate_ref[...])
    head = lambda g: slice(g * HD, (g + 1) * HD)
    hrows = lambda g: slice(g * QB_NSA, (g + 1) * QB_NSA)

    kc = kc_ref[...]
    vc = vc_ref[...]
    psum = jnp.zeros((QB_NSA, nc), F32)
    for g in range(NSA_G):
        bc = bc_ref[g]
        pc = _softmax_rows(_dot_nt(q_ref[:, head(g)], kc) + bc, bc > 0.5 * NEG)
        psum = psum + pc
        o_s[:, head(g)] = gt[:, g:g + 1] * _dot(pc.astype(BF), vc)
    qpos = s0 + lax.broadcasted_iota(jnp.int32, (QB_NSA, LANE), 0)
    selb = _select_blocks(psum, s_ref[...], qpos, ns)

    nch = jnp.maximum(i - 1, 0) // 2
    far_keys = nch * TK
    m_s[...] = jnp.full((rows, LANE), NEG, F32)
    l_s[...] = jnp.zeros((rows, LANE), F32)
    acc_s[...] = jnp.zeros((rows, HD), F32)

    def sel_chunk(r, bias_of):
        kt = ks_s[pl.ds(r, TK), :]
        vt = vs_s[pl.ds(r, TK), :]
        madd = (_dot(selb, e_ref[:, pl.ds(r, TK)]) - 1.0) * (-NEG)
        for g in range(NSA_G):
            _flash_step(q_ref[:, head(g)], kt, vt, bias_of(g, madd), m_s, l_s, acc_s, hrows(g), ones_in_v=True)

    def far(c, carry):
        sel_chunk(pl.multiple_of(PAD_SEL + c * TK, LANE), lambda g, madd: madd)
        return carry

    lax.fori_loop(0, nch, far, 0)
    for kh in range(NEAR_SEL // TK):
        col = lax.broadcasted_iota(jnp.int32, (QB_NSA, TK), 1) + kh * TK
        cut = jnp.where(col < far_keys - s0 + PAD_SEL, NEG, 0.0)
        sel_chunk(pl.multiple_of(s0 + kh * TK, LANE),
                  lambda g, madd, kh=kh, cut=cut: pn_ref[g, :, kh * TK:(kh + 1) * TK] + madd + cut)
    for g in range(NSA_G):
        osel = acc_s[hrows(g), :] / l_s[hrows(g), :]
        o_s[:, head(g)] = o_s[:, head(g)] + gt[:, NSA_G + g:NSA_G + g + 1] * osel

    for g in range(NSA_G):
        st = _online_init(QB_NSA, HD)
        for c0 in range(0, BAND_WIN, TK):
            w = min(TK, BAND_WIN - c0)
            r = pl.multiple_of(s0 + c0, LANE)
            colw = lax.broadcasted_iota(jnp.int32, (QB_NSA, w), 1) + c0
            bias = pw_ref[g, :, c0:c0 + w] + jnp.where(colw < PAD_WIN - s0, NEG, 0.0)
            st = _online(st, _dot_nt(q_ref[:, head(g)], kw_s[pl.ds(r, w), :]) + bias, vw_s[pl.ds(r, w), :],
                         ones_in_v=True)
        ow = st[2] / st[1]
        o_ref[:, head(g)] = (o_s[:, head(g)] + gt[:, 2 * NSA_G + g:2 * NSA_G + g + 1] * ow).astype(o_ref.dtype)


def _nsa_prompt(q_nsa, gates, kc, vc, kv4_p, win_p, tbl_t, batch, seq, m_total):
    nc = seq // CMP
    ns = -(-seq // SELB)
    nqb = seq // QB_NSA
    big = 1 << 30
    pn = _rel_pattern(tbl_t, 0, NSA_H, QB_NSA, NEAR_SEL, PAD_SEL, 0, big, True)
    pw = _rel_pattern(tbl_t, 0, NSA_H, QB_NSA, BAND_WIN, PAD_WIN, 0, WIN, False)
    bc = _rel_pattern(tbl_t, 0, NSA_H, seq, nc, -(CMP - 1), 0, big, False, col_step=CMP)
    e_mat = _expand_matrix(PAD_SEL, seq)
    s_mat = _pair_sum_matrix(nc)
    kv_spec = lambda col: pl.BlockSpec((seq, HD), lambda b, h, i, col=col: (b, col + h))
    return pl.pallas_call(
        functools.partial(_nsa_prompt_kernel, seq=seq, nc=nc, ns=ns),
        grid=(batch, NSA_KV, nqb),
        in_specs=[
            pl.BlockSpec((QB_NSA, NSA_G * HD), lambda b, h, i: (b * nqb + i, h)),
            pl.BlockSpec((QB_NSA, LANE), lambda b, h, i: (b * nqb + i, h)),
            pl.BlockSpec((None, None, nc, HD), lambda b, h, i: (h, b, 0, 0)),
            pl.BlockSpec((None, None, nc, HD), lambda b, h, i: (h, b, 0, 0)),
            kv_spec(2 * NSA_KV), kv_spec(3 * NSA_KV),
            pl.BlockSpec((seq, HD), lambda b, h, i: (b, h)),
            pl.BlockSpec((seq, HD), lambda b, h, i: (b, NSA_KV + h)),
            pl.BlockSpec((NSA_G, QB_NSA, nc), lambda b, h, i: (h, i, 0)),
            pl.BlockSpec((NSA_G, QB_NSA, NEAR_SEL), lambda b, h, i: (h, 0, 0)),
            pl.BlockSpec((NSA_G, QB_NSA, BAND_WIN), lambda b, h, i: (h, 0, 0)),
            pl.BlockSpec((LANE, PAD_SEL + seq), lambda b, h, i: (0, 0)),
            pl.BlockSpec((nc, LANE), lambda b, h, i: (0, 0)),
        ],
        out_specs=pl.BlockSpec((QB_NSA, NSA_G * HD), lambda b, h, i: (b * nqb + i, h)),
        out_shape=jax.ShapeDtypeStruct((m_total, NSA_H * HD), BF),
        scratch_shapes=[pltpu.VMEM((PAD_SEL + seq, HD), BF), pltpu.VMEM((PAD_SEL + seq, 2 * HD), BF),
                        pltpu.VMEM((PAD_WIN + seq, HD), BF), pltpu.VMEM((PAD_WIN + seq, 2 * HD), BF),
                        pltpu.VMEM((NSA_G * QB_NSA, LANE), F32), pltpu.VMEM((NSA_G * QB_NSA, LANE), F32),
                        pltpu.VMEM((NSA_G * QB_NSA, HD), F32), pltpu.VMEM((QB_NSA, NSA_G * HD), F32)],
        compiler_params=_cparams(("arbitrary", "arbitrary", "arbitrary")),
        name="nsa_prompt",
    )(q_nsa, gates, kc, vc, kv4_p, kv4_p, win_p, win_p, bc, pn, pw, e_mat, s_mat)


def _diff_lambda(dl, lam_init):
    a = jnp.sum(dl[0:1] * dl[1:2], axis=-1, keepdims=True)
    b = jnp.sum(dl[2:3] * dl[3:4], axis=-1, keepdims=True)
    return jnp.exp(a) - jnp.exp(b) + lam_init


def _diff_finish(a, sub, lam_init):
    ms = jnp.mean(a * a, axis=-1, keepdims=True)
    return a * lax.rsqrt(ms + EPS) * sub * (1.0 - lam_init)


def _diff_prompt_kernel(q_ref, kv_ref, pn_ref, dl_ref, sub_ref, o_ref, kv_s, m_s, l_s, acc_s, *, seq, lam_init):
    i = pl.program_id(2)
    s0 = i * QB_DF
    rows = DF_G * QB_DF

    @pl.when(i == 0)
    def _():
        kv_s[0:PAD_DF, :] = jnp.zeros((PAD_DF, 4 * HD), BF)
        ch = 256

        def cp(c, carry):
            r = pl.multiple_of(c * ch, ch)
            kv_s[pl.ds(PAD_DF + r, ch), :] = kv_ref[pl.ds(r, ch), :].astype(BF)
            return carry

        lax.fori_loop(0, seq // ch, cp, 0)

    lam = _diff_lambda(dl_ref[...], lam_init)
    nfar = jnp.maximum(i - 1, 0)
    sub_rows = 128
    n_sub = QB_DF // sub_rows
    streams = [(m, g, j) for m in range(2) for g in range(DF_G) for j in range(n_sub)]
    srows = lambda k: slice(k * sub_rows, (k + 1) * sub_rows)
    m_s[...] = jnp.full((2 * rows, LANE), NEG, F32)
    l_s[...] = jnp.zeros((2 * rows, LANE), F32)
    acc_s[...] = jnp.zeros((2 * rows, DF_DV), F32)

    def chunk(r, bias_of):
        vt = kv_s[pl.ds(r, TK), 2 * HD:4 * HD]
        for k, (m, g, j) in enumerate(streams):
            q = q_ref[j * sub_rows:(j + 1) * sub_rows, (g * 2 + m) * HD:(g * 2 + m + 1) * HD]
            kt = kv_s[pl.ds(r, TK), m * HD:(m + 1) * HD]
            _flash_step(q, kt, vt, bias_of(g, j), m_s, l_s, acc_s, srows(k))

    def far(c, carry):
        chunk(pl.multiple_of(PAD_DF + c * TK, TK), lambda g, j: None)
        return carry

    lax.fori_loop(0, nfar, far, 0)
    for kh in range(NEAR_DF // TK):
        col = lax.broadcasted_iota(jnp.int32, (sub_rows, TK), 1) + kh * TK
        cut = jnp.where(col < nfar * TK - s0 + PAD_DF, NEG, 0.0)
        chunk(pl.multiple_of(s0 + kh * TK, TK),
              lambda g, j, kh=kh, cut=cut: pn_ref[g, j * sub_rows:(j + 1) * sub_rows, kh * TK:(kh + 1) * TK] + cut)
    half = len(streams) // 2
    for k, (_, g, j) in enumerate(streams[:half]):
        o1 = acc_s[srows(k), :] / _lanes(l_s[srows(k), :], DF_DV)
        o2 = acc_s[srows(half + k), :] / _lanes(l_s[srows(half + k), :], DF_DV)
        out = _diff_finish(o1 - lam * o2, sub_ref[...], lam_init)
        o_ref[j * sub_rows:(j + 1) * sub_rows, g * DF_DV:(g + 1) * DF_DV] = out.astype(o_ref.dtype)


def _diff_prompt(qd, dkv_p, tbl_t, dl, sub, batch, seq, m_total, lam_init):
    nqb = seq // QB_DF
    pn = _rel_pattern(tbl_t, NSA_H, DF_H, QB_DF, NEAR_DF, PAD_DF, 0, 1 << 30, True)
    width = DF_G * 2 * HD
    return pl.pallas_call(
        functools.partial(_diff_prompt_kernel, seq=seq, lam_init=lam_init),
        grid=(batch, DF_KV, nqb),
        in_specs=[
            pl.BlockSpec((QB_DF, width), lambda b, h, i: (b * nqb + i, h)),
            pl.BlockSpec((seq, 4 * HD), lambda b, h, i: (b, h)),
            pl.BlockSpec((DF_G, QB_DF, NEAR_DF), lambda b, h, i: (h, 0, 0)),
            pl.BlockSpec((4, HD), lambda b, h, i: (0, 0)),
            pl.BlockSpec((1, DF_DV), lambda b, h, i: (0, 0)),
        ],
        out_specs=pl.BlockSpec((QB_DF, DF_G * DF_DV), lambda b, h, i: (b * nqb + i, h)),
        out_shape=jax.ShapeDtypeStruct((m_total, DF_H * DF_DV), BF),
        scratch_shapes=[pltpu.VMEM((PAD_DF + seq, 4 * HD), BF),
                        pltpu.VMEM((2 * DF_G * QB_DF, LANE), F32), pltpu.VMEM((2 * DF_G * QB_DF, LANE), F32),
                        pltpu.VMEM((2 * DF_G * QB_DF, DF_DV), F32)],
        compiler_params=_cparams(("arbitrary", "arbitrary", "arbitrary")),
        name="diff_prompt",
    )(qd, dkv_p, pn, dl, sub.reshape(1, DF_DV))


def _tail_tile(new, width):
    t = new.shape[0]
    return jnp.concatenate([new, jnp.zeros((LANE - t, width), F32)], axis=0).astype(BF)


def _nsa_sample_kernel(pt_ref, *refs, n_pages, page, past, t_new, ncs, ns, wb):
    pages = refs[:n_pages]
    (q_ref, gate_ref, kvn_ref, wn_ref, st_ref, wc_ref, bc_ref, bs_ref, bw_ref, e_ref, s_ref,
     o_ref, kcmp_s, ksel_s, kwin_s) = refs[n_pages:]
    del pt_ref
    rows = NSA_G * t_new
    n_kinds = 4

    for p in range(n_pages):
        for kind in range(n_kinds):
            for h in range(NSA_KV):
                blk = pages[p][pl.ds(kind * NSA_KV + h, page, stride=n_kinds * NSA_KV), :]
                if kind < 2:
                    for nb in range(page // CMP):
                        r0 = (p * (page // CMP) + nb) * CMP_PITCH
                        kcmp_s[kind, h, r0:r0 + CMP, :] = blk[nb * CMP:(nb + 1) * CMP]
                else:
                    ksel_s[kind - 2, h, p * page:(p + 1) * page, :] = blk.astype(BF)
    kvn = kvn_ref[...]
    wn = wn_ref[...]
    for kind in range(2):
        for h in range(NSA_KV):
            c0 = ((kind + 2) * NSA_KV + h) * HD
            ksel_s[kind, h, past:past + LANE, :] = _tail_tile(kvn[:, c0:c0 + HD], HD)
            kwin_s[kind, h, 0:wb, :] = st_ref[pl.ds(kind * NSA_KV + h, wb, stride=2 * NSA_KV), :].astype(BF)
            c0 = (kind * NSA_KV + h) * HD
            kwin_s[kind, h, wb:wb + LANE, :] = _tail_tile(wn[:, c0:c0 + HD], HD)

    q = q_ref[...]
    gt = jax.nn.sigmoid(gate_ref[...])
    qpos = past + lax.broadcasted_iota(jnp.int32, (t_new, LANE), 0)
    for h in range(NSA_KV):
        cmp = []
        for kind in range(2):
            acc = jnp.zeros((ncs, HD), F32)
            for j in range(CMP):
                kj = kcmp_s[kind, h, pl.ds(j, ncs, stride=CMP_PITCH), :]
                acc = acc + _dot(kj.astype(BF), wc_ref[kind, j * HD:(j + 1) * HD, :])
            cmp.append(acc.astype(BF))
        kc, vc = cmp
        qs = jnp.concatenate([q[:, (h * NSA_G + g) * HD:(h * NSA_G + g + 1) * HD] for g in range(NSA_G)],
                             axis=0).astype(BF)
        bc = bc_ref[h * NSA_G:(h + 1) * NSA_G].reshape(rows, ncs)
        pc = _softmax_rows(_dot_nt(qs, kc) + bc, bc > 0.5 * NEG)
        oc = _dot(pc.astype(BF), vc)
        psum = pc[0:t_new]
        for g in range(1, NSA_G):
            psum = psum + pc[g * t_new:(g + 1) * t_new]
        selb = _select_blocks(psum, s_ref[...], qpos, ns)
        lk = past + LANE
        madd = (_dot(selb, e_ref[...]) - 1.0) * (-NEG)
        s = _dot_nt(qs, ksel_s[0, h])
        s = (s.reshape(NSA_G, t_new, lk) + bs_ref[h * NSA_G:(h + 1) * NSA_G] + madd[None]).reshape(rows, lk)
        osel = _dot(_softmax_rows(s).astype(BF), ksel_s[1, h])
        lw = wb + LANE
        s = _dot_nt(qs, kwin_s[0, h]) + bw_ref[h * NSA_G:(h + 1) * NSA_G].reshape(rows, lw)
        ow = _dot(_softmax_rows(s).astype(BF), kwin_s[1, h])
        for g in range(NSA_G):
            sl = slice(g * t_new, (g + 1) * t_new)
            gl = h * LANE + g
            o = (gt[:, gl:gl + 1] * oc[sl] + gt[:, gl + NSA_G:gl + NSA_G + 1] * osel[sl]
                 + gt[:, gl + 2 * NSA_G:gl + 2 * NSA_G + 1] * ow[sl])
            o_ref[:, (h * NSA_G + g) * HD:(h * NSA_G + g + 1) * HD] = o


def _nsa_sample(q_s, gates_s, kv4_s, win_s, cache, state, page_table, w_cmp, tbl_t, t_new):
    db, n_pages = page_table.shape
    n_phys, page = cache.shape[1], cache.shape[2]
    past = n_pages * page
    wb = state.shape[2]
    assert (past + t_new) // CMP * CMP <= past and past % SELB == 0 and wb == min(WIN, past)
    ncs = (past + t_new) // CMP
    ns = -(-(past + t_new) // SELB)
    lk, lw = past + LANE, wb + LANE
    rows_pp = page * 4 * NSA_KV
    cache2 = cache.reshape(cache.shape[0], n_phys, rows_pp, HD)
    state2 = state.reshape(state.shape[0], db, wb * 2 * NSA_KV, HD)
    wc = w_cmp.reshape(2, CMP * HD, HD).astype(BF)
    big = 1 << 30
    bc = _rel_pattern(tbl_t, 0, NSA_H, t_new, ncs, past - (CMP - 1), 0, big, False, col_step=CMP)
    bs = _rel_pattern(tbl_t, 0, NSA_H, t_new, lk, past, 0, big, False, c_lim=past + t_new)
    bw = _rel_pattern(tbl_t, 0, NSA_H, t_new, lw, wb, 0, WIN, False, c_lim=wb + t_new)
    e_mat = _expand_matrix(0, lk)
    s_mat = _pair_sum_matrix(ncs)
    full = lambda shape: pl.BlockSpec(shape, lambda b, pt: (0,) * len(shape))
    page_specs = [pl.BlockSpec((None, None, rows_pp, HD), lambda b, pt, p=p: (0, pt[b, p], 0, 0))
                  for p in range(n_pages)]
    in_specs = page_specs + [
        pl.BlockSpec((t_new, NSA_H * HD), lambda b, pt: (b, 0)),
        pl.BlockSpec((t_new, NSA_KV * LANE), lambda b, pt: (b, 0)),
        pl.BlockSpec((t_new, 4 * NSA_KV * HD), lambda b, pt: (b, 0)),
        pl.BlockSpec((t_new, 2 * NSA_KV * HD), lambda b, pt: (b, 0)),
        pl.BlockSpec((None, None, wb * 2 * NSA_KV, HD), lambda b, pt: (0, b, 0, 0)),
        full((2, CMP * HD, HD)), full((NSA_H, t_new, ncs)), full((NSA_H, t_new, lk)),
        full((NSA_H, t_new, lw)), full((LANE, lk)), full((ncs, LANE)),
    ]
    return pl.pallas_call(
        functools.partial(_nsa_sample_kernel, n_pages=n_pages, page=page, past=past, t_new=t_new,
                          ncs=ncs, ns=ns, wb=wb),
        grid_spec=pltpu.PrefetchScalarGridSpec(
            num_scalar_prefetch=1, grid=(db,), in_specs=in_specs,
            out_specs=pl.BlockSpec((t_new, NSA_H * HD), lambda b, pt: (b, 0)),
            scratch_shapes=[pltpu.VMEM((2, NSA_KV, ncs * CMP_PITCH, HD), F32),
                            pltpu.VMEM((2, NSA_KV, lk, HD), BF),
                            pltpu.VMEM((2, NSA_KV, lw, HD), BF)]),
        out_shape=jax.ShapeDtypeStruct((db * t_new, NSA_H * HD), F32),
        compiler_params=_cparams(("arbitrary",)),
        name="nsa_sample",
    )(page_table, *([cache2] * n_pages), q_s, gates_s, kv4_s, win_s, state2, wc, bc, bs, bw, e_mat, s_mat)


def _diff_sample_kernel(pt_ref, *refs, n_pages, page, past, t_new, lam_init):
    pages = refs[:n_pages]
    q_ref, kvn_ref, b_ref, dl_ref, sub_ref, o_ref, kv_s, stage_s = refs[n_pages:]
    del pt_ref
    rows = DF_G * t_new
    lk = past + LANE
    for p in range(n_pages):
        for h in range(DF_KV):
            stage_s[h] = pages[p][:, h, :]
            kv_s[h, p * page:(p + 1) * page, :] = stage_s[h].astype(BF)
    kvn = kvn_ref[...]
    for h in range(DF_KV):
        kv_s[h, past:past + LANE, :] = _tail_tile(kvn[:, h * 4 * HD:(h + 1) * 4 * HD], 4 * HD)
    lam = _diff_lambda(dl_ref[...], lam_init)
    q = q_ref[...]
    for h in range(DF_KV):
        bias = b_ref[h * DF_G:(h + 1) * DF_G].reshape(rows, lk)
        ps = []
        for m in range(2):
            qm = jnp.concatenate(
                [q[:, ((h * DF_G + g) * 2 + m) * HD:((h * DF_G + g) * 2 + m + 1) * HD] for g in range(DF_G)],
                axis=0).astype(BF)
            ps.append(_softmax_rows(_dot_nt(qm, kv_s[h, :, m * HD:(m + 1) * HD]) + bias))
        a = ps[0] - lam * ps[1]
        out = _diff_finish(_dot(a.astype(BF), kv_s[h, :, 2 * HD:4 * HD]), sub_ref[...], lam_init)
        for g in range(DF_G):
            o_ref[:, (h * DF_G + g) * DF_DV:(h * DF_G + g + 1) * DF_DV] = out[g * t_new:(g + 1) * t_new]


def _diff_sample(qd_s, dkv_s, cache, page_table, tbl_t, dl, sub, t_new, lam_init):
    db, n_pages = page_table.shape
    page = cache.shape[2]
    past = n_pages * page
    lk = past + LANE
    bias = _rel_pattern(tbl_t, NSA_H, DF_H, t_new, lk, past, 0, 1 << 30, False, c_lim=past + t_new)
    full = lambda shape: pl.BlockSpec(shape, lambda b, pt: (0,) * len(shape))
    page_specs = [pl.BlockSpec((None, None, page, DF_KV, 4 * HD), lambda b, pt, p=p: (0, pt[b, p], 0, 0, 0))
                  for p in range(n_pages)]
    in_specs = page_specs + [
        pl.BlockSpec((t_new, DF_H * 2 * HD), lambda b, pt: (b, 0)),
        pl.BlockSpec((t_new, DF_KV * 4 * HD), lambda b, pt: (b, 0)),
        full((DF_H, t_new, lk)), full((4, HD)), full((1, DF_DV)),
    ]
    return pl.pallas_call(
        functools.partial(_diff_sample_kernel, n_pages=n_pages, page=page, past=past, t_new=t_new,
                          lam_init=lam_init),
        grid_spec=pltpu.PrefetchScalarGridSpec(
            num_scalar_prefetch=1, grid=(db,), in_specs=in_specs,
            out_specs=pl.BlockSpec((t_new, DF_H * DF_DV), lambda b, pt: (b, 0)),
            scratch_shapes=[pltpu.VMEM((DF_KV, lk, 4 * HD), BF), pltpu.VMEM((DF_KV, page, 4 * HD), F32)]),
        out_shape=jax.ShapeDtypeStruct((db * t_new, DF_H * DF_DV), F32),
        compiler_params=_cparams(("arbitrary",)),
        name="diff_sample",
    )(page_table, *([cache] * n_pages), qd_s, dkv_s, bias, dl, sub.reshape(1, DF_DV))


def _router_kernel(h_ref, g_ref, wr_ref, br_ref, xn_ref, ids_ref, wts_ref):
    x = h_ref[...]
    ms = jnp.mean(x * x, axis=-1, keepdims=True)
    xn = x * lax.rsqrt(ms + EPS) * g_ref[...]
    xh = xn.astype(BF)
    xn_ref[...] = xh
    xl = (xn - xh.astype(F32)).astype(BF)
    wr = wr_ref[...]
    wh = wr.astype(BF)
    wl = (wr - wh.astype(F32)).astype(BF)
    lg = _dot(xh, wh) + _dot(xl, wh) + _dot(xh, wl) + br_ref[...]
    lane_i = lax.broadcasted_iota(jnp.int32, lg.shape, 1)
    lane = lane_i.astype(F32)
    big = 1000.0
    isg = lane_i < N_GROUPS
    gmax = jnp.max(jnp.where(isg, lg, -3e38), axis=-1, keepdims=True)
    gsel = jnp.min(jnp.where(isg & (lg == gmax), lane, big), axis=-1, keepdims=True)
    gw = 1.0 / jnp.sum(jnp.where(isg, jnp.exp(lg - gmax), 0.0), axis=-1, keepdims=True)
    lo = N_GROUPS + gsel * EPG
    ing = (lane >= lo) & (lane < lo + EPG)
    emax = jnp.max(jnp.where(ing, lg, -3e38), axis=-1, keepdims=True)
    pe = jnp.where(ing, jnp.exp(lg - emax), 0.0)
    pr = jnp.where(ing, pe / jnp.sum(pe, axis=-1, keepdims=True), -1.0)
    v1 = jnp.max(pr, axis=-1, keepdims=True)
    i1 = jnp.min(jnp.where(pr == v1, lane, big), axis=-1, keepdims=True)
    pr2 = jnp.where(lane == i1, -1.0, pr)
    v2 = jnp.max(pr2, axis=-1, keepdims=True)
    i2 = jnp.min(jnp.where(pr2 == v2, lane, big), axis=-1, keepdims=True)
    den = v1 + v2
    e12 = jnp.where(lane_i == 0, i1 - N_GROUPS, jnp.where(lane_i == 1, i2 - N_GROUPS, 0.0))
    ids_ref[...] = e12.astype(jnp.int32)
    wts_ref[...] = jnp.where(lane_i == 0, v1 / den * gw, jnp.where(lane_i == 1, v2 / den * gw, 0.0))


def _router(h, g, wr, br, tm):
    m, d = h.shape
    return pl.pallas_call(
        _router_kernel,
        grid=(m // tm,),
        in_specs=[pl.BlockSpec((tm, d), lambda i: (i, 0)), pl.BlockSpec((1, d), lambda i: (0, 0)),
                  pl.BlockSpec((d, LANE), lambda i: (0, 0)), pl.BlockSpec((1, LANE), lambda i: (0, 0))],
        out_specs=[pl.BlockSpec((tm, d), lambda i: (i, 0)), pl.BlockSpec((tm, LANE), lambda i: (i, 0)),
                   pl.BlockSpec((tm, LANE), lambda i: (i, 0))],
        out_shape=[jax.ShapeDtypeStruct((m, d), BF), jax.ShapeDtypeStruct((m, LANE), jnp.int32),
                   jax.ShapeDtypeStruct((m, LANE), F32)],
        compiler_params=_cparams(("arbitrary",)),
        name="router",
    )(h, g.reshape(1, d), wr, br)


def _split_dot(x_ref, w_refs):
    rows = w_refs[0].shape[0]
    acc = _dot(x_ref[:, 0:rows], w_refs[0][...].astype(BF))
    for s in range(1, len(w_refs)):
        acc = acc + _dot(x_ref[:, s * rows:(s + 1) * rows], w_refs[s][...].astype(BF))
    return acc


def _moe_up_kernel(te_ref, nt_ref, x_ref, *refs, nk, n_split):
    wg = refs[:n_split]
    wu = refs[n_split:2 * n_split]
    o_ref, a_s, u_s = refs[2 * n_split:]
    k = pl.program_id(1)

    @pl.when(pl.program_id(0) < nt_ref[0])
    def _():
        @pl.when(k == 0)
        def _():
            a_s[...] = jnp.zeros(a_s.shape, F32)
            u_s[...] = jnp.zeros(u_s.shape, F32)

        a_s[...] += _split_dot(x_ref, wg)
        u_s[...] += _split_dot(x_ref, wu)

        @pl.when(k == nk - 1)
        def _():
            g = a_s[...]
            o_ref[...] = (g * jax.nn.sigmoid(g) * u_s[...]).astype(o_ref.dtype)


def _moe_down_kernel(te_ref, nt_ref, h_ref, *refs, n_split):
    o_ref = refs[n_split]

    @pl.when(pl.program_id(0) < nt_ref[0])
    def _():
        o_ref[...] = _split_dot(h_ref, refs[:n_split]).astype(o_ref.dtype)


def _moe(xn, ids, w_gate, w_up, w_down):
    m, d = xn.shape
    n_exp, _, ff = w_gate.shape
    tm = MOE_TM
    n_pairs = 2 * m
    n_tiles = n_exp + n_pairs // tm
    n_rows = n_tiles * tm
    flat_e = ids[:, :2].reshape(-1)
    onehot = (flat_e[:, None] == jnp.arange(n_exp, dtype=jnp.int32)[None, :]).astype(jnp.int32)
    counts = jnp.sum(onehot, axis=0)
    tiles_per = (counts + tm - 1) // tm
    tile_end = jnp.cumsum(tiles_per)
    pad_start = (tile_end - tiles_per) * tm
    rank =jnp.sum(onehot * (jnp.cumsum(onehot, axis=0) - 1), axis=1)
    pos_flat = jnp.sum(onehot * pad_start[None, :], axis=1) + rank
    pos = pos_flat.reshape(m, 2)
    nt = tile_end[-1]
    tix = jnp.arange(n_tiles, dtype=jnp.int32)
    tile_e = jnp.minimum(jnp.searchsorted(tile_end, jnp.minimum(tix, nt - 1), side="right"),
                         n_exp - 1).astype(jnp.int32)
    row_tok = (jnp.arange(n_rows, dtype=jnp.int32) % m).at[pos_flat].set(
        jnp.arange(n_pairs, dtype=jnp.int32) // 2, mode="promise_in_bounds", unique_indices=True)
    xs = xn.at[row_tok].get(mode="promise_in_bounds")
    nt_arr = nt.reshape(1).astype(jnp.int32)

    tk = _pick(d, (1024, 512, 256, 128))
    nk = d // tk
    live = lambda t, ntr: t < ntr[0]
    clamp = lambda t, ntr: jnp.minimum(t, ntr[0] - 1)
    kidx = lambda t, k, ntr: jnp.where(live(t, ntr), k, nk - 1)
    ns_up = MOE_DMA_SPLIT if tk % (MOE_DMA_SPLIT * LANE) == 0 else 1
    wspecs = [pl.BlockSpec((None, tk // ns_up, ff),
                           lambda t, k, te, ntr, s=s: (te[t], kidx(t, k, ntr) * ns_up + s, 0)) for s in range(ns_up)]
    hdn = pl.pallas_call(
        functools.partial(_moe_up_kernel, nk=nk, n_split=ns_up),
        grid_spec=pltpu.PrefetchScalarGridSpec(
            num_scalar_prefetch=2, grid=(n_tiles, nk),
            in_specs=[pl.BlockSpec((tm, tk), lambda t, k, te, ntr: (clamp(t, ntr), kidx(t, k, ntr)))]
            + wspecs + wspecs,
            out_specs=pl.BlockSpec((tm, ff), lambda t, k, te, ntr: (clamp(t, ntr), 0)),
            scratch_shapes=[pltpu.VMEM((tm, ff), F32), pltpu.VMEM((tm, ff), F32)]),
        out_shape=jax.ShapeDtypeStruct((n_rows, ff), BF),
        compiler_params=_cparams(("arbitrary", "arbitrary")),
        name="moe_up",
    )(tile_e, nt_arr, xs, *([w_gate] * ns_up), *([w_up] * ns_up))

    tn = _pick(d, (2048, 1024, 512, 256, 128))
    nn = d // tn
    ns_dn = MOE_DMA_SPLIT if ff % (MOE_DMA_SPLIT * LANE) == 0 else 1
    jidx = lambda t, j, ntr: jnp.where(live(t, ntr), j, nn - 1)
    y = pl.pallas_call(
        functools.partial(_moe_down_kernel, n_split=ns_dn),
        grid_spec=pltpu.PrefetchScalarGridSpec(
            num_scalar_prefetch=2, grid=(n_tiles, nn),
            in_specs=[pl.BlockSpec((tm, ff), lambda t, j, te, ntr: (clamp(t, ntr), 0))]
            + [pl.BlockSpec((None, ff // ns_dn, tn), lambda t, j, te, ntr, s=s: (te[t], s, jidx(t, j, ntr)))
               for s in range(ns_dn)],
            out_specs=pl.BlockSpec((tm, tn), lambda t, j, te, ntr: (clamp(t, ntr), jidx(t, j, ntr)))),
        out_shape=jax.ShapeDtypeStruct((n_rows, d), BF),
        compiler_params=_cparams(("arbitrary", "arbitrary")),
        name="moe_down",
    )(tile_e, nt_arr, hdn, *([w_down] * ns_dn))
    return y, pos


def _split_cols_kernel(x_ref, o_ref, *, dims, width):
    for i in range(dims[0]):
        if len(dims) == 1:
            o_ref[:, i, :] = x_ref[:, i * width:(i + 1) * width]
        else:
            for j in range(dims[1]):
                c = (i * dims[1] + j) * width
                o_ref[:, i, j, :] = x_ref[:, c:c + width]


def _split_cols(x, batch, rows_per_batch, row0, dims, width, tb=256):
    seq = x.shape[0] // batch
    tb = _pick(math.gcd(rows_per_batch, math.gcd(row0, seq)) if row0 else math.gcd(rows_per_batch, seq),
               (tb, 128, 64, 32, 16, 8))
    nb = rows_per_batch // tb
    ncol = x.shape[1]
    zeros = (0,) * (len(dims) + 1)
    return pl.pallas_call(
        functools.partial(_split_cols_kernel, dims=dims, width=width),
        grid=(batch, nb),
        in_specs=[pl.BlockSpec((tb, ncol), lambda b, i: ((b * seq + row0) // tb + i, 0))],
        out_specs=pl.BlockSpec((None, tb) + tuple(dims) + (width,), lambda b, i: (b, i) + zeros),
        out_shape=jax.ShapeDtypeStruct((batch, rows_per_batch) + tuple(dims) + (width,), x.dtype),
        compiler_params=_cparams(("arbitrary", "arbitrary")),
        name="split_cols",
    )(x)


O_KV = NSA_H * HD
O_GATE = O_KV + 6 * NSA_KV * HD
O_QD = O_GATE + 3 * NSA_H
W_QD = DF_H * 2 * HD
W_KD = DF_KV * 2 * HD


def _gate_weights(w):
    d = w.shape[0]
    wg = w[:, O_GATE:O_QD].reshape(d, 3, NSA_KV, NSA_G).transpose(0, 2, 1, 3).reshape(d, NSA_KV, 3 * NSA_G)
    return jnp.pad(wg, ((0, 0), (0, 0), (0, LANE - 3 * NSA_G))).reshape(d, NSA_KV * LANE)


def kernel(x_prompt, x_sample, cache_nsa_kv, cache_diff_kv, state_nsa_win, page_table, p_prompt, p_sample,
           rel_bias_table, norm_mix, w_in, w_cmp, diff_lambda, diff_subln, w_out, norm_ffn, w_router_group,
           b_router_group, w_router_expert, b_router_expert, w_exp_gate, w_exp_up, w_exp_down, norm_ple,
           w_ple_gate, w_ple_proj, final_norm):
    assert norm_mix.shape[0] == 1, "single-layer trunk"
    batch, seq, d = x_prompt.shape
    db, t_new, _ = x_sample.shape
    mp, ms = batch * seq, db * t_new
    m = mp + ms
    lam_init = 0.8 - 0.6 * math.exp(-0.3 * 0)
    tm = _pick(math.gcd(mp, ms), (1024, 512, 256, 128))
    tm_s = _pick(math.gcd(mp, ms), (256, 128))
    xp = x_prompt.reshape(mp, d)
    xs = x_sample.reshape(ms, d)
    tbl_t = rel_bias_table

    xn = _rms2(xp, xs, norm_mix[0], tm_s)
    w0 = w_in[0]
    w_diff = w0[:, O_QD:]
    w_gate = _gate_weights(w0)
    scale = HD ** -0.5
    tn = 512
    wide = functools.partial(_matmul, [xn], tm=tm, tn=tn)
    q_nsa = wide([(w0, 0)], rows=m, n_cols=O_KV, out_dtype=BF, scale=scale, name="proj_qn")
    qd = wide([(w_diff, 0)], rows=m, n_cols=W_QD, out_dtype=BF, scale=scale, name="proj_qd")
    gates = _matmul([xn], [(w_gate, 0)], rows=m, n_cols=NSA_KV * LANE, tm=tm, tn=NSA_KV * LANE, out_dtype=F32,
                    name="proj_gate")
    kv_w = [(w0, 0, lambda j: j + O_KV // tn)]
    win_w = [(w0, 0, lambda j: j + (O_KV + 4 * NSA_KV * HD) // tn)]
    tn_d = DF_DV
    dkv_w = [(w_diff, 0, lambda j: jnp.where(j % 2 == 0, W_QD // tn_d + j // 2, (W_QD + W_KD) // tn_d + j // 2))]
    dkv = functools.partial(_matmul, [xn], dkv_w, tm=tm, tn=tn_d, n_cols=DF_KV * 4 * HD, out_dtype=F32)
    kv4_p = wide(kv_w, rows=mp, n_cols=4 * NSA_KV * HD, out_dtype=F32, name="proj_kv_p")
    kv4_s = wide(kv_w, rows=ms, row0=mp, n_cols=4 * NSA_KV * HD, out_dtype=F32, name="proj_kv_s")
    win_p = wide(win_w, rows=mp, n_cols=2 * NSA_KV * HD, out_dtype=F32, name="proj_win_p")
    win_s = wide(win_w, rows=ms, row0=mp, n_cols=2 * NSA_KV * HD, out_dtype=F32, name="proj_win_s")
    dkv_p = dkv(rows=mp, name="proj_dkv_p")
    dkv_s = dkv(rows=ms, row0=mp, name="proj_dkv_s")

    nc = seq // CMP
    kvr = kv4_p.reshape(batch, seq, 4, NSA_KV, HD)
    cmp_out = []
    for kind in range(2):
        a = kvr[:, :, kind].transpose(2, 0, 1, 3).reshape(NSA_KV * batch * nc, CMP * HD).astype(BF)
        r = a.shape[0]
        cmp_out.append(_matmul([a], [(w_cmp[0, kind].reshape(CMP * HD, HD), 0)], rows=r, n_cols=HD,
                               tm=_pick(r, (512, 256, 128, 64, 32, 16)), tn=HD, out_dtype=BF,
                               name="compress").reshape(NSA_KV, batch, nc, HD))
    o_n = _nsa_prompt(q_nsa, gates, cmp_out[0], cmp_out[1], kv4_p, win_p, tbl_t, batch, seq, m)
    o_d = _diff_prompt(qd, dkv_p, tbl_t, diff_lambda[0], diff_subln[0], batch, seq, m, lam_init)

    o_n_s = _nsa_sample(q_nsa[mp:].astype(F32), gates[mp:], kv4_s, win_s, cache_nsa_kv, state_nsa_win,
                        page_table, w_cmp[0], tbl_t, t_new)
    o_d_s = _diff_sample(qd[mp:].astype(F32), dkv_s, cache_diff_kv, page_table, tbl_t, diff_lambda[0],
                         diff_subln[0], t_new, lam_init)
    o_n = lax.dynamic_update_slice(o_n, o_n_s.astype(BF), (mp, 0))
    o_d = lax.dynamic_update_slice(o_d, o_d_s.astype(BF), (mp, 0))

    h1 = _matmul([o_n, o_d], [(w_out[0], 0), (w_out[0], 1)], rows=m, n_cols=d, tm=tm,
                 tn=_pick(d, (512, 256, 128)), out_dtype=F32, epi="res2", epi_args=(xp, xs), name="out_proj")

    wr = jnp.concatenate([w_router_group[0], w_router_expert[0],
                          jnp.zeros((d, LANE - N_GROUPS - N_EXP), F32)], axis=1)
    br = jnp.concatenate([b_router_group[0], b_router_expert[0],
                          jnp.zeros((LANE - N_GROUPS - N_EXP,), F32)]).reshape(1, LANE)
    xn2, ids, wts = _router(h1, norm_ffn[0], wr, br, tm_s)
    y, pos = _moe(xn2, ids, w_exp_gate[0], w_exp_up[0], w_exp_down[0])
    h2 = (h1 + wts[:, 0:1] * y.at[pos[:, 0]].get(mode="promise_in_bounds").astype(F32)
          + wts[:, 1:2] * y.at[pos[:, 1]].get(mode="promise_in_bounds").astype(F32))

    xn3 = _rms(h2, norm_ple[0], tm_s, BF)
    p_all = jnp.concatenate([p_prompt[0].reshape(mp, -1), p_sample[0].reshape(ms, -1)], axis=0).astype(BF)
    h3 = _matmul([xn3], [(w_ple_gate[0], 0)], rows=m, n_cols=d, tm=tm, tn=_pick(d, (512, 256, 128)),
                 out_dtype=F32, epi="ple", epi_args=(h2, p_all, w_ple_proj[0]), name="ple")

    y_p = _rms(h3, final_norm, tm_s, F32, row0=0, rows=mp).reshape(batch, seq, d)
    y_s = _rms(h3, final_norm, tm_s, F32, row0=mp, rows=ms).reshape(db, t_new, d)
    wk = min(WIN, seq)
    nsa_kv_p = _split_cols(kv4_p, batch, seq, 0, (4, NSA_KV), HD)
    diff_kv_p = _split_cols(dkv_p, batch, seq, 0, (DF_KV,), 4 * HD)
    win_p_out = _split_cols(win_p, batch, wk, seq - wk, (2, NSA_KV), HD)
    new_win = jnp.concatenate([state_nsa_win[0], win_s.reshape(db, t_new, 2, NSA_KV, HD)], axis=1)[:, t_new:]
    return (y_p, y_s,
            nsa_kv_p[None], kv4_s.reshape(1, db, t_new, 4, NSA_KV, HD),
            diff_kv_p[None], dkv_s.reshape(1, db, t_new, DF_KV, 4 * HD),
            win_p_out[None], new_win[None])
```

```python
import functools
import math

import numpy as np
import jax
import jax.numpy as jnp
from jax import lax
from jax.experimental import pallas as pl
from jax.experimental.pallas import tpu as pltpu

BF = jnp.bfloat16
F32 = jnp.float32

HD = 128
NSA_H = 16
NSA_KV = 2
NSA_G = NSA_H // NSA_KV
CMP = 32
SELB = 64
TOPK = 16
WIN = 512
DF_H = 8
DF_KV = 4
DF_G = DF_H // DF_KV
DF_DV = 2 * HD
REL_BUCKETS = 32
REL_MAX_DIST = 128
N_GROUPS = 4
EPG = 8
N_EXP = N_GROUPS * EPG
EPS = 1e-6
NEG = -1e30
FORCE = 1e4
LANE = 128
VMEM_LIMIT = 56 * 1024 * 1024
MOE_TM = 768
MOE_XBLK = 128
MOE_DMA_SPLIT = 4

QB_NSA = 128
PAD_SEL = 384
PAD_WIN = WIN
NEAR_SEL = 512
BAND_WIN = WIN + QB_NSA
QB_DF = 256
PAD_DF = 256
NEAR_DF = 512
TK = 256
CMP_PITCH = 40


def _dot(a, b):
    return jnp.dot(a, b, preferred_element_type=F32)


def _dot_nt(a, b):
    return lax.dot_general(a, b, (((1,), (1,)), ((), ())), preferred_element_type=F32)


def _cparams(sem):
    return pltpu.CompilerParams(dimension_semantics=sem, vmem_limit_bytes=VMEM_LIMIT)


def _pick(n, cands):
    for c in cands:
        if n % c == 0:
            return c
    raise ValueError(f"no tile in {cands} divides {n}")


def _rms2_kernel(xp_ref, xs_ref, g_ref, o_ref, *, np_tiles):
    i = pl.program_id(0)

    def go(x_ref):
        x = x_ref[...]
        ms = jnp.mean(x * x, axis=-1, keepdims=True)
        o_ref[...] = (x * lax.rsqrt(ms + EPS) * g_ref[...]).astype(o_ref.dtype)

    @pl.when(i < np_tiles)
    def _():
        go(xp_ref)

    @pl.when(i >= np_tiles)
    def _():
        go(xs_ref)


def _rms2(xp, xs, g, tm):
    mp, d = xp.shape
    ms = xs.shape[0]
    npt, nst = mp // tm, ms // tm
    return pl.pallas_call(
        functools.partial(_rms2_kernel, np_tiles=npt),
        grid=(npt + nst,),
        in_specs=[pl.BlockSpec((tm, d), lambda i: (jnp.minimum(i, npt - 1), 0)),
                  pl.BlockSpec((tm, d), lambda i: (jnp.maximum(i - npt, 0), 0)),
                  pl.BlockSpec((1, d), lambda i: (0, 0))],
        out_specs=pl.BlockSpec((tm, d), lambda i: (i, 0)),
        out_shape=jax.ShapeDtypeStruct((mp + ms, d), BF),
        compiler_params=_cparams(("arbitrary",)),
        name="rms2",
    )(xp, xs, g.reshape(1, d))


def _rms_kernel(x_ref, g_ref, o_ref):
    x = x_ref[...]
    ms = jnp.mean(x * x, axis=-1, keepdims=True)
    o_ref[...] = (x * lax.rsqrt(ms + EPS) * g_ref[...]).astype(o_ref.dtype)


def _rms(x, g, tm, out_dtype, row0=0, rows=None):
    m, d = x.shape
    rows = m if rows is None else rows
    t0 = row0 // tm
    return pl.pallas_call(
        _rms_kernel,
        grid=(rows // tm,),
        in_specs=[pl.BlockSpec((tm, d), lambda i: (i + t0, 0)),
                  pl.BlockSpec((1, d), lambda i: (0, 0))],
        out_specs=pl.BlockSpec((tm, d), lambda i: (i, 0)),
        out_shape=jax.ShapeDtypeStruct((rows, d), out_dtype),
        compiler_params=_cparams(("arbitrary",)),
        name="rms",
    )(x, g.reshape(1, d))


def _cast_rows(src_ref, dst_ref):
    k = src_ref.shape[0]
    ch = 256 if k % 256 == 0 else k

    def body(c, carry):
        r = pl.multiple_of(c * ch, ch)
        dst_ref[pl.ds(r, ch), :] = src_ref[pl.ds(r, ch), :].astype(BF)
        return carry

    lax.fori_loop(0, k // ch, body, 0)


def _mm_kernel(*refs, n_a, cast, epi, scale, np_tiles):
    a = refs[:n_a]
    w = refs[n_a:2 * n_a]
    idx = 2 * n_a
    if epi == "res2":
        rp_ref, rs_ref = refs[idx:idx + 2]
        idx += 2
    elif epi == "ple":
        h_ref, p_ref, wp_ref = refs[idx:idx + 3]
        idx += 3
    o_ref = refs[idx]
    idx += 1
    wb = refs[idx:idx + n_a] if cast else w
    i = pl.program_id(1)

    if cast:
        @pl.when(i == 0)
        def _():
            for k in range(n_a):
                _cast_rows(w[k], wb[k])

    acc = _dot(a[0][...], wb[0][...])
    for k in range(1, n_a):
        acc = acc + _dot(a[k][...], wb[k][...])
    if scale is not None:
        acc = acc * scale
    if epi is None:
        o_ref[...] = acc.astype(o_ref.dtype)
    elif epi == "res2":
        @pl.when(i < np_tiles)
        def _():
            o_ref[...] = (acc + rp_ref[...]).astype(o_ref.dtype)

        @pl.when(i >= np_tiles)
        def _():
            o_ref[...] = (acc + rs_ref[...]).astype(o_ref.dtype)
    elif epi == "ple":
        gate = jax.nn.sigmoid(acc)
        proj = _dot(p_ref[...], wp_ref[...].astype(BF))
        o_ref[...] = (h_ref[...] + gate * proj).astype(o_ref.dtype)


def _matmul(a_list, w_list, *, rows, n_cols, tm, tn, out_dtype, row0=0, col0=0,
            scale=None, epi=None, epi_args=(), name="mm"):
    n_a = len(a_list)
    cast = w_list[0][0].dtype != BF
    t0, c0 = row0 // tm, col0 // tn
    gm, gn = rows // tm, n_cols // tn
    in_specs, args = [], []
    for a in a_list:
        in_specs.append(pl.BlockSpec((tm, a.shape[1]), lambda j, i: (i + t0, 0)))
        args.append(a)
    for ent, a in zip(w_list, a_list):
        w, kb = ent[0], ent[1]
        colfn = ent[2] if len(ent) > 2 else (lambda j: j + c0)
        in_specs.append(pl.BlockSpec((a.shape[1], tn), lambda j, i, kb=kb, colfn=colfn: (kb, colfn(j))))
        args.append(w)
    np_tiles = 0
    if epi == "res2":
        xp, xs = epi_args
        np_tiles = xp.shape[0] // tm
        in_specs.append(pl.BlockSpec((tm, tn), lambda j, i: (jnp.minimum(i, np_tiles - 1), j)))
        in_specs.append(pl.BlockSpec((tm, tn), lambda j, i: (jnp.maximum(i - np_tiles, 0), j)))
        args += [xp, xs]
    elif epi == "ple":
        h, p, wp = epi_args
        in_specs.append(pl.BlockSpec((tm, tn), lambda j, i: (i, j)))
        in_specs.append(pl.BlockSpec((tm, p.shape[1]), lambda j, i: (i, 0)))
        in_specs.append(pl.BlockSpec((wp.shape[0], tn), lambda j, i: (0, j)))
        args += [h, p, wp]
    scratch = [pltpu.VMEM((a.shape[1], tn), BF) for a in a_list] if cast else []
    return pl.pallas_call(
        functools.partial(_mm_kernel, n_a=n_a, cast=cast, epi=epi, scale=scale, np_tiles=np_tiles),
        grid=(gn, gm),
        in_specs=in_specs,
        out_specs=pl.BlockSpec((tm, tn), lambda j, i: (i, j)),
        out_shape=jax.ShapeDtypeStruct((rows, n_cols), out_dtype),
        scratch_shapes=scratch,
        compiler_params=_cparams(("arbitrary", "arbitrary")),
        name=name,
    )(*args)


def _bucket_np(dist):
    n = np.maximum(dist, 0)
    max_exact = REL_BUCKETS // 2
    nf = np.maximum(n, 1).astype(np.float32)
    log_b = max_exact + (np.log(nf / np.float32(max_exact)) / np.float32(math.log(REL_MAX_DIST / max_exact))
                         * np.float32(REL_BUCKETS - max_exact)).astype(np.int32)
    return np.where(n < max_exact, n, np.minimum(log_b, REL_BUCKETS - 1)).astype(np.int32)


def _bucket_edges():
    b = _bucket_np(np.arange(0, 4 * REL_MAX_DIST))
    return [(k, int(np.nonzero(b == k)[0].max())) for k in range(REL_BUCKETS - 1) if (b == k).any()]


def _pattern_kernel(tbl_ref, o_ref, *, h0, base, col_step, v_lo, v_hi, c_lim, sub_far, edges, keys_on_rows):
    h = pl.program_id(0) + h0
    shape = o_ref.shape
    row = lax.broadcasted_iota(jnp.int32, shape, 0) + pl.program_id(1) * shape[0]
    col = lax.broadcasted_iota(jnp.int32, shape, 1)
    if keys_on_rows:
        row, col = col, row
    dist = row + base - col_step * col
    far = tbl_ref[REL_BUCKETS - 1, h]
    b = jnp.full(shape, far, F32)
    for k, hi in reversed(edges):
        b = jnp.where(dist <= hi, tbl_ref[k, h], b)
    if sub_far:
        b = b - far
    valid = (dist >= v_lo) & (dist <= v_hi) & (col < c_lim)
    o_ref[...] = jnp.where(valid, b, NEG)


def _rel_pattern(tbl, h0, nh, nrows, ncols, base, v_lo, v_hi, sub_far, col_step=1, c_lim=None,
                 keys_on_rows=False):
    out_r, out_c = (ncols, nrows) if keys_on_rows else (nrows, ncols)
    tr = _pick(out_r, (512, 256, 128, 64, 32, 16, 8))
    return pl.pallas_call(
        functools.partial(_pattern_kernel, h0=h0, base=base, col_step=col_step, v_lo=v_lo, v_hi=v_hi,
                          c_lim=ncols if c_lim is None else c_lim, sub_far=sub_far, edges=_bucket_edges(),
                          keys_on_rows=keys_on_rows),
        grid=(nh, out_r // tr),
        in_specs=[pl.BlockSpec(memory_space=pltpu.SMEM)],
        out_specs=pl.BlockSpec((None, tr, out_c), lambda h, r: (h, r, 0)),
        out_shape=jax.ShapeDtypeStruct((nh, out_r, out_c), F32),
        compiler_params=_cparams(("arbitrary", "arbitrary")),
        name="rel_bias",
    )(tbl)


def _pair_sum_matrix(nc):
    n = np.arange(nc)[:, None]
    b = np.arange(LANE)[None, :]
    return jnp.asarray((n // (SELB // CMP) == b).astype(np.float32), dtype=BF)


def _expand_matrix(pad, n_keys):
    l = np.arange(LANE)[:, None]
    c = np.arange(pad + n_keys)[None, :]
    return jnp.asarray(((c >= pad) & ((c - pad) // SELB == l)).astype(np.float32), dtype=BF)


def _split3(x):
    hi = x.astype(BF)
    r = x - hi.astype(F32)
    mid = r.astype(BF)
    lo = (r - mid.astype(F32)).astype(BF)
    return hi, mid, lo


def _select_blocks(psum, s_mat, qpos, ns):
    hi, mid, lo = _split3(psum)
    imp = _dot(hi, s_mat) + _dot(mid, s_mat) + _dot(lo, s_mat)
    shape = imp.shape
    lane = lax.broadcasted_iota(jnp.int32, shape, 1)
    valid = lane * SELB <= qpos
    cur = jnp.right_shift(qpos, 6)
    forced = (lane == 0) | (lane == cur) | (lane == cur - 1)
    score = jnp.where(valid, imp + jnp.where(forced, FORCE, 0.0), NEG)
    score = jnp.where(lane < ns, score, -3e38)
    cnt = jnp.zeros(shape, F32)
    for i in range(ns):
        ci = score[:, i:i + 1]
        cnt = cnt + jnp.where(lane > i, jnp.where(ci >= score, 1.0, 0.0), jnp.where(ci > score, 1.0, 0.0))
    sel = (cnt < float(min(TOPK, ns))) & (lane < ns)
    return jnp.where(sel, 1.0, 0.0).astype(BF)


def _softmax_rows(s, valid=None):
    m = jnp.max(s, axis=-1, keepdims=True)
    p = jnp.exp(s - m)
    if valid is not None:
        p = jnp.where(valid, p, 0.0)
    l = jnp.sum(p, axis=-1, keepdims=True)
    return p / jnp.where(l > 0.0, l, 1.0)


def _lanes(x, n):
    return x if n == LANE else jnp.concatenate([x] * (n // LANE), axis=1)


def _online(carry, s, vt, ones_in_v=False):
    m, l, acc = carry
    dv = acc.shape[1]
    m_new = jnp.maximum(m, jnp.max(s, axis=-1, keepdims=True))
    p = jnp.exp(s - _lanes(m_new, s.shape[1]))
    alpha = jnp.exp(m - m_new)
    pv = _dot(p.astype(BF), vt)
    if ones_in_v:
        l = alpha * l + pv[:, dv:dv + LANE]
        pv = pv[:, 0:dv]
    else:
        l = alpha * l + jnp.sum(p, axis=-1, keepdims=True)
    return m_new, l, _lanes(alpha, dv) * acc + pv


def _online_init(rows, dv):
    return jnp.full((rows, LANE), NEG, F32), jnp.zeros((rows, LANE), F32), jnp.zeros((rows, dv), F32)


def _flash_step(q, kt, vt, bias, m_ref, l_ref, acc_ref, rows, ones_in_v=False):
    s = _dot_nt(q, kt)
    if bias is not None:
        s = s + bias
    m_new, l, acc = _online((m_ref[rows, :], l_ref[rows, :], acc_ref[rows, :]), s, vt, ones_in_v)
    m_ref[rows, :] = m_new
    l_ref[rows, :] = l
    acc_ref[rows, :] = acc


def _nsa_prompt_kernel(q_ref, gate_ref, kc_ref, vc_ref, ks_ref, vs_ref, kw_ref, vw_ref,
                       bc_ref, pn_ref, pw_ref, e_ref, s_ref, o_ref,
                       ks_s, vs_s, kw_s, vw_s, m_s, l_s, acc_s, o_s, *, seq, nc, ns):
    i = pl.program_id(2)
    s0 = i * QB_NSA
    rows = NSA_G * QB_NSA

    @pl.when(i == 0)
    def _():
        ks_s[0:PAD_SEL, :] = jnp.zeros((PAD_SEL, HD), BF)
        vs_s[0:PAD_SEL, :] = jnp.zeros((PAD_SEL, 2 * HD), BF)
        kw_s[0:PAD_WIN, :] = jnp.zeros((PAD_WIN, HD), BF)
        vw_s[0:PAD_WIN, :] = jnp.zeros((PAD_WIN, 2 * HD), BF)
        ch = 512
        ones = jnp.ones((ch, HD), BF)

        def cp(c, carry):
            r = pl.multiple_of(c * ch, ch)
            ks_s[pl.ds(PAD_SEL + r, ch), :] = ks_ref[pl.ds(r, ch), :].astype(BF)
            vs_s[pl.ds(PAD_SEL + r, ch), 0:HD] = vs_ref[pl.ds(r, ch), :].astype(BF)
            vs_s[pl.ds(PAD_SEL + r, ch), HD:2 * HD] = ones
            kw_s[pl.ds(PAD_WIN + r, ch), :] = kw_ref[pl.ds(r, ch), :].astype(BF)
            vw_s[pl.ds(PAD_WIN + r, ch), 0:HD] = vw_ref[pl.ds(r, ch), :].astype(BF)
            vw_s[pl.ds(PAD_WIN + r, ch), HD:2 * HD] = ones
            return carry

        lax.fori_loop(0, seq // ch, cp, 0)

    gt = jax.nn.sigmoid(gate_ref[...])
    head = lambda g: slice(g * HD, (g + 1) * HD)
    hrows = lambda g: slice(g * QB_NSA, (g + 1) * QB_NSA)

    kc = kc_ref[...]
    vc = vc_ref[...]
    psum = jnp.zeros((QB_NSA, nc), F32)
    for g in range(NSA_G):
        bc = bc_ref[g]
        pc = _softmax_rows(_dot_nt(q_ref[:, head(g)], kc) + bc, bc > 0.5 * NEG)
        psum = psum + pc
        o_s[:, head(g)] = gt[:, g:g + 1] * _dot(pc.astype(BF), vc)
    qpos = s0 + lax.broadcasted_iota(jnp.int32, (QB_NSA, LANE), 0)
    selb = _select_blocks(psum, s_ref[...], qpos, ns)

    nch = jnp.maximum(i - 1, 0) // 2
    far_keys = nch * TK
    m_s[...] = jnp.full((rows, LANE), NEG, F32)
    l_s[...] = jnp.zeros((rows, LANE), F32)
    acc_s[...] = jnp.zeros((rows, HD), F32)

    def sel_chunk(r, bias_of):
        kt = ks_s[pl.ds(r, TK), :]
        vt = vs_s[pl.ds(r, TK), :]
        madd = (_dot(selb, e_ref[:, pl.ds(r, TK)]) - 1.0) * (-NEG)
        for g in range(NSA_G):
            _flash_step(q_ref[:, head(g)], kt, vt, bias_of(g, madd), m_s, l_s, acc_s, hrows(g), ones_in_v=True)

    def far(c, carry):
        sel_chunk(pl.multiple_of(PAD_SEL + c * TK, LANE), lambda g, madd: madd)
        return carry

    lax.fori_loop(0, nch, far, 0)
    for kh in range(NEAR_SEL // TK):
        col = lax.broadcasted_iota(jnp.int32, (QB_NSA, TK), 1) + kh * TK
        cut = jnp.where(col < far_keys - s0 + PAD_SEL, NEG, 0.0)
        sel_chunk(pl.multiple_of(s0 + kh * TK, LANE),
                  lambda g, madd, kh=kh, cut=cut: pn_ref[g, :, kh * TK:(kh + 1) * TK] + madd + cut)
    for g in range(NSA_G):
        osel = acc_s[hrows(g), :] / l_s[hrows(g), :]
        o_s[:, head(g)] = o_s[:, head(g)] + gt[:, NSA_G + g:NSA_G + g + 1] * osel

    for g in range(NSA_G):
        st = _online_init(QB_NSA, HD)
        for c0 in range(0, BAND_WIN, TK):
            w = min(TK, BAND_WIN - c0)
            r = pl.multiple_of(s0 + c0, LANE)
            colw = lax.broadcasted_iota(jnp.int32, (QB_NSA, w), 1) + c0
            bias = pw_ref[g, :, c0:c0 + w] + jnp.where(colw < PAD_WIN - s0, NEG, 0.0)
            st = _online(st, _dot_nt(q_ref[:, head(g)], kw_s[pl.ds(r, w), :]) + bias, vw_s[pl.ds(r, w), :],
                         ones_in_v=True)
        ow = st[2] / st[1]
        o_ref[:, head(g)] = (o_s[:, head(g)] + gt[:, 2 * NSA_G + g:2 * NSA_G + g + 1] * ow).astype(o_ref.dtype)


def _nsa_prompt(q_nsa, gates, kc, vc, kv4_p, win_p, tbl_t, batch, seq, m_total):
    nc = seq // CMP
    ns = -(-seq // SELB)
    nqb = seq // QB_NSA
    big = 1 << 30
    pn = _rel_pattern(tbl_t, 0, NSA_H, QB_NSA, NEAR_SEL, PAD_SEL, 0, big, True)
    pw = _rel_pattern(tbl_t, 0, NSA_H, QB_NSA, BAND_WIN, PAD_WIN, 0, WIN, False)
    bc = _rel_pattern(tbl_t, 0, NSA_H, seq, nc, -(CMP - 1), 0, big, False, col_step=CMP)
    e_mat = _expand_matrix(PAD_SEL, seq)
    s_mat = _pair_sum_matrix(nc)
    kv_spec = lambda col: pl.BlockSpec((seq, HD), lambda b, h, i, col=col: (b, col + h))
    return pl.pallas_call(
        functools.partial(_nsa_prompt_kernel, seq=seq, nc=nc, ns=ns),
        grid=(batch, NSA_KV, nqb),
        in_specs=[
            pl.BlockSpec((QB_NSA, NSA_G * HD), lambda b, h, i: (b * nqb + i, h)),
            pl.BlockSpec((QB_NSA, LANE), lambda b, h, i: (b * nqb + i, h)),
            pl.BlockSpec((None, None, nc, HD), lambda b, h, i: (h, b, 0, 0)),
            pl.BlockSpec((None, None, nc, HD), lambda b, h, i: (h, b, 0, 0)),
            kv_spec(2 * NSA_KV), kv_spec(3 * NSA_KV),
            pl.BlockSpec((seq, HD), lambda b, h, i: (b, h)),
            pl.BlockSpec((seq, HD), lambda b, h, i: (b, NSA_KV + h)),
            pl.BlockSpec((NSA_G, QB_NSA, nc), lambda b, h, i: (h, i, 0)),
            pl.BlockSpec((NSA_G, QB_NSA, NEAR_SEL), lambda b, h, i: (h, 0, 0)),
            pl.BlockSpec((NSA_G, QB_NSA, BAND_WIN), lambda b, h, i: (h, 0, 0)),
            pl.BlockSpec((LANE, PAD_SEL + seq), lambda b, h, i: (0, 0)),
            pl.BlockSpec((nc, LANE), lambda b, h, i: (0, 0)),
        ],
        out_specs=pl.BlockSpec((QB_NSA, NSA_G * HD), lambda b, h, i: (b * nqb + i, h)),
        out_shape=jax.ShapeDtypeStruct((m_total, NSA_H * HD), BF),
        scratch_shapes=[pltpu.VMEM((PAD_SEL + seq, HD), BF), pltpu.VMEM((PAD_SEL + seq, 2 * HD), BF),
                        pltpu.VMEM((PAD_WIN + seq, HD), BF), pltpu.VMEM((PAD_WIN + seq, 2 * HD), BF),
                        pltpu.VMEM((NSA_G * QB_NSA, LANE), F32), pltpu.VMEM((NSA_G * QB_NSA, LANE), F32),
                        pltpu.VMEM((NSA_G * QB_NSA, HD), F32), pltpu.VMEM((QB_NSA, NSA_G * HD), F32)],
        compiler_params=_cparams(("arbitrary", "arbitrary", "arbitrary")),
        name="nsa_prompt",
    )(q_nsa, gates, kc, vc, kv4_p, kv4_p, win_p, win_p, bc, pn, pw, e_mat, s_mat)


def _diff_lambda(dl, lam_init):
    a = jnp.sum(dl[0:1] * dl[1:2], axis=-1, keepdims=True)
    b = jnp.sum(dl[2:3] * dl[3:4], axis=-1, keepdims=True)
    return jnp.exp(a) - jnp.exp(b) + lam_init


def _diff_finish(a, sub, lam_init):
    ms = jnp.mean(a * a, axis=-1, keepdims=True)
    return a * lax.rsqrt(ms + EPS) * sub * (1.0 - lam_init)


def _diff_prompt_kernel(q_ref, kv_ref, pn_ref, dl_ref, sub_ref, o_ref, kv_s, m_s, l_s, acc_s, *, seq, lam_init):
    i = pl.program_id(2)
    s0 = i * QB_DF
    rows = DF_G * QB_DF

    @pl.when(i == 0)
    def _():
        kv_s[0:PAD_DF, :] = jnp.zeros((PAD_DF, 4 * HD), BF)
        ch = 256

        def cp(c, carry):
            r = pl.multiple_of(c * ch, ch)
            kv_s[pl.ds(PAD_DF + r, ch), :] = kv_ref[pl.ds(r, ch), :].astype(BF)
            return carry

        lax.fori_loop(0, seq // ch, cp, 0)

    lam = _diff_lambda(dl_ref[...], lam_init)
    nfar = jnp.maximum(i - 1, 0)
    sub_rows = 128
    n_sub = QB_DF // sub_rows
    streams = [(m, g, j) for m in range(2) for g in range(DF_G) for j in range(n_sub)]
    srows = lambda k: slice(k * sub_rows, (k + 1) * sub_rows)
    m_s[...] = jnp.full((2 * rows, LANE), NEG, F32)
    l_s[...] = jnp.zeros((2 * rows, LANE), F32)
    acc_s[...] = jnp.zeros((2 * rows, DF_DV), F32)

    def chunk(r, bias_of):
        vt = kv_s[pl.ds(r, TK), 2 * HD:4 * HD]
        for k, (m, g, j) in enumerate(streams):
            q = q_ref[j * sub_rows:(j + 1) * sub_rows, (g * 2 + m) * HD:(g * 2 + m + 1) * HD]
            kt = kv_s[pl.ds(r, TK), m * HD:(m + 1) * HD]
            _flash_step(q, kt, vt, bias_of(g, j), m_s, l_s, acc_s, srows(k))

    def far(c, carry):
        chunk(pl.multiple_of(PAD_DF + c * TK, TK), lambda g, j: None)
        return carry

    lax.fori_loop(0, nfar, far, 0)
    for kh in range(NEAR_DF // TK):
        col = lax.broadcasted_iota(jnp.int32, (sub_rows, TK), 1) + kh * TK
        cut = jnp.where(col < nfar * TK - s0 + PAD_DF, NEG, 0.0)
        chunk(pl.multiple_of(s0 + kh * TK, TK),
              lambda g, j, kh=kh, cut=cut: pn_ref[g, j * sub_rows:(j + 1) * sub_rows, kh * TK:(kh + 1) * TK] + cut)
    half = len(streams) // 2
    for k, (_, g, j) in enumerate(streams[:half]):
        o1 = acc_s[srows(k), :] / _lanes(l_s[srows(k), :], DF_DV)
        o2 = acc_s[srows(half + k), :] / _lanes(l_s[srows(half + k), :], DF_DV)
        out = _diff_finish(o1 - lam * o2, sub_ref[...], lam_init)
        o_ref[j * sub_rows:(j + 1) * sub_rows, g * DF_DV:(g + 1) * DF_DV] = out.astype(o_ref.dtype)


def _diff_prompt(qd, dkv_p, tbl_t, dl, sub, batch, seq, m_total, lam_init):
    nqb = seq // QB_DF
    pn = _rel_pattern(tbl_t, NSA_H, DF_H, QB_DF, NEAR_DF, PAD_DF, 0, 1 << 30, True)
    width = DF_G * 2 * HD
    return pl.pallas_call(
        functools.partial(_diff_prompt_kernel, seq=seq, lam_init=lam_init),
        grid=(batch, DF_KV, nqb),
        in_specs=[
            pl.BlockSpec((QB_DF, width), lambda b, h, i: (b * nqb + i, h)),
            pl.BlockSpec((seq, 4 * HD), lambda b, h, i: (b, h)),
            pl.BlockSpec((DF_G, QB_DF, NEAR_DF), lambda b, h, i: (h, 0, 0)),
            pl.BlockSpec((4, HD), lambda b, h, i: (0, 0)),
            pl.BlockSpec((1, DF_DV), lambda b, h, i: (0, 0)),
        ],
        out_specs=pl.BlockSpec((QB_DF, DF_G * DF_DV), lambda b, h, i: (b * nqb + i, h)),
        out_shape=jax.ShapeDtypeStruct((m_total, DF_H * DF_DV), BF),
        scratch_shapes=[pltpu.VMEM((PAD_DF + seq, 4 * HD), BF),
                        pltpu.VMEM((2 * DF_G * QB_DF, LANE), F32), pltpu.VMEM((2 * DF_G * QB_DF, LANE), F32),
                        pltpu.VMEM((2 * DF_G * QB_DF, DF_DV), F32)],
        compiler_params=_cparams(("arbitrary", "arbitrary", "arbitrary")),
        name="diff_prompt",
    )(qd, dkv_p, pn, dl, sub.reshape(1, DF_DV))


def _tail_tile(new, width):
    t = new.shape[0]
    return jnp.concatenate([new, jnp.zeros((LANE - t, width), F32)], axis=0).astype(BF)


def _nsa_sample_kernel(pt_ref, *refs, n_pages, page, past, t_new, ncs, ns, wb):
    pages = refs[:n_pages]
    (q_ref, gate_ref, kvn_ref, wn_ref, st_ref, wc_ref, bc_ref, bs_ref, bw_ref, e_ref, s_ref,
     o_ref, kcmp_s, ksel_s, kwin_s) = refs[n_pages:]
    del pt_ref
    rows = NSA_G * t_new
    n_kinds = 4

    for p in range(n_pages):
        for kind in range(n_kinds):
            for h in range(NSA_KV):
                blk = pages[p][pl.ds(kind * NSA_KV + h, page, stride=n_kinds * NSA_KV), :]
                if kind < 2:
                    for nb in range(page // CMP):
                        r0 = (p * (page // CMP) + nb) * CMP_PITCH
                        kcmp_s[kind, h, r0:r0 + CMP, :] = blk[nb * CMP:(nb + 1) * CMP]
                else:
                    ksel_s[kind - 2, h, p * page:(p + 1) * page, :] = blk.astype(BF)
    kvn = kvn_ref[...]
    wn = wn_ref[...]
    for kind in range(2):
        for h in range(NSA_KV):
            c0 = ((kind + 2) * NSA_KV + h) * HD
            ksel_s[kind, h, past:past + LANE, :] = _tail_tile(kvn[:, c0:c0 + HD], HD)
            kwin_s[kind, h, 0:wb, :] = st_ref[pl.ds(kind * NSA_KV + h, wb, stride=2 * NSA_KV), :].astype(BF)
            c0 = (kind * NSA_KV + h) * HD
            kwin_s[kind, h, wb:wb + LANE, :] = _tail_tile(wn[:, c0:c0 + HD], HD)

    q = q_ref[...]
    gt = jax.nn.sigmoid(gate_ref[...])
    qpos = past + lax.broadcasted_iota(jnp.int32, (t_new, LANE), 0)
    for h in range(NSA_KV):
        cmp = []
        for kind in range(2):
            acc = jnp.zeros((ncs, HD), F32)
            for j in range(CMP):
                kj = kcmp_s[kind, h, pl.ds(j, ncs, stride=CMP_PITCH), :]
                acc = acc + _dot(kj.astype(BF), wc_ref[kind, j * HD:(j + 1) * HD, :])
            cmp.append(acc.astype(BF))
        kc, vc = cmp
        qs = jnp.concatenate([q[:, (h * NSA_G + g) * HD:(h * NSA_G + g + 1) * HD] for g in range(NSA_G)],
                             axis=0).astype(BF)
        bc = bc_ref[h * NSA_G:(h + 1) * NSA_G].reshape(rows, ncs)
        pc = _softmax_rows(_dot_nt(qs, kc) + bc, bc > 0.5 * NEG)
        oc = _dot(pc.astype(BF), vc)
        psum = pc[0:t_new]
        for g in range(1, NSA_G):
            psum = psum + pc[g * t_new:(g + 1) * t_new]
        selb = _select_blocks(psum, s_ref[...], qpos, ns)
        lk = past + LANE
        madd = (_dot(selb, e_ref[...]) - 1.0) * (-NEG)
        s = _dot_nt(qs, ksel_s[0, h])
        s = (s.reshape(NSA_G, t_new, lk) + bs_ref[h * NSA_G:(h + 1) * NSA_G] + madd[None]).reshape(rows, lk)
        osel = _dot(_softmax_rows(s).astype(BF), ksel_s[1, h])
        lw = wb + LANE
        s = _dot_nt(qs, kwin_s[0, h]) + bw_ref[h * NSA_G:(h + 1) * NSA_G].reshape(rows, lw)
        ow = _dot(_softmax_rows(s).astype(BF), kwin_s[1, h])
        for g in range(NSA_G):
            sl = slice(g * t_new, (g + 1) * t_new)
            gl = h * LANE + g
            o = (gt[:, gl:gl + 1] * oc[sl] + gt[:, gl + NSA_G:gl + NSA_G + 1] * osel[sl]
                 + gt[:, gl + 2 * NSA_G:gl + 2 * NSA_G + 1] * ow[sl])
            o_ref[:, (h * NSA_G + g) * HD:(h * NSA_G + g + 1) * HD] = o


def _nsa_sample(q_s, gates_s, kv4_s, win_s, cache, state, page_table, w_cmp, tbl_t, t_new):
    db, n_pages = page_table.shape
    n_phys, page = cache.shape[1], cache.shape[2]
    past = n_pages * page
    wb = state.shape[2]
    assert (past + t_new) // CMP * CMP <= past and past % SELB == 0 and wb == min(WIN, past)
    ncs = (past + t_new) // CMP
    ns = -(-(past + t_new) // SELB)
    lk, lw = past + LANE, wb + LANE
    rows_pp = page * 4 * NSA_KV
    cache2 = cache.reshape(cache.shape[0], n_phys, rows_pp, HD)
    state2 = state.reshape(state.shape[0], db, wb * 2 * NSA_KV, HD)
    wc = w_cmp.reshape(2, CMP * HD, HD).astype(BF)
    big = 1 << 30
    bc = _rel_pattern(tbl_t, 0, NSA_H, t_new, ncs, past - (CMP - 1), 0, big, False, col_step=CMP)
    bs = _rel_pattern(tbl_t, 0, NSA_H, t_new, lk, past, 0, big, False, c_lim=past + t_new)
    bw = _rel_pattern(tbl_t, 0, NSA_H, t_new, lw, wb, 0, WIN, False, c_lim=wb + t_new)
    e_mat = _expand_matrix(0, lk)
    s_mat = _pair_sum_matrix(ncs)
    full = lambda shape: pl.BlockSpec(shape, lambda b, pt: (0,) * len(shape))
    page_specs = [pl.BlockSpec((None, None, rows_pp, HD), lambda b, pt, p=p: (0, pt[b, p], 0, 0))
                  for p in range(n_pages)]
    in_specs = page_specs + [
        pl.BlockSpec((t_new, NSA_H * HD), lambda b, pt: (b, 0)),
        pl.BlockSpec((t_new, NSA_KV * LANE), lambda b, pt: (b, 0)),
        pl.BlockSpec((t_new, 4 * NSA_KV * HD), lambda b, pt: (b, 0)),
        pl.BlockSpec((t_new, 2 * NSA_KV * HD), lambda b, pt: (b, 0)),
        pl.BlockSpec((None, None, wb * 2 * NSA_KV, HD), lambda b, pt: (0, b, 0, 0)),
        full((2, CMP * HD, HD)), full((NSA_H, t_new, ncs)), full((NSA_H, t_new, lk)),
        full((NSA_H, t_new, lw)), full((LANE, lk)), full((ncs, LANE)),
    ]
    return pl.pallas_call(
        functools.partial(_nsa_sample_kernel, n_pages=n_pages, page=page, past=past, t_new=t_new,
                          ncs=ncs, ns=ns, wb=wb),
        grid_spec=pltpu.PrefetchScalarGridSpec(
            num_scalar_prefetch=1, grid=(db,), in_specs=in_specs,
            out_specs=pl.BlockSpec((t_new, NSA_H * HD), lambda b, pt: (b, 0)),
            scratch_shapes=[pltpu.VMEM((2, NSA_KV, ncs * CMP_PITCH, HD), F32),
                            pltpu.VMEM((2, NSA_KV, lk, HD), BF),
                            pltpu.VMEM((2, NSA_KV, lw, HD), BF)]),
        out_shape=jax.ShapeDtypeStruct((db * t_new, NSA_H * HD), F32),
        compiler_params=_cparams(("arbitrary",)),
        name="nsa_sample",
    )(page_table, *([cache2] * n_pages), q_s, gates_s, kv4_s, win_s, state2, wc, bc, bs, bw, e_mat, s_mat)


def _diff_sample_kernel(pt_ref, *refs, n_pages, page, past, t_new, lam_init):
    pages = refs[:n_pages]
    q_ref, kvn_ref, b_ref, dl_ref, sub_ref, o_ref, kv_s, stage_s = refs[n_pages:]
    del pt_ref
    rows = DF_G * t_new
    lk = past + LANE
    for p in range(n_pages):
        for h in range(DF_KV):
            stage_s[h] = pages[p][:, h, :]
            kv_s[h, p * page:(p + 1) * page, :] = stage_s[h].astype(BF)
    kvn = kvn_ref[...]
    for h in range(DF_KV):
        kv_s[h, past:past + LANE, :] = _tail_tile(kvn[:, h * 4 * HD:(h + 1) * 4 * HD], 4 * HD)
    lam = _diff_lambda(dl_ref[...], lam_init)
    q = q_ref[...]
    for h in range(DF_KV):
        bias = b_ref[h * DF_G:(h + 1) * DF_G].reshape(rows, lk)
        ps = []
        for m in range(2):
            qm = jnp.concatenate(
                [q[:, ((h * DF_G + g) * 2 + m) * HD:((h * DF_G + g) * 2 + m + 1) * HD] for g in range(DF_G)],
                axis=0).astype(BF)
            ps.append(_softmax_rows(_dot_nt(qm, kv_s[h, :, m * HD:(m + 1) * HD]) + bias))
        a = ps[0] - lam * ps[1]
        out = _diff_finish(_dot(a.astype(BF), kv_s[h, :, 2 * HD:4 * HD]), sub_ref[...], lam_init)
        for g in range(DF_G):
            o_ref[:, (h * DF_G + g) * DF_DV:(h * DF_G + g + 1) * DF_DV] = out[g * t_new:(g + 1) * t_new]


def _diff_sample(qd_s, dkv_s, cache, page_table, tbl_t, dl, sub, t_new, lam_init):
    db, n_pages = page_table.shape
    page = cache.shape[2]
    past = n_pages * page
    lk = past + LANE
    bias = _rel_pattern(tbl_t, NSA_H, DF_H, t_new, lk, past, 0, 1 << 30, False, c_lim=past + t_new)
    full = lambda shape: pl.BlockSpec(shape, lambda b, pt: (0,) * len(shape))
    page_specs = [pl.BlockSpec((None, None, page, DF_KV, 4 * HD), lambda b, pt, p=p: (0, pt[b, p], 0, 0, 0))
                  for p in range(n_pages)]
    in_specs = page_specs + [
        pl.BlockSpec((t_new, DF_H * 2 * HD), lambda b, pt: (b, 0)),
        pl.BlockSpec((t_new, DF_KV * 4 * HD), lambda b, pt: (b, 0)),
        full((DF_H, t_new, lk)), full((4, HD)), full((1, DF_DV)),
    ]
    return pl.pallas_call(
        functools.partial(_diff_sample_kernel, n_pages=n_pages, page=page, past=past, t_new=t_new,
                          lam_init=lam_init),
        grid_spec=pltpu.PrefetchScalarGridSpec(
            num_scalar_prefetch=1, grid=(db,), in_specs=in_specs,
            out_specs=pl.BlockSpec((t_new, DF_H * DF_DV), lambda b, pt: (b, 0)),
            scratch_shapes=[pltpu.VMEM((DF_KV, lk, 4 * HD), BF), pltpu.VMEM((DF_KV, page, 4 * HD), F32)]),
        out_shape=jax.ShapeDtypeStruct((db * t_new, DF_H * DF_DV), F32),
        compiler_params=_cparams(("arbitrary",)),
        name="diff_sample",
    )(page_table, *([cache] * n_pages), qd_s, dkv_s, bias, dl, sub.reshape(1, DF_DV))


def _router_kernel(h_ref, g_ref, wr_ref, br_ref, xn_ref, ids_ref, wts_ref):
    x = h_ref[...]
    ms = jnp.mean(x * x, axis=-1, keepdims=True)
    xn = x * lax.rsqrt(ms + EPS) * g_ref[...]
    xh = xn.astype(BF)
    xn_ref[...] = xh
    xl = (xn - xh.astype(F32)).astype(BF)
    wr = wr_ref[...]
    wh = wr.astype(BF)
    wl = (wr - wh.astype(F32)).astype(BF)
    lg = _dot(xh, wh) + _dot(xl, wh) + _dot(xh, wl) + br_ref[...]
    lane_i = lax.broadcasted_iota(jnp.int32, lg.shape, 1)
    lane = lane_i.astype(F32)
    big = 1000.0
    isg = lane_i < N_GROUPS
    gmax = jnp.max(jnp.where(isg, lg, -3e38), axis=-1, keepdims=True)
    gsel = jnp.min(jnp.where(isg & (lg == gmax), lane, big), axis=-1, keepdims=True)
    gw = 1.0 / jnp.sum(jnp.where(isg, jnp.exp(lg - gmax), 0.0), axis=-1, keepdims=True)
    lo = N_GROUPS + gsel * EPG
    ing = (lane >= lo) & (lane < lo + EPG)
    emax = jnp.max(jnp.where(ing, lg, -3e38), axis=-1, keepdims=True)
    pe = jnp.where(ing, jnp.exp(lg - emax), 0.0)
    pr = jnp.where(ing, pe / jnp.sum(pe, axis=-1, keepdims=True), -1.0)
    v1 = jnp.max(pr, axis=-1, keepdims=True)
    i1 = jnp.min(jnp.where(pr == v1, lane, big), axis=-1, keepdims=True)
    pr2 = jnp.where(lane == i1, -1.0, pr)
    v2 = jnp.max(pr2, axis=-1, keepdims=True)
    i2 = jnp.min(jnp.where(pr2 == v2, lane, big), axis=-1, keepdims=True)
    den = v1 + v2
    e12 = jnp.where(lane_i == 0, i1 - N_GROUPS, jnp.where(lane_i == 1, i2 - N_GROUPS, 0.0))
    ids_ref[...] = e12.astype(jnp.int32)
    wts_ref[...] = jnp.where(lane_i == 0, v1 / den * gw, jnp.where(lane_i == 1, v2 / den * gw, 0.0))


def _router(h, g, wr, br, tm):
    m, d = h.shape
    return pl.pallas_call(
        _router_kernel,
        grid=(m // tm,),
        in_specs=[pl.BlockSpec((tm, d), lambda i: (i, 0)), pl.BlockSpec((1, d), lambda i: (0, 0)),
                  pl.BlockSpec((d, LANE), lambda i: (0, 0)), pl.BlockSpec((1, LANE), lambda i: (0, 0))],
        out_specs=[pl.BlockSpec((tm, d), lambda i: (i, 0)), pl.BlockSpec((tm, LANE), lambda i: (i, 0)),
                   pl.BlockSpec((tm, LANE), lambda i: (i, 0))],
        out_shape=[jax.ShapeDtypeStruct((m, d), BF), jax.ShapeDtypeStruct((m, LANE), jnp.int32),
                   jax.ShapeDtypeStruct((m, LANE), F32)],
        compiler_params=_cparams(("arbitrary",)),
        name="router",
    )(h, g.reshape(1, d), wr, br)


def _split_dot(x_ref, w_refs):
    rows = w_refs[0].shape[0]
    acc = _dot(x_ref[:, 0:rows], w_refs[0][...].astype(BF))
    for s in range(1, len(w_refs)):
        acc = acc + _dot(x_ref[:, s * rows:(s + 1) * rows], w_refs[s][...].astype(BF))
    return acc


def _moe_up_kernel(te_ref, txb_ref, nt_ref, *refs, nk, n_split, nxb):
    xb = refs[:nxb]
    wg = refs[nxb:nxb + n_split]
    wu = refs[nxb + n_split:nxb + 2 * n_split]
    o_ref, x_s, a_s, u_s = refs[nxb + 2 * n_split:]
    k = pl.program_id(1)
    xg = xb[0].shape[0]

    @pl.when(pl.program_id(0) < nt_ref[0])
    def _():
        @pl.when(k == 0)
        def _():
            a_s[...] = jnp.zeros(a_s.shape, F32)
            u_s[...] = jnp.zeros(u_s.shape, F32)

        for j in range(nxb):
            x_s[j * xg:(j + 1) * xg, :] = xb[j][...]
        a_s[...] += _split_dot(x_s, wg)
        u_s[...] += _split_dot(x_s, wu)

        @pl.when(k == nk - 1)
        def _():
            g = a_s[...]
            o_ref[...] = (g * jax.nn.sigmoid(g) * u_s[...]).astype(o_ref.dtype)


def _moe_down_kernel(te_ref, nt_ref, h_ref, *refs, n_split):
    o_ref = refs[n_split]

    @pl.when(pl.program_id(0) < nt_ref[0])
    def _():
        o_ref[...] = _split_dot(h_ref, refs[:n_split]).astype(o_ref.dtype)


def _moe(xn, ids, w_gate, w_up, w_down):
    m, d = xn.shape
    n_exp, _, ff = w_gate.shape
    tm = MOE_TM
    n_pairs = 2 * m
    xg = MOE_XBLK
    nxb = tm // xg
    n_tiles = n_exp + n_pairs // tm
    n_rows = n_tiles * tm
    n_xblk = n_exp + n_pairs // xg
    flat_e = ids[:, :2].reshape(-1)
    onehot = (flat_e[:, None] == jnp.arange(n_exp, dtype=jnp.int32)[None, :]).astype(jnp.int32)
    counts = jnp.sum(onehot, axis=0)
    tiles_per = (counts + tm - 1) // tm
    tile_end = jnp.cumsum(tiles_per)
    tile_start = tile_end - tiles_per
    xblk_per = (counts + xg - 1) // xg
    xblk_start = jnp.cumsum(xblk_per) - xblk_per
    rank = jnp.sum(onehot * (jnp.cumsum(onehot, axis=0) - 1), axis=1)
    pos = (jnp.sum(onehot * (tile_start * tm)[None, :], axis=1) + rank).reshape(m, 2)
    xpos = jnp.sum(onehot * (xblk_start * xg)[None, :], axis=1) + rank
    nt = tile_end[-1]
    tix = jnp.minimum(jnp.arange(n_tiles, dtype=jnp.int32), nt - 1)
    tile_e = jnp.minimum(jnp.searchsorted(tile_end, tix, side="right"), n_exp - 1).astype(jnp.int32)
    tile_xb = (xblk_start[tile_e] + (tix - tile_start[tile_e]) * nxb).astype(jnp.int32)
    row_tok = (jnp.arange(n_xblk * xg, dtype=jnp.int32) % m).at[xpos].set(
        jnp.arange(n_pairs, dtype=jnp.int32) // 2, mode="promise_in_bounds", unique_indices=True)
    xs = xn.at[row_tok].get(mode="promise_in_bounds")
    nt_arr = nt.reshape(1).astype(jnp.int32)

    tk = _pick(d, (1024, 512, 256, 128))
    nk = d // tk
    live = lambda t, ntr: t < ntr[0]
    clamp = lambda t, ntr: jnp.minimum(t, ntr[0] - 1)
    kidx = lambda t, k, ntr: jnp.where(live(t, ntr), k, nk - 1)
    ns_up = MOE_DMA_SPLIT if tk % (MOE_DMA_SPLIT * LANE) == 0 else 1
    xspecs = [pl.BlockSpec((xg, tk), lambda t, k, te, txb, ntr, j=j: (jnp.minimum(txb[t] + j, n_xblk - 1),
                                                                     kidx(t, k, ntr))) for j in range(nxb)]
    wspecs = [pl.BlockSpec((None, tk // ns_up, ff),
                           lambda t, k, te, txb, ntr, s=s: (te[t], kidx(t, k, ntr) * ns_up + s, 0))
              for s in range(ns_up)]
    hdn = pl.pallas_call(
        functools.partial(_moe_up_kernel, nk=nk, n_split=ns_up, nxb=nxb),
        grid_spec=pltpu.PrefetchScalarGridSpec(
            num_scalar_prefetch=3, grid=(n_tiles, nk),
            in_specs=xspecs + wspecs + wspecs,
            out_specs=pl.BlockSpec((tm, ff), lambda t, k, te, txb, ntr: (clamp(t, ntr), 0)),
            scratch_shapes=[pltpu.VMEM((tm, tk), BF), pltpu.VMEM((tm, ff), F32), pltpu.VMEM((tm, ff), F32)]),
        out_shape=jax.ShapeDtypeStruct((n_rows, ff), BF),
        compiler_params=_cparams(("arbitrary", "arbitrary")),
        name="moe_up",
    )(tile_e, tile_xb, nt_arr, *([xs] * nxb), *([w_gate] * ns_up), *([w_up] * ns_up))

    tn = _pick(d, (2048, 1024, 512, 256, 128))
    nn = d // tn
    ns_dn = MOE_DMA_SPLIT if ff % (MOE_DMA_SPLIT * LANE) == 0 else 1
    jidx = lambda t, j, ntr: jnp.where(live(t, ntr), j, nn - 1)
    y = pl.pallas_call(
        functools.partial(_moe_down_kernel, n_split=ns_dn),
        grid_spec=pltpu.PrefetchScalarGridSpec(
            num_scalar_prefetch=2, grid=(n_tiles, nn),
            in_specs=[pl.BlockSpec((tm, ff), lambda t, j, te, ntr: (clamp(t, ntr), 0))]
            + [pl.BlockSpec((None, ff // ns_dn, tn), lambda t, j, te, ntr, s=s: (te[t], s, jidx(t, j, ntr)))
               for s in range(ns_dn)],
            out_specs=pl.BlockSpec((tm, tn), lambda t, j, te, ntr: (clamp(t, ntr), jidx(t, j, ntr)))),
        out_shape=jax.ShapeDtypeStruct((n_rows, d), BF),
        compiler_params=_cparams(("arbitrary", "arbitrary")),
        name="moe_down",
    )(tile_e, nt_arr, hdn, *([w_down] * ns_dn))
    return y, pos


def _split_cols_kernel(x_ref, o_ref, *, dims, width):
    for i in range(dims[0]):
        if len(dims) == 1:
            o_ref[:, i, :] = x_ref[:, i * width:(i + 1) * width]
        else:
            for j in range(dims[1]):
                c = (i * dims[1] + j) * width
                o_ref[:, i, j, :] = x_ref[:, c:c + width]


def _split_cols(x, batch, rows_per_batch, row0, dims, width, tb=256):
    seq = x.shape[0] // batch
    tb = _pick(math.gcd(rows_per_batch, math.gcd(row0, seq)) if row0 else math.gcd(rows_per_batch, seq),
               (tb, 128, 64, 32, 16, 8))
    nb = rows_per_batch // tb
    ncol = x.shape[1]
    zeros = (0,) * (len(dims) + 1)
    return pl.pallas_call(
        functools.partial(_split_cols_kernel, dims=dims, width=width),
        grid=(batch, nb),
        in_specs=[pl.BlockSpec((tb, ncol), lambda b, i: ((b * seq + row0) // tb + i, 0))],
        out_specs=pl.BlockSpec((None, tb) + tuple(dims) + (width,), lambda b, i: (b, i) + zeros),
        out_shape=jax.ShapeDtypeStruct((batch, rows_per_batch) + tuple(dims) + (width,), x.dtype),
        compiler_params=_cparams(("arbitrary", "arbitrary")),
        name="split_cols",
    )(x)


O_KV = NSA_H * HD
O_GATE = O_KV + 6 * NSA_KV * HD
O_QD = O_GATE + 3 * NSA_H
W_QD = DF_H * 2 * HD
W_KD = DF_KV * 2 * HD


def _gate_weights(w):
    d = w.shape[0]
    wg = w[:, O_GATE:O_QD].reshape(d, 3, NSA_KV, NSA_G).transpose(0, 2, 1, 3).reshape(d, NSA_KV, 3 * NSA_G)
    return jnp.pad(wg, ((0, 0), (0, 0), (0, LANE - 3 * NSA_G))).reshape(d, NSA_KV * LANE)


def kernel(x_prompt, x_sample, cache_nsa_kv, cache_diff_kv, state_nsa_win, page_table, p_prompt, p_sample,
           rel_bias_table, norm_mix, w_in, w_cmp, diff_lambda, diff_subln, w_out, norm_ffn, w_router_group,
           b_router_group, w_router_expert, b_router_expert, w_exp_gate, w_exp_up, w_exp_down, norm_ple,
           w_ple_gate, w_ple_proj, final_norm):
    assert norm_mix.shape[0] == 1, "single-layer trunk"
    batch, seq, d = x_prompt.shape
    db, t_new, _ = x_sample.shape
    mp, ms = batch * seq, db * t_new
    m = mp + ms
    lam_init = 0.8 - 0.6 * math.exp(-0.3 * 0)
    tm = _pick(math.gcd(mp, ms), (1024, 512, 256, 128))
    tm_s = _pick(math.gcd(mp, ms), (256, 128))
    xp = x_prompt.reshape(mp, d)
    xs = x_sample.reshape(ms, d)
    tbl_t = rel_bias_table

    xn = _rms2(xp, xs, norm_mix[0], tm_s)
    w0 = w_in[0]
    w_diff = w0[:, O_QD:]
    w_gate = _gate_weights(w0)
    scale = HD ** -0.5
    tn = 512
    wide = functools.partial(_matmul, [xn], tm=tm, tn=tn)
    q_nsa = wide([(w0, 0)], rows=m, n_cols=O_KV, out_dtype=BF, scale=scale, name="proj_qn")
    qd = wide([(w_diff, 0)], rows=m, n_cols=W_QD, out_dtype=BF, scale=scale, name="proj_qd")
    gates = _matmul([xn], [(w_gate, 0)], rows=m, n_cols=NSA_KV * LANE, tm=tm, tn=NSA_KV * LANE, out_dtype=F32,
                    name="proj_gate")
    kv_w = [(w0, 0, lambda j: j + O_KV // tn)]
    win_w = [(w0, 0, lambda j: j + (O_KV + 4 * NSA_KV * HD) // tn)]
    tn_d = DF_DV
    dkv_w = [(w_diff, 0, lambda j: jnp.where(j % 2 == 0, W_QD // tn_d + j // 2, (W_QD + W_KD) // tn_d + j // 2))]
    dkv = functools.partial(_matmul, [xn], dkv_w, tm=tm, tn=tn_d, n_cols=DF_KV * 4 * HD, out_dtype=F32)
    kv4_p = wide(kv_w, rows=mp, n_cols=4 * NSA_KV * HD, out_dtype=F32, name="proj_kv_p")
    kv4_s = wide(kv_w, rows=ms, row0=mp, n_cols=4 * NSA_KV * HD, out_dtype=F32, name="proj_kv_s")
    win_p = wide(win_w, rows=mp, n_cols=2 * NSA_KV * HD, out_dtype=F32, name="proj_win_p")
    win_s = wide(win_w, rows=ms, row0=mp, n_cols=2 * NSA_KV * HD, out_dtype=F32, name="proj_win_s")
    dkv_p = dkv(rows=mp, name="proj_dkv_p")
    dkv_s = dkv(rows=ms, row0=mp, name="proj_dkv_s")

    nc = seq // CMP
    kvr = kv4_p.reshape(batch, seq, 4, NSA_KV, HD)
    cmp_out = []
    for kind in range(2):
        a = kvr[:, :, kind].transpose(2, 0, 1, 3).reshape(NSA_KV * batch * nc, CMP * HD).astype(BF)
        r = a.shape[0]
        cmp_out.append(_matmul([a], [(w_cmp[0, kind].reshape(CMP * HD, HD), 0)], rows=r, n_cols=HD,
                               tm=_pick(r, (512, 256, 128, 64, 32, 16)), tn=HD, out_dtype=BF,
                               name="compress").reshape(NSA_KV, batch, nc, HD))
    o_n = _nsa_prompt(q_nsa, gates, cmp_out[0], cmp_out[1], kv4_p, win_p, tbl_t, batch, seq, m)
    o_d = _diff_prompt(qd, dkv_p, tbl_t, diff_lambda[0], diff_subln[0], batch, seq, m, lam_init)

    o_n_s = _nsa_sample(q_nsa[mp:].astype(F32), gates[mp:], kv4_s, win_s, cache_nsa_kv, state_nsa_win,
                        page_table, w_cmp[0], tbl_t, t_new)
    o_d_s = _diff_sample(qd[mp:].astype(F32), dkv_s, cache_diff_kv, page_table, tbl_t, diff_lambda[0],
                         diff_subln[0], t_new, lam_init)
    o_n = lax.dynamic_update_slice(o_n, o_n_s.astype(BF), (mp, 0))
    o_d = lax.dynamic_update_slice(o_d, o_d_s.astype(BF), (mp, 0))

    h1 = _matmul([o_n, o_d], [(w_out[0], 0), (w_out[0], 1)], rows=m, n_cols=d, tm=tm,
                 tn=_pick(d, (512, 256, 128)), out_dtype=F32, epi="res2", epi_args=(xp, xs), name="out_proj")

    wr = jnp.concatenate([w_router_group[0], w_router_expert[0],
                          jnp.zeros((d, LANE - N_GROUPS - N_EXP), F32)], axis=1)
    br = jnp.concatenate([b_router_group[0], b_router_expert[0],
                          jnp.zeros((LANE - N_GROUPS - N_EXP,), F32)]).reshape(1, LANE)
    xn2, ids, wts = _router(h1, norm_ffn[0], wr, br, tm_s)
    y, pos = _moe(xn2, ids, w_exp_gate[0], w_exp_up[0], w_exp_down[0])
    h2 = (h1 + wts[:, 0:1] * y.at[pos[:, 0]].get(mode="promise_in_bounds").astype(F32)
          + wts[:, 1:2] * y.at[pos[:, 1]].get(mode="promise_in_bounds").astype(F32))

    xn3 = _rms(h2, norm_ple[0], tm_s, BF)
    p_all = jnp.concatenate([p_prompt[0].reshape(mp, -1), p_sample[0].reshape(ms, -1)], axis=0).astype(BF)
    h3 = _matmul([xn3], [(w_ple_gate[0], 0)], rows=m, n_cols=d, tm=tm, tn=_pick(d, (512, 256, 128)),
                 out_dtype=F32, epi="ple", epi_args=(h2, p_all, w_ple_proj[0]), name="ple")

    y_p = _rms(h3, final_norm, tm_s, F32, row0=0, rows=mp).reshape(batch, seq, d)
    y_s = _rms(h3, final_norm, tm_s, F32, row0=mp, rows=ms).reshape(db, t_new, d)
    wk = min(WIN, seq)
    nsa_kv_p = _split_cols(kv4_p, batch, seq, 0, (4, NSA_KV), HD)
    diff_kv_p = _split_cols(dkv_p, batch, seq, 0, (DF_KV,), 4 * HD)
    win_p_out = _split_cols(win_p, batch, wk, seq - wk, (2, NSA_KV), HD)
    new_win = jnp.concatenate([state_nsa_win[0], win_s.reshape(db, t_new, 2, NSA_KV, HD)], axis=1)[:, t_new:]
    return (y_p, y_s,
            nsa_kv_p[None], kv4_s.reshape(1, db, t_new, 4, NSA_KV, HD),
            diff_kv_p[None], dkv_s.reshape(1, db, t_new, DF_KV, 4 * HD),
            win_p_out[None], new_win[None])
```

```python
import functools
import math

import numpy as np
import jax
import jax.numpy as jnp
from jax import lax
from jax.experimental import pallas as pl
from jax.experimental.pallas import tpu as pltpu

BF = jnp.bfloat16
F32 = jnp.float32

HD = 128
NSA_H = 16
NSA_KV = 2
NSA_G = NSA_H // NSA_KV
CMP = 32
SELB = 64
TOPK = 16
WIN = 512
DF_H = 8
DF_KV = 4
DF_G = DF_H // DF_KV
DF_DV = 2 * HD
REL_BUCKETS = 32
REL_MAX_DIST = 128
N_GROUPS = 4
EPG = 8
N_EXP = N_GROUPS * EPG
EPS = 1e-6
NEG = -1e30
FORCE = 1e4
LANE = 128
VMEM_LIMIT = 56 * 1024 * 1024
MOE_TM = 768
MOE_XBLK = 128
MOE_DMA_SPLIT = 4

QB_NSA = 128
PAD_SEL = 384
PAD_WIN = WIN
NEAR_SEL = 512
BAND_WIN = WIN + QB_NSA
QB_DF = 256
PAD_DF = 256
NEAR_DF = 512
TK = 256
HEADS_PER_PASS = 8
CMP_PITCH = 40


def _dot(a, b):
    return jnp.dot(a, b, preferred_element_type=F32)


def _dot_nt(a, b):
    return lax.dot_general(a, b, (((1,), (1,)), ((), ())), preferred_element_type=F32)


def _cparams(sem):
    return pltpu.CompilerParams(dimension_semantics=sem, vmem_limit_bytes=VMEM_LIMIT)


def _pick(n, cands):
    for c in cands:
        if n % c == 0:
            return c
    raise ValueError(f"no tile in {cands} divides {n}")


def _rms2_kernel(xp_ref, xs_ref, g_ref, o_ref, *, np_tiles):
    i = pl.program_id(0)

    def go(x_ref):
        x = x_ref[...]
        ms = jnp.mean(x * x, axis=-1, keepdims=True)
        o_ref[...] = (x * lax.rsqrt(ms + EPS) * g_ref[...]).astype(o_ref.dtype)

    @pl.when(i < np_tiles)
    def _():
        go(xp_ref)

    @pl.when(i >= np_tiles)
    def _():
        go(xs_ref)


def _rms2(xp, xs, g, tm):
    mp, d = xp.shape
    ms = xs.shape[0]
    npt, nst = mp // tm, ms // tm
    return pl.pallas_call(
        functools.partial(_rms2_kernel, np_tiles=npt),
        grid=(npt + nst,),
        in_specs=[pl.BlockSpec((tm, d), lambda i: (jnp.minimum(i, npt - 1), 0)),
                  pl.BlockSpec((tm, d), lambda i: (jnp.maximum(i - npt, 0), 0)),
                  pl.BlockSpec((1, d), lambda i: (0, 0))],
        out_specs=pl.BlockSpec((tm, d), lambda i: (i, 0)),
        out_shape=jax.ShapeDtypeStruct((mp + ms, d), BF),
        compiler_params=_cparams(("arbitrary",)),
        name="rms2",
    )(xp, xs, g.reshape(1, d))


def _rms_kernel(x_ref, g_ref, o_ref):
    x = x_ref[...]
    ms = jnp.mean(x * x, axis=-1, keepdims=True)
    o_ref[...] = (x * lax.rsqrt(ms + EPS) * g_ref[...]).astype(o_ref.dtype)


def _rms(x, g, tm, out_dtype, row0=0, rows=None):
    m, d = x.shape
    rows = m if rows is None else rows
    t0 = row0 // tm
    return pl.pallas_call(
        _rms_kernel,
        grid=(rows // tm,),
        in_specs=[pl.BlockSpec((tm, d), lambda i: (i + t0, 0)),
                  pl.BlockSpec((1, d), lambda i: (0, 0))],
        out_specs=pl.BlockSpec((tm, d), lambda i: (i, 0)),
        out_shape=jax.ShapeDtypeStruct((rows, d), out_dtype),
        compiler_params=_cparams(("arbitrary",)),
        name="rms",
    )(x, g.reshape(1, d))


def _combine_rms_kernel(h_ref, y0_ref, y1_ref, w_ref, g_ref, h2_ref, xn_ref):
    w = w_ref[...]
    h2 = h_ref[...] + w[:, 0:1] * y0_ref[...].astype(F32) + w[:, 1:2] * y1_ref[...].astype(F32)
    h2_ref[...] = h2
    ms = jnp.mean(h2 * h2, axis=-1, keepdims=True)
    xn_ref[...] = (h2 * lax.rsqrt(ms + EPS) * g_ref[...]).astype(xn_ref.dtype)


def _combine_rms(h, y0, y1, wts, g, tm):
    m, d = h.shape
    row = lambda width: pl.BlockSpec((tm, width), lambda i: (i, 0))
    return pl.pallas_call(
        _combine_rms_kernel,
        grid=(m // tm,),
        in_specs=[row(d), row(d), row(d), row(LANE), pl.BlockSpec((1, d), lambda i: (0, 0))],
        out_specs=[row(d), row(d)],
        out_shape=[jax.ShapeDtypeStruct((m, d), F32), jax.ShapeDtypeStruct((m, d), BF)],
        compiler_params=_cparams(("arbitrary",)),
        name="combine_rms",
    )(h, y0, y1, wts, g.reshape(1, d))


def _cast_rows(src_ref, dst_ref):
    k = src_ref.shape[0]
    ch = 256 if k % 256 == 0 else k

    def body(c, carry):
        r = pl.multiple_of(c * ch, ch)
        dst_ref[pl.ds(r, ch), :] = src_ref[pl.ds(r, ch), :].astype(BF)
        return carry

    lax.fori_loop(0, k // ch, body, 0)


def _mm_kernel(*refs, n_a, n_w, cast, epi, scale, np_tiles):
    a = refs[:n_a]
    w = refs[n_a:n_a + n_w]
    idx = n_a + n_w
    if epi == "res2":
        rp_ref, rs_ref = refs[idx:idx + 2]
        idx += 2
    elif epi == "ple":
        h_ref, p_ref, wp_ref = refs[idx:idx + 3]
        idx += 3
    o_ref = refs[idx]
    idx += 1
    wb = refs[idx:idx + n_w] if cast else w
    i = pl.program_id(1)

    if cast:
        @pl.when(i == 0)
        def _():
            for k in range(n_w):
                _cast_rows(w[k], wb[k])

    if epi == "cat":
        x = a[0][...]
        wd = wb[0].shape[1]
        for k in range(n_w):
            o_ref[:, k * wd:(k + 1) * wd] = _dot(x, wb[k][...]).astype(o_ref.dtype)
        return
    acc = _dot(a[0][...], wb[0][...])
    for k in range(1, n_a):
        acc = acc + _dot(a[k][...], wb[k][...])
    if scale is not None:
        acc = acc * scale
    if epi is None:
        o_ref[...] = acc.astype(o_ref.dtype)
    elif epi == "res2":
        @pl.when(i < np_tiles)
        def _():
            o_ref[...] = (acc + rp_ref[...]).astype(o_ref.dtype)

        @pl.when(i >= np_tiles)
        def _():
            o_ref[...] = (acc + rs_ref[...]).astype(o_ref.dtype)
    elif epi == "ple":
        gate = jax.nn.sigmoid(acc)
        proj = _dot(p_ref[...], wp_ref[...].astype(BF))
        o_ref[...] = (h_ref[...] + gate * proj).astype(o_ref.dtype)


def _matmul(a_list, w_list, *, rows, n_cols, tm, tn, out_dtype, row0=0, col0=0,
            scale=None, epi=None, epi_args=(), name="mm"):
    n_a, n_w = len(a_list), len(w_list)
    cast = w_list[0][0].dtype != BF
    t0, c0 = row0 // tm, col0 // tn
    gm, gn = rows // tm, n_cols // tn
    wn = tn // n_w if epi == "cat" else tn
    in_specs, args = [], []
    for a in a_list:
        in_specs.append(pl.BlockSpec((tm, a.shape[1]), lambda j, i: (i + t0, 0)))
        args.append(a)
    for k, ent in enumerate(w_list):
        w, kb = ent[0], ent[1]
        colfn = ent[2] if len(ent) > 2 else (lambda j: j + c0)
        kdim = a_list[min(k, n_a - 1)].shape[1]
        in_specs.append(pl.BlockSpec((kdim, wn), lambda j, i, kb=kb, colfn=colfn: (kb, colfn(j))))
        args.append(w)
    np_tiles = 0
    if epi == "res2":
        xp, xs = epi_args
        np_tiles = xp.shape[0] // tm
        in_specs.append(pl.BlockSpec((tm, tn), lambda j, i: (jnp.minimum(i, np_tiles - 1), j)))
        in_specs.append(pl.BlockSpec((tm, tn), lambda j, i: (jnp.maximum(i - np_tiles, 0), j)))
        args += [xp, xs]
    elif epi == "ple":
        h, p, wp = epi_args
        in_specs.append(pl.BlockSpec((tm, tn), lambda j, i: (i, j)))
        in_specs.append(pl.BlockSpec((tm, p.shape[1]), lambda j, i: (i, 0)))
        in_specs.append(pl.BlockSpec((wp.shape[0], tn), lambda j, i: (0, j)))
        args += [h, p, wp]
    scratch = [pltpu.VMEM((a_list[min(k, n_a - 1)].shape[1], wn), BF) for k in range(n_w)] if cast else []
    return pl.pallas_call(
        functools.partial(_mm_kernel, n_a=n_a, n_w=n_w, cast=cast, epi=epi, scale=scale, np_tiles=np_tiles),
        grid=(gn, gm),
        in_specs=in_specs,
        out_specs=pl.BlockSpec((tm, tn), lambda j, i: (i, j)),
        out_shape=jax.ShapeDtypeStruct((rows, n_cols), out_dtype),
        scratch_shapes=scratch,
        compiler_params=_cparams(("arbitrary", "arbitrary")),
        name=name,
    )(*args)


def _bucket_np(dist):
    n = np.maximum(dist, 0)
    max_exact = REL_BUCKETS // 2
    nf = np.maximum(n, 1).astype(np.float32)
    log_b = max_exact + (np.log(nf / np.float32(max_exact)) / np.float32(math.log(REL_MAX_DIST / max_exact))
                         * np.float32(REL_BUCKETS - max_exact)).astype(np.int32)
    return np.where(n < max_exact, n, np.minimum(log_b, REL_BUCKETS - 1)).astype(np.int32)


def _bucket_edges():
    b = _bucket_np(np.arange(0, 4 * REL_MAX_DIST))
    return [(k, int(np.nonzero(b == k)[0].max())) for k in range(REL_BUCKETS - 1) if (b == k).any()]


def _pattern_kernel(tbl_ref, o_ref, *, h0, base, col_step, v_lo, v_hi, c_lim, sub_far, edges, keys_on_rows):
    h = pl.program_id(0) + h0
    shape = o_ref.shape
    row = lax.broadcasted_iota(jnp.int32, shape, 0) + pl.program_id(1) * shape[0]
    col = lax.broadcasted_iota(jnp.int32, shape, 1)
    if keys_on_rows:
        row, col = col, row
    dist = row + base - col_step * col
    far = tbl_ref[REL_BUCKETS - 1, h]
    b = jnp.full(shape, far, F32)
    for k, hi in reversed(edges):
        b = jnp.where(dist <= hi, tbl_ref[k, h], b)
    if sub_far:
        b = b - far
    valid = (dist >= v_lo) & (dist <= v_hi) & (col < c_lim)
    o_ref[...] = jnp.where(valid, b, NEG)


def _rel_pattern(tbl, h0, nh, nrows, ncols, base, v_lo, v_hi, sub_far, col_step=1, c_lim=None,
                 keys_on_rows=False):
    out_r, out_c = (ncols, nrows) if keys_on_rows else (nrows, ncols)
    tr = _pick(out_r, (512, 256, 128, 64, 32, 16, 8))
    return pl.pallas_call(
        functools.partial(_pattern_kernel, h0=h0, base=base, col_step=col_step, v_lo=v_lo, v_hi=v_hi,
                          c_lim=ncols if c_lim is None else c_lim, sub_far=sub_far, edges=_bucket_edges(),
                          keys_on_rows=keys_on_rows),
        grid=(nh, out_r // tr),
        in_specs=[pl.BlockSpec(memory_space=pltpu.SMEM)],
        out_specs=pl.BlockSpec((None, tr, out_c), lambda h, r: (h, r, 0)),
        out_shape=jax.ShapeDtypeStruct((nh, out_r, out_c), F32),
        compiler_params=_cparams(("arbitrary", "arbitrary")),
        name="rel_bias",
    )(tbl)


def _pair_sum_matrix(nc):
    n = np.arange(nc)[:, None]
    b = np.arange(LANE)[None, :]
    return jnp.asarray((n // (SELB // CMP) == b).astype(np.float32), dtype=BF)


def _expand_matrix(pad, n_keys):
    l = np.arange(LANE)[:, None]
    c = np.arange(pad + n_keys)[None, :]
    return jnp.asarray(((c >= pad) & ((c - pad) // SELB == l)).astype(np.float32), dtype=BF)


def _split3(x):
    hi = x.astype(BF)
    r = x - hi.astype(F32)
    mid = r.astype(BF)
    lo = (r - mid.astype(F32)).astype(BF)
    return hi, mid, lo


def _select_blocks(psum, s_mat, qpos, ns):
    hi, mid, lo = _split3(psum)
    imp = _dot(hi, s_mat) + _dot(mid, s_mat) + _dot(lo, s_mat)
    shape = imp.shape
    lane = lax.broadcasted_iota(jnp.int32, shape, 1)
    valid = lane * SELB <= qpos
    cur = jnp.right_shift(qpos, 6)
    forced = (lane == 0) | (lane == cur) | (lane == cur - 1)
    score = jnp.where(valid, imp + jnp.where(forced, FORCE, 0.0), NEG)
    score = jnp.where(lane < ns, score, -3e38)
    cnt = jnp.zeros(shape, F32)
    for i in range(ns):
        ci = score[:, i:i + 1]
        cnt = cnt + jnp.where(lane > i, jnp.where(ci >= score, 1.0, 0.0), jnp.where(ci > score, 1.0, 0.0))
    sel = (cnt < float(min(TOPK, ns))) & (lane < ns)
    return jnp.where(sel, 1.0, 0.0).astype(BF)


def _softmax_rows(s, valid=None):
    m = jnp.max(s, axis=-1, keepdims=True)
    p = jnp.exp(s - m)
    if valid is not None:
        p = jnp.where(valid, p, 0.0)
    l = jnp.sum(p, axis=-1, keepdims=True)
    return p / jnp.where(l > 0.0, l, 1.0)


def _lanes(x, n):
    return x if n == LANE else jnp.concatenate([x] * (n // LANE), axis=1)


def _online(carry, s, vt, ones_in_v=False):
    m, l, acc = carry
    dv = acc.shape[1]
    m_new = jnp.maximum(m, jnp.max(s, axis=-1, keepdims=True))
    p = jnp.exp(s - _lanes(m_new, s.shape[1]))
    alpha = jnp.exp(m - m_new)
    pv = _dot(p.astype(BF), vt)
    if ones_in_v:
        l = alpha * l + pv[:, dv:dv + LANE]
        pv = pv[:, 0:dv]
    else:
        l = alpha * l + jnp.sum(p, axis=-1, keepdims=True)
    return m_new, l, _lanes(alpha, dv) * acc + pv


def _online_init(rows, dv):
    return jnp.full((rows, LANE), NEG, F32), jnp.zeros((rows, LANE), F32), jnp.zeros((rows, dv), F32)


def _flash_step(q, kt, vt, bias, m_ref, l_ref, acc_ref, rows, ones_in_v=False):
    s = _dot_nt(q, kt)
    if bias is not None:
        s = s + bias
    m_new, l, acc = _online((m_ref[rows, :], l_ref[rows, :], acc_ref[rows, :]), s, vt, ones_in_v)
    m_ref[rows, :] = m_new
    l_ref[rows, :] = l
    acc_ref[rows, :] = acc


def _nsa_prompt_kernel(q_ref, gate_ref, kc_ref, vc_ref, ks_ref, vs_ref, kw_ref, vw_ref,
                       bc_ref, pn_ref, pw_ref, e_ref, s_ref, o_ref,
                       ks_s, vs_s, kw_s, vw_s, m_s, l_s, acc_s, o_s, *, seq, nc, ns):
    i = pl.program_id(2)
    s0 = i * QB_NSA
    rows = NSA_G * QB_NSA

    @pl.when(i == 0)
    def _():
        ks_s[0:PAD_SEL, :] = jnp.zeros((PAD_SEL, HD), BF)
        vs_s[0:PAD_SEL, :] = jnp.zeros((PAD_SEL, 2 * HD), BF)
        kw_s[0:PAD_WIN, :] = jnp.zeros((PAD_WIN, HD), BF)
        vw_s[0:PAD_WIN, :] = jnp.zeros((PAD_WIN, 2 * HD), BF)
        ch = 512
        ones = jnp.ones((ch, HD), BF)

        def cp(c, carry):
            r = pl.multiple_of(c * ch, ch)
            ks_s[pl.ds(PAD_SEL + r, ch), :] = ks_ref[pl.ds(r, ch), :].astype(BF)
            vs_s[pl.ds(PAD_SEL + r, ch), 0:HD] = vs_ref[pl.ds(r, ch), :].astype(BF)
            vs_s[pl.ds(PAD_SEL + r, ch), HD:2 * HD] = ones
            kw_s[pl.ds(PAD_WIN + r, ch), :] = kw_ref[pl.ds(r, ch), :].astype(BF)
            vw_s[pl.ds(PAD_WIN + r, ch), 0:HD] = vw_ref[pl.ds(r, ch), :].astype(BF)
            vw_s[pl.ds(PAD_WIN + r, ch), HD:2 * HD] = ones
            return carry

        lax.fori_loop(0, seq // ch, cp, 0)

    gt = jax.nn.sigmoid(gate_ref[...])
    head = lambda g: slice(g * HD, (g + 1) * HD)
    hrows = lambda g: slice(g * QB_NSA, (g + 1) * QB_NSA)

    kc = kc_ref[...]
    vc = vc_ref[...]
    psum = jnp.zeros((QB_NSA, nc), F32)
    for g in range(NSA_G):
        bc = bc_ref[g]
        pc = _softmax_rows(_dot_nt(q_ref[:, head(g)], kc) + bc, bc > 0.5 * NEG)
        psum = psum + pc
        o_s[:, head(g)] = gt[:, g:g + 1] * _dot(pc.astype(BF), vc)
    qpos = s0 + lax.broadcasted_iota(jnp.int32, (QB_NSA, LANE), 0)
    selb = _select_blocks(psum, s_ref[...], qpos, ns)

    nch = jnp.maximum(i - 1, 0) // 2
    far_keys = nch * TK
    m_s[...] = jnp.full((rows, LANE), NEG, F32)
    l_s[...] = jnp.zeros((rows, LANE), F32)
    acc_s[...] = jnp.zeros((rows, HD), F32)

    def sel_chunk(r, bias_of, heads):
        kt = ks_s[pl.ds(r, TK), :]
        vt = vs_s[pl.ds(r, TK), :]
        madd = (_dot(selb, e_ref[:, pl.ds(r, TK)]) - 1.0) * (-NEG)
        for g in heads:
            _flash_step(q_ref[:, head(g)], kt, vt, bias_of(g, madd), m_s, l_s, acc_s, hrows(g), ones_in_v=True)

    for g0 in range(0, NSA_G, HEADS_PER_PASS):
        heads = range(g0, g0 + HEADS_PER_PASS)

        def far(c, carry, heads=heads):
            sel_chunk(pl.multiple_of(PAD_SEL + c * TK, LANE), lambda g, madd: madd, heads)
            return carry

        lax.fori_loop(0, nch, far, 0)
    for kh in range(NEAR_SEL // TK):
        col = lax.broadcasted_iota(jnp.int32, (QB_NSA, TK), 1) + kh * TK
        cut = jnp.where(col < far_keys - s0 + PAD_SEL, NEG, 0.0)
        sel_chunk(pl.multiple_of(s0 + kh * TK, LANE),
                  lambda g, madd, kh=kh, cut=cut: pn_ref[g, :, kh * TK:(kh + 1) * TK] + madd + cut, range(NSA_G))
    for g in range(NSA_G):
        osel = acc_s[hrows(g), :] / l_s[hrows(g), :]
        o_s[:, head(g)] = o_s[:, head(g)] + gt[:, NSA_G + g:NSA_G + g + 1] * osel

    for g in range(NSA_G):
        st = _online_init(QB_NSA, HD)
        for c0 in range(0, BAND_WIN, TK):
            w = min(TK, BAND_WIN - c0)
            r = pl.multiple_of(s0 + c0, LANE)
            colw = lax.broadcasted_iota(jnp.int32, (QB_NSA, w), 1) + c0
            bias = pw_ref[g, :, c0:c0 + w] + jnp.where(colw < PAD_WIN - s0, NEG, 0.0)
            st = _online(st, _dot_nt(q_ref[:, head(g)], kw_s[pl.ds(r, w), :]) + bias, vw_s[pl.ds(r, w), :],
                         ones_in_v=True)
        ow = st[2] / st[1]
        o_ref[:, head(g)] = (o_s[:, head(g)] + gt[:, 2 * NSA_G + g:2 * NSA_G + g + 1] * ow).astype(o_ref.dtype)


def _nsa_prompt(q_nsa, gates, kc, vc, kv4_p, win_p, tbl_t, batch, seq, m_total):
    nc = seq // CMP
    ns = -(-seq // SELB)
    nqb = seq // QB_NSA
    big = 1 << 30
    pn = _rel_pattern(tbl_t, 0, NSA_H, QB_NSA, NEAR_SEL, PAD_SEL, 0, big, True)
    pw = _rel_pattern(tbl_t, 0, NSA_H, QB_NSA, BAND_WIN, PAD_WIN, 0, WIN, False)
    bc = _rel_pattern(tbl_t, 0, NSA_H, seq, nc, -(CMP - 1), 0, big, False, col_step=CMP)
    e_mat = _expand_matrix(PAD_SEL, seq)
    s_mat = _pair_sum_matrix(nc)
    kv_spec = lambda col: pl.BlockSpec((seq, HD), lambda b, h, i, col=col: (b, col + h))
    return pl.pallas_call(
        functools.partial(_nsa_prompt_kernel, seq=seq, nc=nc, ns=ns),
        grid=(batch, NSA_KV, nqb),
        in_specs=[
            pl.BlockSpec((QB_NSA, NSA_G * HD), lambda b, h, i: (b * nqb + i, h)),
            pl.BlockSpec((QB_NSA, LANE), lambda b, h, i: (b * nqb + i, h)),
            pl.BlockSpec((None, None, nc, HD), lambda b, h, i: (h, b, 0, 0)),
            pl.BlockSpec((None, None, nc, HD), lambda b, h, i: (h, b, 0, 0)),
            kv_spec(2 * NSA_KV), kv_spec(3 * NSA_KV),
            pl.BlockSpec((seq, HD), lambda b, h, i: (b, h)),
            pl.BlockSpec((seq, HD), lambda b, h, i: (b, NSA_KV + h)),
            pl.BlockSpec((NSA_G, QB_NSA, nc), lambda b, h, i: (h, i, 0)),
            pl.BlockSpec((NSA_G, QB_NSA, NEAR_SEL), lambda b, h, i: (h, 0, 0)),
            pl.BlockSpec((NSA_G, QB_NSA, BAND_WIN), lambda b, h, i: (h, 0, 0)),
            pl.BlockSpec((LANE, PAD_SEL + seq), lambda b, h, i: (0, 0)),
            pl.BlockSpec((nc, LANE), lambda b, h, i: (0, 0)),
        ],
        out_specs=pl.BlockSpec((QB_NSA, NSA_G * HD), lambda b, h, i: (b * nqb + i, h)),
        out_shape=jax.ShapeDtypeStruct((m_total, NSA_H * HD), BF),
        scratch_shapes=[pltpu.VMEM((PAD_SEL + seq, HD), BF), pltpu.VMEM((PAD_SEL + seq, 2 * HD), BF),
                        pltpu.VMEM((PAD_WIN + seq, HD), BF), pltpu.VMEM((PAD_WIN + seq, 2 * HD), BF),
                        pltpu.VMEM((NSA_G * QB_NSA, LANE), F32), pltpu.VMEM((NSA_G * QB_NSA, LANE), F32),
                        pltpu.VMEM((NSA_G * QB_NSA, HD), F32), pltpu.VMEM((QB_NSA, NSA_G * HD), F32)],
        compiler_params=_cparams(("arbitrary", "arbitrary", "arbitrary")),
        name="nsa_prompt",
    )(q_nsa, gates, kc, vc, kv4_p, kv4_p, win_p, win_p, bc, pn, pw, e_mat, s_mat)


def _diff_lambda(dl, lam_init):
    a = jnp.sum(dl[0:1] * dl[1:2], axis=-1, keepdims=True)
    b = jnp.sum(dl[2:3] * dl[3:4], axis=-1, keepdims=True)
    return jnp.exp(a) - jnp.exp(b) + lam_init


def _diff_finish(a, sub, lam_init):
    ms = jnp.mean(a * a, axis=-1, keepdims=True)
    return a * lax.rsqrt(ms + EPS) * sub * (1.0 - lam_init)


def _diff_prompt_kernel(q_ref, kv_ref, pn_ref, dl_ref, sub_ref, o_ref, kv_s, m_s, l_s, acc_s, *, seq, lam_init):
    i = pl.program_id(2)
    s0 = i * QB_DF
    rows = DF_G * QB_DF

    @pl.when(i == 0)
    def _():
        kv_s[0:PAD_DF, :] = jnp.zeros((PAD_DF, 4 * HD), BF)
        ch = 256

        def cp(c, carry):
            r = pl.multiple_of(c * ch, ch)
            kv_s[pl.ds(PAD_DF + r, ch), :] = kv_ref[pl.ds(r, ch), :].astype(BF)
            return carry

        lax.fori_loop(0, seq // ch, cp, 0)

    lam = _diff_lambda(dl_ref[...], lam_init)
    nfar = jnp.maximum(i - 1, 0)
    sub_rows = 128
    n_sub = QB_DF // sub_rows
    streams = [(m, g, j) for m in range(2) for g in range(DF_G) for j in range(n_sub)]
    srows = lambda k: slice(k * sub_rows, (k + 1) * sub_rows)
    m_s[...] = jnp.full((2 * rows, LANE), NEG, F32)
    l_s[...] = jnp.zeros((2 * rows, LANE), F32)
    acc_s[...] = jnp.zeros((2 * rows, DF_DV), F32)

    def chunk(r, bias_of):
        vt = kv_s[pl.ds(r, TK), 2 * HD:4 * HD]
        for k, (m, g, j) in enumerate(streams):
            q = q_ref[j * sub_rows:(j + 1) * sub_rows, (g * 2 + m) * HD:(g * 2 + m + 1) * HD]
            kt = kv_s[pl.ds(r, TK), m * HD:(m + 1) * HD]
            _flash_step(q, kt, vt, bias_of(g, j), m_s, l_s, acc_s, srows(k))

    def far(c, carry):
        chunk(pl.multiple_of(PAD_DF + c * TK, TK), lambda g, j: None)
        return carry

    lax.fori_loop(0, nfar, far, 0)
    for kh in range(NEAR_DF // TK):
        col = lax.broadcasted_iota(jnp.int32, (sub_rows, TK), 1) + kh * TK
        cut = jnp.where(col < nfar * TK - s0 + PAD_DF, NEG, 0.0)
        chunk(pl.multiple_of(s0 + kh * TK, TK),
              lambda g, j, kh=kh, cut=cut: pn_ref[g, j * sub_rows:(j + 1) * sub_rows, kh * TK:(kh + 1) * TK] + cut)
    half = len(streams) // 2
    for k, (_, g, j) in enumerate(streams[:half]):
        o1 = acc_s[srows(k), :] / _lanes(l_s[srows(k), :], DF_DV)
        o2 = acc_s[srows(half + k), :] / _lanes(l_s[srows(half + k), :], DF_DV)
        out = _diff_finish(o1 - lam * o2, sub_ref[...], lam_init)
        o_ref[j * sub_rows:(j + 1) * sub_rows, g * DF_DV:(g + 1) * DF_DV] = out.astype(o_ref.dtype)


def _diff_prompt(qd, dkv_p, tbl_t, dl, sub, batch, seq, m_total, lam_init):
    nqb = seq // QB_DF
    pn = _rel_pattern(tbl_t, NSA_H, DF_H, QB_DF, NEAR_DF, PAD_DF, 0, 1 << 30, True)
    width = DF_G * 2 * HD
    return pl.pallas_call(
        functools.partial(_diff_prompt_kernel, seq=seq, lam_init=lam_init),
        grid=(batch, DF_KV, nqb),
        in_specs=[
            pl.BlockSpec((QB_DF, width), lambda b, h, i: (b * nqb + i, h)),
            pl.BlockSpec((seq, 4 * HD), lambda b, h, i: (b, h)),
            pl.BlockSpec((DF_G, QB_DF, NEAR_DF), lambda b, h, i: (h, 0, 0)),
            pl.BlockSpec((4, HD), lambda b, h, i: (0, 0)),
            pl.BlockSpec((1, DF_DV), lambda b, h, i: (0, 0)),
        ],
        out_specs=pl.BlockSpec((QB_DF, DF_G * DF_DV), lambda b, h, i: (b * nqb + i, h)),
        out_shape=jax.ShapeDtypeStruct((m_total, DF_H * DF_DV), BF),
        scratch_shapes=[pltpu.VMEM((PAD_DF + seq, 4 * HD), BF),
                        pltpu.VMEM((2 * DF_G * QB_DF, LANE), F32), pltpu.VMEM((2 * DF_G * QB_DF, LANE), F32),
                        pltpu.VMEM((2 * DF_G * QB_DF, DF_DV), F32)],
        compiler_params=_cparams(("arbitrary", "arbitrary", "arbitrary")),
        name="diff_prompt",
    )(qd, dkv_p, pn, dl, sub.reshape(1, DF_DV))


def _tail_tile(new, width):
    t = new.shape[0]
    return jnp.concatenate([new, jnp.zeros((LANE - t, width), F32)], axis=0).astype(BF)


def _nsa_sample_kernel(pt_ref, *refs, n_pages, page, past, t_new, ncs, ns, wb):
    pages = refs[:n_pages]
    (q_ref, gate_ref, kvn_ref, wn_ref, st_ref, wc_ref, bc_ref, bs_ref, bw_ref, e_ref, s_ref,
     o_ref, kcmp_s, ksel_s, kwin_s) = refs[n_pages:]
    del pt_ref
    rows = NSA_G * t_new
    n_kinds = 4

    for p in range(n_pages):
        for kind in range(n_kinds):
            for h in range(NSA_KV):
                blk = pages[p][pl.ds(kind * NSA_KV + h, page, stride=n_kinds * NSA_KV), :]
                if kind < 2:
                    for nb in range(page // CMP):
                        r0 = (p * (page // CMP) + nb) * CMP_PITCH
                        kcmp_s[kind, h, r0:r0 + CMP, :] = blk[nb * CMP:(nb + 1) * CMP]
                else:
                    ksel_s[kind - 2, h, p * page:(p + 1) * page, :] = blk.astype(BF)
    kvn = kvn_ref[...]
    wn = wn_ref[...]
    for kind in range(2):
        for h in range(NSA_KV):
            c0 = ((kind + 2) * NSA_KV + h) * HD
            ksel_s[kind, h, past:past + LANE, :] = _tail_tile(kvn[:, c0:c0 + HD], HD)
            kwin_s[kind, h, 0:wb, :] = st_ref[pl.ds(kind * NSA_KV + h, wb, stride=2 * NSA_KV), :].astype(BF)
            c0 = (kind * NSA_KV + h) * HD
            kwin_s[kind, h, wb:wb + LANE, :] = _tail_tile(wn[:, c0:c0 + HD], HD)

    q = q_ref[...]
    gt = jax.nn.sigmoid(gate_ref[...])
    qpos = past + lax.broadcasted_iota(jnp.int32, (t_new, LANE), 0)
    for h in range(NSA_KV):
        cmp = []
        for kind in range(2):
            acc = jnp.zeros((ncs, HD), F32)
            for j in range(CMP):
                kj = kcmp_s[kind, h, pl.ds(j, ncs, stride=CMP_PITCH), :]
                acc = acc + _dot(kj.astype(BF), wc_ref[kind, j * HD:(j + 1) * HD, :])
            cmp.append(acc.astype(BF))
        kc, vc = cmp
        qs = jnp.concatenate([q[:, (h * NSA_G + g) * HD:(h * NSA_G + g + 1) * HD] for g in range(NSA_G)],
                             axis=0).astype(BF)
        bc = bc_ref[h * NSA_G:(h + 1) * NSA_G].reshape(rows, ncs)
        pc = _softmax_rows(_dot_nt(qs, kc) + bc, bc > 0.5 * NEG)
        oc = _dot(pc.astype(BF), vc)
        psum = pc[0:t_new]
        for g in range(1, NSA_G):
            psum = psum + pc[g * t_new:(g + 1) * t_new]
        selb = _select_blocks(psum, s_ref[...], qpos, ns)
        lk = past + LANE
        madd = (_dot(selb, e_ref[...]) - 1.0) * (-NEG)
        s = _dot_nt(qs, ksel_s[0, h])
        s = (s.reshape(NSA_G, t_new, lk) + bs_ref[h * NSA_G:(h + 1) * NSA_G] + madd[None]).reshape(rows, lk)
        osel = _dot(_softmax_rows(s).astype(BF), ksel_s[1, h])
        lw = wb + LANE
        s = _dot_nt(qs, kwin_s[0, h]) + bw_ref[h * NSA_G:(h + 1) * NSA_G].reshape(rows, lw)
        ow = _dot(_softmax_rows(s).astype(BF), kwin_s[1, h])
        for g in range(NSA_G):
            sl = slice(g * t_new, (g + 1) * t_new)
            gl = h * LANE + g
            o = (gt[:, gl:gl + 1] * oc[sl] + gt[:, gl + NSA_G:gl + NSA_G + 1] * osel[sl]
                 + gt[:, gl + 2 * NSA_G:gl + 2 * NSA_G + 1] * ow[sl])
            o_ref[:, (h * NSA_G + g) * HD:(h * NSA_G + g + 1) * HD] = o


def _nsa_sample(q_s, gates_s, kv4_s, win_s, cache, state, page_table, w_cmp, tbl_t, t_new):
    db, n_pages = page_table.shape
    n_phys, page = cache.shape[1], cache.shape[2]
    past = n_pages * page
    wb = state.shape[2]
    assert (past + t_new) // CMP * CMP <= past and past % SELB == 0 and wb == min(WIN, past)
    ncs = (past + t_new) // CMP
    ns = -(-(past + t_new) // SELB)
    lk, lw = past + LANE, wb + LANE
    rows_pp = page * 4 * NSA_KV
    cache2 = cache.reshape(cache.shape[0], n_phys, rows_pp, HD)
    state2 = state.reshape(state.shape[0], db, wb * 2 * NSA_KV, HD)
    wc = w_cmp.reshape(2, CMP * HD, HD).astype(BF)
    big = 1 << 30
    bc = _rel_pattern(tbl_t, 0, NSA_H, t_new, ncs, past - (CMP - 1), 0, big, False, col_step=CMP)
    bs = _rel_pattern(tbl_t, 0, NSA_H, t_new, lk, past, 0, big, False, c_lim=past + t_new)
    bw = _rel_pattern(tbl_t, 0, NSA_H, t_new, lw, wb, 0, WIN, False, c_lim=wb + t_new)
    e_mat = _expand_matrix(0, lk)
    s_mat = _pair_sum_matrix(ncs)
    full = lambda shape: pl.BlockSpec(shape, lambda b, pt: (0,) * len(shape))
    page_specs = [pl.BlockSpec((None, None, rows_pp, HD), lambda b, pt, p=p: (0, pt[b, p], 0, 0))
                  for p in range(n_pages)]
    in_specs = page_specs + [
        pl.BlockSpec((t_new, NSA_H * HD), lambda b, pt: (b, 0)),
        pl.BlockSpec((t_new, NSA_KV * LANE), lambda b, pt: (b, 0)),
        pl.BlockSpec((t_new, 4 * NSA_KV * HD), lambda b, pt: (b, 0)),
        pl.BlockSpec((t_new, 2 * NSA_KV * HD), lambda b, pt: (b, 0)),
        pl.BlockSpec((None, None, wb * 2 * NSA_KV, HD), lambda b, pt: (0, b, 0, 0)),
        full((2, CMP * HD, HD)), full((NSA_H, t_new, ncs)), full((NSA_H, t_new, lk)),
        full((NSA_H, t_new, lw)), full((LANE, lk)), full((ncs, LANE)),
    ]
    return pl.pallas_call(
        functools.partial(_nsa_sample_kernel, n_pages=n_pages, page=page, past=past, t_new=t_new,
                          ncs=ncs, ns=ns, wb=wb),
        grid_spec=pltpu.PrefetchScalarGridSpec(
            num_scalar_prefetch=1, grid=(db,), in_specs=in_specs,
            out_specs=pl.BlockSpec((t_new, NSA_H * HD), lambda b, pt: (b, 0)),
            scratch_shapes=[pltpu.VMEM((2, NSA_KV, ncs * CMP_PITCH, HD), F32),
                            pltpu.VMEM((2, NSA_KV, lk, HD), BF),
                            pltpu.VMEM((2, NSA_KV, lw, HD), BF)]),
        out_shape=jax.ShapeDtypeStruct((db * t_new, NSA_H * HD), F32),
        compiler_params=_cparams(("arbitrary",)),
        name="nsa_sample",
    )(page_table, *([cache2] * n_pages), q_s, gates_s, kv4_s, win_s, state2, wc, bc, bs, bw, e_mat, s_mat)


def _diff_sample_kernel(pt_ref, *refs, n_pages, page, past, t_new, lam_init):
    pages = refs[:n_pages]
    q_ref, kvn_ref, b_ref, dl_ref, sub_ref, o_ref, kv_s, stage_s = refs[n_pages:]
    del pt_ref
    rows = DF_G * t_new
    lk = past + LANE
    for p in range(n_pages):
        for h in range(DF_KV):
            stage_s[h] = pages[p][:, h, :]
            kv_s[h, p * page:(p + 1) * page, :] = stage_s[h].astype(BF)
    kvn = kvn_ref[...]
    for h in range(DF_KV):
        kv_s[h, past:past + LANE, :] = _tail_tile(kvn[:, h * 4 * HD:(h + 1) * 4 * HD], 4 * HD)
    lam = _diff_lambda(dl_ref[...], lam_init)
    q = q_ref[...]
    for h in range(DF_KV):
        bias = b_ref[h * DF_G:(h + 1) * DF_G].reshape(rows, lk)
        ps = []
        for m in range(2):
            qm = jnp.concatenate(
                [q[:, ((h * DF_G + g) * 2 + m) * HD:((h * DF_G + g) * 2 + m + 1) * HD] for g in range(DF_G)],
                axis=0).astype(BF)
            ps.append(_softmax_rows(_dot_nt(qm, kv_s[h, :, m * HD:(m + 1) * HD]) + bias))
        a = ps[0] - lam * ps[1]
        out = _diff_finish(_dot(a.astype(BF), kv_s[h, :, 2 * HD:4 * HD]), sub_ref[...], lam_init)
        for g in range(DF_G):
            o_ref[:, (h * DF_G + g) * DF_DV:(h * DF_G + g + 1) * DF_DV] = out[g * t_new:(g + 1) * t_new]


def _diff_sample(qd_s, dkv_s, cache, page_table, tbl_t, dl, sub, t_new, lam_init):
    db, n_pages = page_table.shape
    page = cache.shape[2]
    past = n_pages * page
    lk = past + LANE
    bias = _rel_pattern(tbl_t, NSA_H, DF_H, t_new, lk, past, 0, 1 << 30, False, c_lim=past + t_new)
    full = lambda shape: pl.BlockSpec(shape, lambda b, pt: (0,) * len(shape))
    page_specs = [pl.BlockSpec((None, None, page, DF_KV, 4 * HD), lambda b, pt, p=p: (0, pt[b, p], 0, 0, 0))
                  for p in range(n_pages)]
    in_specs = page_specs + [
        pl.BlockSpec((t_new, DF_H * 2 * HD), lambda b, pt: (b, 0)),
        pl.BlockSpec((t_new, DF_KV * 4 * HD), lambda b, pt: (b, 0)),
        full((DF_H, t_new, lk)), full((4, HD)), full((1, DF_DV)),
    ]
    return pl.pallas_call(
        functools.partial(_diff_sample_kernel, n_pages=n_pages, page=page, past=past, t_new=t_new,
                          lam_init=lam_init),
        grid_spec=pltpu.PrefetchScalarGridSpec(
            num_scalar_prefetch=1, grid=(db,), in_specs=in_specs,
            out_specs=pl.BlockSpec((t_new, DF_H * DF_DV), lambda b, pt: (b, 0)),
            scratch_shapes=[pltpu.VMEM((DF_KV, lk, 4 * HD), BF), pltpu.VMEM((DF_KV, page, 4 * HD), F32)]),
        out_shape=jax.ShapeDtypeStruct((db * t_new, DF_H * DF_DV), F32),
        compiler_params=_cparams(("arbitrary",)),
        name="diff_sample",
    )(page_table, *([cache] * n_pages), qd_s, dkv_s, bias, dl, sub.reshape(1, DF_DV))


def _router_kernel(h_ref, g_ref, wr_ref, br_ref, xn_ref, ids_ref, wts_ref):
    x = h_ref[...]
    ms = jnp.mean(x * x, axis=-1, keepdims=True)
    xn = x * lax.rsqrt(ms + EPS) * g_ref[...]
    xh = xn.astype(BF)
    xn_ref[...] = xh
    xl = (xn - xh.astype(F32)).astype(BF)
    wr = wr_ref[...]
    wh = wr.astype(BF)
    wl = (wr - wh.astype(F32)).astype(BF)
    lg = _dot(xh, wh) + _dot(xl, wh) + _dot(xh, wl) + br_ref[...]
    lane_i = lax.broadcasted_iota(jnp.int32, lg.shape, 1)
    lane = lane_i.astype(F32)
    big = 1000.0
    isg = lane_i < N_GROUPS
    gmax = jnp.max(jnp.where(isg, lg, -3e38), axis=-1, keepdims=True)
    gsel = jnp.min(jnp.where(isg & (lg == gmax), lane, big), axis=-1, keepdims=True)
    gw = 1.0 / jnp.sum(jnp.where(isg, jnp.exp(lg - gmax), 0.0), axis=-1, keepdims=True)
    lo = N_GROUPS + gsel * EPG
    ing = (lane >= lo) & (lane < lo + EPG)
    emax = jnp.max(jnp.where(ing, lg, -3e38), axis=-1, keepdims=True)
    pe = jnp.where(ing, jnp.exp(lg - emax), 0.0)
    pr = jnp.where(ing, pe / jnp.sum(pe, axis=-1, keepdims=True), -1.0)
    v1 = jnp.max(pr, axis=-1, keepdims=True)
    i1 = jnp.min(jnp.where(pr == v1, lane, big), axis=-1, keepdims=True)
    pr2 = jnp.where(lane == i1, -1.0, pr)
    v2 = jnp.max(pr2, axis=-1, keepdims=True)
    i2 = jnp.min(jnp.where(pr2 == v2, lane, big), axis=-1, keepdims=True)
    den = v1 + v2
    e12 = jnp.where(lane_i == 0, i1 - N_GROUPS, jnp.where(lane_i == 1, i2 - N_GROUPS, 0.0))
    ids_ref[...] = e12.astype(jnp.int32)
    wts_ref[...] = jnp.where(lane_i == 0, v1 / den * gw, jnp.where(lane_i == 1, v2 / den * gw, 0.0))


def _router(h, g, wr, br, tm):
    m, d = h.shape
    return pl.pallas_call(
        _router_kernel,
        grid=(m // tm,),
        in_specs=[pl.BlockSpec((tm, d), lambda i: (i, 0)), pl.BlockSpec((1, d), lambda i: (0, 0)),
                  pl.BlockSpec((d, LANE), lambda i: (0, 0)), pl.BlockSpec((1, LANE), lambda i: (0, 0))],
        out_specs=[pl.BlockSpec((tm, d), lambda i: (i, 0)), pl.BlockSpec((tm, LANE), lambda i: (i, 0)),
                   pl.BlockSpec((tm, LANE), lambda i: (i, 0))],
        out_shape=[jax.ShapeDtypeStruct((m, d), BF), jax.ShapeDtypeStruct((m, LANE), jnp.int32),
                   jax.ShapeDtypeStruct((m, LANE), F32)],
        compiler_params=_cparams(("arbitrary",)),
        name="router",
    )(h, g.reshape(1, d), wr, br)


def _split_dot(x_ref, w_refs):
    rows = w_refs[0].shape[0]
    acc = _dot(x_ref[:, 0:rows], w_refs[0][...].astype(BF))
    for s in range(1, len(w_refs)):
        acc = acc + _dot(x_ref[:, s * rows:(s + 1) * rows], w_refs[s][...].astype(BF))
    return acc


def _moe_up_kernel(te_ref, txb_ref, nt_ref, *refs, nk, n_split, nxb):
    xb = refs[:nxb]
    wg = refs[nxb:nxb + n_split]
    wu = refs[nxb + n_split:nxb + 2 * n_split]
    o_ref, x_s, a_s, u_s = refs[nxb + 2 * n_split:]
    k = pl.program_id(1)
    xg = xb[0].shape[0]

    @pl.when(pl.program_id(0) < nt_ref[0])
    def _():
        @pl.when(k == 0)
        def _():
            a_s[...] = jnp.zeros(a_s.shape, F32)
            u_s[...] = jnp.zeros(u_s.shape, F32)

        for j in range(nxb):
            x_s[j * xg:(j + 1) * xg, :] = xb[j][...]
        a_s[...] += _split_dot(x_s, wg)
        u_s[...] += _split_dot(x_s, wu)

        @pl.when(k == nk - 1)
        def _():
            g = a_s[...]
            o_ref[...] = (g * jax.nn.sigmoid(g) * u_s[...]).astype(o_ref.dtype)


def _moe_down_kernel(te_ref, nt_ref, h_ref, *refs, n_split):
    o_ref = refs[n_split]

    @pl.when(pl.program_id(0) < nt_ref[0])
    def _():
        o_ref[...] = _split_dot(h_ref, refs[:n_split]).astype(o_ref.dtype)


def _moe(xn, ids, w_gate, w_up, w_down):
    m, d = xn.shape
    n_exp, _, ff = w_gate.shape
    tm = MOE_TM
    n_pairs = 2 * m
    xg = MOE_XBLK
    nxb = tm // xg
    n_tiles = n_exp + n_pairs // tm
    n_rows = n_tiles * tm
    n_xblk = n_exp + n_pairs // xg
    flat_e = ids[:, :2].reshape(-1)
    onehot = (flat_e[:, None] == jnp.arange(n_exp, dtype=jnp.int32)[None, :]).astype(jnp.int32)
    counts = jnp.sum(onehot, axis=0)
    tiles_per = (counts + tm - 1) // tm
    tile_end = jnp.cumsum(tiles_per)
    tile_start = tile_end - tiles_per
    xblk_per = (counts + xg - 1) // xg
    xblk_start = jnp.cumsum(xblk_per) - xblk_per
    rank = jnp.sum(onehot * (jnp.cumsum(onehot, axis=0) - 1), axis=1)
    pos = (jnp.sum(onehot * (tile_start * tm)[None, :], axis=1) + rank).reshape(m, 2)
    xpos = jnp.sum(onehot * (xblk_start * xg)[None, :], axis=1) + rank
    nt = tile_end[-1]
    tix = jnp.minimum(jnp.arange(n_tiles, dtype=jnp.int32), nt - 1)
    tile_e = jnp.minimum(jnp.searchsorted(tile_end, tix, side="right"), n_exp - 1).astype(jnp.int32)
    tile_xb = (xblk_start[tile_e] + (tix - tile_start[tile_e]) * nxb).astype(jnp.int32)
    row_tok = (jnp.arange(n_xblk * xg, dtype=jnp.int32) % m).at[xpos].set(
        jnp.arange(n_pairs, dtype=jnp.int32) // 2, mode="promise_in_bounds", unique_indices=True)
    xs = xn.at[row_tok].get(mode="promise_in_bounds")
    nt_arr = nt.reshape(1).astype(jnp.int32)

    tk = _pick(d, (1024, 512, 256, 128))
    nk = d // tk
    live = lambda t, ntr: t < ntr[0]
    clamp = lambda t, ntr: jnp.minimum(t, ntr[0] - 1)
    kidx = lambda t, k, ntr: jnp.where(live(t, ntr), k, nk - 1)
    ns_up = MOE_DMA_SPLIT if tk % (MOE_DMA_SPLIT * LANE) == 0 else 1
    xspecs = [pl.BlockSpec((xg, tk), lambda t, k, te, txb, ntr, j=j: (jnp.minimum(txb[t] + j, n_xblk - 1),
                                                                     kidx(t, k, ntr))) for j in range(nxb)]
    wspecs = [pl.BlockSpec((None, tk // ns_up, ff),
                           lambda t, k, te, txb, ntr, s=s: (te[t], kidx(t, k, ntr) * ns_up + s, 0))
              for s in range(ns_up)]
    hdn = pl.pallas_call(
        functools.partial(_moe_up_kernel, nk=nk, n_split=ns_up, nxb=nxb),
        grid_spec=pltpu.PrefetchScalarGridSpec(
            num_scalar_prefetch=3, grid=(n_tiles, nk),
            in_specs=xspecs + wspecs + wspecs,
            out_specs=pl.BlockSpec((tm, ff), lambda t, k, te, txb, ntr: (clamp(t, ntr), 0)),
            scratch_shapes=[pltpu.VMEM((tm, tk), BF), pltpu.VMEM((tm, ff), F32), pltpu.VMEM((tm, ff), F32)]),
        out_shape=jax.ShapeDtypeStruct((n_rows, ff), BF),
        compiler_params=_cparams(("arbitrary", "arbitrary")),
        name="moe_up",
    )(tile_e, tile_xb, nt_arr, *([xs] * nxb), *([w_gate] * ns_up), *([w_up] * ns_up))

    tn = _pick(d, (2048, 1024, 512, 256, 128))
    nn = d // tn
    ns_dn = MOE_DMA_SPLIT if ff % (MOE_DMA_SPLIT * LANE) == 0 else 1
    jidx = lambda t, j, ntr: jnp.where(live(t, ntr), j, nn - 1)
    y = pl.pallas_call(
        functools.partial(_moe_down_kernel, n_split=ns_dn),
        grid_spec=pltpu.PrefetchScalarGridSpec(
            num_scalar_prefetch=2, grid=(n_tiles, nn),
            in_specs=[pl.BlockSpec((tm, ff), lambda t, j, te, ntr: (clamp(t, ntr), 0))]
            + [pl.BlockSpec((None, ff // ns_dn, tn), lambda t, j, te, ntr, s=s: (te[t], s, jidx(t, j, ntr)))
               for s in range(ns_dn)],
            out_specs=pl.BlockSpec((tm, tn), lambda t, j, te, ntr: (clamp(t, ntr), jidx(t, j, ntr)))),
        out_shape=jax.ShapeDtypeStruct((n_rows, d), BF),
        compiler_params=_cparams(("arbitrary", "arbitrary")),
        name="moe_down",
    )(tile_e, nt_arr, hdn, *([w_down] * ns_dn))
    return y, pos


def _split_cols_kernel(x_ref, o_ref, *, dims, width):
    for i in range(dims[0]):
        if len(dims) == 1:
            o_ref[:, i, :] = x_ref[:, i * width:(i + 1) * width]
        else:
            for j in range(dims[1]):
                c = (i * dims[1] + j) * width
                o_ref[:, i, j, :] = x_ref[:, c:c + width]


def _split_cols(x, batch, rows_per_batch, row0, dims, width, tb=256):
    seq = x.shape[0] // batch
    tb = _pick(math.gcd(rows_per_batch, math.gcd(row0, seq)) if row0 else math.gcd(rows_per_batch, seq),
               (tb, 128, 64, 32, 16, 8))
    nb = rows_per_batch // tb
    ncol = x.shape[1]
    zeros = (0,) * (len(dims) + 1)
    return pl.pallas_call(
        functools.partial(_split_cols_kernel, dims=dims, width=width),
        grid=(batch, nb),
        in_specs=[pl.BlockSpec((tb, ncol), lambda b, i: ((b * seq + row0) // tb + i, 0))],
        out_specs=pl.BlockSpec((None, tb) + tuple(dims) + (width,), lambda b, i: (b, i) + zeros),
        out_shape=jax.ShapeDtypeStruct((batch, rows_per_batch) + tuple(dims) + (width,), x.dtype),
        compiler_params=_cparams(("arbitrary", "arbitrary")),
        name="split_cols",
    )(x)


def _shift_window_kernel(s_ref, w_ref, o_ref, *, t_new, n_kh):
    nbb, rows, _ = s_ref.shape
    keep = rows - t_new * n_kh
    o_ref[:, 0:keep, :] = s_ref[:, t_new * n_kh:rows, :]
    w = w_ref[...]
    for bb in range(nbb):
        for k in range(n_kh):
            o_ref[bb, pl.ds(keep + k, t_new, stride=n_kh), :] = w[bb * t_new:(bb + 1) * t_new, k * HD:(k + 1) * HD]


def _shift_window(state, win_s, t_new):
    _, db, wb, n_kind, n_head, hd = state.shape
    n_kh = n_kind * n_head
    rows = wb * n_kh
    nbb = _pick(db, (4, 2, 1))
    out = pl.pallas_call(
        functools.partial(_shift_window_kernel, t_new=t_new, n_kh=n_kh),
        grid=(db // nbb,),
        in_specs=[pl.BlockSpec((None, nbb, rows, hd), lambda i: (0, i, 0, 0)),
                  pl.BlockSpec((nbb * t_new, n_kh * hd), lambda i: (i, 0))],
        out_specs=pl.BlockSpec((nbb, rows, hd), lambda i: (i, 0, 0)),
        out_shape=jax.ShapeDtypeStruct((db, rows, hd), state.dtype),
        compiler_params=_cparams(("arbitrary",)),
        name="shift_window",
    )(state.reshape(state.shape[0], db, rows, hd), win_s)
    return out.reshape(1, db, wb, n_kind, n_head, hd)


O_KV = NSA_H * HD
O_GATE = O_KV + 6 * NSA_KV * HD
O_QD = O_GATE + 3 * NSA_H
W_QD = DF_H * 2 * HD
W_KD = DF_KV * 2 * HD


def _gate_weights(w):
    d = w.shape[0]
    wg = w[:, O_GATE:O_QD].reshape(d, 3, NSA_KV, NSA_G).transpose(0, 2, 1, 3).reshape(d, NSA_KV, 3 * NSA_G)
    return jnp.pad(wg, ((0, 0), (0, 0), (0, LANE - 3 * NSA_G))).reshape(d, NSA_KV * LANE)


def kernel(x_prompt, x_sample, cache_nsa_kv, cache_diff_kv, state_nsa_win, page_table, p_prompt, p_sample,
           rel_bias_table, norm_mix, w_in, w_cmp, diff_lambda, diff_subln, w_out, norm_ffn, w_router_group,
           b_router_group, w_router_expert, b_router_expert, w_exp_gate, w_exp_up, w_exp_down, norm_ple,
           w_ple_gate, w_ple_proj, final_norm):
    assert norm_mix.shape[0] == 1, "single-layer trunk"
    batch, seq, d = x_prompt.shape
    db, t_new, _ = x_sample.shape
    mp, ms = batch * seq, db * t_new
    m = mp + ms
    lam_init = 0.8 - 0.6 * math.exp(-0.3 * 0)
    tm = _pick(math.gcd(mp, ms), (1024, 512, 256, 128))
    tm_s = _pick(math.gcd(mp, ms), (256, 128))
    xp = x_prompt.reshape(mp, d)
    xs = x_sample.reshape(ms, d)
    tbl_t = rel_bias_table

    xn = _rms2(xp, xs, norm_mix[0], tm_s)
    w0 = w_in[0]
    w_diff = w0[:, O_QD:]
    w_gate = _gate_weights(w0)
    scale = HD ** -0.5
    tn = 512
    wide = functools.partial(_matmul, [xn], tm=tm, tn=tn)
    q_nsa = wide([(w0, 0)], rows=m, n_cols=O_KV, out_dtype=BF, scale=scale, name="proj_qn")
    qd = wide([(w_diff, 0)], rows=m, n_cols=W_QD, out_dtype=BF, scale=scale, name="proj_qd")
    gates = _matmul([xn], [(w_gate, 0)], rows=m, n_cols=NSA_KV * LANE, tm=tm, tn=NSA_KV * LANE, out_dtype=F32,
                    name="proj_gate")
    kv_w = [(w0, 0, lambda j: j + O_KV // tn)]
    win_w = [(w0, 0, lambda j: j + (O_KV + 4 * NSA_KV * HD) // tn)]
    dkv_w = [(w_diff, 0, lambda j: W_QD // DF_DV + j), (w_diff, 0, lambda j: (W_QD + W_KD) // DF_DV + j)]
    dkv = functools.partial(_matmul, [xn], dkv_w, tm=tm, tn=2 * DF_DV, n_cols=DF_KV * 4 * HD, out_dtype=F32,
                            epi="cat")
    kv4_p = wide(kv_w, rows=mp, n_cols=4 * NSA_KV * HD, out_dtype=F32, name="proj_kv_p")
    kv4_s = wide(kv_w, rows=ms, row0=mp, n_cols=4 * NSA_KV * HD, out_dtype=F32, name="proj_kv_s")
    win_p = wide(win_w, rows=mp, n_cols=2 * NSA_KV * HD, out_dtype=F32, name="proj_win_p")
    win_s = wide(win_w, rows=ms, row0=mp, n_cols=2 * NSA_KV * HD, out_dtype=F32, name="proj_win_s")
    dkv_p = dkv(rows=mp, name="proj_dkv_p")
    dkv_s = dkv(rows=ms, row0=mp, name="proj_dkv_s")

    nc = seq // CMP
    kvr = kv4_p.reshape(batch, seq, 4, NSA_KV, HD)
    cmp_out = []
    for kind in range(2):
        a = kvr[:, :, kind].transpose(2, 0, 1, 3).reshape(NSA_KV * batch * nc, CMP * HD).astype(BF)
        r = a.shape[0]
        cmp_out.append(_matmul([a], [(w_cmp[0, kind].reshape(CMP * HD, HD), 0)], rows=r, n_cols=HD,
                               tm=_pick(r, (512, 256, 128, 64, 32, 16)), tn=HD, out_dtype=BF,
                               name="compress").reshape(NSA_KV, batch, nc, HD))
    o_n = _nsa_prompt(q_nsa, gates, cmp_out[0], cmp_out[1], kv4_p, win_p, tbl_t, batch, seq, m)
    o_d = _diff_prompt(qd, dkv_p, tbl_t, diff_lambda[0], diff_subln[0], batch, seq, m, lam_init)

    o_n_s = _nsa_sample(q_nsa[mp:].astype(F32), gates[mp:], kv4_s, win_s, cache_nsa_kv, state_nsa_win,
                        page_table, w_cmp[0], tbl_t, t_new)
    o_d_s = _diff_sample(qd[mp:].astype(F32), dkv_s, cache_diff_kv, page_table, tbl_t, diff_lambda[0],
                         diff_subln[0], t_new, lam_init)
    o_n = lax.dynamic_update_slice(o_n, o_n_s.astype(BF), (mp, 0))
    o_d = lax.dynamic_update_slice(o_d, o_d_s.astype(BF), (mp, 0))

    h1 = _matmul([o_n, o_d], [(w_out[0], 0), (w_out[0], 1)], rows=m, n_cols=d, tm=tm,
                 tn=_pick(d, (512, 256, 128)), out_dtype=F32, epi="res2", epi_args=(xp, xs), name="out_proj")

    wr = jnp.concatenate([w_router_group[0], w_router_expert[0],
                          jnp.zeros((d, LANE - N_GROUPS - N_EXP), F32)], axis=1)
    br = jnp.concatenate([b_router_group[0], b_router_expert[0],
                          jnp.zeros((LANE - N_GROUPS - N_EXP,), F32)]).reshape(1, LANE)
    xn2, ids, wts = _router(h1, norm_ffn[0], wr, br, tm_s)
    y, pos = _moe(xn2, ids, w_exp_gate[0], w_exp_up[0], w_exp_down[0])
    h2, xn3 = _combine_rms(h1, y.at[pos[:, 0]].get(mode="promise_in_bounds"),
                           y.at[pos[:, 1]].get(mode="promise_in_bounds"), wts, norm_ple[0], tm_s)

    p_all = jnp.concatenate([p_prompt[0].reshape(mp, -1), p_sample[0].reshape(ms, -1)], axis=0).astype(BF)
    h3 = _matmul([xn3], [(w_ple_gate[0], 0)], rows=m, n_cols=d, tm=tm, tn=_pick(d, (512, 256, 128)),
                 out_dtype=F32, epi="ple", epi_args=(h2, p_all, w_ple_proj[0]), name="ple")

    y_p = _rms(h3, final_norm, tm_s, F32, row0=0, rows=mp).reshape(batch, seq, d)
    y_s = _rms(h3, final_norm, tm_s, F32, row0=mp, rows=ms).reshape(db, t_new, d)
    wk = min(WIN, seq)
    nsa_kv_p = _split_cols(kv4_p, batch, seq, 0, (4, NSA_KV), HD)
    diff_kv_p = _split_cols(dkv_p, batch, seq, 0, (DF_KV,), 4 * HD)
    win_p_out = _split_cols(win_p, batch, wk, seq - wk, (2, NSA_KV), HD)
    new_win = _shift_window(state_nsa_win, win_s, t_new)
    return (y_p, y_s,
            nsa_kv_p[None], kv4_s.reshape(1, db, t_new, 4, NSA_KV, HD),
            diff_kv_p[None], dkv_s.reshape(1, db, t_new, DF_KV, 4 * HD),
            win_p_out[None], new_win)
```

```python
import functools
import math

import numpy as np
import jax
import jax.numpy as jnp
from jax import lax
from jax.experimental import pallas as pl
from jax.experimental.pallas import tpu as pltpu

BF = jnp.bfloat16
F32 = jnp.float32

HD = 128
NSA_H = 16
NSA_KV = 2
NSA_G = NSA_H // NSA_KV
CMP = 32
SELB = 64
TOPK = 16
WIN = 512
DF_H = 8
DF_KV = 4
DF_G = DF_H // DF_KV
DF_DV = 2 * HD
REL_BUCKETS = 32
REL_MAX_DIST = 128
N_GROUPS = 4
EPG = 8
N_EXP = N_GROUPS * EPG
EPS = 1e-6
NEG = -1e30
FORCE = 1e4
LANE = 128
VMEM_LIMIT = 56 * 1024 * 1024
MOE_TM = 768
MOE_XBLK = 128
MOE_DMA_SPLIT = 4

QB_NSA = 128
PAD_SEL = 384
PAD_WIN = WIN
NEAR_SEL = 512
BAND_WIN = WIN + QB_NSA
QB_DF = 256
PAD_DF = 256
NEAR_DF = 512
TK = 256
CMP_PITCH = 40


def _dot(a, b):
    return jnp.dot(a, b, preferred_element_type=F32)


def _dot_nt(a, b):
    return lax.dot_general(a, b, (((1,), (1,)), ((), ())), preferred_element_type=F32)


def _cparams(sem):
    return pltpu.CompilerParams(dimension_semantics=sem, vmem_limit_bytes=VMEM_LIMIT)


def _pick(n, cands):
    for c in cands:
        if n % c == 0:
            return c
    raise ValueError(f"no tile in {cands} divides {n}")


def _rms2_kernel(xp_ref, xs_ref, g_ref, o_ref, *, np_tiles):
    i = pl.program_id(0)

    def go(x_ref):
        x = x_ref[...]
        ms = jnp.mean(x * x, axis=-1, keepdims=True)
        o_ref[...] = (x * lax.rsqrt(ms + EPS) * g_ref[...]).astype(o_ref.dtype)

    @pl.when(i < np_tiles)
    def _():
        go(xp_ref)

    @pl.when(i >= np_tiles)
    def _():
        go(xs_ref)


def _rms2(xp, xs, g, tm):
    mp, d = xp.shape
    ms = xs.shape[0]
    npt, nst = mp // tm, ms // tm
    return pl.pallas_call(
        functools.partial(_rms2_kernel, np_tiles=npt),
        grid=(npt + nst,),
        in_specs=[pl.BlockSpec((tm, d), lambda i: (jnp.minimum(i, npt - 1), 0)),
                  pl.BlockSpec((tm, d), lambda i: (jnp.maximum(i - npt, 0), 0)),
                  pl.BlockSpec((1, d), lambda i: (0, 0))],
        out_specs=pl.BlockSpec((tm, d), lambda i: (i, 0)),
        out_shape=jax.ShapeDtypeStruct((mp + ms, d), BF),
        compiler_params=_cparams(("arbitrary",)),
        name="rms2",
    )(xp, xs, g.reshape(1, d))


def _rms_kernel(x_ref, g_ref, o_ref):
    x = x_ref[...]
    ms = jnp.mean(x * x, axis=-1, keepdims=True)
    o_ref[...] = (x * lax.rsqrt(ms + EPS) * g_ref[...]).astype(o_ref.dtype)


def _rms(x, g, tm, out_dtype, row0=0, rows=None):
    m, d = x.shape
    rows = m if rows is None else rows
    t0 = row0 // tm
    return pl.pallas_call(
        _rms_kernel,
        grid=(rows // tm,),
        in_specs=[pl.BlockSpec((tm, d), lambda i: (i + t0, 0)),
                  pl.BlockSpec((1, d), lambda i: (0, 0))],
        out_specs=pl.BlockSpec((tm, d), lambda i: (i, 0)),
        out_shape=jax.ShapeDtypeStruct((rows, d), out_dtype),
        compiler_params=_cparams(("arbitrary",)),
        name="rms",
    )(x, g.reshape(1, d))


def _combine_rms_kernel(h_ref, y0_ref, y1_ref, w_ref, g_ref, h2_ref, xn_ref):
    w = w_ref[...]
    h2 = h_ref[...] + w[:, 0:1] * y0_ref[...].astype(F32) + w[:, 1:2] * y1_ref[...].astype(F32)
    h2_ref[...] = h2
    ms = jnp.mean(h2 * h2, axis=-1, keepdims=True)
    xn_ref[...] = (h2 * lax.rsqrt(ms + EPS) * g_ref[...]).astype(xn_ref.dtype)


def _combine_rms(h, y0, y1, wts, g, tm):
    m, d = h.shape
    row = lambda width: pl.BlockSpec((tm, width), lambda i: (i, 0))
    return pl.pallas_call(
        _combine_rms_kernel,
        grid=(m // tm,),
        in_specs=[row(d), row(d), row(d), row(LANE), pl.BlockSpec((1, d), lambda i: (0, 0))],
        out_specs=[row(d), row(d)],
        out_shape=[jax.ShapeDtypeStruct((m, d), F32), jax.ShapeDtypeStruct((m, d), BF)],
        compiler_params=_cparams(("arbitrary",)),
        name="combine_rms",
    )(h, y0, y1, wts, g.reshape(1, d))


def _cast_rows(src_ref, dst_ref):
    k = src_ref.shape[0]
    ch = 256 if k % 256 == 0 else k

    def body(c, carry):
        r = pl.multiple_of(c * ch, ch)
        dst_ref[pl.ds(r, ch), :] = src_ref[pl.ds(r, ch), :].astype(BF)
        return carry

    lax.fori_loop(0, k // ch, body, 0)


def _mm_kernel(*refs, n_a, n_w, cast, epi, scale, np_tiles):
    a = refs[:n_a]
    w = refs[n_a:n_a + n_w]
    idx = n_a + n_w
    if epi == "res2":
        rp_ref, rs_ref = refs[idx:idx + 2]
        idx += 2
    elif epi == "ple":
        h_ref, p_ref, wp_ref = refs[idx:idx + 3]
        idx += 3
    o_ref = refs[idx]
    idx += 1
    wb = refs[idx:idx + n_w] if cast else w
    i = pl.program_id(1)

    if cast:
        @pl.when(i == 0)
        def _():
            for k in range(n_w):
                _cast_rows(w[k], wb[k])

    if epi == "cat":
        x = a[0][...]
        wd = wb[0].shape[1]
        for k in range(n_w):
            o_ref[:, k * wd:(k + 1) * wd] = _dot(x, wb[k][...]).astype(o_ref.dtype)
        return
    acc = _dot(a[0][...], wb[0][...])
    for k in range(1, n_a):
        acc = acc + _dot(a[k][...], wb[k][...])
    if scale is not None:
        acc = acc * scale
    if epi is None:
        o_ref[...] = acc.astype(o_ref.dtype)
    elif epi == "res2":
        @pl.when(i < np_tiles)
        def _():
            o_ref[...] = (acc + rp_ref[...]).astype(o_ref.dtype)

        @pl.when(i >= np_tiles)
        def _():
            o_ref[...] = (acc + rs_ref[...]).astype(o_ref.dtype)
    elif epi == "ple":
        gate = jax.nn.sigmoid(acc)
        proj = _dot(p_ref[...], wp_ref[...].astype(BF))
        o_ref[...] = (h_ref[...] + gate * proj).astype(o_ref.dtype)


def _matmul(a_list, w_list, *, rows, n_cols, tm, tn, out_dtype, row0=0, col0=0,
            scale=None, epi=None, epi_args=(), name="mm"):
    n_a, n_w = len(a_list), len(w_list)
    cast = w_list[0][0].dtype != BF
    t0, c0 = row0 // tm, col0 // tn
    gm, gn = rows // tm, n_cols // tn
    wn = tn // n_w if epi == "cat" else tn
    in_specs, args = [], []
    for a in a_list:
        in_specs.append(pl.BlockSpec((tm, a.shape[1]), lambda j, i: (i + t0, 0)))
        args.append(a)
    for k, ent in enumerate(w_list):
        w, kb = ent[0], ent[1]
        colfn = ent[2] if len(ent) > 2 else (lambda j: j + c0)
        kdim = a_list[min(k, n_a - 1)].shape[1]
        in_specs.append(pl.BlockSpec((kdim, wn), lambda j, i, kb=kb, colfn=colfn: (kb, colfn(j))))
        args.append(w)
    np_tiles = 0
    if epi == "res2":
        xp, xs = epi_args
        np_tiles = xp.shape[0] // tm
        in_specs.append(pl.BlockSpec((tm, tn), lambda j, i: (jnp.minimum(i, np_tiles - 1), j)))
        in_specs.append(pl.BlockSpec((tm, tn), lambda j, i: (jnp.maximum(i - np_tiles, 0), j)))
        args += [xp, xs]
    elif epi == "ple":
        h, p, wp = epi_args
        in_specs.append(pl.BlockSpec((tm, tn), lambda j, i: (i, j)))
        in_specs.append(pl.BlockSpec((tm, p.shape[1]), lambda j, i: (i, 0)))
        in_specs.append(pl.BlockSpec((wp.shape[0], tn), lambda j, i: (0, j)))
        args += [h, p, wp]
    scratch = [pltpu.VMEM((a_list[min(k, n_a - 1)].shape[1], wn), BF) for k in range(n_w)] if cast else []
    return pl.pallas_call(
        functools.partial(_mm_kernel, n_a=n_a, n_w=n_w, cast=cast, epi=epi, scale=scale, np_tiles=np_tiles),
        grid=(gn, gm),
        in_specs=in_specs,
        out_specs=pl.BlockSpec((tm, tn), lambda j, i: (i, j)),
        out_shape=jax.ShapeDtypeStruct((rows, n_cols), out_dtype),
        scratch_shapes=scratch,
        compiler_params=_cparams(("arbitrary", "arbitrary")),
        name=name,
    )(*args)


def _bucket_np(dist):
    n = np.maximum(dist, 0)
    max_exact = REL_BUCKETS // 2
    nf = np.maximum(n, 1).astype(np.float32)
    log_b = max_exact + (np.log(nf / np.float32(max_exact)) / np.float32(math.log(REL_MAX_DIST / max_exact))
                         * np.float32(REL_BUCKETS - max_exact)).astype(np.int32)
    return np.where(n < max_exact, n, np.minimum(log_b, REL_BUCKETS - 1)).astype(np.int32)


def _bucket_edges():
    b = _bucket_np(np.arange(0, 4 * REL_MAX_DIST))
    return [(k, int(np.nonzero(b == k)[0].max())) for k in range(REL_BUCKETS - 1) if (b == k).any()]


def _pattern_kernel(tbl_ref, o_ref, *, h0, base, col_step, v_lo, v_hi, c_lim, sub_far, edges, keys_on_rows):
    h = pl.program_id(0) + h0
    shape = o_ref.shape
    row = lax.broadcasted_iota(jnp.int32, shape, 0) + pl.program_id(1) * shape[0]
    col = lax.broadcasted_iota(jnp.int32, shape, 1)
    if keys_on_rows:
        row, col = col, row
    dist = row + base - col_step * col
    far = tbl_ref[REL_BUCKETS - 1, h]
    b = jnp.full(shape, far, F32)
    for k, hi in reversed(edges):
        b = jnp.where(dist <= hi, tbl_ref[k, h], b)
    if sub_far:
        b = b - far
    valid = (dist >= v_lo) & (dist <= v_hi) & (col < c_lim)
    o_ref[...] = jnp.where(valid, b, NEG)


def _rel_pattern(tbl, h0, nh, nrows, ncols, base, v_lo, v_hi, sub_far, col_step=1, c_lim=None,
                 keys_on_rows=False):
    out_r, out_c = (ncols, nrows) if keys_on_rows else (nrows, ncols)
    tr = _pick(out_r, (512, 256, 128, 64, 32, 16, 8))
    return pl.pallas_call(
        functools.partial(_pattern_kernel, h0=h0, base=base, col_step=col_step, v_lo=v_lo, v_hi=v_hi,
                          c_lim=ncols if c_lim is None else c_lim, sub_far=sub_far, edges=_bucket_edges(),
                          keys_on_rows=keys_on_rows),
        grid=(nh, out_r // tr),
        in_specs=[pl.BlockSpec(memory_space=pltpu.SMEM)],
        out_specs=pl.BlockSpec((None, tr, out_c), lambda h, r: (h, r, 0)),
        out_shape=jax.ShapeDtypeStruct((nh, out_r, out_c), F32),
        compiler_params=_cparams(("arbitrary", "arbitrary")),
        name="rel_bias",
    )(tbl)


def _pair_sum_matrix(nc):
    n = np.arange(nc)[:, None]
    b = np.arange(LANE)[None, :]
    return jnp.asarray((n // (SELB // CMP) == b).astype(np.float32), dtype=BF)


def _expand_matrix(pad, n_keys):
    l = np.arange(LANE)[:, None]
    c = np.arange(pad + n_keys)[None, :]
    return jnp.asarray(((c >= pad) & ((c - pad) // SELB == l)).astype(np.float32), dtype=BF)


def _split3(x):
    hi = x.astype(BF)
    r = x - hi.astype(F32)
    mid = r.astype(BF)
    lo = (r - mid.astype(F32)).astype(BF)
    return hi, mid, lo


def _select_blocks(psum, s_mat, qpos, ns):
    hi, mid, lo = _split3(psum)
    imp = _dot(hi, s_mat) + _dot(mid, s_mat) + _dot(lo, s_mat)
    shape = imp.shape
    lane = lax.broadcasted_iota(jnp.int32, shape, 1)
    valid = lane * SELB <= qpos
    cur = jnp.right_shift(qpos, 6)
    forced = (lane == 0) | (lane == cur) | (lane == cur - 1)
    score = jnp.where(valid, imp + jnp.where(forced, FORCE, 0.0), NEG)
    score = jnp.where(lane < ns, score, -3e38)
    cnt = jnp.zeros(shape, F32)
    for i in range(ns):
        ci = score[:, i:i + 1]
        cnt = cnt + jnp.where(lane > i, jnp.where(ci >= score, 1.0, 0.0), jnp.where(ci > score, 1.0, 0.0))
    sel = (cnt < float(min(TOPK, ns))) & (lane < ns)
    return jnp.where(sel, 1.0, 0.0).astype(BF)


def _softmax_rows(s, valid=None):
    m = jnp.max(s, axis=-1, keepdims=True)
    p = jnp.exp(s - m)
    if valid is not None:
        p = jnp.where(valid, p, 0.0)
    l = jnp.sum(p, axis=-1, keepdims=True)
    return p / jnp.where(l > 0.0, l, 1.0)


def _lanes(x, n):
    return x if n == LANE else jnp.concatenate([x] * (n // LANE), axis=1)


def _online(carry, s, vt, ones_in_v=False):
    m, l, acc = carry
    dv = acc.shape[1]
    m_new = jnp.maximum(m, jnp.max(s, axis=-1, keepdims=True))
    p = jnp.exp(s - _lanes(m_new, s.shape[1]))
    alpha = jnp.exp(m - m_new)
    pv = _dot(p.astype(BF), vt)
    if ones_in_v:
        l = alpha * l + pv[:, dv:dv + LANE]
        pv = pv[:, 0:dv]
    else:
        l = alpha * l + jnp.sum(p, axis=-1, keepdims=True)
    return m_new, l, _lanes(alpha, dv) * acc + pv


def _online_init(rows, dv):
    return jnp.full((rows, LANE), NEG, F32), jnp.zeros((rows, LANE), F32), jnp.zeros((rows, dv), F32)


def _flash_step(q, kt, vt, bias, m_ref, l_ref, acc_ref, rows, ones_in_v=False):
    s = _dot_nt(q, kt)
    if bias is not None:
        s = s + bias
    m_new, l, acc = _online((m_ref[rows, :], l_ref[rows, :], acc_ref[rows, :]), s, vt, ones_in_v)
    m_ref[rows, :] = m_new
    l_ref[rows, :] = l
    acc_ref[rows, :] = acc


def _nsa_prompt_kernel(q_ref, gate_ref, kc_ref, vc_ref, ks_ref, vs_ref, kw_ref, vw_ref,
                       bc_ref, pn_ref, pw_ref, e_ref, s_ref, o_ref,
                       ks_s, vs_s, kw_s, vw_s, m_s, l_s, acc_s, o_s, *, seq, nc, ns):
    i = pl.program_id(2)
    s0 = i * QB_NSA
    rows = NSA_G * QB_NSA

    @pl.when(i == 0)
    def _():
        ks_s[0:PAD_SEL, :] = jnp.zeros((PAD_SEL, HD), BF)
        vs_s[0:PAD_SEL, :] = jnp.zeros((PAD_SEL, 2 * HD), BF)
        kw_s[0:PAD_WIN, :] = jnp.zeros((PAD_WIN, HD), BF)
        vw_s[0:PAD_WIN, :] = jnp.zeros((PAD_WIN, 2 * HD), BF)
        ch = 512
        ones = jnp.ones((ch, HD), BF)

        def cp(c, carry):
            r = pl.multiple_of(c * ch, ch)
            ks_s[pl.ds(PAD_SEL + r, ch), :] = ks_ref[pl.ds(r, ch), :].astype(BF)
            vs_s[pl.ds(PAD_SEL + r, ch), 0:HD] = vs_ref[pl.ds(r, ch), :].astype(BF)
            vs_s[pl.ds(PAD_SEL + r, ch), HD:2 * HD] = ones
            kw_s[pl.ds(PAD_WIN + r, ch), :] = kw_ref[pl.ds(r, ch), :].astype(BF)
            vw_s[pl.ds(PAD_WIN + r, ch), 0:HD] = vw_ref[pl.ds(r, ch), :].astype(BF)
            vw_s[pl.ds(PAD_WIN + r, ch), HD:2 * HD] = ones
            return carry

        lax.fori_loop(0, seq // ch, cp, 0)

    gt = jax.nn.sigmoid(gate_ref[...])
    head = lambda g: slice(g * HD, (g + 1) * HD)
    hrows = lambda g: slice(g * QB_NSA, (g + 1) * QB_NSA)

    kc = kc_ref[...]
    vc = vc_ref[...]
    psum = jnp.zeros((QB_NSA, nc), F32)
    for g in range(NSA_G):
        bc = bc_ref[g]
        pc = _softmax_rows(_dot_nt(q_ref[:, head(g)], kc) + bc, bc > 0.5 * NEG)
        psum = psum + pc
        o_s[:, head(g)] = gt[:, g:g + 1] * _dot(pc.astype(BF), vc)
    qpos = s0 + lax.broadcasted_iota(jnp.int32, (QB_NSA, LANE), 0)
    selb = _select_blocks(psum, s_ref[...], qpos, ns)

    nch = jnp.maximum(i - 1, 0) // 2
    far_keys = nch * TK
    m_s[...] = jnp.full((rows, LANE), NEG, F32)
    l_s[...] = jnp.zeros((rows, LANE), F32)
    acc_s[...] = jnp.zeros((rows, HD), F32)

    def sel_chunk(r, bias_of):
        kt = ks_s[pl.ds(r, TK), :]
        vt = vs_s[pl.ds(r, TK), :]
        madd = (_dot(selb, e_ref[:, pl.ds(r, TK)]) - 1.0) * (-NEG)
        for g in range(NSA_G):
            _flash_step(q_ref[:, head(g)], kt, vt, bias_of(g, madd), m_s, l_s, acc_s, hrows(g), ones_in_v=True)

    def far(c, carry):
        sel_chunk(pl.multiple_of(PAD_SEL + c * TK, LANE), lambda g, madd: madd)
        return carry

    lax.fori_loop(0, nch, far, 0)
    for kh in range(NEAR_SEL // TK):
        col = lax.broadcasted_iota(jnp.int32, (QB_NSA, TK), 1) + kh * TK
        cut = jnp.where(col < far_keys - s0 + PAD_SEL, NEG, 0.0)
        sel_chunk(pl.multiple_of(s0 + kh * TK, LANE),
                  lambda g, madd, kh=kh, cut=cut: pn_ref[g, :, kh * TK:(kh + 1) * TK] + madd + cut)
    for g in range(NSA_G):
        osel = acc_s[hrows(g), :] / l_s[hrows(g), :]
        o_s[:, head(g)] = o_s[:, head(g)] + gt[:, NSA_G + g:NSA_G + g + 1] * osel

    for g in range(NSA_G):
        st = _online_init(QB_NSA, HD)
        for c0 in range(0, BAND_WIN, TK):
            w = min(TK, BAND_WIN - c0)
            r = pl.multiple_of(s0 + c0, LANE)
            colw = lax.broadcasted_iota(jnp.int32, (QB_NSA, w), 1) + c0
            bias = pw_ref[g, :, c0:c0 + w] + jnp.where(colw < PAD_WIN - s0, NEG, 0.0)
            st = _online(st, _dot_nt(q_ref[:, head(g)], kw_s[pl.ds(r, w), :]) + bias, vw_s[pl.ds(r, w), :],
                         ones_in_v=True)
        ow = st[2] / st[1]
        o_ref[:, head(g)] = (o_s[:, head(g)] + gt[:, 2 * NSA_G + g:2 * NSA_G + g + 1] * ow).astype(o_ref.dtype)


def _nsa_prompt(q_nsa, gates, kc, vc, kv4_p, win_p, tbl_t, batch, seq, m_total):
    nc = seq // CMP
    ns = -(-seq // SELB)
    nqb = seq // QB_NSA
    big = 1 << 30
    pn = _rel_pattern(tbl_t, 0, NSA_H, QB_NSA, NEAR_SEL, PAD_SEL, 0, big, True)
    pw = _rel_pattern(tbl_t, 0, NSA_H, QB_NSA, BAND_WIN, PAD_WIN, 0, WIN, False)
    bc = _rel_pattern(tbl_t, 0, NSA_H, seq, nc, -(CMP - 1), 0, big, False, col_step=CMP)
    e_mat = _expand_matrix(PAD_SEL, seq)
    s_mat = _pair_sum_matrix(nc)
    kv_spec = lambda col: pl.BlockSpec((seq, HD), lambda b, h, i, col=col: (b, col + h))
    return pl.pallas_call(
        functools.partial(_nsa_prompt_kernel, seq=seq, nc=nc, ns=ns),
        grid=(batch, NSA_KV, nqb),
        in_specs=[
            pl.BlockSpec((QB_NSA, NSA_G * HD), lambda b, h, i: (b * nqb + i, h)),
            pl.BlockSpec((QB_NSA, LANE), lambda b, h, i: (b * nqb + i, h)),
            pl.BlockSpec((None, None, nc, HD), lambda b, h, i: (h, b, 0, 0)),
            pl.BlockSpec((None, None, nc, HD), lambda b, h, i: (h, b, 0, 0)),
            kv_spec(2 * NSA_KV), kv_spec(3 * NSA_KV),
            pl.BlockSpec((seq, HD), lambda b, h, i: (b, h)),
            pl.BlockSpec((seq, HD), lambda b, h, i: (b, NSA_KV + h)),
            pl.BlockSpec((NSA_G, QB_NSA, nc), lambda b, h, i: (h, i, 0)),
            pl.BlockSpec((NSA_G, QB_NSA, NEAR_SEL), lambda b, h, i: (h, 0, 0)),
            pl.BlockSpec((NSA_G, QB_NSA, BAND_WIN), lambda b, h, i: (h, 0, 0)),
            pl.BlockSpec((LANE, PAD_SEL + seq), lambda b, h, i: (0, 0)),
            pl.BlockSpec((nc, LANE), lambda b, h, i: (0, 0)),
        ],
        out_specs=pl.BlockSpec((QB_NSA, NSA_G * HD), lambda b, h, i: (b * nqb + i, h)),
        out_shape=jax.ShapeDtypeStruct((m_total, NSA_H * HD), BF),
        scratch_shapes=[pltpu.VMEM((PAD_SEL + seq, HD), BF), pltpu.VMEM((PAD_SEL + seq, 2 * HD), BF),
                        pltpu.VMEM((PAD_WIN + seq, HD), BF), pltpu.VMEM((PAD_WIN + seq, 2 * HD), BF),
                        pltpu.VMEM((NSA_G * QB_NSA, LANE), F32), pltpu.VMEM((NSA_G * QB_NSA, LANE), F32),
                        pltpu.VMEM((NSA_G * QB_NSA, HD), F32), pltpu.VMEM((QB_NSA, NSA_G * HD), F32)],
        compiler_params=_cparams(("arbitrary", "arbitrary", "arbitrary")),
        name="nsa_prompt",
    )(q_nsa, gates, kc, vc, kv4_p, kv4_p, win_p, win_p, bc, pn, pw, e_mat, s_mat)


def _diff_lambda(dl, lam_init):
    a = jnp.sum(dl[0:1] * dl[1:2], axis=-1, keepdims=True)
    b = jnp.sum(dl[2:3] * dl[3:4], axis=-1, keepdims=True)
    return jnp.exp(a) - jnp.exp(b) + lam_init


def _diff_finish(a, sub, lam_init):
    ms = jnp.mean(a * a, axis=-1, keepdims=True)
    return a * lax.rsqrt(ms + EPS) * sub * (1.0 - lam_init)


def _diff_prompt_kernel(q_ref, kv_ref, pn_ref, dl_ref, sub_ref, o_ref, kv_s, m_s, l_s, acc_s, *, seq, lam_init):
    i = pl.program_id(2)
    s0 = i * QB_DF
    rows = DF_G * QB_DF

    @pl.when(i == 0)
    def _():
        kv_s[0:PAD_DF, :] = jnp.zeros((PAD_DF, 4 * HD), BF)
        ch = 256

        def cp(c, carry):
            r = pl.multiple_of(c * ch, ch)
            kv_s[pl.ds(PAD_DF + r, ch), :] = kv_ref[pl.ds(r, ch), :].astype(BF)
            return carry

        lax.fori_loop(0, seq // ch, cp, 0)

    lam = _diff_lambda(dl_ref[...], lam_init)
    nfar = jnp.maximum(i - 1, 0)
    sub_rows = 128
    n_sub = QB_DF // sub_rows
    streams = [(m, g, j) for m in range(2) for g in range(DF_G) for j in range(n_sub)]
    srows = lambda k: slice(k * sub_rows, (k + 1) * sub_rows)
    m_s[...] = jnp.full((2 * rows, LANE), NEG, F32)
    l_s[...] = jnp.zeros((2 * rows, LANE), F32)
    acc_s[...] = jnp.zeros((2 * rows, DF_DV), F32)

    def chunk(r, bias_of):
        vt = kv_s[pl.ds(r, TK), 2 * HD:4 * HD]
        for k, (m, g, j) in enumerate(streams):
            q = q_ref[j * sub_rows:(j + 1) * sub_rows, (g * 2 + m) * HD:(g * 2 + m + 1) * HD]
            kt = kv_s[pl.ds(r, TK), m * HD:(m + 1) * HD]
            _flash_step(q, kt, vt, bias_of(g, j), m_s, l_s, acc_s, srows(k))

    def far(c, carry):
        chunk(pl.multiple_of(PAD_DF + c * TK, TK), lambda g, j: None)
        return carry

    lax.fori_loop(0, nfar, far, 0)
    for kh in range(NEAR_DF // TK):
        col = lax.broadcasted_iota(jnp.int32, (sub_rows, TK), 1) + kh * TK
        cut = jnp.where(col < nfar * TK - s0 + PAD_DF, NEG, 0.0)
        chunk(pl.multiple_of(s0 + kh * TK, TK),
              lambda g, j, kh=kh, cut=cut: pn_ref[g, j * sub_rows:(j + 1) * sub_rows, kh * TK:(kh + 1) * TK] + cut)
    half = len(streams) // 2
    for k, (_, g, j) in enumerate(streams[:half]):
        o1 = acc_s[srows(k), :] / _lanes(l_s[srows(k), :], DF_DV)
        o2 = acc_s[srows(half + k), :] / _lanes(l_s[srows(half + k), :], DF_DV)
        out = _diff_finish(o1 - lam * o2, sub_ref[...], lam_init)
        o_ref[j * sub_rows:(j + 1) * sub_rows, g * DF_DV:(g + 1) * DF_DV] = out.astype(o_ref.dtype)


def _diff_prompt(qd, dkv_p, tbl_t, dl, sub, batch, seq, m_total, lam_init):
    nqb = seq // QB_DF
    pn = _rel_pattern(tbl_t, NSA_H, DF_H, QB_DF, NEAR_DF, PAD_DF, 0, 1 << 30, True)
    width = DF_G * 2 * HD
    return pl.pallas_call(
        functools.partial(_diff_prompt_kernel, seq=seq, lam_init=lam_init),
        grid=(batch, DF_KV, nqb),
        in_specs=[
            pl.BlockSpec((QB_DF, width), lambda b, h, i: (b * nqb + i, h)),
            pl.BlockSpec((seq, 4 * HD), lambda b, h, i: (b, h)),
            pl.BlockSpec((DF_G, QB_DF, NEAR_DF), lambda b, h, i: (h, 0, 0)),
            pl.BlockSpec((4, HD), lambda b, h, i: (0, 0)),
            pl.BlockSpec((1, DF_DV), lambda b, h, i: (0, 0)),
        ],
        out_specs=pl.BlockSpec((QB_DF, DF_G * DF_DV), lambda b, h, i: (b * nqb + i, h)),
        out_shape=jax.ShapeDtypeStruct((m_total, DF_H * DF_DV), BF),
        scratch_shapes=[pltpu.VMEM((PAD_DF + seq, 4 * HD), BF),
                        pltpu.VMEM((2 * DF_G * QB_DF, LANE), F32), pltpu.VMEM((2 * DF_G * QB_DF, LANE), F32),
                        pltpu.VMEM((2 * DF_G * QB_DF, DF_DV), F32)],
        compiler_params=_cparams(("arbitrary", "arbitrary", "arbitrary")),
        name="diff_prompt",
    )(qd, dkv_p, pn, dl, sub.reshape(1, DF_DV))


def _tail_tile(new, width):
    t = new.shape[0]
    return jnp.concatenate([new, jnp.zeros((LANE - t, width), F32)], axis=0).astype(BF)


def _nsa_sample_kernel(pt_ref, *refs, n_pages, page, past, t_new, ncs, ns, wb):
    pages = refs[:n_pages]
    (q_ref, gate_ref, kvn_ref, wn_ref, st_ref, wc_ref, bc_ref, bs_ref, bw_ref, e_ref, s_ref,
     o_ref, kcmp_s, ksel_s, kwin_s) = refs[n_pages:]
    del pt_ref
    rows = NSA_G * t_new
    n_kinds = 4

    for p in range(n_pages):
        for kind in range(n_kinds):
            for h in range(NSA_KV):
                blk = pages[p][pl.ds(kind * NSA_KV + h, page, stride=n_kinds * NSA_KV), :]
                if kind < 2:
                    for nb in range(page // CMP):
                        r0 = (p * (page // CMP) + nb) * CMP_PITCH
                        kcmp_s[kind, h, r0:r0 + CMP, :] = blk[nb * CMP:(nb + 1) * CMP]
                else:
                    ksel_s[kind - 2, h, p * page:(p + 1) * page, :] = blk.astype(BF)
    kvn = kvn_ref[...]
    wn = wn_ref[...]
    for kind in range(2):
        for h in range(NSA_KV):
            c0 = ((kind + 2) * NSA_KV + h) * HD
            ksel_s[kind, h, past:past + LANE, :] = _tail_tile(kvn[:, c0:c0 + HD], HD)
            kwin_s[kind, h, 0:wb, :] = st_ref[pl.ds(kind * NSA_KV + h, wb, stride=2 * NSA_KV), :].astype(BF)
            c0 = (kind * NSA_KV + h) * HD
            kwin_s[kind, h, wb:wb + LANE, :] = _tail_tile(wn[:, c0:c0 + HD], HD)

    q = q_ref[...]
    gt = jax.nn.sigmoid(gate_ref[...])
    qpos = past + lax.broadcasted_iota(jnp.int32, (t_new, LANE), 0)
    for h in range(NSA_KV):
        cmp = []
        for kind in range(2):
            acc = jnp.zeros((ncs, HD), F32)
            for j in range(CMP):
                kj = kcmp_s[kind, h, pl.ds(j, ncs, stride=CMP_PITCH), :]
                acc = acc + _dot(kj.astype(BF), wc_ref[kind, j * HD:(j + 1) * HD, :])
            cmp.append(acc.astype(BF))
        kc, vc = cmp
        qs = jnp.concatenate([q[:, (h * NSA_G + g) * HD:(h * NSA_G + g + 1) * HD] for g in range(NSA_G)],
                             axis=0).astype(BF)
        bc = bc_ref[h * NSA_G:(h + 1) * NSA_G].reshape(rows, ncs)
        pc = _softmax_rows(_dot_nt(qs, kc) + bc, bc > 0.5 * NEG)
        oc = _dot(pc.astype(BF), vc)
        psum = pc[0:t_new]
        for g in range(1, NSA_G):
            psum = psum + pc[g * t_new:(g + 1) * t_new]
        selb = _select_blocks(psum, s_ref[...], qpos, ns)
        lk = past + LANE
        madd = (_dot(selb, e_ref[...]) - 1.0) * (-NEG)
        s = _dot_nt(qs, ksel_s[0, h])
        s = (s.reshape(NSA_G, t_new, lk) + bs_ref[h * NSA_G:(h + 1) * NSA_G] + madd[None]).reshape(rows, lk)
        osel = _dot(_softmax_rows(s).astype(BF), ksel_s[1, h])
        lw = wb + LANE
        s = _dot_nt(qs, kwin_s[0, h]) + bw_ref[h * NSA_G:(h + 1) * NSA_G].reshape(rows, lw)
        ow = _dot(_softmax_rows(s).astype(BF), kwin_s[1, h])
        for g in range(NSA_G):
            sl = slice(g * t_new, (g + 1) * t_new)
            gl = h * LANE + g
            o = (gt[:, gl:gl + 1] * oc[sl] + gt[:, gl + NSA_G:gl + NSA_G + 1] * osel[sl]
                 + gt[:, gl + 2 * NSA_G:gl + 2 * NSA_G + 1] * ow[sl])
            o_ref[:, (h * NSA_G + g) * HD:(h * NSA_G + g + 1) * HD] = o


def _nsa_sample(q_s, gates_s, kv4_s, win_s, cache, state, page_table, w_cmp, tbl_t, t_new):
    db, n_pages = page_table.shape
    n_phys, page = cache.shape[1], cache.shape[2]
    past = n_pages * page
    wb = state.shape[2]
    assert (past + t_new) // CMP * CMP <= past and past % SELB == 0 and wb == min(WIN, past)
    ncs = (past + t_new) // CMP
    ns = -(-(past + t_new) // SELB)
    lk, lw = past + LANE, wb + LANE
    rows_pp = page * 4 * NSA_KV
    cache2 = cache.reshape(cache.shape[0], n_phys, rows_pp, HD)
    state2 = state.reshape(state.shape[0], db, wb * 2 * NSA_KV, HD)
    wc = w_cmp.reshape(2, CMP * HD, HD).astype(BF)
    big = 1 << 30
    bc = _rel_pattern(tbl_t, 0, NSA_H, t_new, ncs, past - (CMP - 1), 0, big, False, col_step=CMP)
    bs = _rel_pattern(tbl_t, 0, NSA_H, t_new, lk, past, 0, big, False, c_lim=past + t_new)
    bw = _rel_pattern(tbl_t, 0, NSA_H, t_new, lw, wb, 0, WIN, False, c_lim=wb + t_new)
    e_mat = _expand_matrix(0, lk)
    s_mat = _pair_sum_matrix(ncs)
    full = lambda shape: pl.BlockSpec(shape, lambda b, pt: (0,) * len(shape))
    page_specs = [pl.BlockSpec((None, None, rows_pp, HD), lambda b, pt, p=p: (0, pt[b, p], 0, 0))
                  for p in range(n_pages)]
    in_specs = page_specs + [
        pl.BlockSpec((t_new, NSA_H * HD), lambda b, pt: (b, 0)),
        pl.BlockSpec((t_new, NSA_KV * LANE), lambda b, pt: (b, 0)),
        pl.BlockSpec((t_new, 4 * NSA_KV * HD), lambda b, pt: (b, 0)),
        pl.BlockSpec((t_new, 2 * NSA_KV * HD), lambda b, pt: (b, 0)),
        pl.BlockSpec((None, None, wb * 2 * NSA_KV, HD), lambda b, pt: (0, b, 0, 0)),
        full((2, CMP * HD, HD)), full((NSA_H, t_new, ncs)), full((NSA_H, t_new, lk)),
        full((NSA_H, t_new, lw)), full((LANE, lk)), full((ncs, LANE)),
    ]
    return pl.pallas_call(
        functools.partial(_nsa_sample_kernel, n_pages=n_pages, page=page, past=past, t_new=t_new,
                          ncs=ncs, ns=ns, wb=wb),
        grid_spec=pltpu.PrefetchScalarGridSpec(
            num_scalar_prefetch=1, grid=(db,), in_specs=in_specs,
            out_specs=pl.BlockSpec((t_new, NSA_H * HD), lambda b, pt: (b, 0)),
            scratch_shapes=[pltpu.VMEM((2, NSA_KV, ncs * CMP_PITCH, HD), F32),
                            pltpu.VMEM((2, NSA_KV, lk, HD), BF),
                            pltpu.VMEM((2, NSA_KV, lw, HD), BF)]),
        out_shape=jax.ShapeDtypeStruct((db * t_new, NSA_H * HD), F32),
        compiler_params=_cparams(("arbitrary",)),
        name="nsa_sample",
    )(page_table, *([cache2] * n_pages), q_s, gates_s, kv4_s, win_s, state2, wc, bc, bs, bw, e_mat, s_mat)


def _diff_sample_kernel(pt_ref, *refs, n_pages, page, past, t_new, lam_init):
    pages = refs[:n_pages]
    q_ref, kvn_ref, b_ref, dl_ref, sub_ref, o_ref, kv_s, stage_s = refs[n_pages:]
    del pt_ref
    rows = DF_G * t_new
    lk = past + LANE
    for p in range(n_pages):
        for h in range(DF_KV):
            stage_s[h] = pages[p][:, h, :]
            kv_s[h, p * page:(p + 1) * page, :] = stage_s[h].astype(BF)
    kvn = kvn_ref[...]
    for h in range(DF_KV):
        kv_s[h, past:past + LANE, :] = _tail_tile(kvn[:, h * 4 * HD:(h + 1) * 4 * HD], 4 * HD)
    lam = _diff_lambda(dl_ref[...], lam_init)
    q = q_ref[...]
    for h in range(DF_KV):
        bias = b_ref[h * DF_G:(h + 1) * DF_G].reshape(rows, lk)
        ps = []
        for m in range(2):
            qm = jnp.concatenate(
                [q[:, ((h * DF_G + g) * 2 + m) * HD:((h * DF_G + g) * 2 + m + 1) * HD] for g in range(DF_G)],
                axis=0).astype(BF)
            ps.append(_softmax_rows(_dot_nt(qm, kv_s[h, :, m * HD:(m + 1) * HD]) + bias))
        a = ps[0] - lam * ps[1]
        out = _diff_finish(_dot(a.astype(BF), kv_s[h, :, 2 * HD:4 * HD]), sub_ref[...], lam_init)
        for g in range(DF_G):
            o_ref[:, (h * DF_G + g) * DF_DV:(h * DF_G + g + 1) * DF_DV] = out[g * t_new:(g + 1) * t_new]


def _diff_sample(qd_s, dkv_s, cache, page_table, tbl_t, dl, sub, t_new, lam_init):
    db, n_pages = page_table.shape
    page = cache.shape[2]
    past = n_pages * page
    lk = past + LANE
    bias = _rel_pattern(tbl_t, NSA_H, DF_H, t_new, lk, past, 0, 1 << 30, False, c_lim=past + t_new)
    full = lambda shape: pl.BlockSpec(shape, lambda b, pt: (0,) * len(shape))
    page_specs = [pl.BlockSpec((None, None, page, DF_KV, 4 * HD), lambda b, pt, p=p: (0, pt[b, p], 0, 0, 0))
                  for p in range(n_pages)]
    in_specs = page_specs + [
        pl.BlockSpec((t_new, DF_H * 2 * HD), lambda b, pt: (b, 0)),
        pl.BlockSpec((t_new, DF_KV * 4 * HD), lambda b, pt: (b, 0)),
        full((DF_H, t_new, lk)), full((4, HD)), full((1, DF_DV)),
    ]
    return pl.pallas_call(
        functools.partial(_diff_sample_kernel, n_pages=n_pages, page=page, past=past, t_new=t_new,
                          lam_init=lam_init),
        grid_spec=pltpu.PrefetchScalarGridSpec(
            num_scalar_prefetch=1, grid=(db,), in_specs=in_specs,
            out_specs=pl.BlockSpec((t_new, DF_H * DF_DV), lambda b, pt: (b, 0)),
            scratch_shapes=[pltpu.VMEM((DF_KV, lk, 4 * HD), BF), pltpu.VMEM((DF_KV, page, 4 * HD), F32)]),
        out_shape=jax.ShapeDtypeStruct((db * t_new, DF_H * DF_DV), F32),
        compiler_params=_cparams(("arbitrary",)),
        name="diff_sample",
    )(page_table, *([cache] * n_pages), qd_s, dkv_s, bias, dl, sub.reshape(1, DF_DV))


def _router_kernel(h_ref, g_ref, wr_ref, br_ref, xn_ref, ids_ref, wts_ref):
    x = h_ref[...]
    ms = jnp.mean(x * x, axis=-1, keepdims=True)
    xn = x * lax.rsqrt(ms + EPS) * g_ref[...]
    xh = xn.astype(BF)
    xn_ref[...] = xh
    xl = (xn - xh.astype(F32)).astype(BF)
    wr = wr_ref[...]
    wh = wr.astype(BF)
    wl = (wr - wh.astype(F32)).astype(BF)
    lg = _dot(xh, wh) + _dot(xl, wh) + _dot(xh, wl) + br_ref[...]
    lane_i = lax.broadcasted_iota(jnp.int32, lg.shape, 1)
    lane = lane_i.astype(F32)
    big = 1000.0
    isg = lane_i < N_GROUPS
    gmax = jnp.max(jnp.where(isg, lg, -3e38), axis=-1, keepdims=True)
    gsel = jnp.min(jnp.where(isg & (lg == gmax), lane, big), axis=-1, keepdims=True)
    gw = 1.0 / jnp.sum(jnp.where(isg, jnp.exp(lg - gmax), 0.0), axis=-1, keepdims=True)
    lo = N_GROUPS + gsel * EPG
    ing = (lane >= lo) & (lane < lo + EPG)
    emax = jnp.max(jnp.where(ing, lg, -3e38), axis=-1, keepdims=True)
    pe = jnp.where(ing, jnp.exp(lg - emax), 0.0)
    pr = jnp.where(ing, pe / jnp.sum(pe, axis=-1, keepdims=True), -1.0)
    v1 = jnp.max(pr, axis=-1, keepdims=True)
    i1 = jnp.min(jnp.where(pr == v1, lane, big), axis=-1, keepdims=True)
    pr2 = jnp.where(lane == i1, -1.0, pr)
    v2 = jnp.max(pr2, axis=-1, keepdims=True)
    i2 = jnp.min(jnp.where(pr2 == v2, lane, big), axis=-1, keepdims=True)
    den = v1 + v2
    e12 = jnp.where(lane_i == 0, i1 - N_GROUPS, jnp.where(lane_i == 1, i2 - N_GROUPS, 0.0))
    ids_ref[...] = e12.astype(jnp.int32)
    wts_ref[...] = jnp.where(lane_i == 0, v1 / den * gw, jnp.where(lane_i == 1, v2 / den * gw, 0.0))


def _router(h, g, wr, br, tm):
    m, d = h.shape
    return pl.pallas_call(
        _router_kernel,
        grid=(m // tm,),
        in_specs=[pl.BlockSpec((tm, d), lambda i: (i, 0)), pl.BlockSpec((1, d), lambda i: (0, 0)),
                  pl.BlockSpec((d, LANE), lambda i: (0, 0)), pl.BlockSpec((1, LANE), lambda i: (0, 0))],
        out_specs=[pl.BlockSpec((tm, d), lambda i: (i, 0)), pl.BlockSpec((tm, LANE), lambda i: (i, 0)),
                   pl.BlockSpec((tm, LANE), lambda i: (i, 0))],
        out_shape=[jax.ShapeDtypeStruct((m, d), BF), jax.ShapeDtypeStruct((m, LANE), jnp.int32),
                   jax.ShapeDtypeStruct((m, LANE), F32)],
        compiler_params=_cparams(("arbitrary",)),
        name="router",
    )(h, g.reshape(1, d), wr, br)


def _split_dot(x_ref, w_refs):
    rows = w_refs[0].shape[0]
    acc = _dot(x_ref[:, 0:rows], w_refs[0][...].astype(BF))
    for s in range(1, len(w_refs)):
        acc = acc + _dot(x_ref[:, s * rows:(s + 1) * rows], w_refs[s][...].astype(BF))
    return acc


def _moe_up_kernel(te_ref, txb_ref, nt_ref, *refs, nk, n_split, nxb):
    xb = refs[:nxb]
    wg = refs[nxb:nxb + n_split]
    wu = refs[nxb + n_split:nxb + 2 * n_split]
    o_ref, x_s, a_s, u_s = refs[nxb + 2 * n_split:]
    k = pl.program_id(1)
    xg = xb[0].shape[0]

    @pl.when(pl.program_id(0) < nt_ref[0])
    def _():
        @pl.when(k == 0)
        def _():
            a_s[...] = jnp.zeros(a_s.shape, F32)
            u_s[...] = jnp.zeros(u_s.shape, F32)

        for j in range(nxb):
            x_s[j * xg:(j + 1) * xg, :] = xb[j][...]
        a_s[...] += _split_dot(x_s, wg)
        u_s[...] += _split_dot(x_s, wu)

        @pl.when(k == nk - 1)
        def _():
            g = a_s[...]
            o_ref[...] = (g * jax.nn.sigmoid(g) * u_s[...]).astype(o_ref.dtype)


def _moe_down_kernel(te_ref, nt_ref, h_ref, *refs, n_split):
    o_ref = refs[n_split]

    @pl.when(pl.program_id(0) < nt_ref[0])
    def _():
        o_ref[...] = _split_dot(h_ref, refs[:n_split]).astype(o_ref.dtype)


def _moe_sizes(m, n_exp):
    n_pairs = 2 * m
    return n_exp + n_pairs // MOE_TM, n_exp + n_pairs // MOE_XBLK


def _moe_dispatch(xn, ids, n_exp):
    m, d = xn.shape
    tm, xg = MOE_TM, MOE_XBLK
    n_pairs = 2 * m
    nxb = tm // xg
    n_tiles, n_xblk = _moe_sizes(m, n_exp)
    flat_e = ids[:, :2].reshape(-1)
    onehot = (flat_e[:, None] == jnp.arange(n_exp, dtype=jnp.int32)[None, :]).astype(jnp.int32)
    counts = jnp.sum(onehot, axis=0)
    tiles_per = (counts + tm - 1) // tm
    tile_end = jnp.cumsum(tiles_per)
    tile_start = tile_end - tiles_per
    xblk_per = (counts + xg - 1) // xg
    xblk_start = jnp.cumsum(xblk_per) - xblk_per
    rank = jnp.sum(onehot * (jnp.cumsum(onehot, axis=0) - 1), axis=1)
    pos = (jnp.sum(onehot * (tile_start * tm)[None, :], axis=1) + rank).reshape(m, 2)
    xpos = jnp.sum(onehot * (xblk_start * xg)[None, :], axis=1) + rank
    nt = tile_end[-1]
    tix = jnp.minimum(jnp.arange(n_tiles, dtype=jnp.int32), nt - 1)
    tile_e = jnp.minimum(jnp.searchsorted(tile_end, tix, side="right"), n_exp - 1).astype(jnp.int32)
    tile_xb = (xblk_start[tile_e] + (tix - tile_start[tile_e]) * nxb).astype(jnp.int32)
    row_tok = (jnp.arange(n_xblk * xg, dtype=jnp.int32) % m).at[xpos].set(
        jnp.arange(n_pairs, dtype=jnp.int32) // 2, mode="promise_in_bounds", unique_indices=True)
    xs = xn.at[row_tok].get(mode="promise_in_bounds")
    return xs, pos, tile_e, tile_xb, nt.reshape(1).astype(jnp.int32)


def _moe_experts(xs, tile_e, tile_xb, nt_arr, w_gate, w_up, w_down, m):
    d = xs.shape[1]
    n_exp, _, ff = w_gate.shape
    tm, xg = MOE_TM, MOE_XBLK
    nxb = tm // xg
    n_tiles, n_xblk = _moe_sizes(m, n_exp)
    n_rows = n_tiles * tm
    tk = _pick(d, (1024, 512, 256, 128))
    nk = d // tk
    live = lambda t, ntr: t < ntr[0]
    clamp = lambda t, ntr: jnp.minimum(t, ntr[0] - 1)
    kidx = lambda t, k, ntr: jnp.where(live(t, ntr), k, nk - 1)
    ns_up = MOE_DMA_SPLIT if tk % (MOE_DMA_SPLIT * LANE) == 0 else 1
    xspecs = [pl.BlockSpec((xg, tk), lambda t, k, te, txb, ntr, j=j: (jnp.minimum(txb[t] + j, n_xblk - 1),
                                                                     kidx(t, k, ntr))) for j in range(nxb)]
    wspecs = [pl.BlockSpec((None, tk // ns_up, ff),
                           lambda t, k, te, txb, ntr, s=s: (te[t], kidx(t, k, ntr) * ns_up + s, 0))
              for s in range(ns_up)]
    hdn = pl.pallas_call(
        functools.partial(_moe_up_kernel, nk=nk, n_split=ns_up, nxb=nxb),
        grid_spec=pltpu.PrefetchScalarGridSpec(
            num_scalar_prefetch=3, grid=(n_tiles, nk),
            in_specs=xspecs + wspecs + wspecs,
            out_specs=pl.BlockSpec((tm, ff), lambda t, k, te, txb, ntr: (clamp(t, ntr), 0)),
            scratch_shapes=[pltpu.VMEM((tm, tk), BF), pltpu.VMEM((tm, ff), F32), pltpu.VMEM((tm, ff), F32)]),
        out_shape=jax.ShapeDtypeStruct((n_rows, ff), BF),
        compiler_params=_cparams(("arbitrary", "arbitrary")),
        name="moe_up",
    )(tile_e, tile_xb, nt_arr, *([xs] * nxb), *([w_gate] * ns_up), *([w_up] * ns_up))

    tn = _pick(d, (2048, 1024, 512, 256, 128))
    nn = d // tn
    ns_dn = MOE_DMA_SPLIT if ff % (MOE_DMA_SPLIT * LANE) == 0 else 1
    jidx = lambda t, j, ntr: jnp.where(live(t, ntr), j, nn - 1)
    y = pl.pallas_call(
        functools.partial(_moe_down_kernel, n_split=ns_dn),
        grid_spec=pltpu.PrefetchScalarGridSpec(
            num_scalar_prefetch=2, grid=(n_tiles, nn),
            in_specs=[pl.BlockSpec((tm, ff), lambda t, j, te, ntr: (clamp(t, ntr), 0))]
            + [pl.BlockSpec((None, ff // ns_dn, tn), lambda t, j, te, ntr, s=s: (te[t], s, jidx(t, j, ntr)))
               for s in range(ns_dn)],
            out_specs=pl.BlockSpec((tm, tn), lambda t, j, te, ntr: (clamp(t, ntr), jidx(t, j, ntr)))),
        out_shape=jax.ShapeDtypeStruct((n_rows, d), BF),
        compiler_params=_cparams(("arbitrary", "arbitrary")),
        name="moe_down",
    )(tile_e, nt_arr, hdn, *([w_down] * ns_dn))
    return y


def _split_cols_kernel(x_ref, o_ref, *, dims, width):
    for i in range(dims[0]):
        if len(dims) == 1:
            o_ref[:, i, :] = x_ref[:, i * width:(i + 1) * width]
        else:
            for j in range(dims[1]):
                c = (i * dims[1] + j) * width
                o_ref[:, i, j, :] = x_ref[:, c:c + width]


def _split_cols(x, batch, rows_per_batch, row0, dims, width, tb=256):
    seq = x.shape[0] // batch
    tb = _pick(math.gcd(rows_per_batch, math.gcd(row0, seq)) if row0 else math.gcd(rows_per_batch, seq),
               (tb, 128, 64, 32, 16, 8))
    nb = rows_per_batch // tb
    ncol = x.shape[1]
    zeros = (0,) * (len(dims) + 1)
    return pl.pallas_call(
        functools.partial(_split_cols_kernel, dims=dims, width=width),
        grid=(batch, nb),
        in_specs=[pl.BlockSpec((tb, ncol), lambda b, i: ((b * seq + row0) // tb + i, 0))],
        out_specs=pl.BlockSpec((None, tb) + tuple(dims) + (width,), lambda b, i: (b, i) + zeros),
        out_shape=jax.ShapeDtypeStruct((batch, rows_per_batch) + tuple(dims) + (width,), x.dtype),
        compiler_params=_cparams(("arbitrary", "arbitrary")),
        name="split_cols",
    )(x)


def _shift_window_kernel(s_ref, w_ref, o_ref, *, t_new, n_kh):
    nbb, rows, _ = s_ref.shape
    keep = rows - t_new * n_kh
    o_ref[:, 0:keep, :] = s_ref[:, t_new * n_kh:rows, :]
    w = w_ref[...]
    for bb in range(nbb):
        for k in range(n_kh):
            o_ref[bb, pl.ds(keep + k, t_new, stride=n_kh), :] = w[bb * t_new:(bb + 1) * t_new, k * HD:(k + 1) * HD]


def _shift_window(state, win_s, t_new):
    _, db, wb, n_kind, n_head, hd = state.shape
    n_kh = n_kind * n_head
    rows = wb * n_kh
    nbb = _pick(db, (4, 2, 1))
    out = pl.pallas_call(
        functools.partial(_shift_window_kernel, t_new=t_new, n_kh=n_kh),
        grid=(db // nbb,),
        in_specs=[pl.BlockSpec((None, nbb, rows, hd), lambda i: (0, i, 0, 0)),
                  pl.BlockSpec((nbb * t_new, n_kh * hd), lambda i: (i, 0))],
        out_specs=pl.BlockSpec((nbb, rows, hd), lambda i: (i, 0, 0)),
        out_shape=jax.ShapeDtypeStruct((db, rows, hd), state.dtype),
        compiler_params=_cparams(("arbitrary",)),
        name="shift_window",
    )(state.reshape(state.shape[0], db, rows, hd), win_s)
    return out.reshape(1, db, wb, n_kind, n_head, hd)


O_KV = NSA_H * HD
O_GATE = O_KV + 6 * NSA_KV * HD
O_QD = O_GATE + 3 * NSA_H
W_QD = DF_H * 2 * HD
W_KD = DF_KV * 2 * HD


def _gate_weights(w):
    d = w.shape[0]
    wg = w[:, O_GATE:O_QD].reshape(d, 3, NSA_KV, NSA_G).transpose(0, 2, 1, 3).reshape(d, NSA_KV, 3 * NSA_G)
    return jnp.pad(wg, ((0, 0), (0, 0), (0, LANE - 3 * NSA_G))).reshape(d, NSA_KV * LANE)


def kernel(x_prompt, x_sample, cache_nsa_kv, cache_diff_kv, state_nsa_win, page_table, p_prompt, p_sample,
           rel_bias_table, norm_mix, w_in, w_cmp, diff_lambda, diff_subln, w_out, norm_ffn, w_router_group,
           b_router_group, w_router_expert, b_router_expert, w_exp_gate, w_exp_up, w_exp_down, norm_ple,
           w_ple_gate, w_ple_proj, final_norm):
    assert norm_mix.shape[0] == 1, "single-layer trunk"
    batch, seq, d = x_prompt.shape
    db, t_new, _ = x_sample.shape
    mp, ms = batch * seq, db * t_new
    m = mp + ms
    lam_init = 0.8 - 0.6 * math.exp(-0.3 * 0)
    tm = _pick(math.gcd(mp, ms), (1024, 512, 256, 128))
    tm_s = _pick(math.gcd(mp, ms), (256, 128))
    xp = x_prompt.reshape(mp, d)
    xs = x_sample.reshape(ms, d)
    tbl_t = rel_bias_table

    xn = _rms2(xp, xs, norm_mix[0], tm_s)
    w0 = w_in[0]
    w_diff = w0[:, O_QD:]
    w_gate = _gate_weights(w0)
    scale = HD ** -0.5
    tn = 512
    wide = functools.partial(_matmul, [xn], tm=tm, tn=tn)
    q_nsa = wide([(w0, 0)], rows=m, n_cols=O_KV, out_dtype=BF, scale=scale, name="proj_qn")
    qd = wide([(w_diff, 0)], rows=m, n_cols=W_QD, out_dtype=BF, scale=scale, name="proj_qd")
    gates = _matmul([xn], [(w_gate, 0)], rows=m, n_cols=NSA_KV * LANE, tm=tm, tn=NSA_KV * LANE, out_dtype=F32,
                    name="proj_gate")
    kv_w = [(w0, 0, lambda j: j + O_KV // tn)]
    win_w = [(w0, 0, lambda j: j + (O_KV + 4 * NSA_KV * HD) // tn)]
    dkv_w = [(w_diff, 0, lambda j: W_QD // DF_DV + j), (w_diff, 0, lambda j: (W_QD + W_KD) // DF_DV + j)]
    dkv = functools.partial(_matmul, [xn], dkv_w, tm=tm, tn=2 * DF_DV, n_cols=DF_KV * 4 * HD, out_dtype=F32,
                            epi="cat")
    kv4_p = wide(kv_w, rows=mp, n_cols=4 * NSA_KV * HD, out_dtype=F32, name="proj_kv_p")
    kv4_s = wide(kv_w, rows=ms, row0=mp, n_cols=4 * NSA_KV * HD, out_dtype=F32, name="proj_kv_s")
    win_p = wide(win_w, rows=mp, n_cols=2 * NSA_KV * HD, out_dtype=F32, name="proj_win_p")
    win_s = wide(win_w, rows=ms, row0=mp, n_cols=2 * NSA_KV * HD, out_dtype=F32, name="proj_win_s")
    dkv_p = dkv(rows=mp, name="proj_dkv_p")
    dkv_s = dkv(rows=ms, row0=mp, name="proj_dkv_s")

    nc = seq // CMP
    kvr = kv4_p.reshape(batch, seq, 4, NSA_KV, HD)
    cmp_out = []
    for kind in range(2):
        a = kvr[:, :, kind].transpose(2, 0, 1, 3).reshape(NSA_KV * batch * nc, CMP * HD).astype(BF)
        r = a.shape[0]
        cmp_out.append(_matmul([a], [(w_cmp[0, kind].reshape(CMP * HD, HD), 0)], rows=r, n_cols=HD,
                               tm=_pick(r, (512, 256, 128, 64, 32, 16)), tn=HD, out_dtype=BF,
                               name="compress").reshape(NSA_KV, batch, nc, HD))
    o_n = _nsa_prompt(q_nsa, gates, cmp_out[0], cmp_out[1], kv4_p, win_p, tbl_t, batch, seq, m)
    o_d = _diff_prompt(qd, dkv_p, tbl_t, diff_lambda[0], diff_subln[0], batch, seq, m, lam_init)

    o_n_s = _nsa_sample(q_nsa[mp:].astype(F32), gates[mp:], kv4_s, win_s, cache_nsa_kv, state_nsa_win,
                        page_table, w_cmp[0], tbl_t, t_new)
    o_d_s = _diff_sample(qd[mp:].astype(F32), dkv_s, cache_diff_kv, page_table, tbl_t, diff_lambda[0],
                         diff_subln[0], t_new, lam_init)
    o_n = lax.dynamic_update_slice(o_n, o_n_s.astype(BF), (mp, 0))
    o_d = lax.dynamic_update_slice(o_d, o_d_s.astype(BF), (mp, 0))

    h1 = _matmul([o_n, o_d], [(w_out[0], 0), (w_out[0], 1)], rows=m, n_cols=d, tm=tm,
                 tn=_pick(d, (512, 256, 128)), out_dtype=F32, epi="res2", epi_args=(xp, xs), name="out_proj")

    wr = jnp.concatenate([w_router_group[0], w_router_expert[0],
                          jnp.zeros((d, LANE - N_GROUPS - N_EXP), F32)], axis=1)
    br = jnp.concatenate([b_router_group[0], b_router_expert[0],
                          jnp.zeros((LANE - N_GROUPS - N_EXP,), F32)]).reshape(1, LANE)
    xn2, ids, wts = _router(h1, norm_ffn[0], wr, br, tm_s)
    xs_moe, pos, tile_e, tile_xb, nt_arr = _moe_dispatch(xn2, ids, N_EXP)
    wk = min(WIN, seq)
    nsa_kv_p = _split_cols(kv4_p, batch, seq, 0, (4, NSA_KV), HD)
    diff_kv_p = _split_cols(dkv_p, batch, seq, 0, (DF_KV,), 4 * HD)
    win_p_out = _split_cols(win_p, batch, wk, seq - wk, (2, NSA_KV), HD)
    new_win = _shift_window(state_nsa_win, win_s, t_new)
    y = _moe_experts(xs_moe, tile_e, tile_xb, nt_arr, w_exp_gate[0], w_exp_up[0], w_exp_down[0], m)
    h2, xn3 = _combine_rms(h1, y.at[pos[:, 0]].get(mode="promise_in_bounds"),
                           y.at[pos[:, 1]].get(mode="promise_in_bounds"), wts, norm_ple[0], tm_s)

    p_all = jnp.concatenate([p_prompt[0].reshape(mp, -1), p_sample[0].reshape(ms, -1)], axis=0).astype(BF)
    h3 = _matmul([xn3], [(w_ple_gate[0], 0)], rows=m, n_cols=d, tm=tm, tn=_pick(d, (512, 256, 128)),
                 out_dtype=F32, epi="ple", epi_args=(h2, p_all, w_ple_proj[0]), name="ple")

    y_p = _rms(h3, final_norm, tm_s, F32, row0=0, rows=mp).reshape(batch, seq, d)
    y_s = _rms(h3, final_norm, tm_s, F32, row0=mp, rows=ms).reshape(db, t_new, d)
    return (y_p, y_s,
            nsa_kv_p[None], kv4_s.reshape(1, db, t_new, 4, NSA_KV, HD),
            diff_kv_p[None], dkv_s.reshape(1, db, t_new, DF_KV, 4 * HD),
            win_p_out[None], new_win)
```

```python
import functools
import math

import numpy as np
import jax
import jax.numpy as jnp
from jax import lax
from jax.experimental import pallas as pl
from jax.experimental.pallas import tpu as pltpu

BF = jnp.bfloat16
F32 = jnp.float32

HD = 128
NSA_H = 16
NSA_KV = 2
NSA_G = NSA_H // NSA_KV
CMP = 32
SELB = 64
TOPK = 16
WIN = 512
DF_H = 8
DF_KV = 4
DF_G = DF_H // DF_KV
DF_DV = 2 * HD
REL_BUCKETS = 32
REL_MAX_DIST = 128
N_GROUPS = 4
EPG = 8
N_EXP = N_GROUPS * EPG
EPS = 1e-6
NEG = -1e30
FORCE = 1e4
LANE = 128
VMEM_LIMIT = 56 * 1024 * 1024
MOE_TM = 768
MOE_XBLK = 128
MOE_DMA_SPLIT = 4

QB_NSA = 128
PAD_SEL = 384
PAD_WIN = WIN
NEAR_SEL = 512
BAND_WIN = WIN + QB_NSA
QB_DF = 256
PAD_DF = 256
NEAR_DF = 512
TK = 256
CMP_PITCH = 40


def _dot(a, b):
    return jnp.dot(a, b, preferred_element_type=F32)


def _dot_nt(a, b):
    return lax.dot_general(a, b, (((1,), (1,)), ((), ())), preferred_element_type=F32)


def _cparams(sem):
    return pltpu.CompilerParams(dimension_semantics=sem, vmem_limit_bytes=VMEM_LIMIT)


def _pick(n, cands):
    for c in cands:
        if n % c == 0:
            return c
    raise ValueError(f"no tile in {cands} divides {n}")


def _rms2_kernel(xp_ref, xs_ref, g_ref, o_ref, *, np_tiles):
    i = pl.program_id(0)

    def go(x_ref):
        x = x_ref[...]
        ms = jnp.mean(x * x, axis=-1, keepdims=True)
        o_ref[...] = (x * lax.rsqrt(ms + EPS) * g_ref[...]).astype(o_ref.dtype)

    @pl.when(i < np_tiles)
    def _():
        go(xp_ref)

    @pl.when(i >= np_tiles)
    def _():
        go(xs_ref)


def _rms2(xp, xs, g, tm):
    mp, d = xp.shape
    ms = xs.shape[0]
    npt, nst = mp // tm, ms // tm
    return pl.pallas_call(
        functools.partial(_rms2_kernel, np_tiles=npt),
        grid=(npt + nst,),
        in_specs=[pl.BlockSpec((tm, d), lambda i: (jnp.minimum(i, npt - 1), 0)),
                  pl.BlockSpec((tm, d), lambda i: (jnp.maximum(i - npt, 0), 0)),
                  pl.BlockSpec((1, d), lambda i: (0, 0))],
        out_specs=pl.BlockSpec((tm, d), lambda i: (i, 0)),
        out_shape=jax.ShapeDtypeStruct((mp + ms, d), BF),
        compiler_params=_cparams(("arbitrary",)),
        name="rms2",
    )(xp, xs, g.reshape(1, d))


def _rms_kernel(x_ref, g_ref, o_ref):
    x = x_ref[...]
    ms = jnp.mean(x * x, axis=-1, keepdims=True)
    o_ref[...] = (x * lax.rsqrt(ms + EPS) * g_ref[...]).astype(o_ref.dtype)


def _rms(x, g, tm, out_dtype, row0=0, rows=None):
    m, d = x.shape
    rows = m if rows is None else rows
    t0 = row0 // tm
    return pl.pallas_call(
        _rms_kernel,
        grid=(rows // tm,),
        in_specs=[pl.BlockSpec((tm, d), lambda i: (i + t0, 0)),
                  pl.BlockSpec((1, d), lambda i: (0, 0))],
        out_specs=pl.BlockSpec((tm, d), lambda i: (i, 0)),
        out_shape=jax.ShapeDtypeStruct((rows, d), out_dtype),
        compiler_params=_cparams(("arbitrary",)),
        name="rms",
    )(x, g.reshape(1, d))


def _combine_rms_kernel(h_ref, y0_ref, y1_ref, w_ref, g_ref, h2_ref, xn_ref):
    w = w_ref[...]
    h2 = h_ref[...] + w[:, 0:1] * y0_ref[...].astype(F32) + w[:, 1:2] * y1_ref[...].astype(F32)
    h2_ref[...] = h2
    ms = jnp.mean(h2 * h2, axis=-1, keepdims=True)
    xn_ref[...] = (h2 * lax.rsqrt(ms + EPS) * g_ref[...]).astype(xn_ref.dtype)


def _combine_rms(h, y0, y1, wts, g, tm):
    m, d = h.shape
    row = lambda width: pl.BlockSpec((tm, width), lambda i: (i, 0))
    return pl.pallas_call(
        _combine_rms_kernel,
        grid=(m // tm,),
        in_specs=[row(d), row(d), row(d), row(LANE), pl.BlockSpec((1, d), lambda i: (0, 0))],
        out_specs=[row(d), row(d)],
        out_shape=[jax.ShapeDtypeStruct((m, d), F32), jax.ShapeDtypeStruct((m, d), BF)],
        compiler_params=_cparams(("arbitrary",)),
        name="combine_rms",
    )(h, y0, y1, wts, g.reshape(1, d))


def _cast_rows(src_ref, dst_ref):
    k = src_ref.shape[0]
    ch = 256 if k % 256 == 0 else k

    def body(c, carry):
        r = pl.multiple_of(c * ch, ch)
        dst_ref[pl.ds(r, ch), :] = src_ref[pl.ds(r, ch), :].astype(BF)
        return carry

    lax.fori_loop(0, k // ch, body, 0)


def _cast_rows_shifted(main_ref, tail_ref, dst_ref, off):
    k, width = dst_ref.shape
    ch = 256 if k % 256 == 0 else k

    def body(c, carry):
        r = pl.multiple_of(c * ch, ch)
        x = jnp.concatenate([main_ref[pl.ds(r, ch), :], tail_ref[pl.ds(r, ch), :]], axis=1)
        dst_ref[pl.ds(r, ch), :] = x[:, off:off + width].astype(BF)
        return carry

    lax.fori_loop(0, k // ch, body, 0)


def _mm_kernel(*refs, n_a, n_w, cast, epi, scale, np_tiles, lane_off):
    a = refs[:n_a]
    w = refs[n_a:n_a + n_w]
    idx = n_a + n_w
    if lane_off:
        tails = refs[idx:idx + n_w]
        idx += n_w
    if epi == "res2":
        rp_ref, rs_ref = refs[idx:idx + 2]
        idx += 2
    elif epi == "ple":
        h_ref, p_ref, wp_ref = refs[idx:idx + 3]
        idx += 3
    o_ref = refs[idx]
    idx += 1
    wb = refs[idx:idx + n_w] if cast else w
    i = pl.program_id(1)

    if cast:
        @pl.when(i == 0)
        def _():
            for k in range(n_w):
                if lane_off:
                    _cast_rows_shifted(w[k], tails[k], wb[k], lane_off)
                else:
                    _cast_rows(w[k], wb[k])

    if epi == "cat":
        x = a[0][...]
        wd = wb[0].shape[1]
        for k in range(n_w):
            o_ref[:, k * wd:(k + 1) * wd] = _dot(x, wb[k][...]).astype(o_ref.dtype)
        return
    acc = _dot(a[0][...], wb[0][...])
    for k in range(1, n_a):
        acc = acc + _dot(a[k][...], wb[k][...])
    if scale is not None:
        acc = acc * scale
    if epi is None:
        o_ref[...] = acc.astype(o_ref.dtype)
    elif epi == "res2":
        @pl.when(i < np_tiles)
        def _():
            o_ref[...] = (acc + rp_ref[...]).astype(o_ref.dtype)

        @pl.when(i >= np_tiles)
        def _():
            o_ref[...] = (acc + rs_ref[...]).astype(o_ref.dtype)
    elif epi == "ple":
        gate = jax.nn.sigmoid(acc)
        proj = _dot(p_ref[...], wp_ref[...].astype(BF))
        o_ref[...] = (h_ref[...] + gate * proj).astype(o_ref.dtype)


def _matmul(a_list, w_list, *, rows, n_cols, tm, tn, out_dtype, row0=0, col0=0,
            scale=None, epi=None, epi_args=(), lane_off=0, name="mm"):
    n_a, n_w = len(a_list), len(w_list)
    cast = w_list[0][0].dtype != BF
    assert cast or not lane_off
    t0, c0 = row0 // tm, col0 // tn
    gm, gn = rows // tm, n_cols // tn
    wn = tn // n_w if epi == "cat" else tn
    in_specs, args = [], []
    for a in a_list:
        in_specs.append(pl.BlockSpec((tm, a.shape[1]), lambda j, i: (i + t0, 0)))
        args.append(a)
    for k, ent in enumerate(w_list):
        w, kb = ent[0], ent[1]
        colfn = ent[2] if len(ent) > 2 else (lambda j: j + c0)
        kdim = a_list[min(k, n_a - 1)].shape[1]
        in_specs.append(pl.BlockSpec((kdim, wn), lambda j, i, kb=kb, colfn=colfn: (kb, colfn(j))))
        args.append(w)
    if lane_off:
        for k, ent in enumerate(w_list):
            kdim = a_list[min(k, n_a - 1)].shape[1]
            in_specs.append(pl.BlockSpec((kdim, LANE), lambda j, i, kb=ent[1], tailfn=ent[3]: (kb, tailfn(j))))
            args.append(ent[0])
    np_tiles = 0
    if epi == "res2":
        xp, xs = epi_args
        np_tiles = xp.shape[0] // tm
        in_specs.append(pl.BlockSpec((tm, tn), lambda j, i: (jnp.minimum(i, np_tiles - 1), j)))
        in_specs.append(pl.BlockSpec((tm, tn), lambda j, i: (jnp.maximum(i - np_tiles, 0), j)))
        args += [xp, xs]
    elif epi == "ple":
        h, p, wp = epi_args
        in_specs.append(pl.BlockSpec((tm, tn), lambda j, i: (i, j)))
        in_specs.append(pl.BlockSpec((tm, p.shape[1]), lambda j, i: (i, 0)))
        in_specs.append(pl.BlockSpec((wp.shape[0], tn), lambda j, i: (0, j)))
        args += [h, p, wp]
    scratch = [pltpu.VMEM((a_list[min(k, n_a - 1)].shape[1], wn), BF) for k in range(n_w)] if cast else []
    return pl.pallas_call(
        functools.partial(_mm_kernel, n_a=n_a, n_w=n_w, cast=cast, epi=epi, scale=scale, np_tiles=np_tiles,
                          lane_off=lane_off),
        grid=(gn, gm),
        in_specs=in_specs,
        out_specs=pl.BlockSpec((tm, tn), lambda j, i: (i, j)),
        out_shape=jax.ShapeDtypeStruct((rows, n_cols), out_dtype),
        scratch_shapes=scratch,
        compiler_params=_cparams(("arbitrary", "arbitrary")),
        name=name,
    )(*args)


def _bucket_np(dist):
    n = np.maximum(dist, 0)
    max_exact = REL_BUCKETS // 2
    nf = np.maximum(n, 1).astype(np.float32)
    log_b = max_exact + (np.log(nf / np.float32(max_exact)) / np.float32(math.log(REL_MAX_DIST / max_exact))
                         * np.float32(REL_BUCKETS - max_exact)).astype(np.int32)
    return np.where(n < max_exact, n, np.minimum(log_b, REL_BUCKETS - 1)).astype(np.int32)


def _bucket_edges():
    b = _bucket_np(np.arange(0, 4 * REL_MAX_DIST))
    return [(k, int(np.nonzero(b == k)[0].max())) for k in range(REL_BUCKETS - 1) if (b == k).any()]


def _pattern_kernel(tbl_ref, o_ref, *, h0, base, col_step, v_lo, v_hi, c_lim, sub_far, edges, keys_on_rows):
    h = pl.program_id(0) + h0
    shape = o_ref.shape
    row = lax.broadcasted_iota(jnp.int32, shape, 0) + pl.program_id(1) * shape[0]
    col = lax.broadcasted_iota(jnp.int32, shape, 1)
    if keys_on_rows:
        row, col = col, row
    dist = row + base - col_step * col
    far = tbl_ref[REL_BUCKETS - 1, h]
    b = jnp.full(shape, far, F32)
    for k, hi in reversed(edges):
        b = jnp.where(dist <= hi, tbl_ref[k, h], b)
    if sub_far:
        b = b - far
    valid = (dist >= v_lo) & (dist <= v_hi) & (col < c_lim)
    o_ref[...] = jnp.where(valid, b, NEG)


def _rel_pattern(tbl, h0, nh, nrows, ncols, base, v_lo, v_hi, sub_far, col_step=1, c_lim=None,
                 keys_on_rows=False):
    out_r, out_c = (ncols, nrows) if keys_on_rows else (nrows, ncols)
    tr = _pick(out_r, (512, 256, 128, 64, 32, 16, 8))
    return pl.pallas_call(
        functools.partial(_pattern_kernel, h0=h0, base=base, col_step=col_step, v_lo=v_lo, v_hi=v_hi,
                          c_lim=ncols if c_lim is None else c_lim, sub_far=sub_far, edges=_bucket_edges(),
                          keys_on_rows=keys_on_rows),
        grid=(nh, out_r // tr),
        in_specs=[pl.BlockSpec(memory_space=pltpu.SMEM)],
        out_specs=pl.BlockSpec((None, tr, out_c), lambda h, r: (h, r, 0)),
        out_shape=jax.ShapeDtypeStruct((nh, out_r, out_c), F32),
        compiler_params=_cparams(("arbitrary", "arbitrary")),
        name="rel_bias",
    )(tbl)


def _pair_sum_matrix(nc):
    n = np.arange(nc)[:, None]
    b = np.arange(LANE)[None, :]
    return jnp.asarray((n // (SELB // CMP) == b).astype(np.float32), dtype=BF)


def _expand_matrix(pad, n_keys):
    l = np.arange(LANE)[:, None]
    c = np.arange(pad + n_keys)[None, :]
    return jnp.asarray(((c >= pad) & ((c - pad) // SELB == l)).astype(np.float32), dtype=BF)


def _split3(x):
    hi = x.astype(BF)
    r = x - hi.astype(F32)
    mid = r.astype(BF)
    lo = (r - mid.astype(F32)).astype(BF)
    return hi, mid, lo


def _select_blocks(psum, s_mat, qpos, ns):
    hi, mid, lo = _split3(psum)
    imp = _dot(hi, s_mat) + _dot(mid, s_mat) + _dot(lo, s_mat)
    shape = imp.shape
    lane = lax.broadcasted_iota(jnp.int32, shape, 1)
    valid = lane * SELB <= qpos
    cur = jnp.right_shift(qpos, 6)
    forced = (lane == 0) | (lane == cur) | (lane == cur - 1)
    score = jnp.where(valid, imp + jnp.where(forced, FORCE, 0.0), NEG)
    score = jnp.where(lane < ns, score, -3e38)
    cnt = jnp.zeros(shape, F32)
    for i in range(ns):
        ci = score[:, i:i + 1]
        cnt = cnt + jnp.where(lane > i, jnp.where(ci >= score, 1.0, 0.0), jnp.where(ci > score, 1.0, 0.0))
    sel = (cnt < float(min(TOPK, ns))) & (lane < ns)
    return jnp.where(sel, 1.0, 0.0).astype(BF)


def _softmax_rows(s, valid=None):
    m = jnp.max(s, axis=-1, keepdims=True)
    p = jnp.exp(s - m)
    if valid is not None:
        p = jnp.where(valid, p, 0.0)
    l = jnp.sum(p, axis=-1, keepdims=True)
    return p / jnp.where(l > 0.0, l, 1.0)


def _lanes(x, n):
    return x if n == LANE else jnp.concatenate([x] * (n // LANE), axis=1)


def _online(carry, s, vt, ones_in_v=False):
    m, l, acc = carry
    dv = acc.shape[1]
    m_new = jnp.maximum(m, jnp.max(s, axis=-1, keepdims=True))
    p = jnp.exp(s - _lanes(m_new, s.shape[1]))
    alpha = jnp.exp(m - m_new)
    pv = _dot(p.astype(BF), vt)
    if ones_in_v:
        l = alpha * l + pv[:, dv:dv + LANE]
        pv = pv[:, 0:dv]
    else:
        l = alpha * l + jnp.sum(p, axis=-1, keepdims=True)
    return m_new, l, _lanes(alpha, dv) * acc + pv


def _online_init(rows, dv):
    return jnp.full((rows, LANE), NEG, F32), jnp.zeros((rows, LANE), F32), jnp.zeros((rows, dv), F32)


def _flash_step(q, kt, vt, bias, m_ref, l_ref, acc_ref, rows, ones_in_v=False):
    s = _dot_nt(q, kt)
    if bias is not None:
        s = s + bias
    m_new, l, acc = _online((m_ref[rows, :], l_ref[rows, :], acc_ref[rows, :]), s, vt, ones_in_v)
    m_ref[rows, :] = m_new
    l_ref[rows, :] = l
    acc_ref[rows, :] = acc


def _nsa_prompt_kernel(q_ref, gate_ref, kc_ref, vc_ref, ks_ref, vs_ref, kw_ref, vw_ref,
                       bc_ref, pn_ref, pw_ref, e_ref, s_ref, o_ref,
                       ks_s, vs_s, kw_s, vw_s, m_s, l_s, acc_s, o_s, *, seq, nc, ns):
    i = pl.program_id(2)
    s0 = i * QB_NSA
    rows = NSA_G * QB_NSA

    @pl.when(i == 0)
    def _():
        ks_s[0:PAD_SEL, :] = jnp.zeros((PAD_SEL, HD), BF)
        vs_s[0:PAD_SEL, :] = jnp.zeros((PAD_SEL, 2 * HD), BF)
        kw_s[0:PAD_WIN, :] = jnp.zeros((PAD_WIN, HD), BF)
        vw_s[0:PAD_WIN, :] = jnp.zeros((PAD_WIN, 2 * HD), BF)
        ch = 512
        ones = jnp.ones((ch, HD), BF)

        def cp(c, carry):
            r = pl.multiple_of(c * ch, ch)
            ks_s[pl.ds(PAD_SEL + r, ch), :] = ks_ref[pl.ds(r, ch), :].astype(BF)
            vs_s[pl.ds(PAD_SEL + r, ch), 0:HD] = vs_ref[pl.ds(r, ch), :].astype(BF)
            vs_s[pl.ds(PAD_SEL + r, ch), HD:2 * HD] = ones
            kw_s[pl.ds(PAD_WIN + r, ch), :] = kw_ref[pl.ds(r, ch), :].astype(BF)
            vw_s[pl.ds(PAD_WIN + r, ch), 0:HD] = vw_ref[pl.ds(r, ch), :].astype(BF)
            vw_s[pl.ds(PAD_WIN + r, ch), HD:2 * HD] = ones
            return carry

        lax.fori_loop(0, seq // ch, cp, 0)

    gt = jax.nn.sigmoid(gate_ref[...])
    head = lambda g: slice(g * HD, (g + 1) * HD)
    hrows = lambda g: slice(g * QB_NSA, (g + 1) * QB_NSA)

    kc = kc_ref[...]
    vc = vc_ref[...]
    psum = jnp.zeros((QB_NSA, nc), F32)
    for g in range(NSA_G):
        bc = bc_ref[g]
        pc = _softmax_rows(_dot_nt(q_ref[:, head(g)], kc) + bc, bc > 0.5 * NEG)
        psum = psum + pc
        o_s[:, head(g)] = gt[:, g:g + 1] * _dot(pc.astype(BF), vc)
    qpos = s0 + lax.broadcasted_iota(jnp.int32, (QB_NSA, LANE), 0)
    selb = _select_blocks(psum, s_ref[...], qpos, ns)

    nch = jnp.maximum(i - 1, 0) // 2
    far_keys = nch * TK
    m_s[...] = jnp.full((rows, LANE), NEG, F32)
    l_s[...] = jnp.zeros((rows, LANE), F32)
    acc_s[...] = jnp.zeros((rows, HD), F32)

    def sel_chunk(r, bias_of):
        kt = ks_s[pl.ds(r, TK), :]
        vt = vs_s[pl.ds(r, TK), :]
        madd = (_dot(selb, e_ref[:, pl.ds(r, TK)]) - 1.0) * (-NEG)
        for g in range(NSA_G):
            _flash_step(q_ref[:, head(g)], kt, vt, bias_of(g, madd), m_s, l_s, acc_s, hrows(g), ones_in_v=True)

    def far(c, carry):
        sel_chunk(pl.multiple_of(PAD_SEL + c * TK, LANE), lambda g, madd: madd)
        return carry

    lax.fori_loop(0, nch, far, 0)
    for kh in range(NEAR_SEL // TK):
        col = lax.broadcasted_iota(jnp.int32, (QB_NSA, TK), 1) + kh * TK
        cut = jnp.where(col < far_keys - s0 + PAD_SEL, NEG, 0.0)
        sel_chunk(pl.multiple_of(s0 + kh * TK, LANE),
                  lambda g, madd, kh=kh, cut=cut: pn_ref[g, :, kh * TK:(kh + 1) * TK] + madd + cut)
    for g in range(NSA_G):
        osel = acc_s[hrows(g), :] / l_s[hrows(g), :]
        o_s[:, head(g)] = o_s[:, head(g)] + gt[:, NSA_G + g:NSA_G + g + 1] * osel

    for g in range(NSA_G):
        st = _online_init(QB_NSA, HD)
        for c0 in range(0, BAND_WIN, TK):
            w = min(TK, BAND_WIN - c0)
            r = pl.multiple_of(s0 + c0, LANE)
            colw = lax.broadcasted_iota(jnp.int32, (QB_NSA, w), 1) + c0
            bias = pw_ref[g, :, c0:c0 + w] + jnp.where(colw < PAD_WIN - s0, NEG, 0.0)
            st = _online(st, _dot_nt(q_ref[:, head(g)], kw_s[pl.ds(r, w), :]) + bias, vw_s[pl.ds(r, w), :],
                         ones_in_v=True)
        ow = st[2] / st[1]
        o_ref[:, head(g)] = (o_s[:, head(g)] + gt[:, 2 * NSA_G + g:2 * NSA_G + g + 1] * ow).astype(o_ref.dtype)


def _nsa_prompt(q_nsa, gates, kc, vc, kv4_p, win_p, tbl_t, batch, seq, m_total):
    nc = seq // CMP
    ns = -(-seq // SELB)
    nqb = seq // QB_NSA
    big = 1 << 30
    pn = _rel_pattern(tbl_t, 0, NSA_H, QB_NSA, NEAR_SEL, PAD_SEL, 0, big, True)
    pw = _rel_pattern(tbl_t, 0, NSA_H, QB_NSA, BAND_WIN, PAD_WIN, 0, WIN, False)
    bc = _rel_pattern(tbl_t, 0, NSA_H, seq, nc, -(CMP - 1), 0, big, False, col_step=CMP)
    e_mat = _expand_matrix(PAD_SEL, seq)
    s_mat = _pair_sum_matrix(nc)
    kv_spec = lambda col: pl.BlockSpec((seq, HD), lambda b, h, i, col=col: (b, col + h))
    return pl.pallas_call(
        functools.partial(_nsa_prompt_kernel, seq=seq, nc=nc, ns=ns),
        grid=(batch, NSA_KV, nqb),
        in_specs=[
            pl.BlockSpec((QB_NSA, NSA_G * HD), lambda b, h, i: (b * nqb + i, h)),
            pl.BlockSpec((QB_NSA, LANE), lambda b, h, i: (b * nqb + i, h)),
            pl.BlockSpec((None, None, nc, HD), lambda b, h, i: (h, b, 0, 0)),
            pl.BlockSpec((None, None, nc, HD), lambda b, h, i: (h, b, 0, 0)),
            kv_spec(2 * NSA_KV), kv_spec(3 * NSA_KV),
            pl.BlockSpec((seq, HD), lambda b, h, i: (b, h)),
            pl.BlockSpec((seq, HD), lambda b, h, i: (b, NSA_KV + h)),
            pl.BlockSpec((NSA_G, QB_NSA, nc), lambda b, h, i: (h, i, 0)),
            pl.BlockSpec((NSA_G, QB_NSA, NEAR_SEL), lambda b, h, i: (h, 0, 0)),
            pl.BlockSpec((NSA_G, QB_NSA, BAND_WIN), lambda b, h, i: (h, 0, 0)),
            pl.BlockSpec((LANE, PAD_SEL + seq), lambda b, h, i: (0, 0)),
            pl.BlockSpec((nc, LANE), lambda b, h, i: (0, 0)),
        ],
        out_specs=pl.BlockSpec((QB_NSA, NSA_G * HD), lambda b, h, i: (b * nqb + i, h)),
        out_shape=jax.ShapeDtypeStruct((m_total, NSA_H * HD), BF),
        scratch_shapes=[pltpu.VMEM((PAD_SEL + seq, HD), BF), pltpu.VMEM((PAD_SEL + seq, 2 * HD), BF),
                        pltpu.VMEM((PAD_WIN + seq, HD), BF), pltpu.VMEM((PAD_WIN + seq, 2 * HD), BF),
                        pltpu.VMEM((NSA_G * QB_NSA, LANE), F32), pltpu.VMEM((NSA_G * QB_NSA, LANE), F32),
                        pltpu.VMEM((NSA_G * QB_NSA, HD), F32), pltpu.VMEM((QB_NSA, NSA_G * HD), F32)],
        compiler_params=_cparams(("arbitrary", "arbitrary", "arbitrary")),
        name="nsa_prompt",
    )(q_nsa, gates, kc, vc, kv4_p, kv4_p, win_p, win_p, bc, pn, pw, e_mat, s_mat)


def _diff_lambda(dl, lam_init):
    a = jnp.sum(dl[0:1] * dl[1:2], axis=-1, keepdims=True)
    b = jnp.sum(dl[2:3] * dl[3:4], axis=-1, keepdims=True)
    return jnp.exp(a) - jnp.exp(b) + lam_init


def _diff_finish(a, sub, lam_init):
    ms = jnp.mean(a * a, axis=-1, keepdims=True)
    return a * lax.rsqrt(ms + EPS) * sub * (1.0 - lam_init)


def _diff_prompt_kernel(q_ref, kv_ref, pn_ref, dl_ref, sub_ref, o_ref, kv_s, m_s, l_s, acc_s, *, seq, lam_init):
    i = pl.program_id(2)
    s0 = i * QB_DF
    rows = DF_G * QB_DF

    @pl.when(i == 0)
    def _():
        kv_s[0:PAD_DF, :] = jnp.zeros((PAD_DF, 4 * HD), BF)
        ch = 256

        def cp(c, carry):
            r = pl.multiple_of(c * ch, ch)
            kv_s[pl.ds(PAD_DF + r, ch), :] = kv_ref[pl.ds(r, ch), :].astype(BF)
            return carry

        lax.fori_loop(0, seq // ch, cp, 0)

    lam = _diff_lambda(dl_ref[...], lam_init)
    nfar = jnp.maximum(i - 1, 0)
    sub_rows = 128
    n_sub = QB_DF // sub_rows
    streams = [(m, g, j) for m in range(2) for g in range(DF_G) for j in range(n_sub)]
    srows = lambda k: slice(k * sub_rows, (k + 1) * sub_rows)
    m_s[...] = jnp.full((2 * rows, LANE), NEG, F32)
    l_s[...] = jnp.zeros((2 * rows, LANE), F32)
    acc_s[...] = jnp.zeros((2 * rows, DF_DV), F32)

    def chunk(r, bias_of):
        vt = kv_s[pl.ds(r, TK), 2 * HD:4 * HD]
        for k, (m, g, j) in enumerate(streams):
            q = q_ref[j * sub_rows:(j + 1) * sub_rows, (g * 2 + m) * HD:(g * 2 + m + 1) * HD]
            kt = kv_s[pl.ds(r, TK), m * HD:(m + 1) * HD]
            _flash_step(q, kt, vt, bias_of(g, j), m_s, l_s, acc_s, srows(k))

    def far(c, carry):
        chunk(pl.multiple_of(PAD_DF + c * TK, TK), lambda g, j: None)
        return carry

    lax.fori_loop(0, nfar, far, 0)
    for kh in range(NEAR_DF // TK):
        col = lax.broadcasted_iota(jnp.int32, (sub_rows, TK), 1) + kh * TK
        cut = jnp.where(col < nfar * TK - s0 + PAD_DF, NEG, 0.0)
        chunk(pl.multiple_of(s0 + kh * TK, TK),
              lambda g, j, kh=kh, cut=cut: pn_ref[g, j * sub_rows:(j + 1) * sub_rows, kh * TK:(kh + 1) * TK] + cut)
    half = len(streams) // 2
    for k, (_, g, j) in enumerate(streams[:half]):
        o1 = acc_s[srows(k), :] / _lanes(l_s[srows(k), :], DF_DV)
        o2 = acc_s[srows(half + k), :] / _lanes(l_s[srows(half + k), :], DF_DV)
        out = _diff_finish(o1 - lam * o2, sub_ref[...], lam_init)
        o_ref[j * sub_rows:(j + 1) * sub_rows, g * DF_DV:(g + 1) * DF_DV] = out.astype(o_ref.dtype)


def _diff_prompt(qd, dkv_p, tbl_t, dl, sub, batch, seq, m_total, lam_init):
    nqb = seq // QB_DF
    pn = _rel_pattern(tbl_t, NSA_H, DF_H, QB_DF, NEAR_DF, PAD_DF, 0, 1 << 30, True)
    width = DF_G * 2 * HD
    return pl.pallas_call(
        functools.partial(_diff_prompt_kernel, seq=seq, lam_init=lam_init),
        grid=(batch, DF_KV, nqb),
        in_specs=[
            pl.BlockSpec((QB_DF, width), lambda b, h, i: (b * nqb + i, h)),
            pl.BlockSpec((seq, 4 * HD), lambda b, h, i: (b, h)),
            pl.BlockSpec((DF_G, QB_DF, NEAR_DF), lambda b, h, i: (h, 0, 0)),
            pl.BlockSpec((4, HD), lambda b, h, i: (0, 0)),
            pl.BlockSpec((1, DF_DV), lambda b, h, i: (0, 0)),
        ],
        out_specs=pl.BlockSpec((QB_DF, DF_G * DF_DV), lambda b, h, i: (b * nqb + i, h)),
        out_shape=jax.ShapeDtypeStruct((m_total, DF_H * DF_DV), BF),
        scratch_shapes=[pltpu.VMEM((PAD_DF + seq, 4 * HD), BF),
                        pltpu.VMEM((2 * DF_G * QB_DF, LANE), F32), pltpu.VMEM((2 * DF_G * QB_DF, LANE), F32),
                        pltpu.VMEM((2 * DF_G * QB_DF, DF_DV), F32)],
        compiler_params=_cparams(("arbitrary", "arbitrary", "arbitrary")),
        name="diff_prompt",
    )(qd, dkv_p, pn, dl, sub.reshape(1, DF_DV))


def _tail_tile(new, width):
    t = new.shape[0]
    return jnp.concatenate([new, jnp.zeros((LANE - t, width), F32)], axis=0).astype(BF)


def _nsa_sample_kernel(pt_ref, *refs, n_pages, page, past, t_new, ncs, ns, wb):
    pages = refs[:n_pages]
    (q_ref, gate_ref, kvn_ref, wn_ref, st_ref, wc_ref, bc_ref, bs_ref, bw_ref, e_ref, s_ref,
     o_ref, kcmp_s, ksel_s, kwin_s) = refs[n_pages:]
    del pt_ref
    rows = NSA_G * t_new
    n_kinds = 4

    for p in range(n_pages):
        for kind in range(n_kinds):
            for h in range(NSA_KV):
                blk = pages[p][pl.ds(kind * NSA_KV + h, page, stride=n_kinds * NSA_KV), :]
                if kind < 2:
                    for nb in range(page // CMP):
                        r0 = (p * (page // CMP) + nb) * CMP_PITCH
                        kcmp_s[kind, h, r0:r0 + CMP, :] = blk[nb * CMP:(nb + 1) * CMP]
                else:
                    ksel_s[kind - 2, h, p * page:(p + 1) * page, :] = blk.astype(BF)
    kvn = kvn_ref[...]
    wn = wn_ref[...]
    for kind in range(2):
        for h in range(NSA_KV):
            c0 = ((kind + 2) * NSA_KV + h) * HD
            ksel_s[kind, h, past:past + LANE, :] = _tail_tile(kvn[:, c0:c0 + HD], HD)
            kwin_s[kind, h, 0:wb, :] = st_ref[pl.ds(kind * NSA_KV + h, wb, stride=2 * NSA_KV), :].astype(BF)
            c0 = (kind * NSA_KV + h) * HD
            kwin_s[kind, h, wb:wb + LANE, :] = _tail_tile(wn[:, c0:c0 + HD], HD)

    q = q_ref[...]
    gt = jax.nn.sigmoid(gate_ref[...])
    qpos = past + lax.broadcasted_iota(jnp.int32, (t_new, LANE), 0)
    for h in range(NSA_KV):
        cmp = []
        for kind in range(2):
            acc = jnp.zeros((ncs, HD), F32)
            for j in range(CMP):
                kj = kcmp_s[kind, h, pl.ds(j, ncs, stride=CMP_PITCH), :]
                acc = acc + _dot(kj.astype(BF), wc_ref[kind, j * HD:(j + 1) * HD, :])
            cmp.append(acc.astype(BF))
        kc, vc = cmp
        qs = jnp.concatenate([q[:, (h * NSA_G + g) * HD:(h * NSA_G + g + 1) * HD] for g in range(NSA_G)],
                             axis=0).astype(BF)
        bc = bc_ref[h * NSA_G:(h + 1) * NSA_G].reshape(rows, ncs)
        pc = _softmax_rows(_dot_nt(qs, kc) + bc, bc > 0.5 * NEG)
        oc = _dot(pc.astype(BF), vc)
        psum = pc[0:t_new]
        for g in range(1, NSA_G):
            psum = psum + pc[g * t_new:(g + 1) * t_new]
        selb = _select_blocks(psum, s_ref[...], qpos, ns)
        lk = past + LANE
        madd = (_dot(selb, e_ref[...]) - 1.0) * (-NEG)
        s = _dot_nt(qs, ksel_s[0, h])
        s = (s.reshape(NSA_G, t_new, lk) + bs_ref[h * NSA_G:(h + 1) * NSA_G] + madd[None]).reshape(rows, lk)
        osel = _dot(_softmax_rows(s).astype(BF), ksel_s[1, h])
        lw = wb + LANE
        s = _dot_nt(qs, kwin_s[0, h]) + bw_ref[h * NSA_G:(h + 1) * NSA_G].reshape(rows, lw)
        ow = _dot(_softmax_rows(s).astype(BF), kwin_s[1, h])
        for g in range(NSA_G):
            sl = slice(g * t_new, (g + 1) * t_new)
            gl = h * LANE + g
            o = (gt[:, gl:gl + 1] * oc[sl] + gt[:, gl + NSA_G:gl + NSA_G + 1] * osel[sl]
                 + gt[:, gl + 2 * NSA_G:gl + 2 * NSA_G + 1] * ow[sl])
            o_ref[:, (h * NSA_G + g) * HD:(h * NSA_G + g + 1) * HD] = o


def _nsa_sample(q_s, gates_s, kv4_s, win_s, cache, state, page_table, w_cmp, tbl_t, t_new):
    db, n_pages = page_table.shape
    n_phys, page = cache.shape[1], cache.shape[2]
    past = n_pages * page
    wb = state.shape[2]
    assert (past + t_new) // CMP * CMP <= past and past % SELB == 0 and wb == min(WIN, past)
    ncs = (past + t_new) // CMP
    ns = -(-(past + t_new) // SELB)
    lk, lw = past + LANE, wb + LANE
    rows_pp = page * 4 * NSA_KV
    cache2 = cache.reshape(cache.shape[0], n_phys, rows_pp, HD)
    state2 = state.reshape(state.shape[0], db, wb * 2 * NSA_KV, HD)
    wc = w_cmp.reshape(2, CMP * HD, HD).astype(BF)
    big = 1 << 30
    bc = _rel_pattern(tbl_t, 0, NSA_H, t_new, ncs, past - (CMP - 1), 0, big, False, col_step=CMP)
    bs = _rel_pattern(tbl_t, 0, NSA_H, t_new, lk, past, 0, big, False, c_lim=past + t_new)
    bw = _rel_pattern(tbl_t, 0, NSA_H, t_new, lw, wb, 0, WIN, False, c_lim=wb + t_new)
    e_mat = _expand_matrix(0, lk)
    s_mat = _pair_sum_matrix(ncs)
    full = lambda shape: pl.BlockSpec(shape, lambda b, pt: (0,) * len(shape))
    page_specs = [pl.BlockSpec((None, None, rows_pp, HD), lambda b, pt, p=p: (0, pt[b, p], 0, 0))
                  for p in range(n_pages)]
    in_specs = page_specs + [
        pl.BlockSpec((t_new, NSA_H * HD), lambda b, pt: (b, 0)),
        pl.BlockSpec((t_new, NSA_KV * LANE), lambda b, pt: (b, 0)),
        pl.BlockSpec((t_new, 4 * NSA_KV * HD), lambda b, pt: (b, 0)),
        pl.BlockSpec((t_new, 2 * NSA_KV * HD), lambda b, pt: (b, 0)),
        pl.BlockSpec((None, None, wb * 2 * NSA_KV, HD), lambda b, pt: (0, b, 0, 0)),
        full((2, CMP * HD, HD)), full((NSA_H, t_new, ncs)), full((NSA_H, t_new, lk)),
        full((NSA_H, t_new, lw)), full((LANE, lk)), full((ncs, LANE)),
    ]
    return pl.pallas_call(
        functools.partial(_nsa_sample_kernel, n_pages=n_pages, page=page, past=past, t_new=t_new,
                          ncs=ncs, ns=ns, wb=wb),
        grid_spec=pltpu.PrefetchScalarGridSpec(
            num_scalar_prefetch=1, grid=(db,), in_specs=in_specs,
            out_specs=pl.BlockSpec((t_new, NSA_H * HD), lambda b, pt: (b, 0)),
            scratch_shapes=[pltpu.VMEM((2, NSA_KV, ncs * CMP_PITCH, HD), F32),
                            pltpu.VMEM((2, NSA_KV, lk, HD), BF),
                            pltpu.VMEM((2, NSA_KV, lw, HD), BF)]),
        out_shape=jax.ShapeDtypeStruct((db * t_new, NSA_H * HD), F32),
        compiler_params=_cparams(("arbitrary",)),
        name="nsa_sample",
    )(page_table, *([cache2] * n_pages), q_s, gates_s, kv4_s, win_s, state2, wc, bc, bs, bw, e_mat, s_mat)


def _diff_sample_kernel(pt_ref, *refs, n_pages, page, past, t_new, lam_init):
    pages = refs[:n_pages]
    q_ref, kvn_ref, b_ref, dl_ref, sub_ref, o_ref, kv_s, stage_s = refs[n_pages:]
    del pt_ref
    rows = DF_G * t_new
    lk = past + LANE
    for p in range(n_pages):
        for h in range(DF_KV):
            stage_s[h] = pages[p][:, h, :]
            kv_s[h, p * page:(p + 1) * page, :] = stage_s[h].astype(BF)
    kvn = kvn_ref[...]
    for h in range(DF_KV):
        kv_s[h, past:past + LANE, :] = _tail_tile(kvn[:, h * 4 * HD:(h + 1) * 4 * HD], 4 * HD)
    lam = _diff_lambda(dl_ref[...], lam_init)
    q = q_ref[...]
    for h in range(DF_KV):
        bias = b_ref[h * DF_G:(h + 1) * DF_G].reshape(rows, lk)
        ps = []
        for m in range(2):
            qm = jnp.concatenate(
                [q[:, ((h * DF_G + g) * 2 + m) * HD:((h * DF_G + g) * 2 + m + 1) * HD] for g in range(DF_G)],
                axis=0).astype(BF)
            ps.append(_softmax_rows(_dot_nt(qm, kv_s[h, :, m * HD:(m + 1) * HD]) + bias))
        a = ps[0] - lam * ps[1]
        out = _diff_finish(_dot(a.astype(BF), kv_s[h, :, 2 * HD:4 * HD]), sub_ref[...], lam_init)
        for g in range(DF_G):
            o_ref[:, (h * DF_G + g) * DF_DV:(h * DF_G + g + 1) * DF_DV] = out[g * t_new:(g + 1) * t_new]


def _diff_sample(qd_s, dkv_s, cache, page_table, tbl_t, dl, sub, t_new, lam_init):
    db, n_pages = page_table.shape
    page = cache.shape[2]
    past = n_pages * page
    lk = past + LANE
    bias = _rel_pattern(tbl_t, NSA_H, DF_H, t_new, lk, past, 0, 1 << 30, False, c_lim=past + t_new)
    full = lambda shape: pl.BlockSpec(shape, lambda b, pt: (0,) * len(shape))
    page_specs = [pl.BlockSpec((None, None, page, DF_KV, 4 * HD), lambda b, pt, p=p: (0, pt[b, p], 0, 0, 0))
                  for p in range(n_pages)]
    in_specs = page_specs + [
        pl.BlockSpec((t_new, DF_H * 2 * HD), lambda b, pt: (b, 0)),
        pl.BlockSpec((t_new, DF_KV * 4 * HD), lambda b, pt: (b, 0)),
        full((DF_H, t_new, lk)), full((4, HD)), full((1, DF_DV)),
    ]
    return pl.pallas_call(
        functools.partial(_diff_sample_kernel, n_pages=n_pages, page=page, past=past, t_new=t_new,
                          lam_init=lam_init),
        grid_spec=pltpu.PrefetchScalarGridSpec(
            num_scalar_prefetch=1, grid=(db,), in_specs=in_specs,
            out_specs=pl.BlockSpec((t_new, DF_H * DF_DV), lambda b, pt: (b, 0)),
            scratch_shapes=[pltpu.VMEM((DF_KV, lk, 4 * HD), BF), pltpu.VMEM((DF_KV, page, 4 * HD), F32)]),
        out_shape=jax.ShapeDtypeStruct((db * t_new, DF_H * DF_DV), F32),
        compiler_params=_cparams(("arbitrary",)),
        name="diff_sample",
    )(page_table, *([cache] * n_pages), qd_s, dkv_s, bias, dl, sub.reshape(1, DF_DV))


def _router_kernel(h_ref, g_ref, wr_ref, br_ref, xn_ref, ids_ref, wts_ref):
    x = h_ref[...]
    ms = jnp.mean(x * x, axis=-1, keepdims=True)
    xn = x * lax.rsqrt(ms + EPS) * g_ref[...]
    xh = xn.astype(BF)
    xn_ref[...] = xh
    xl = (xn - xh.astype(F32)).astype(BF)
    wr = wr_ref[...]
    wh = wr.astype(BF)
    wl = (wr - wh.astype(F32)).astype(BF)
    lg = _dot(xh, wh) + _dot(xl, wh) + _dot(xh, wl) + br_ref[...]
    lane_i = lax.broadcasted_iota(jnp.int32, lg.shape, 1)
    lane = lane_i.astype(F32)
    big = 1000.0
    isg = lane_i < N_GROUPS
    gmax = jnp.max(jnp.where(isg, lg, -3e38), axis=-1, keepdims=True)
    gsel = jnp.min(jnp.where(isg & (lg == gmax), lane, big), axis=-1, keepdims=True)
    gw = 1.0 / jnp.sum(jnp.where(isg, jnp.exp(lg - gmax), 0.0), axis=-1, keepdims=True)
    lo = N_GROUPS + gsel * EPG
    ing = (lane >= lo) & (lane < lo + EPG)
    emax = jnp.max(jnp.where(ing, lg, -3e38), axis=-1, keepdims=True)
    pe = jnp.where(ing, jnp.exp(lg - emax), 0.0)
    pr = jnp.where(ing, pe / jnp.sum(pe, axis=-1, keepdims=True), -1.0)
    v1 = jnp.max(pr, axis=-1, keepdims=True)
    i1 = jnp.min(jnp.where(pr == v1, lane, big), axis=-1, keepdims=True)
    pr2 = jnp.where(lane == i1, -1.0, pr)
    v2 = jnp.max(pr2, axis=-1, keepdims=True)
    i2 = jnp.min(jnp.where(pr2 == v2, lane, big), axis=-1, keepdims=True)
    den = v1 + v2
    e12 = jnp.where(lane_i == 0, i1 - N_GROUPS, jnp.where(lane_i == 1, i2 - N_GROUPS, 0.0))
    ids_ref[...] = e12.astype(jnp.int32)
    wts_ref[...] = jnp.where(lane_i == 0, v1 / den * gw, jnp.where(lane_i == 1, v2 / den * gw, 0.0))


def _router(h, g, wr, br, tm):
    m, d = h.shape
    return pl.pallas_call(
        _router_kernel,
        grid=(m // tm,),
        in_specs=[pl.BlockSpec((tm, d), lambda i: (i, 0)), pl.BlockSpec((1, d), lambda i: (0, 0)),
                  pl.BlockSpec((d, LANE), lambda i: (0, 0)), pl.BlockSpec((1, LANE), lambda i: (0, 0))],
        out_specs=[pl.BlockSpec((tm, d), lambda i: (i, 0)), pl.BlockSpec((tm, LANE), lambda i: (i, 0)),
                   pl.BlockSpec((tm, LANE), lambda i: (i, 0))],
        out_shape=[jax.ShapeDtypeStruct((m, d), BF), jax.ShapeDtypeStruct((m, LANE), jnp.int32),
                   jax.ShapeDtypeStruct((m, LANE), F32)],
        compiler_params=_cparams(("arbitrary",)),
        name="router",
    )(h, g.reshape(1, d), wr, br)


def _split_dot(x_ref, w_refs):
    rows = w_refs[0].shape[0]
    acc = _dot(x_ref[:, 0:rows], w_refs[0][...].astype(BF))
    for s in range(1, len(w_refs)):
        acc = acc + _dot(x_ref[:, s * rows:(s + 1) * rows], w_refs[s][...].astype(BF))
    return acc


def _moe_up_kernel(te_ref, txb_ref, nt_ref, *refs, nk, n_split, nxb):
    xb = refs[:nxb]
    wg = refs[nxb:nxb + n_split]
    wu = refs[nxb + n_split:nxb + 2 * n_split]
    o_ref, x_s, a_s, u_s = refs[nxb + 2 * n_split:]
    k = pl.program_id(1)
    xg = xb[0].shape[0]

    @pl.when(pl.program_id(0) < nt_ref[0])
    def _():
        @pl.when(k == 0)
        def _():
            a_s[...] = jnp.zeros(a_s.shape, F32)
            u_s[...] = jnp.zeros(u_s.shape, F32)

        for j in range(nxb):
            x_s[j * xg:(j + 1) * xg, :] = xb[j][...]
        a_s[...] += _split_dot(x_s, wg)
        u_s[...] += _split_dot(x_s, wu)

        @pl.when(k == nk - 1)
        def _():
            g = a_s[...]
            o_ref[...] = (g * jax.nn.sigmoid(g) * u_s[...]).astype(o_ref.dtype)


def _moe_down_kernel(te_ref, nt_ref, h_ref, *refs, n_split):
    o_ref = refs[n_split]

    @pl.when(pl.program_id(0) < nt_ref[0])
    def _():
        o_ref[...] = _split_dot(h_ref, refs[:n_split]).astype(o_ref.dtype)


def _moe_sizes(m, n_exp):
    n_pairs = 2 * m
    return n_exp + n_pairs // MOE_TM, n_exp + n_pairs // MOE_XBLK


def _moe_dispatch(xn, ids, n_exp):
    m, d = xn.shape
    tm, xg = MOE_TM, MOE_XBLK
    n_pairs = 2 * m
    nxb = tm // xg
    n_tiles, n_xblk = _moe_sizes(m, n_exp)
    flat_e = ids[:, :2].reshape(-1)
    onehot = (flat_e[:, None] == jnp.arange(n_exp, dtype=jnp.int32)[None, :]).astype(jnp.int32)
    counts = jnp.sum(onehot, axis=0)
    tiles_per = (counts + tm - 1) // tm
    tile_end = jnp.cumsum(tiles_per)
    tile_start = tile_end - tiles_per
    xblk_per = (counts + xg - 1) // xg
    xblk_start = jnp.cumsum(xblk_per) - xblk_per
    rank = jnp.sum(onehot * (jnp.cumsum(onehot, axis=0) - 1), axis=1)
    pos = (jnp.sum(onehot * (tile_start * tm)[None, :], axis=1) + rank).reshape(m, 2)
    xpos = jnp.sum(onehot * (xblk_start * xg)[None, :], axis=1) + rank
    nt = tile_end[-1]
    tix = jnp.minimum(jnp.arange(n_tiles, dtype=jnp.int32), nt - 1)
    tile_e = jnp.minimum(jnp.searchsorted(tile_end, tix, side="right"), n_exp - 1).astype(jnp.int32)
    tile_xb = (xblk_start[tile_e] + (tix - tile_start[tile_e]) * nxb).astype(jnp.int32)
    row_tok = (jnp.arange(n_xblk * xg, dtype=jnp.int32) % m).at[xpos].set(
        jnp.arange(n_pairs, dtype=jnp.int32) // 2, mode="promise_in_bounds", unique_indices=True)
    xs = xn.at[row_tok].get(mode="promise_in_bounds")
    return xs, pos, tile_e, tile_xb, nt.reshape(1).astype(jnp.int32)


def _moe_experts(xs, tile_e, tile_xb, nt_arr, w_gate, w_up, w_down, m):
    d = xs.shape[1]
    n_exp, _, ff = w_gate.shape
    tm, xg = MOE_TM, MOE_XBLK
    nxb = tm // xg
    n_tiles, n_xblk = _moe_sizes(m, n_exp)
    n_rows = n_tiles * tm
    tk = _pick(d, (1024, 512, 256, 128))
    nk = d // tk
    live = lambda t, ntr: t < ntr[0]
    clamp = lambda t, ntr: jnp.minimum(t, ntr[0] - 1)
    kidx = lambda t, k, ntr: jnp.where(live(t, ntr), k, nk - 1)
    ns_up = MOE_DMA_SPLIT if tk % (MOE_DMA_SPLIT * LANE) == 0 else 1
    xspecs = [pl.BlockSpec((xg, tk), lambda t, k, te, txb, ntr, j=j: (jnp.minimum(txb[t] + j, n_xblk - 1),
                                                                     kidx(t, k, ntr))) for j in range(nxb)]
    wspecs = [pl.BlockSpec((None, tk // ns_up, ff),
                           lambda t, k, te, txb, ntr, s=s: (te[t], kidx(t, k, ntr) * ns_up + s, 0))
              for s in range(ns_up)]
    hdn = pl.pallas_call(
        functools.partial(_moe_up_kernel, nk=nk, n_split=ns_up, nxb=nxb),
        grid_spec=pltpu.PrefetchScalarGridSpec(
            num_scalar_prefetch=3, grid=(n_tiles, nk),
            in_specs=xspecs + wspecs + wspecs,
            out_specs=pl.BlockSpec((tm, ff), lambda t, k, te, txb, ntr: (clamp(t, ntr), 0)),
            scratch_shapes=[pltpu.VMEM((tm, tk), BF), pltpu.VMEM((tm, ff), F32), pltpu.VMEM((tm, ff), F32)]),
        out_shape=jax.ShapeDtypeStruct((n_rows, ff), BF),
        compiler_params=_cparams(("arbitrary", "arbitrary")),
        name="moe_up",
    )(tile_e, tile_xb, nt_arr, *([xs] * nxb), *([w_gate] * ns_up), *([w_up] * ns_up))

    tn = _pick(d, (2048, 1024, 512, 256, 128))
    nn = d // tn
    ns_dn = MOE_DMA_SPLIT if ff % (MOE_DMA_SPLIT * LANE) == 0 else 1
    jidx = lambda t, j, ntr: jnp.where(live(t, ntr), j, nn - 1)
    y = pl.pallas_call(
        functools.partial(_moe_down_kernel, n_split=ns_dn),
        grid_spec=pltpu.PrefetchScalarGridSpec(
            num_scalar_prefetch=2, grid=(n_tiles, nn),
            in_specs=[pl.BlockSpec((tm, ff), lambda t, j, te, ntr: (clamp(t, ntr), 0))]
            + [pl.BlockSpec((None, ff // ns_dn, tn), lambda t, j, te, ntr, s=s: (te[t], s, jidx(t, j, ntr)))
               for s in range(ns_dn)],
            out_specs=pl.BlockSpec((tm, tn), lambda t, j, te, ntr: (clamp(t, ntr), jidx(t, j, ntr)))),
        out_shape=jax.ShapeDtypeStruct((n_rows, d), BF),
        compiler_params=_cparams(("arbitrary", "arbitrary")),
        name="moe_down",
    )(tile_e, nt_arr, hdn, *([w_down] * ns_dn))
    return y


def _split_cols_kernel(x_ref, o_ref, *, dims, width):
    for i in range(dims[0]):
        if len(dims) == 1:
            o_ref[:, i, :] = x_ref[:, i * width:(i + 1) * width]
        else:
            for j in range(dims[1]):
                c = (i * dims[1] + j) * width
                o_ref[:, i, j, :] = x_ref[:, c:c + width]


def _split_cols(x, batch, rows_per_batch, row0, dims, width, tb=256):
    seq = x.shape[0] // batch
    tb = _pick(math.gcd(rows_per_batch, math.gcd(row0, seq)) if row0 else math.gcd(rows_per_batch, seq),
               (tb, 128, 64, 32, 16, 8))
    nb = rows_per_batch // tb
    ncol = x.shape[1]
    zeros = (0,) * (len(dims) + 1)
    return pl.pallas_call(
        functools.partial(_split_cols_kernel, dims=dims, width=width),
        grid=(batch, nb),
        in_specs=[pl.BlockSpec((tb, ncol), lambda b, i: ((b * seq + row0) // tb + i, 0))],
        out_specs=pl.BlockSpec((None, tb) + tuple(dims) + (width,), lambda b, i: (b, i) + zeros),
        out_shape=jax.ShapeDtypeStruct((batch, rows_per_batch) + tuple(dims) + (width,), x.dtype),
        compiler_params=_cparams(("arbitrary", "arbitrary")),
        name="split_cols",
    )(x)


def _shift_window_kernel(s_ref, w_ref, o_ref, *, t_new, n_kh):
    nbb, rows, _ = s_ref.shape
    keep = rows - t_new * n_kh
    o_ref[:, 0:keep, :] = s_ref[:, t_new * n_kh:rows, :]
    w = w_ref[...]
    for bb in range(nbb):
        for k in range(n_kh):
            o_ref[bb, pl.ds(keep + k, t_new, stride=n_kh), :] = w[bb * t_new:(bb + 1) * t_new, k * HD:(k + 1) * HD]


def _shift_window(state, win_s, t_new):
    _, db, wb, n_kind, n_head, hd = state.shape
    n_kh = n_kind * n_head
    rows = wb * n_kh
    nbb = _pick(db, (4, 2, 1))
    out = pl.pallas_call(
        functools.partial(_shift_window_kernel, t_new=t_new, n_kh=n_kh),
        grid=(db // nbb,),
        in_specs=[pl.BlockSpec((None, nbb, rows, hd), lambda i: (0, i, 0, 0)),
                  pl.BlockSpec((nbb * t_new, n_kh * hd), lambda i: (i, 0))],
        out_specs=pl.BlockSpec((nbb, rows, hd), lambda i: (i, 0, 0)),
        out_shape=jax.ShapeDtypeStruct((db, rows, hd), state.dtype),
        compiler_params=_cparams(("arbitrary",)),
        name="shift_window",
    )(state.reshape(state.shape[0], db, rows, hd), win_s)
    return out.reshape(1, db, wb, n_kind, n_head, hd)


O_KV = NSA_H * HD
O_GATE = O_KV + 6 * NSA_KV * HD
O_QD = O_GATE + 3 * NSA_H
W_QD = DF_H * 2 * HD
W_KD = DF_KV * 2 * HD


def _gate_lanes(raw):
    m = raw.shape[0]
    g = raw[:, :3 * NSA_H].reshape(m, 3, NSA_KV, NSA_G).transpose(0, 2, 1, 3).reshape(m, NSA_KV, 3 * NSA_G)
    return jnp.pad(g, ((0, 0), (0, 0), (0, LANE - 3 * NSA_G))).reshape(m, NSA_KV * LANE)


def kernel(x_prompt, x_sample, cache_nsa_kv, cache_diff_kv, state_nsa_win, page_table, p_prompt, p_sample,
           rel_bias_table, norm_mix, w_in, w_cmp, diff_lambda, diff_subln, w_out, norm_ffn, w_router_group,
           b_router_group, w_router_expert, b_router_expert, w_exp_gate, w_exp_up, w_exp_down, norm_ple,
           w_ple_gate, w_ple_proj, final_norm):
    assert norm_mix.shape[0] == 1, "single-layer trunk"
    batch, seq, d = x_prompt.shape
    db, t_new, _ = x_sample.shape
    mp, ms = batch * seq, db * t_new
    m = mp + ms
    lam_init = 0.8 - 0.6 * math.exp(-0.3 * 0)
    tm = _pick(math.gcd(mp, ms), (1024, 512, 256, 128))
    tm_s = _pick(math.gcd(mp, ms), (256, 128))
    xp = x_prompt.reshape(mp, d)
    xs = x_sample.reshape(ms, d)
    tbl_t = rel_bias_table

    xn = _rms2(xp, xs, norm_mix[0], tm_s)
    w0 = w_in[0]
    off = O_QD - O_GATE
    scale = HD ** -0.5
    tn = 512
    wide = functools.partial(_matmul, [xn], tm=tm, tn=tn)
    q_nsa = wide([(w0, 0)], rows=m, n_cols=O_KV, out_dtype=BF, scale=scale, name="proj_qn")
    shifted = lambda base, width: (w0, 0, lambda j: base // width + j,
                                   lambda j: (base + width) // LANE + j * (width // LANE))
    qd = wide([shifted(O_GATE, tn)], rows=m, n_cols=W_QD, out_dtype=BF, scale=scale, lane_off=off, name="proj_qd")
    gates = _gate_lanes(_matmul([xn], [(w0, 0, lambda j: O_GATE // LANE)], rows=m, n_cols=LANE, tm=tm, tn=LANE,
                                out_dtype=F32, name="proj_gate"))
    kv_w = [(w0, 0, lambda j: j + O_KV // tn)]
    win_w = [(w0, 0, lambda j: j + (O_KV + 4 * NSA_KV * HD) // tn)]
    dkv_w = [shifted(O_GATE + W_QD, DF_DV), shifted(O_GATE + W_QD + W_KD, DF_DV)]
    dkv = functools.partial(_matmul, [xn], dkv_w, tm=tm, tn=2 * DF_DV, n_cols=DF_KV * 4 * HD, out_dtype=F32,
                            epi="cat", lane_off=off)
    kv4_p = wide(kv_w, rows=mp, n_cols=4 * NSA_KV * HD, out_dtype=F32, name="proj_kv_p")
    kv4_s = wide(kv_w, rows=ms, row0=mp, n_cols=4 * NSA_KV * HD, out_dtype=F32, name="proj_kv_s")
    win_p = wide(win_w, rows=mp, n_cols=2 * NSA_KV * HD, out_dtype=F32, name="proj_win_p")
    win_s = wide(win_w, rows=ms, row0=mp, n_cols=2 * NSA_KV * HD, out_dtype=F32, name="proj_win_s")
    dkv_p = dkv(rows=mp, name="proj_dkv_p")
    dkv_s = dkv(rows=ms, row0=mp, name="proj_dkv_s")

    nc = seq // CMP
    kvr = kv4_p.reshape(batch, seq, 4, NSA_KV, HD)
    cmp_out = []
    for kind in range(2):
        a = kvr[:, :, kind].transpose(2, 0, 1, 3).reshape(NSA_KV * batch * nc, CMP * HD).astype(BF)
        r = a.shape[0]
        cmp_out.append(_matmul([a], [(w_cmp[0, kind].reshape(CMP * HD, HD), 0)], rows=r, n_cols=HD,
                               tm=_pick(r, (512, 256, 128, 64, 32, 16)), tn=HD, out_dtype=BF,
                               name="compress").reshape(NSA_KV, batch, nc, HD))
    o_n = _nsa_prompt(q_nsa, gates, cmp_out[0], cmp_out[1], kv4_p, win_p, tbl_t, batch, seq, m)
    o_d = _diff_prompt(qd, dkv_p, tbl_t, diff_lambda[0], diff_subln[0], batch, seq, m, lam_init)

    o_n_s = _nsa_sample(q_nsa[mp:].astype(F32), gates[mp:], kv4_s, win_s, cache_nsa_kv, state_nsa_win,
                        page_table, w_cmp[0], tbl_t, t_new)
    o_d_s = _diff_sample(qd[mp:].astype(F32), dkv_s, cache_diff_kv, page_table, tbl_t, diff_lambda[0],
                         diff_subln[0], t_new, lam_init)
    o_n = lax.dynamic_update_slice(o_n, o_n_s.astype(BF), (mp, 0))
    o_d = lax.dynamic_update_slice(o_d, o_d_s.astype(BF), (mp, 0))

    h1 = _matmul([o_n, o_d], [(w_out[0], 0), (w_out[0], 1)], rows=m, n_cols=d, tm=tm,
                 tn=_pick(d, (512, 256, 128)), out_dtype=F32, epi="res2", epi_args=(xp, xs), name="out_proj")

    wr = jnp.concatenate([w_router_group[0], w_router_expert[0],
                          jnp.zeros((d, LANE - N_GROUPS - N_EXP), F32)], axis=1)
    br = jnp.concatenate([b_router_group[0], b_router_expert[0],
                          jnp.zeros((LANE - N_GROUPS - N_EXP,), F32)]).reshape(1, LANE)
    xn2, ids, wts = _router(h1, norm_ffn[0], wr, br, tm_s)
    xs_moe, pos, tile_e, tile_xb, nt_arr = _moe_dispatch(xn2, ids, N_EXP)
    wk = min(WIN, seq)
    nsa_kv_p = _split_cols(kv4_p, batch, seq, 0, (4, NSA_KV), HD)
    diff_kv_p = _split_cols(dkv_p, batch, seq, 0, (DF_KV,), 4 * HD)
    win_p_out = _split_cols(win_p, batch, wk, seq - wk, (2, NSA_KV), HD)
    new_win = _shift_window(state_nsa_win, win_s, t_new)
    y = _moe_experts(xs_moe, tile_e, tile_xb, nt_arr, w_exp_gate[0], w_exp_up[0], w_exp_down[0], m)
    h2, xn3 = _combine_rms(h1, y.at[pos[:, 0]].get(mode="promise_in_bounds"),
                           y.at[pos[:, 1]].get(mode="promise_in_bounds"), wts, norm_ple[0], tm_s)

    p_all = jnp.concatenate([p_prompt[0].reshape(mp, -1), p_sample[0].reshape(ms, -1)], axis=0).astype(BF)
    h3 = _matmul([xn3], [(w_ple_gate[0], 0)], rows=m, n_cols=d, tm=tm, tn=_pick(d, (512, 256, 128)),
                 out_dtype=F32, epi="ple", epi_args=(h2, p_all, w_ple_proj[0]), name="ple")

    y_p = _rms(h3, final_norm, tm_s, F32, row0=0, rows=mp).reshape(batch, seq, d)
    y_s = _rms(h3, final_norm, tm_s, F32, row0=mp, rows=ms).reshape(db, t_new, d)
    return (y_p, y_s,
            nsa_kv_p[None], kv4_s.reshape(1, db, t_new, 4, NSA_KV, HD),
            diff_kv_p[None], dkv_s.reshape(1, db, t_new, DF_KV, 4 * HD),
            win_p_out[None], new_win)
```

```python
import functools
import math

import numpy as np
import jax
import jax.numpy as jnp
from jax import lax
from jax.experimental import pallas as pl
from jax.experimental.pallas import tpu as pltpu

BF = jnp.bfloat16
F32 = jnp.float32

HD = 128
NSA_H = 16
NSA_KV = 2
NSA_G = NSA_H // NSA_KV
CMP = 32
SELB = 64
TOPK = 16
WIN = 512
DF_H = 8
DF_KV = 4
DF_G = DF_H // DF_KV
DF_DV = 2 * HD
REL_BUCKETS = 32
REL_MAX_DIST = 128
N_GROUPS = 4
EPG = 8
N_EXP = N_GROUPS * EPG
EPS = 1e-6
NEG = -1e30
FORCE = 1e4
LANE = 128
VMEM_LIMIT = 56 * 1024 * 1024
MOE_TM = 768
MOE_XBLK = 128
MOE_DMA_SPLIT = 4

QB_NSA = 128
PAD_SEL = 384
PAD_WIN = WIN
NEAR_SEL = 512
BAND_WIN = WIN + QB_NSA
QB_DF = 256
PAD_DF = 256
NEAR_DF = 512
TK = 256
CMP_PITCH = 40


def _dot(a, b):
    return jnp.dot(a, b, preferred_element_type=F32)


def _dot_nt(a, b):
    return lax.dot_general(a, b, (((1,), (1,)), ((), ())), preferred_element_type=F32)


def _cparams(sem):
    return pltpu.CompilerParams(dimension_semantics=sem, vmem_limit_bytes=VMEM_LIMIT)


def _pick(n, cands):
    for c in cands:
        if n % c == 0:
            return c
    raise ValueError(f"no tile in {cands} divides {n}")


def _rms2_kernel(xp_ref, xs_ref, g_ref, o_ref, *, np_tiles):
    i = pl.program_id(0)

    def go(x_ref):
        x = x_ref[...]
        ms = jnp.mean(x * x, axis=-1, keepdims=True)
        o_ref[...] = (x * lax.rsqrt(ms + EPS) * g_ref[...]).astype(o_ref.dtype)

    @pl.when(i < np_tiles)
    def _():
        go(xp_ref)

    @pl.when(i >= np_tiles)
    def _():
        go(xs_ref)


def _rms2(xp, xs, g, tm):
    mp, d = xp.shape
    ms = xs.shape[0]
    npt, nst = mp // tm, ms // tm
    return pl.pallas_call(
        functools.partial(_rms2_kernel, np_tiles=npt),
        grid=(npt + nst,),
        in_specs=[pl.BlockSpec((tm, d), lambda i: (jnp.minimum(i, npt - 1), 0)),
                  pl.BlockSpec((tm, d), lambda i: (jnp.maximum(i - npt, 0), 0)),
                  pl.BlockSpec((1, d), lambda i: (0, 0))],
        out_specs=pl.BlockSpec((tm, d), lambda i: (i, 0)),
        out_shape=jax.ShapeDtypeStruct((mp + ms, d), BF),
        compiler_params=_cparams(("arbitrary",)),
        name="rms2",
    )(xp, xs, g.reshape(1, d))


def _rms_kernel(x_ref, g_ref, o_ref):
    x = x_ref[...]
    ms = jnp.mean(x * x, axis=-1, keepdims=True)
    o_ref[...] = (x * lax.rsqrt(ms + EPS) * g_ref[...]).astype(o_ref.dtype)


def _rms(x, g, tm, out_dtype, row0=0, rows=None):
    m, d = x.shape
    rows = m if rows is None else rows
    t0 = row0 // tm
    return pl.pallas_call(
        _rms_kernel,
        grid=(rows // tm,),
        in_specs=[pl.BlockSpec((tm, d), lambda i: (i + t0, 0)),
                  pl.BlockSpec((1, d), lambda i: (0, 0))],
        out_specs=pl.BlockSpec((tm, d), lambda i: (i, 0)),
        out_shape=jax.ShapeDtypeStruct((rows, d), out_dtype),
        compiler_params=_cparams(("arbitrary",)),
        name="rms",
    )(x, g.reshape(1, d))


def _combine_rms_kernel(h_ref, y0_ref, y1_ref, w_ref, g_ref, h2_ref, xn_ref):
    w = w_ref[...]
    h2 = h_ref[...] + w[:, 0:1] * y0_ref[...].astype(F32) + w[:, 1:2] * y1_ref[...].astype(F32)
    h2_ref[...] = h2
    ms = jnp.mean(h2 * h2, axis=-1, keepdims=True)
    xn_ref[...] = (h2 * lax.rsqrt(ms + EPS) * g_ref[...]).astype(xn_ref.dtype)


def _combine_rms(h, y0, y1, wts, g, tm):
    m, d = h.shape
    row = lambda width: pl.BlockSpec((tm, width), lambda i: (i, 0))
    return pl.pallas_call(
        _combine_rms_kernel,
        grid=(m // tm,),
        in_specs=[row(d), row(d), row(d), row(LANE), pl.BlockSpec((1, d), lambda i: (0, 0))],
        out_specs=[row(d), row(d)],
        out_shape=[jax.ShapeDtypeStruct((m, d), F32), jax.ShapeDtypeStruct((m, d), BF)],
        compiler_params=_cparams(("arbitrary",)),
        name="combine_rms",
    )(h, y0, y1, wts, g.reshape(1, d))


def _cast_rows(src_ref, dst_ref):
    k = src_ref.shape[0]
    ch = 256 if k % 256 == 0 else k

    def body(c, carry):
        r = pl.multiple_of(c * ch, ch)
        dst_ref[pl.ds(r, ch), :] = src_ref[pl.ds(r, ch), :].astype(BF)
        return carry

    lax.fori_loop(0, k // ch, body, 0)


def _cast_rows_shifted(main_ref, tail_ref, dst_ref, off):
    k, width = dst_ref.shape
    ch = 256 if k % 256 == 0 else k

    def body(c, carry):
        r = pl.multiple_of(c * ch, ch)
        x = jnp.concatenate([main_ref[pl.ds(r, ch), :], tail_ref[pl.ds(r, ch), :]], axis=1)
        dst_ref[pl.ds(r, ch), :] = x[:, off:off + width].astype(BF)
        return carry

    lax.fori_loop(0, k // ch, body, 0)


def _cast_out_rows(main_ref, tail_ref, dst_ref, off):
    n = dst_ref.shape[0]
    ch = min(n, LANE)
    for c in range(0, n, ch):
        lo = c + off
        if lo + ch <= n:
            blk = main_ref[lo:lo + ch, :]
        else:
            blk = jnp.concatenate([main_ref[lo:n, :], tail_ref[0:lo + ch - n, :]], axis=0)
        dst_ref[c:c + ch, :] = blk.astype(BF)


def _mm_kernel(*refs, n_a, n_w, cast, epi, scale, np_tiles, lane_off, wt):
    a = refs[:n_a]
    w = refs[n_a:n_a + n_w]
    idx = n_a + n_w
    tails = (None,) * n_w
    if lane_off:
        tails = refs[idx:idx + n_w]
        idx += n_w
    if epi == "res2":
        rp_ref, rs_ref = refs[idx:idx + 2]
        idx += 2
    elif epi == "ple":
        h_ref, p_ref, wp_ref = refs[idx:idx + 3]
        idx += 3
    o_ref = refs[idx]
    idx += 1
    wb = refs[idx:idx + n_w] if cast else w
    i = pl.program_id(1)

    if cast:
        @pl.when(i == 0)
        def _():
            for k in range(n_w):
                if wt:
                    _cast_out_rows(w[k], tails[k], wb[k], lane_off)
                elif lane_off:
                    _cast_rows_shifted(w[k], tails[k], wb[k], lane_off)
                else:
                    _cast_rows(w[k], wb[k])

    dot = _dot_nt if wt else _dot
    if epi == "cat":
        x = a[0][...]
        wd = o_ref.shape[1] // n_w
        for k in range(n_w):
            o_ref[:, k * wd:(k + 1) * wd] = dot(x, wb[k][...]).astype(o_ref.dtype)
        return
    acc = dot(a[0][...], wb[0][...])
    for k in range(1, n_a):
        acc = acc + dot(a[k][...], wb[k][...])
    if scale is not None:
        acc = acc * scale
    if epi is None:
        o_ref[...] = acc.astype(o_ref.dtype)
    elif epi == "res2":
        @pl.when(i < np_tiles)
        def _():
            o_ref[...] = (acc + rp_ref[...]).astype(o_ref.dtype)

        @pl.when(i >= np_tiles)
        def _():
            o_ref[...] = (acc + rs_ref[...]).astype(o_ref.dtype)
    elif epi == "ple":
        gate = jax.nn.sigmoid(acc)
        proj = _dot(p_ref[...], wp_ref[...].astype(BF))
        o_ref[...] = (h_ref[...] + gate * proj).astype(o_ref.dtype)


def _matmul(a_list, w_list, *, rows, n_cols, tm, tn, out_dtype, row0=0, col0=0,
            scale=None, epi=None, epi_args=(), lane_off=0, wt=False, name="mm"):
    n_a, n_w = len(a_list), len(w_list)
    cast = w_list[0][0].dtype != BF
    assert cast or not (lane_off or wt)
    wblock = (lambda kdim, width, kb, col: ((width, kdim), (col, kb))) if wt else \
             (lambda kdim, width, kb, col: ((kdim, width), (kb, col)))
    t0, c0 = row0 // tm, col0 // tn
    gm, gn = rows // tm, n_cols // tn
    wn = tn // n_w if epi == "cat" else tn
    in_specs, args = [], []
    for a in a_list:
        in_specs.append(pl.BlockSpec((tm, a.shape[1]), lambda j, i: (i + t0, 0)))
        args.append(a)
    for k, ent in enumerate(w_list):
        w, kb = ent[0], ent[1]
        colfn = ent[2] if len(ent) > 2 else (lambda j: j + c0)
        kdim = a_list[min(k, n_a - 1)].shape[1]
        shape = wblock(kdim, wn, 0, 0)[0]
        in_specs.append(pl.BlockSpec(shape, lambda j, i, kb=kb, colfn=colfn: wblock(0, 0, kb, colfn(j))[1]))
        args.append(w)
    if lane_off:
        for k, ent in enumerate(w_list):
            kdim = a_list[min(k, n_a - 1)].shape[1]
            shape = wblock(kdim, LANE, 0, 0)[0]
            in_specs.append(pl.BlockSpec(shape, lambda j, i, kb=ent[1], tailfn=ent[3]: wblock(0, 0, kb, tailfn(j))[1]))
            args.append(ent[0])
    np_tiles = 0
    if epi == "res2":
        xp, xs = epi_args
        np_tiles = xp.shape[0] // tm
        in_specs.append(pl.BlockSpec((tm, tn), lambda j, i: (jnp.minimum(i, np_tiles - 1), j)))
        in_specs.append(pl.BlockSpec((tm, tn), lambda j, i: (jnp.maximum(i - np_tiles, 0), j)))
        args += [xp, xs]
    elif epi == "ple":
        h, p, wp = epi_args
        in_specs.append(pl.BlockSpec((tm, tn), lambda j, i: (i, j)))
        in_specs.append(pl.BlockSpec((tm, p.shape[1]), lambda j, i: (i, 0)))
        in_specs.append(pl.BlockSpec((wp.shape[0], tn), lambda j, i: (0, j)))
        args += [h, p, wp]
    scratch = [pltpu.VMEM(wblock(a_list[min(k, n_a - 1)].shape[1], wn, 0, 0)[0], BF)
               for k in range(n_w)] if cast else []
    return pl.pallas_call(
        functools.partial(_mm_kernel, n_a=n_a, n_w=n_w, cast=cast, epi=epi, scale=scale, np_tiles=np_tiles,
                          lane_off=lane_off, wt=wt),
        grid=(gn, gm),
        in_specs=in_specs,
        out_specs=pl.BlockSpec((tm, tn), lambda j, i: (i, j)),
        out_shape=jax.ShapeDtypeStruct((rows, n_cols), out_dtype),
        scratch_shapes=scratch,
        compiler_params=_cparams(("arbitrary", "arbitrary")),
        name=name,
    )(*args)


def _bucket_np(dist):
    n = np.maximum(dist, 0)
    max_exact = REL_BUCKETS // 2
    nf = np.maximum(n, 1).astype(np.float32)
    log_b = max_exact + (np.log(nf / np.float32(max_exact)) / np.float32(math.log(REL_MAX_DIST / max_exact))
                         * np.float32(REL_BUCKETS - max_exact)).astype(np.int32)
    return np.where(n < max_exact, n, np.minimum(log_b, REL_BUCKETS - 1)).astype(np.int32)


def _bucket_edges():
    b = _bucket_np(np.arange(0, 4 * REL_MAX_DIST))
    return [(k, int(np.nonzero(b == k)[0].max())) for k in range(REL_BUCKETS - 1) if (b == k).any()]


def _pattern_kernel(tbl_ref, o_ref, *, h0, base, col_step, v_lo, v_hi, c_lim, sub_far, edges, keys_on_rows):
    h = pl.program_id(0) + h0
    shape = o_ref.shape
    row = lax.broadcasted_iota(jnp.int32, shape, 0) + pl.program_id(1) * shape[0]
    col = lax.broadcasted_iota(jnp.int32, shape, 1)
    if keys_on_rows:
        row, col = col, row
    dist = row + base - col_step * col
    far = tbl_ref[REL_BUCKETS - 1, h]
    b = jnp.full(shape, far, F32)
    for k, hi in reversed(edges):
        b = jnp.where(dist <= hi, tbl_ref[k, h], b)
    if sub_far:
        b = b - far
    valid = (dist >= v_lo) & (dist <= v_hi) & (col < c_lim)
    o_ref[...] = jnp.where(valid, b, NEG)


def _rel_pattern(tbl, h0, nh, nrows, ncols, base, v_lo, v_hi, sub_far, col_step=1, c_lim=None,
                 keys_on_rows=False):
    out_r, out_c = (ncols, nrows) if keys_on_rows else (nrows, ncols)
    tr = _pick(out_r, (512, 256, 128, 64, 32, 16, 8))
    return pl.pallas_call(
        functools.partial(_pattern_kernel, h0=h0, base=base, col_step=col_step, v_lo=v_lo, v_hi=v_hi,
                          c_lim=ncols if c_lim is None else c_lim, sub_far=sub_far, edges=_bucket_edges(),
                          keys_on_rows=keys_on_rows),
        grid=(nh, out_r // tr),
        in_specs=[pl.BlockSpec(memory_space=pltpu.SMEM)],
        out_specs=pl.BlockSpec((None, tr, out_c), lambda h, r: (h, r, 0)),
        out_shape=jax.ShapeDtypeStruct((nh, out_r, out_c), F32),
        compiler_params=_cparams(("arbitrary", "arbitrary")),
        name="rel_bias",
    )(tbl)


def _pair_sum_matrix(nc):
    n = np.arange(nc)[:, None]
    b = np.arange(LANE)[None, :]
    return jnp.asarray((n // (SELB // CMP) == b).astype(np.float32), dtype=BF)


def _expand_matrix(pad, n_keys):
    l = np.arange(LANE)[:, None]
    c = np.arange(pad + n_keys)[None, :]
    return jnp.asarray(((c >= pad) & ((c - pad) // SELB == l)).astype(np.float32), dtype=BF)


def _split3(x):
    hi = x.astype(BF)
    r = x - hi.astype(F32)
    mid = r.astype(BF)
    lo = (r - mid.astype(F32)).astype(BF)
    return hi, mid, lo


def _select_blocks(psum, s_mat, qpos, ns):
    hi, mid, lo = _split3(psum)
    imp = _dot(hi, s_mat) + _dot(mid, s_mat) + _dot(lo, s_mat)
    shape = imp.shape
    lane = lax.broadcasted_iota(jnp.int32, shape, 1)
    valid = lane * SELB <= qpos
    cur = jnp.right_shift(qpos, 6)
    forced = (lane == 0) | (lane == cur) | (lane == cur - 1)
    score = jnp.where(valid, imp + jnp.where(forced, FORCE, 0.0), NEG)
    score = jnp.where(lane < ns, score, -3e38)
    cnt = jnp.zeros(shape, F32)
    for i in range(ns):
        ci = score[:, i:i + 1]
        cnt = cnt + jnp.where(lane > i, jnp.where(ci >= score, 1.0, 0.0), jnp.where(ci > score, 1.0, 0.0))
    sel = (cnt < float(min(TOPK, ns))) & (lane < ns)
    return jnp.where(sel, 1.0, 0.0).astype(BF)


def _softmax_rows(s, valid=None):
    m = jnp.max(s, axis=-1, keepdims=True)
    p = jnp.exp(s - m)
    if valid is not None:
        p = jnp.where(valid, p, 0.0)
    l = jnp.sum(p, axis=-1, keepdims=True)
    return p / jnp.where(l > 0.0, l, 1.0)


def _lanes(x, n):
    return x if n == LANE else jnp.concatenate([x] * (n // LANE), axis=1)


def _online(carry, s, vt, ones_in_v=False):
    m, l, acc = carry
    dv = acc.shape[1]
    m_new = jnp.maximum(m, jnp.max(s, axis=-1, keepdims=True))
    p = jnp.exp(s - _lanes(m_new, s.shape[1]))
    alpha = jnp.exp(m - m_new)
    pv = _dot(p.astype(BF), vt)
    if ones_in_v:
        l = alpha * l + pv[:, dv:dv + LANE]
        pv = pv[:, 0:dv]
    else:
        l = alpha * l + jnp.sum(p, axis=-1, keepdims=True)
    return m_new, l, _lanes(alpha, dv) * acc + pv


def _online_init(rows, dv):
    return jnp.full((rows, LANE), NEG, F32), jnp.zeros((rows, LANE), F32), jnp.zeros((rows, dv), F32)


def _flash_step(q, kt, vt, bias, m_ref, l_ref, acc_ref, rows, ones_in_v=False):
    s = _dot_nt(q, kt)
    if bias is not None:
        s = s + bias
    m_new, l, acc = _online((m_ref[rows, :], l_ref[rows, :], acc_ref[rows, :]), s, vt, ones_in_v)
    m_ref[rows, :] = m_new
    l_ref[rows, :] = l
    acc_ref[rows, :] = acc


def _nsa_prompt_kernel(q_ref, gate_ref, kc_ref, vc_ref, ks_ref, vs_ref, kw_ref, vw_ref,
                       bc_ref, pn_ref, pw_ref, e_ref, s_ref, o_ref,
                       ks_s, vs_s, kw_s, vw_s, m_s, l_s, acc_s, o_s, *, seq, nc, ns):
    i = pl.program_id(2)
    s0 = i * QB_NSA
    rows = NSA_G * QB_NSA

    @pl.when(i == 0)
    def _():
        ks_s[0:PAD_SEL, :] = jnp.zeros((PAD_SEL, HD), BF)
        vs_s[0:PAD_SEL, :] = jnp.zeros((PAD_SEL, 2 * HD), BF)
        kw_s[0:PAD_WIN, :] = jnp.zeros((PAD_WIN, HD), BF)
        vw_s[0:PAD_WIN, :] = jnp.zeros((PAD_WIN, 2 * HD), BF)
        ch = 512
        ones = jnp.ones((ch, HD), BF)

        def cp(c, carry):
            r = pl.multiple_of(c * ch, ch)
            ks_s[pl.ds(PAD_SEL + r, ch), :] = ks_ref[pl.ds(r, ch), :].astype(BF)
            vs_s[pl.ds(PAD_SEL + r, ch), 0:HD] = vs_ref[pl.ds(r, ch), :].astype(BF)
            vs_s[pl.ds(PAD_SEL + r, ch), HD:2 * HD] = ones
            kw_s[pl.ds(PAD_WIN + r, ch), :] = kw_ref[pl.ds(r, ch), :].astype(BF)
            vw_s[pl.ds(PAD_WIN + r, ch), 0:HD] = vw_ref[pl.ds(r, ch), :].astype(BF)
            vw_s[pl.ds(PAD_WIN + r, ch), HD:2 * HD] = ones
            return carry

        lax.fori_loop(0, seq // ch, cp, 0)

    gt = jax.nn.sigmoid(gate_ref[...])
    head = lambda g: slice(g * HD, (g + 1) * HD)
    hrows = lambda g: slice(g * QB_NSA, (g + 1) * QB_NSA)

    kc = kc_ref[...]
    vc = vc_ref[...]
    psum = jnp.zeros((QB_NSA, nc), F32)
    for g in range(NSA_G):
        bc = bc_ref[g]
        pc = _softmax_rows(_dot_nt(q_ref[:, head(g)], kc) + bc, bc > 0.5 * NEG)
        psum = psum + pc
        o_s[:, head(g)] = gt[:, g:g + 1] * _dot(pc.astype(BF), vc)
    qpos = s0 + lax.broadcasted_iota(jnp.int32, (QB_NSA, LANE), 0)
    selb = _select_blocks(psum, s_ref[...], qpos, ns)

    nch = jnp.maximum(i - 1, 0) // 2
    far_keys = nch * TK
    m_s[...] = jnp.full((rows, LANE), NEG, F32)
    l_s[...] = jnp.zeros((rows, LANE), F32)
    acc_s[...] = jnp.zeros((rows, HD), F32)

    def sel_chunk(r, bias_of):
        kt = ks_s[pl.ds(r, TK), :]
        vt = vs_s[pl.ds(r, TK), :]
        madd = (_dot(selb, e_ref[:, pl.ds(r, TK)]) - 1.0) * (-NEG)
        for g in range(NSA_G):
            _flash_step(q_ref[:, head(g)], kt, vt, bias_of(g, madd), m_s, l_s, acc_s, hrows(g), ones_in_v=True)

    def far(c, carry):
        sel_chunk(pl.multiple_of(PAD_SEL + c * TK, LANE), lambda g, madd: madd)
        return carry

    lax.fori_loop(0, nch, far, 0)
    for kh in range(NEAR_SEL // TK):
        col = lax.broadcasted_iota(jnp.int32, (QB_NSA, TK), 1) + kh * TK
        cut = jnp.where(col < far_keys - s0 + PAD_SEL, NEG, 0.0)
        sel_chunk(pl.multiple_of(s0 + kh * TK, LANE),
                  lambda g, madd, kh=kh, cut=cut: pn_ref[g, :, kh * TK:(kh + 1) * TK] + madd + cut)
    for g in range(NSA_G):
        osel = acc_s[hrows(g), :] / l_s[hrows(g), :]
        o_s[:, head(g)] = o_s[:, head(g)] + gt[:, NSA_G + g:NSA_G + g + 1] * osel

    for g in range(NSA_G):
        st = _online_init(QB_NSA, HD)
        for c0 in range(0, BAND_WIN, TK):
            w = min(TK, BAND_WIN - c0)
            r = pl.multiple_of(s0 + c0, LANE)
            colw = lax.broadcasted_iota(jnp.int32, (QB_NSA, w), 1) + c0
            bias = pw_ref[g, :, c0:c0 + w] + jnp.where(colw < PAD_WIN - s0, NEG, 0.0)
            st = _online(st, _dot_nt(q_ref[:, head(g)], kw_s[pl.ds(r, w), :]) + bias, vw_s[pl.ds(r, w), :],
                         ones_in_v=True)
        ow = st[2] / st[1]
        o_ref[:, head(g)] = (o_s[:, head(g)] + gt[:, 2 * NSA_G + g:2 * NSA_G + g + 1] * ow).astype(o_ref.dtype)


def _nsa_prompt(q_nsa, gates, kc, vc, kv4_p, win_p, tbl_t, batch, seq, m_total):
    nc = seq // CMP
    ns = -(-seq // SELB)
    nqb = seq // QB_NSA
    big = 1 << 30
    pn = _rel_pattern(tbl_t, 0, NSA_H, QB_NSA, NEAR_SEL, PAD_SEL, 0, big, True)
    pw = _rel_pattern(tbl_t, 0, NSA_H, QB_NSA, BAND_WIN, PAD_WIN, 0, WIN, False)
    bc = _rel_pattern(tbl_t, 0, NSA_H, seq, nc, -(CMP - 1), 0, big, False, col_step=CMP)
    e_mat = _expand_matrix(PAD_SEL, seq)
    s_mat = _pair_sum_matrix(nc)
    kv_spec = lambda col: pl.BlockSpec((seq, HD), lambda b, h, i, col=col: (b, col + h))
    return pl.pallas_call(
        functools.partial(_nsa_prompt_kernel, seq=seq, nc=nc, ns=ns),
        grid=(batch, NSA_KV, nqb),
        in_specs=[
            pl.BlockSpec((QB_NSA, NSA_G * HD), lambda b, h, i: (b * nqb + i, h)),
            pl.BlockSpec((QB_NSA, LANE), lambda b, h, i: (b * nqb + i, h)),
            pl.BlockSpec((None, None, nc, HD), lambda b, h, i: (h, b, 0, 0)),
            pl.BlockSpec((None, None, nc, HD), lambda b, h, i: (h, b, 0, 0)),
            kv_spec(2 * NSA_KV), kv_spec(3 * NSA_KV),
            pl.BlockSpec((seq, HD), lambda b, h, i: (b, h)),
            pl.BlockSpec((seq, HD), lambda b, h, i: (b, NSA_KV + h)),
            pl.BlockSpec((NSA_G, QB_NSA, nc), lambda b, h, i: (h, i, 0)),
            pl.BlockSpec((NSA_G, QB_NSA, NEAR_SEL), lambda b, h, i: (h, 0, 0)),
            pl.BlockSpec((NSA_G, QB_NSA, BAND_WIN), lambda b, h, i: (h, 0, 0)),
            pl.BlockSpec((LANE, PAD_SEL + seq), lambda b, h, i: (0, 0)),
            pl.BlockSpec((nc, LANE), lambda b, h, i: (0, 0)),
        ],
        out_specs=pl.BlockSpec((QB_NSA, NSA_G * HD), lambda b, h, i: (b * nqb + i, h)),
        out_shape=jax.ShapeDtypeStruct((m_total, NSA_H * HD), BF),
        scratch_shapes=[pltpu.VMEM((PAD_SEL + seq, HD), BF), pltpu.VMEM((PAD_SEL + seq, 2 * HD), BF),
                        pltpu.VMEM((PAD_WIN + seq, HD), BF), pltpu.VMEM((PAD_WIN + seq, 2 * HD), BF),
                        pltpu.VMEM((NSA_G * QB_NSA, LANE), F32), pltpu.VMEM((NSA_G * QB_NSA, LANE), F32),
                        pltpu.VMEM((NSA_G * QB_NSA, HD), F32), pltpu.VMEM((QB_NSA, NSA_G * HD), F32)],
        compiler_params=_cparams(("arbitrary", "arbitrary", "arbitrary")),
        name="nsa_prompt",
    )(q_nsa, gates, kc, vc, kv4_p, kv4_p, win_p, win_p, bc, pn, pw, e_mat, s_mat)


def _diff_lambda(dl, lam_init):
    a = jnp.sum(dl[0:1] * dl[1:2], axis=-1, keepdims=True)
    b = jnp.sum(dl[2:3] * dl[3:4], axis=-1, keepdims=True)
    return jnp.exp(a) - jnp.exp(b) + lam_init


def _diff_finish(a, sub, lam_init):
    ms = jnp.mean(a * a, axis=-1, keepdims=True)
    return a * lax.rsqrt(ms + EPS) * sub * (1.0 - lam_init)


def _diff_prompt_kernel(q_ref, kv_ref, pn_ref, dl_ref, sub_ref, o_ref, kv_s, m_s, l_s, acc_s, *, seq, lam_init):
    i = pl.program_id(2)
    s0 = i * QB_DF
    rows = DF_G * QB_DF

    @pl.when(i == 0)
    def _():
        kv_s[0:PAD_DF, :] = jnp.zeros((PAD_DF, 4 * HD), BF)
        ch = 256

        def cp(c, carry):
            r = pl.multiple_of(c * ch, ch)
            kv_s[pl.ds(PAD_DF + r, ch), :] = kv_ref[pl.ds(r, ch), :].astype(BF)
            return carry

        lax.fori_loop(0, seq // ch, cp, 0)

    lam = _diff_lambda(dl_ref[...], lam_init)
    nfar = jnp.maximum(i - 1, 0)
    sub_rows = 128
    n_sub = QB_DF // sub_rows
    streams = [(m, g, j) for m in range(2) for g in range(DF_G) for j in range(n_sub)]
    srows = lambda k: slice(k * sub_rows, (k + 1) * sub_rows)
    m_s[...] = jnp.full((2 * rows, LANE), NEG, F32)
    l_s[...] = jnp.zeros((2 * rows, LANE), F32)
    acc_s[...] = jnp.zeros((2 * rows, DF_DV), F32)

    def chunk(r, bias_of):
        vt = kv_s[pl.ds(r, TK), 2 * HD:4 * HD]
        for k, (m, g, j) in enumerate(streams):
            q = q_ref[j * sub_rows:(j + 1) * sub_rows, (g * 2 + m) * HD:(g * 2 + m + 1) * HD]
            kt = kv_s[pl.ds(r, TK), m * HD:(m + 1) * HD]
            _flash_step(q, kt, vt, bias_of(g, j), m_s, l_s, acc_s, srows(k))

    def far(c, carry):
        chunk(pl.multiple_of(PAD_DF + c * TK, TK), lambda g, j: None)
        return carry

    lax.fori_loop(0, nfar, far, 0)
    for kh in range(NEAR_DF // TK):
        col = lax.broadcasted_iota(jnp.int32, (sub_rows, TK), 1) + kh * TK
        cut = jnp.where(col < nfar * TK - s0 + PAD_DF, NEG, 0.0)
        chunk(pl.multiple_of(s0 + kh * TK, TK),
              lambda g, j, kh=kh, cut=cut: pn_ref[g, j * sub_rows:(j + 1) * sub_rows, kh * TK:(kh + 1) * TK] + cut)
    half = len(streams) // 2
    for k, (_, g, j) in enumerate(streams[:half]):
        o1 = acc_s[srows(k), :] / _lanes(l_s[srows(k), :], DF_DV)
        o2 = acc_s[srows(half + k), :] / _lanes(l_s[srows(half + k), :], DF_DV)
        out = _diff_finish(o1 - lam * o2, sub_ref[...], lam_init)
        o_ref[j * sub_rows:(j + 1) * sub_rows, g * DF_DV:(g + 1) * DF_DV] = out.astype(o_ref.dtype)


def _diff_prompt(qd, dkv_p, tbl_t, dl, sub, batch, seq, m_total, lam_init):
    nqb = seq // QB_DF
    pn = _rel_pattern(tbl_t, NSA_H, DF_H, QB_DF, NEAR_DF, PAD_DF, 0, 1 << 30, True)
    width = DF_G * 2 * HD
    return pl.pallas_call(
        functools.partial(_diff_prompt_kernel, seq=seq, lam_init=lam_init),
        grid=(batch, DF_KV, nqb),
        in_specs=[
            pl.BlockSpec((QB_DF, width), lambda b, h, i: (b * nqb + i, h)),
            pl.BlockSpec((seq, 4 * HD), lambda b, h, i: (b, h)),
            pl.BlockSpec((DF_G, QB_DF, NEAR_DF), lambda b, h, i: (h, 0, 0)),
            pl.BlockSpec((4, HD), lambda b, h, i: (0, 0)),
            pl.BlockSpec((1, DF_DV), lambda b, h, i: (0, 0)),
        ],
        out_specs=pl.BlockSpec((QB_DF, DF_G * DF_DV), lambda b, h, i: (b * nqb + i, h)),
        out_shape=jax.ShapeDtypeStruct((m_total, DF_H * DF_DV), BF),
        scratch_shapes=[pltpu.VMEM((PAD_DF + seq, 4 * HD), BF),
                        pltpu.VMEM((2 * DF_G * QB_DF, LANE), F32), pltpu.VMEM((2 * DF_G * QB_DF, LANE), F32),
                        pltpu.VMEM((2 * DF_G * QB_DF, DF_DV), F32)],
        compiler_params=_cparams(("arbitrary", "arbitrary", "arbitrary")),
        name="diff_prompt",
    )(qd, dkv_p, pn, dl, sub.reshape(1, DF_DV))


def _tail_tile(new, width):
    t = new.shape[0]
    return jnp.concatenate([new, jnp.zeros((LANE - t, width), F32)], axis=0).astype(BF)


def _nsa_sample_kernel(pt_ref, *refs, n_pages, page, past, t_new, ncs, ns, wb):
    pages = refs[:n_pages]
    (q_ref, gate_ref, kvn_ref, wn_ref, st_ref, wc_ref, bc_ref, bs_ref, bw_ref, e_ref, s_ref,
     o_ref, kcmp_s, ksel_s, kwin_s) = refs[n_pages:]
    del pt_ref
    rows = NSA_G * t_new
    n_kinds = 4

    for p in range(n_pages):
        for kind in range(n_kinds):
            for h in range(NSA_KV):
                blk = pages[p][pl.ds(kind * NSA_KV + h, page, stride=n_kinds * NSA_KV), :]
                if kind < 2:
                    for nb in range(page // CMP):
                        r0 = (p * (page // CMP) + nb) * CMP_PITCH
                        kcmp_s[kind, h, r0:r0 + CMP, :] = blk[nb * CMP:(nb + 1) * CMP]
                else:
                    ksel_s[kind - 2, h, p * page:(p + 1) * page, :] = blk.astype(BF)
    kvn = kvn_ref[...]
    wn = wn_ref[...]
    for kind in range(2):
        for h in range(NSA_KV):
            c0 = ((kind + 2) * NSA_KV + h) * HD
            ksel_s[kind, h, past:past + LANE, :] = _tail_tile(kvn[:, c0:c0 + HD], HD)
            kwin_s[kind, h, 0:wb, :] = st_ref[pl.ds(kind * NSA_KV + h, wb, stride=2 * NSA_KV), :].astype(BF)
            c0 = (kind * NSA_KV + h) * HD
            kwin_s[kind, h, wb:wb + LANE, :] = _tail_tile(wn[:, c0:c0 + HD], HD)

    q = q_ref[...]
    gt = jax.nn.sigmoid(gate_ref[...])
    qpos = past + lax.broadcasted_iota(jnp.int32, (t_new, LANE), 0)
    for h in range(NSA_KV):
        cmp = []
        for kind in range(2):
            acc = jnp.zeros((ncs, HD), F32)
            for j in range(CMP):
                kj = kcmp_s[kind, h, pl.ds(j, ncs, stride=CMP_PITCH), :]
                acc = acc + _dot(kj.astype(BF), wc_ref[kind, j * HD:(j + 1) * HD, :])
            cmp.append(acc.astype(BF))
        kc, vc = cmp
        qs = jnp.concatenate([q[:, (h * NSA_G + g) * HD:(h * NSA_G + g + 1) * HD] for g in range(NSA_G)],
                             axis=0).astype(BF)
        bc = bc_ref[h * NSA_G:(h + 1) * NSA_G].reshape(rows, ncs)
        pc = _softmax_rows(_dot_nt(qs, kc) + bc, bc > 0.5 * NEG)
        oc = _dot(pc.astype(BF), vc)
        psum = pc[0:t_new]
        for g in range(1, NSA_G):
            psum = psum + pc[g * t_new:(g + 1) * t_new]
        selb = _select_blocks(psum, s_ref[...], qpos, ns)
        lk = past + LANE
        madd = (_dot(selb, e_ref[...]) - 1.0) * (-NEG)
        s = _dot_nt(qs, ksel_s[0, h])
        s = (s.reshape(NSA_G, t_new, lk) + bs_ref[h * NSA_G:(h + 1) * NSA_G] + madd[None]).reshape(rows, lk)
        osel = _dot(_softmax_rows(s).astype(BF), ksel_s[1, h])
        lw = wb + LANE
        s = _dot_nt(qs, kwin_s[0, h]) + bw_ref[h * NSA_G:(h + 1) * NSA_G].reshape(rows, lw)
        ow = _dot(_softmax_rows(s).astype(BF), kwin_s[1, h])
        for g in range(NSA_G):
            sl = slice(g * t_new, (g + 1) * t_new)
            gl = h * LANE + g
            o = (gt[:, gl:gl + 1] * oc[sl] + gt[:, gl + NSA_G:gl + NSA_G + 1] * osel[sl]
                 + gt[:, gl + 2 * NSA_G:gl + 2 * NSA_G + 1] * ow[sl])
            o_ref[:, (h * NSA_G + g) * HD:(h * NSA_G + g + 1) * HD] = o


def _nsa_sample(q_s, gates_s, kv4_s, win_s, cache, state, page_table, w_cmp, tbl_t, t_new):
    db, n_pages = page_table.shape
    n_phys, page = cache.shape[1], cache.shape[2]
    past = n_pages * page
    wb = state.shape[2]
    assert (past + t_new) // CMP * CMP <= past and past % SELB == 0 and wb == min(WIN, past)
    ncs = (past + t_new) // CMP
    ns = -(-(past + t_new) // SELB)
    lk, lw = past + LANE, wb + LANE
    rows_pp = page * 4 * NSA_KV
    cache2 = cache.reshape(cache.shape[0], n_phys, rows_pp, HD)
    state2 = state.reshape(state.shape[0], db, wb * 2 * NSA_KV, HD)
    wc = w_cmp.reshape(2, CMP * HD, HD).astype(BF)
    big = 1 << 30
    bc = _rel_pattern(tbl_t, 0, NSA_H, t_new, ncs, past - (CMP - 1), 0, big, False, col_step=CMP)
    bs = _rel_pattern(tbl_t, 0, NSA_H, t_new, lk, past, 0, big, False, c_lim=past + t_new)
    bw = _rel_pattern(tbl_t, 0, NSA_H, t_new, lw, wb, 0, WIN, False, c_lim=wb + t_new)
    e_mat = _expand_matrix(0, lk)
    s_mat = _pair_sum_matrix(ncs)
    full = lambda shape: pl.BlockSpec(shape, lambda b, pt: (0,) * len(shape))
    page_specs = [pl.BlockSpec((None, None, rows_pp, HD), lambda b, pt, p=p: (0, pt[b, p], 0, 0))
                  for p in range(n_pages)]
    in_specs = page_specs + [
        pl.BlockSpec((t_new, NSA_H * HD), lambda b, pt: (b, 0)),
        pl.BlockSpec((t_new, NSA_KV * LANE), lambda b, pt: (b, 0)),
        pl.BlockSpec((t_new, 4 * NSA_KV * HD), lambda b, pt: (b, 0)),
        pl.BlockSpec((t_new, 2 * NSA_KV * HD), lambda b, pt: (b, 0)),
        pl.BlockSpec((None, None, wb * 2 * NSA_KV, HD), lambda b, pt: (0, b, 0, 0)),
        full((2, CMP * HD, HD)), full((NSA_H, t_new, ncs)), full((NSA_H, t_new, lk)),
        full((NSA_H, t_new, lw)), full((LANE, lk)), full((ncs, LANE)),
    ]
    return pl.pallas_call(
        functools.partial(_nsa_sample_kernel, n_pages=n_pages, page=page, past=past, t_new=t_new,
                          ncs=ncs, ns=ns, wb=wb),
        grid_spec=pltpu.PrefetchScalarGridSpec(
            num_scalar_prefetch=1, grid=(db,), in_specs=in_specs,
            out_specs=pl.BlockSpec((t_new, NSA_H * HD), lambda b, pt: (b, 0)),
            scratch_shapes=[pltpu.VMEM((2, NSA_KV, ncs * CMP_PITCH, HD), F32),
                            pltpu.VMEM((2, NSA_KV, lk, HD), BF),
                            pltpu.VMEM((2, NSA_KV, lw, HD), BF)]),
        out_shape=jax.ShapeDtypeStruct((db * t_new, NSA_H * HD), F32),
        compiler_params=_cparams(("arbitrary",)),
        name="nsa_sample",
    )(page_table, *([cache2] * n_pages), q_s, gates_s, kv4_s, win_s, state2, wc, bc, bs, bw, e_mat, s_mat)


def _diff_sample_kernel(pt_ref, *refs, n_pages, page, past, t_new, lam_init):
    pages = refs[:n_pages]
    q_ref, kvn_ref, b_ref, dl_ref, sub_ref, o_ref, kv_s, stage_s = refs[n_pages:]
    del pt_ref
    rows = DF_G * t_new
    lk = past + LANE
    for p in range(n_pages):
        for h in range(DF_KV):
            stage_s[h] = pages[p][:, h, :]
            kv_s[h, p * page:(p + 1) * page, :] = stage_s[h].astype(BF)
    kvn = kvn_ref[...]
    for h in range(DF_KV):
        kv_s[h, past:past + LANE, :] = _tail_tile(kvn[:, h * 4 * HD:(h + 1) * 4 * HD], 4 * HD)
    lam = _diff_lambda(dl_ref[...], lam_init)
    q = q_ref[...]
    for h in range(DF_KV):
        bias = b_ref[h * DF_G:(h + 1) * DF_G].reshape(rows, lk)
        ps = []
        for m in range(2):
            qm = jnp.concatenate(
                [q[:, ((h * DF_G + g) * 2 + m) * HD:((h * DF_G + g) * 2 + m + 1) * HD] for g in range(DF_G)],
                axis=0).astype(BF)
            ps.append(_softmax_rows(_dot_nt(qm, kv_s[h, :, m * HD:(m + 1) * HD]) + bias))
        a = ps[0] - lam * ps[1]
        out = _diff_finish(_dot(a.astype(BF), kv_s[h, :, 2 * HD:4 * HD]), sub_ref[...], lam_init)
        for g in range(DF_G):
            o_ref[:, (h * DF_G + g) * DF_DV:(h * DF_G + g + 1) * DF_DV] = out[g * t_new:(g + 1) * t_new]


def _diff_sample(qd_s, dkv_s, cache, page_table, tbl_t, dl, sub, t_new, lam_init):
    db, n_pages = page_table.shape
    page = cache.shape[2]
    past = n_pages * page
    lk = past + LANE
    bias = _rel_pattern(tbl_t, NSA_H, DF_H, t_new, lk, past, 0, 1 << 30, False, c_lim=past + t_new)
    full = lambda shape: pl.BlockSpec(shape, lambda b, pt: (0,) * len(shape))
    page_specs = [pl.BlockSpec((None, None, page, DF_KV, 4 * HD), lambda b, pt, p=p: (0, pt[b, p], 0, 0, 0))
                  for p in range(n_pages)]
    in_specs = page_specs + [
        pl.BlockSpec((t_new, DF_H * 2 * HD), lambda b, pt: (b, 0)),
        pl.BlockSpec((t_new, DF_KV * 4 * HD), lambda b, pt: (b, 0)),
        full((DF_H, t_new, lk)), full((4, HD)), full((1, DF_DV)),
    ]
    return pl.pallas_call(
        functools.partial(_diff_sample_kernel, n_pages=n_pages, page=page, past=past, t_new=t_new,
                          lam_init=lam_init),
        grid_spec=pltpu.PrefetchScalarGridSpec(
            num_scalar_prefetch=1, grid=(db,), in_specs=in_specs,
            out_specs=pl.BlockSpec((t_new, DF_H * DF_DV), lambda b, pt: (b, 0)),
            scratch_shapes=[pltpu.VMEM((DF_KV, lk, 4 * HD), BF), pltpu.VMEM((DF_KV, page, 4 * HD), F32)]),
        out_shape=jax.ShapeDtypeStruct((db * t_new, DF_H * DF_DV), F32),
        compiler_params=_cparams(("arbitrary",)),
        name="diff_sample",
    )(page_table, *([cache] * n_pages), qd_s, dkv_s, bias, dl, sub.reshape(1, DF_DV))


def _router_kernel(h_ref, g_ref, wr_ref, br_ref, xn_ref, ids_ref, wts_ref):
    x = h_ref[...]
    ms = jnp.mean(x * x, axis=-1, keepdims=True)
    xn = x * lax.rsqrt(ms + EPS) * g_ref[...]
    xh = xn.astype(BF)
    xn_ref[...] = xh
    xl = (xn - xh.astype(F32)).astype(BF)
    wr = wr_ref[...]
    wh = wr.astype(BF)
    wl = (wr - wh.astype(F32)).astype(BF)
    lg = _dot(xh, wh) + _dot(xl, wh) + _dot(xh, wl) + br_ref[...]
    lane_i = lax.broadcasted_iota(jnp.int32, lg.shape, 1)
    lane = lane_i.astype(F32)
    big = 1000.0
    isg = lane_i < N_GROUPS
    gmax = jnp.max(jnp.where(isg, lg, -3e38), axis=-1, keepdims=True)
    gsel = jnp.min(jnp.where(isg & (lg == gmax), lane, big), axis=-1, keepdims=True)
    gw = 1.0 / jnp.sum(jnp.where(isg, jnp.exp(lg - gmax), 0.0), axis=-1, keepdims=True)
    lo = N_GROUPS + gsel * EPG
    ing = (lane >= lo) & (lane < lo + EPG)
    emax = jnp.max(jnp.where(ing, lg, -3e38), axis=-1, keepdims=True)
    pe = jnp.where(ing, jnp.exp(lg - emax), 0.0)
    pr = jnp.where(ing, pe / jnp.sum(pe, axis=-1, keepdims=True), -1.0)
    v1 = jnp.max(pr, axis=-1, keepdims=True)
    i1 = jnp.min(jnp.where(pr == v1, lane, big), axis=-1, keepdims=True)
    pr2 = jnp.where(lane == i1, -1.0, pr)
    v2 = jnp.max(pr2, axis=-1, keepdims=True)
    i2 = jnp.min(jnp.where(pr2 == v2, lane, big), axis=-1, keepdims=True)
    den = v1 + v2
    e12 = jnp.where(lane_i == 0, i1 - N_GROUPS, jnp.where(lane_i == 1, i2 - N_GROUPS, 0.0))
    ids_ref[...] = e12.astype(jnp.int32)
    wts_ref[...] = jnp.where(lane_i == 0, v1 / den * gw, jnp.where(lane_i == 1, v2 / den * gw, 0.0))


def _router(h, g, wr, br, tm):
    m, d = h.shape
    return pl.pallas_call(
        _router_kernel,
        grid=(m // tm,),
        in_specs=[pl.BlockSpec((tm, d), lambda i: (i, 0)), pl.BlockSpec((1, d), lambda i: (0, 0)),
                  pl.BlockSpec((d, LANE), lambda i: (0, 0)), pl.BlockSpec((1, LANE), lambda i: (0, 0))],
        out_specs=[pl.BlockSpec((tm, d), lambda i: (i, 0)), pl.BlockSpec((tm, LANE), lambda i: (i, 0)),
                   pl.BlockSpec((tm, LANE), lambda i: (i, 0))],
        out_shape=[jax.ShapeDtypeStruct((m, d), BF), jax.ShapeDtypeStruct((m, LANE), jnp.int32),
                   jax.ShapeDtypeStruct((m, LANE), F32)],
        compiler_params=_cparams(("arbitrary",)),
        name="router",
    )(h, g.reshape(1, d), wr, br)


def _split_dot(x_ref, w_refs):
    rows = w_refs[0].shape[0]
    acc = _dot(x_ref[:, 0:rows], w_refs[0][...].astype(BF))
    for s in range(1, len(w_refs)):
        acc = acc + _dot(x_ref[:, s * rows:(s + 1) * rows], w_refs[s][...].astype(BF))
    return acc


def _moe_up_kernel(te_ref, txb_ref, nt_ref, *refs, nk, n_split, nxb):
    xb = refs[:nxb]
    wg = refs[nxb:nxb + n_split]
    wu = refs[nxb + n_split:nxb + 2 * n_split]
    o_ref, x_s, a_s, u_s = refs[nxb + 2 * n_split:]
    k = pl.program_id(1)
    xg = xb[0].shape[0]

    @pl.when(pl.program_id(0) < nt_ref[0])
    def _():
        @pl.when(k == 0)
        def _():
            a_s[...] = jnp.zeros(a_s.shape, F32)
            u_s[...] = jnp.zeros(u_s.shape, F32)

        for j in range(nxb):
            x_s[j * xg:(j + 1) * xg, :] = xb[j][...]
        a_s[...] += _split_dot(x_s, wg)
        u_s[...] += _split_dot(x_s, wu)

        @pl.when(k == nk - 1)
        def _():
            g = a_s[...]
            o_ref[...] = (g * jax.nn.sigmoid(g) * u_s[...]).astype(o_ref.dtype)


def _moe_down_kernel(te_ref, nt_ref, h_ref, *refs, n_split):
    o_ref = refs[n_split]

    @pl.when(pl.program_id(0) < nt_ref[0])
    def _():
        o_ref[...] = _split_dot(h_ref, refs[:n_split]).astype(o_ref.dtype)


def _moe_sizes(m, n_exp):
    n_pairs = 2 * m
    return n_exp + n_pairs // MOE_TM, n_exp + n_pairs // MOE_XBLK


def _moe_dispatch(xn, ids, n_exp):
    m, d = xn.shape
    tm, xg = MOE_TM, MOE_XBLK
    n_pairs = 2 * m
    nxb = tm // xg
    n_tiles, n_xblk = _moe_sizes(m, n_exp)
    flat_e = ids[:, :2].reshape(-1)
    onehot = (flat_e[:, None] == jnp.arange(n_exp, dtype=jnp.int32)[None, :]).astype(jnp.int32)
    counts = jnp.sum(onehot, axis=0)
    tiles_per = (counts + tm - 1) // tm
    tile_end = jnp.cumsum(tiles_per)
    tile_start = tile_end - tiles_per
    xblk_per = (counts + xg - 1) // xg
    xblk_start = jnp.cumsum(xblk_per) - xblk_per
    rank = jnp.sum(onehot * (jnp.cumsum(onehot, axis=0) - 1), axis=1)
    pos = (jnp.sum(onehot * (tile_start * tm)[None, :], axis=1) + rank).reshape(m, 2)
    xpos = jnp.sum(onehot * (xblk_start * xg)[None, :], axis=1) + rank
    nt = tile_end[-1]
    tix = jnp.minimum(jnp.arange(n_tiles, dtype=jnp.int32), nt - 1)
    tile_e = jnp.minimum(jnp.searchsorted(tile_end, tix, side="right"), n_exp - 1).astype(jnp.int32)
    tile_xb = (xblk_start[tile_e] + (tix - tile_start[tile_e]) * nxb).astype(jnp.int32)
    row_tok = (jnp.arange(n_xblk * xg, dtype=jnp.int32) % m).at[xpos].set(
        jnp.arange(n_pairs, dtype=jnp.int32) // 2, mode="promise_in_bounds", unique_indices=True)
    xs = xn.at[row_tok].get(mode="promise_in_bounds")
    return xs, pos, tile_e, tile_xb, nt.reshape(1).astype(jnp.int32)


def _moe_experts(xs, tile_e, tile_xb, nt_arr, w_gate, w_up, w_down, m):
    d = xs.shape[1]
    n_exp, _, ff = w_gate.shape
    tm, xg = MOE_TM, MOE_XBLK
    nxb = tm // xg
    n_tiles, n_xblk = _moe_sizes(m, n_exp)
    n_rows = n_tiles * tm
    tk = _pick(d, (1024, 512, 256, 128))
    nk = d // tk
    live = lambda t, ntr: t < ntr[0]
    clamp = lambda t, ntr: jnp.minimum(t, ntr[0] - 1)
    kidx = lambda t, k, ntr: jnp.where(live(t, ntr), k, nk - 1)
    ns_up = MOE_DMA_SPLIT if tk % (MOE_DMA_SPLIT * LANE) == 0 else 1
    xspecs = [pl.BlockSpec((xg, tk), lambda t, k, te, txb, ntr, j=j: (jnp.minimum(txb[t] + j, n_xblk - 1),
                                                                     kidx(t, k, ntr))) for j in range(nxb)]
    wspecs = [pl.BlockSpec((None, tk // ns_up, ff),
                           lambda t, k, te, txb, ntr, s=s: (te[t], kidx(t, k, ntr) * ns_up + s, 0))
              for s in range(ns_up)]
    hdn = pl.pallas_call(
        functools.partial(_moe_up_kernel, nk=nk, n_split=ns_up, nxb=nxb),
        grid_spec=pltpu.PrefetchScalarGridSpec(
            num_scalar_prefetch=3, grid=(n_tiles, nk),
            in_specs=xspecs + wspecs + wspecs,
            out_specs=pl.BlockSpec((tm, ff), lambda t, k, te, txb, ntr: (clamp(t, ntr), 0)),
            scratch_shapes=[pltpu.VMEM((tm, tk), BF), pltpu.VMEM((tm, ff), F32), pltpu.VMEM((tm, ff), F32)]),
        out_shape=jax.ShapeDtypeStruct((n_rows, ff), BF),
        compiler_params=_cparams(("arbitrary", "arbitrary")),
        name="moe_up",
    )(tile_e, tile_xb, nt_arr, *([xs] * nxb), *([w_gate] * ns_up), *([w_up] * ns_up))

    tn = _pick(d, (2048, 1024, 512, 256, 128))
    nn = d // tn
    ns_dn = MOE_DMA_SPLIT if ff % (MOE_DMA_SPLIT * LANE) == 0 else 1
    jidx = lambda t, j, ntr: jnp.where(live(t, ntr), j, nn - 1)
    y = pl.pallas_call(
        functools.partial(_moe_down_kernel, n_split=ns_dn),
        grid_spec=pltpu.PrefetchScalarGridSpec(
            num_scalar_prefetch=2, grid=(n_tiles, nn),
            in_specs=[pl.BlockSpec((tm, ff), lambda t, j, te, ntr: (clamp(t, ntr), 0))]
            + [pl.BlockSpec((None, ff // ns_dn, tn), lambda t, j, te, ntr, s=s: (te[t], s, jidx(t, j, ntr)))
               for s in range(ns_dn)],
            out_specs=pl.BlockSpec((tm, tn), lambda t, j, te, ntr: (clamp(t, ntr), jidx(t, j, ntr)))),
        out_shape=jax.ShapeDtypeStruct((n_rows, d), BF),
        compiler_params=_cparams(("arbitrary", "arbitrary")),
        name="moe_down",
    )(tile_e, nt_arr, hdn, *([w_down] * ns_dn))
    return y


def _split_cols_kernel(x_ref, o_ref, *, dims, width):
    for i in range(dims[0]):
        if len(dims) == 1:
            o_ref[:, i, :] = x_ref[:, i * width:(i + 1) * width]
        else:
            for j in range(dims[1]):
                c = (i * dims[1] + j) * width
                o_ref[:, i, j, :] = x_ref[:, c:c + width]


def _split_cols(x, batch, rows_per_batch, row0, dims, width, tb=256):
    seq = x.shape[0] // batch
    tb = _pick(math.gcd(rows_per_batch, math.gcd(row0, seq)) if row0 else math.gcd(rows_per_batch, seq),
               (tb, 128, 64, 32, 16, 8))
    nb = rows_per_batch // tb
    ncol = x.shape[1]
    zeros = (0,) * (len(dims) + 1)
    return pl.pallas_call(
        functools.partial(_split_cols_kernel, dims=dims, width=width),
        grid=(batch, nb),
        in_specs=[pl.BlockSpec((tb, ncol), lambda b, i: ((b * seq + row0) // tb + i, 0))],
        out_specs=pl.BlockSpec((None, tb) + tuple(dims) + (width,), lambda b, i: (b, i) + zeros),
        out_shape=jax.ShapeDtypeStruct((batch, rows_per_batch) + tuple(dims) + (width,), x.dtype),
        compiler_params=_cparams(("arbitrary", "arbitrary")),
        name="split_cols",
    )(x)


def _shift_window_kernel(s_ref, w_ref, o_ref, *, t_new, n_kh):
    nbb, rows, _ = s_ref.shape
    keep = rows - t_new * n_kh
    o_ref[:, 0:keep, :] = s_ref[:, t_new * n_kh:rows, :]
    w = w_ref[...]
    for bb in range(nbb):
        for k in range(n_kh):
            o_ref[bb, pl.ds(keep + k, t_new, stride=n_kh), :] = w[bb * t_new:(bb + 1) * t_new, k * HD:(k + 1) * HD]


def _shift_window(state, win_s, t_new):
    _, db, wb, n_kind, n_head, hd = state.shape
    n_kh = n_kind * n_head
    rows = wb * n_kh
    nbb = _pick(db, (4, 2, 1))
    out = pl.pallas_call(
        functools.partial(_shift_window_kernel, t_new=t_new, n_kh=n_kh),
        grid=(db // nbb,),
        in_specs=[pl.BlockSpec((None, nbb, rows, hd), lambda i: (0, i, 0, 0)),
                  pl.BlockSpec((nbb * t_new, n_kh * hd), lambda i: (i, 0))],
        out_specs=pl.BlockSpec((nbb, rows, hd), lambda i: (i, 0, 0)),
        out_shape=jax.ShapeDtypeStruct((db, rows, hd), state.dtype),
        compiler_params=_cparams(("arbitrary",)),
        name="shift_window",
    )(state.reshape(state.shape[0], db, rows, hd), win_s)
    return out.reshape(1, db, wb, n_kind, n_head, hd)


O_KV = NSA_H * HD
O_GATE = O_KV + 6 * NSA_KV * HD
O_QD = O_GATE + 3 * NSA_H
W_QD = DF_H * 2 * HD
W_KD = DF_KV * 2 * HD


def _gate_lanes(raw):
    m = raw.shape[0]
    g = raw[:, :3 * NSA_H].reshape(m, 3, NSA_KV, NSA_G).transpose(0, 2, 1, 3).reshape(m, NSA_KV, 3 * NSA_G)
    return jnp.pad(g, ((0, 0), (0, 0), (0, LANE - 3 * NSA_G))).reshape(m, NSA_KV * LANE)


def kernel(x_prompt, x_sample, cache_nsa_kv, cache_diff_kv, state_nsa_win, page_table, p_prompt, p_sample,
           rel_bias_table, norm_mix, w_in, w_cmp, diff_lambda, diff_subln, w_out, norm_ffn, w_router_group,
           b_router_group, w_router_expert, b_router_expert, w_exp_gate, w_exp_up, w_exp_down, norm_ple,
           w_ple_gate, w_ple_proj, final_norm):
    assert norm_mix.shape[0] == 1, "single-layer trunk"
    batch, seq, d = x_prompt.shape
    db, t_new, _ = x_sample.shape
    mp, ms = batch * seq, db * t_new
    m = mp + ms
    lam_init = 0.8 - 0.6 * math.exp(-0.3 * 0)
    tm = _pick(math.gcd(mp, ms), (1024, 512, 256, 128))
    tm_s = _pick(math.gcd(mp, ms), (256, 128))
    xp = x_prompt.reshape(mp, d)
    xs = x_sample.reshape(ms, d)
    tbl_t = rel_bias_table

    xn = _rms2(xp, xs, norm_mix[0], tm_s)
    w0 = jnp.swapaxes(w_in[0], 0, 1)
    off = O_QD - O_GATE
    scale = HD ** -0.5
    tn = 512
    wide = functools.partial(_matmul, [xn], tm=tm, tn=tn, wt=True)
    q_nsa = wide([(w0, 0)], rows=m, n_cols=O_KV, out_dtype=BF, scale=scale, name="proj_qn")
    shifted = lambda base, width: (w0, 0, lambda j: base // width + j,
                                   lambda j: (base + width) // LANE + j * (width // LANE))
    qd = wide([shifted(O_GATE, tn)], rows=m, n_cols=W_QD, out_dtype=BF, scale=scale, lane_off=off, name="proj_qd")
    gates = _gate_lanes(_matmul([xn], [(w0, 0, lambda j: O_GATE // LANE)], rows=m, n_cols=LANE, tm=tm, tn=LANE,
                                out_dtype=F32, wt=True, name="proj_gate"))
    kv_w = [(w0, 0, lambda j: j + O_KV // tn)]
    win_w = [(w0, 0, lambda j: j + (O_KV + 4 * NSA_KV * HD) // tn)]
    dkv_w = [shifted(O_GATE + W_QD, DF_DV), shifted(O_GATE + W_QD + W_KD, DF_DV)]
    dkv = functools.partial(_matmul, [xn], dkv_w, tm=tm, tn=2 * DF_DV, n_cols=DF_KV * 4 * HD, out_dtype=F32,
                            epi="cat", lane_off=off, wt=True)
    kv4_p = wide(kv_w, rows=mp, n_cols=4 * NSA_KV * HD, out_dtype=F32, name="proj_kv_p")
    kv4_s = wide(kv_w, rows=ms, row0=mp, n_cols=4 * NSA_KV * HD, out_dtype=F32, name="proj_kv_s")
    win_p = wide(win_w, rows=mp, n_cols=2 * NSA_KV * HD, out_dtype=F32, name="proj_win_p")
    win_s = wide(win_w, rows=ms, row0=mp, n_cols=2 * NSA_KV * HD, out_dtype=F32, name="proj_win_s")
    dkv_p = dkv(rows=mp, name="proj_dkv_p")
    dkv_s = dkv(rows=ms, row0=mp, name="proj_dkv_s")

    nc = seq // CMP
    kvr = kv4_p.reshape(batch, seq, 4, NSA_KV, HD)
    cmp_out = []
    for kind in range(2):
        a = kvr[:, :, kind].transpose(2, 0, 1, 3).reshape(NSA_KV * batch * nc, CMP * HD).astype(BF)
        r = a.shape[0]
        cmp_out.append(_matmul([a], [(w_cmp[0, kind].reshape(CMP * HD, HD), 0)], rows=r, n_cols=HD,
                               tm=_pick(r, (512, 256, 128, 64, 32, 16)), tn=HD, out_dtype=BF,
                               name="compress").reshape(NSA_KV, batch, nc, HD))
    o_n = _nsa_prompt(q_nsa, gates, cmp_out[0], cmp_out[1], kv4_p, win_p, tbl_t, batch, seq, m)
    o_d = _diff_prompt(qd, dkv_p, tbl_t, diff_lambda[0], diff_subln[0], batch, seq, m, lam_init)

    o_n_s = _nsa_sample(q_nsa[mp:].astype(F32), gates[mp:], kv4_s, win_s, cache_nsa_kv, state_nsa_win,
                        page_table, w_cmp[0], tbl_t, t_new)
    o_d_s = _diff_sample(qd[mp:].astype(F32), dkv_s, cache_diff_kv, page_table, tbl_t, diff_lambda[0],
                         diff_subln[0], t_new, lam_init)
    o_n = lax.dynamic_update_slice(o_n, o_n_s.astype(BF), (mp, 0))
    o_d = lax.dynamic_update_slice(o_d, o_d_s.astype(BF), (mp, 0))

    h1 = _matmul([o_n, o_d], [(w_out[0], 0), (w_out[0], 1)], rows=m, n_cols=d, tm=tm,
                 tn=_pick(d, (512, 256, 128)), out_dtype=F32, epi="res2", epi_args=(xp, xs), name="out_proj")

    wr = jnp.concatenate([w_router_group[0], w_router_expert[0],
                          jnp.zeros((d, LANE - N_GROUPS - N_EXP), F32)], axis=1)
    br = jnp.concatenate([b_router_group[0], b_router_expert[0],
                          jnp.zeros((LANE - N_GROUPS - N_EXP,), F32)]).reshape(1, LANE)
    xn2, ids, wts = _router(h1, norm_ffn[0], wr, br, tm_s)
    xs_moe, pos, tile_e, tile_xb, nt_arr = _moe_dispatch(xn2, ids, N_EXP)
    wk = min(WIN, seq)
    nsa_kv_p = _split_cols(kv4_p, batch, seq, 0, (4, NSA_KV), HD)
    diff_kv_p = _split_cols(dkv_p, batch, seq, 0, (DF_KV,), 4 * HD)
    win_p_out = _split_cols(win_p, batch, wk, seq - wk, (2, NSA_KV), HD)
    new_win = _shift_window(state_nsa_win, win_s, t_new)
    y = _moe_experts(xs_moe, tile_e, tile_xb, nt_arr, w_exp_gate[0], w_exp_up[0], w_exp_down[0], m)
    h2, xn3 = _combine_rms(h1, y.at[pos[:, 0]].get(mode="promise_in_bounds"),
                           y.at[pos[:, 1]].get(mode="promise_in_bounds"), wts, norm_ple[0], tm_s)

    p_all = jnp.concatenate([p_prompt[0].reshape(mp, -1), p_sample[0].reshape(ms, -1)], axis=0).astype(BF)
    h3 = _matmul([xn3], [(w_ple_gate[0], 0)], rows=m, n_cols=d, tm=tm, tn=_pick(d, (512, 256, 128)),
                 out_dtype=F32, epi="ple", epi_args=(h2, p_all, w_ple_proj[0]), name="ple")

    y_p = _rms(h3, final_norm, tm_s, F32, row0=0, rows=mp).reshape(batch, seq, d)
    y_s = _rms(h3, final_norm, tm_s, F32, row0=mp, rows=ms).reshape(db, t_new, d)
    return (y_p, y_s,
            nsa_kv_p[None], kv4_s.reshape(1, db, t_new, 4, NSA_KV, HD),
            diff_kv_p[None], dkv_s.reshape(1, db, t_new, DF_KV, 4 * HD),
            win_p_out[None], new_win)
```

```python
import functools
import math

import numpy as np
import jax
import jax.numpy as jnp
from jax import lax
from jax.experimental import pallas as pl
from jax.experimental.pallas import tpu as pltpu

BF = jnp.bfloat16
F32 = jnp.float32

HD = 128
NSA_H = 16
NSA_KV = 2
NSA_G = NSA_H // NSA_KV
CMP = 32
SELB = 64
TOPK = 16
WIN = 512
DF_H = 8
DF_KV = 4
DF_G = DF_H // DF_KV
DF_DV = 2 * HD
REL_BUCKETS = 32
REL_MAX_DIST = 128
N_GROUPS = 4
EPG = 8
N_EXP = N_GROUPS * EPG
EPS = 1e-6
NEG = -1e30
FORCE = 1e4
LANE = 128
VMEM_LIMIT = 56 * 1024 * 1024
MOE_TM = 768
MOE_XBLK = 128
MOE_DMA_SPLIT = 4

QB_NSA = 128
PAD_SEL = 384
PAD_WIN = WIN
NEAR_SEL = 512
BAND_WIN = WIN + QB_NSA
QB_DF = 256
PAD_DF = 256
NEAR_DF = 512
TK = 256
CMP_PITCH = 40


def _dot(a, b):
    return jnp.dot(a, b, preferred_element_type=F32)


def _dot_nt(a, b):
    return lax.dot_general(a, b, (((1,), (1,)), ((), ())), preferred_element_type=F32)


def _cparams(sem):
    return pltpu.CompilerParams(dimension_semantics=sem, vmem_limit_bytes=VMEM_LIMIT)


def _pick(n, cands):
    for c in cands:
        if n % c == 0:
            return c
    raise ValueError(f"no tile in {cands} divides {n}")


def _rms2_kernel(xp_ref, xs_ref, g_ref, o_ref, *, np_tiles):
    i = pl.program_id(0)

    def go(x_ref):
        x = x_ref[...]
        ms = jnp.mean(x * x, axis=-1, keepdims=True)
        o_ref[...] = (x * lax.rsqrt(ms + EPS) * g_ref[...]).astype(o_ref.dtype)

    @pl.when(i < np_tiles)
    def _():
        go(xp_ref)

    @pl.when(i >= np_tiles)
    def _():
        go(xs_ref)


def _rms2(xp, xs, g, tm):
    mp, d = xp.shape
    ms = xs.shape[0]
    npt, nst = mp // tm, ms // tm
    return pl.pallas_call(
        functools.partial(_rms2_kernel, np_tiles=npt),
        grid=(npt + nst,),
        in_specs=[pl.BlockSpec((tm, d), lambda i: (jnp.minimum(i, npt - 1), 0)),
                  pl.BlockSpec((tm, d), lambda i: (jnp.maximum(i - npt, 0), 0)),
                  pl.BlockSpec((1, d), lambda i: (0, 0))],
        out_specs=pl.BlockSpec((tm, d), lambda i: (i, 0)),
        out_shape=jax.ShapeDtypeStruct((mp + ms, d), BF),
        compiler_params=_cparams(("arbitrary",)),
        name="rms2",
    )(xp, xs, g.reshape(1, d))


def _rms_kernel(x_ref, g_ref, o_ref):
    x = x_ref[...]
    ms = jnp.mean(x * x, axis=-1, keepdims=True)
    o_ref[...] = (x * lax.rsqrt(ms + EPS) * g_ref[...]).astype(o_ref.dtype)


def _rms(x, g, tm, out_dtype, row0=0, rows=None):
    m, d = x.shape
    rows = m if rows is None else rows
    t0 = row0 // tm
    return pl.pallas_call(
        _rms_kernel,
        grid=(rows // tm,),
        in_specs=[pl.BlockSpec((tm, d), lambda i: (i + t0, 0)),
                  pl.BlockSpec((1, d), lambda i: (0, 0))],
        out_specs=pl.BlockSpec((tm, d), lambda i: (i, 0)),
        out_shape=jax.ShapeDtypeStruct((rows, d), out_dtype),
        compiler_params=_cparams(("arbitrary",)),
        name="rms",
    )(x, g.reshape(1, d))


def _combine_rms_kernel(h_ref, y0_ref, y1_ref, w_ref, g_ref, h2_ref, xn_ref):
    w = w_ref[...]
    h2 = h_ref[...] + w[:, 0:1] * y0_ref[...].astype(F32) + w[:, 1:2] * y1_ref[...].astype(F32)
    h2_ref[...] = h2
    ms = jnp.mean(h2 * h2, axis=-1, keepdims=True)
    xn_ref[...] = (h2 * lax.rsqrt(ms + EPS) * g_ref[...]).astype(xn_ref.dtype)


def _combine_rms(h, y0, y1, wts, g, tm):
    m, d = h.shape
    row = lambda width: pl.BlockSpec((tm, width), lambda i: (i, 0))
    return pl.pallas_call(
        _combine_rms_kernel,
        grid=(m // tm,),
        in_specs=[row(d), row(d), row(d), row(LANE), pl.BlockSpec((1, d), lambda i: (0, 0))],
        out_specs=[row(d), row(d)],
        out_shape=[jax.ShapeDtypeStruct((m, d), F32), jax.ShapeDtypeStruct((m, d), BF)],
        compiler_params=_cparams(("arbitrary",)),
        name="combine_rms",
    )(h, y0, y1, wts, g.reshape(1, d))


def _cast_rows(src_ref, dst_ref):
    k = src_ref.shape[0]
    ch = 256 if k % 256 == 0 else k

    def body(c, carry):
        r = pl.multiple_of(c * ch, ch)
        dst_ref[pl.ds(r, ch), :] = src_ref[pl.ds(r, ch), :].astype(BF)
        return carry

    lax.fori_loop(0, k // ch, body, 0)


def _cast_out_rows(main_ref, tail_ref, dst_ref, off):
    n = dst_ref.shape[0]
    ch = min(n, LANE)
    for c in range(0, n, ch):
        lo = c + off
        if lo + ch <= n:
            blk = main_ref[lo:lo + ch, :]
        else:
            blk = jnp.concatenate([main_ref[lo:n, :], tail_ref[0:lo + ch - n, :]], axis=0)
        dst_ref[c:c + ch, :] = blk.astype(BF)


def _mm_kernel(*refs, n_a, n_w, cast, epi, scale, np_tiles, lane_off, wt):
    a = refs[:n_a]
    w = refs[n_a:n_a + n_w]
    idx = n_a + n_w
    tails = (None,) * n_w
    if lane_off:
        tails = refs[idx:idx + n_w]
        idx += n_w
    if epi == "res2":
        rp_ref, rs_ref = refs[idx:idx + 2]
        idx += 2
    elif epi == "ple":
        h_ref, p_ref, wp_ref = refs[idx:idx + 3]
        idx += 3
    o_ref = refs[idx]
    idx += 1
    wb = refs[idx:idx + n_w] if cast else w
    i = pl.program_id(1)

    if cast:
        @pl.when(i == 0)
        def _():
            for k in range(n_w):
                if wt:
                    _cast_out_rows(w[k], tails[k], wb[k], lane_off)
                else:
                    _cast_rows(w[k], wb[k])

    dot = _dot_nt if wt else _dot
    if epi == "cat":
        x = a[0][...]
        wd = o_ref.shape[1] // n_w
        for k in range(n_w):
            o_ref[:, k * wd:(k + 1) * wd] = dot(x, wb[k][...]).astype(o_ref.dtype)
        return
    acc = dot(a[0][...], wb[0][...])
    for k in range(1, n_a):
        acc = acc + dot(a[k][...], wb[k][...])
    if scale is not None:
        acc = acc * scale
    if epi is None:
        o_ref[...] = acc.astype(o_ref.dtype)
    elif epi == "res2":
        @pl.when(i < np_tiles)
        def _():
            o_ref[...] = (acc + rp_ref[...]).astype(o_ref.dtype)

        @pl.when(i >= np_tiles)
        def _():
            o_ref[...] = (acc + rs_ref[...]).astype(o_ref.dtype)
    elif epi == "ple":
        gate = jax.nn.sigmoid(acc)
        proj = _dot(p_ref[...], wp_ref[...].astype(BF))
        o_ref[...] = (h_ref[...] + gate * proj).astype(o_ref.dtype)


def _matmul(a_list, w_list, *, rows, n_cols, tm, tn, out_dtype, row0=0, col0=0,
            scale=None, epi=None, epi_args=(), lane_off=0, wt=False, name="mm"):
    n_a, n_w = len(a_list), len(w_list)
    cast = w_list[0][0].dtype != BF
    assert cast or not wt
    assert wt or not lane_off
    wblock = (lambda kdim, width, kb, col: ((width, kdim), (col, kb))) if wt else \
             (lambda kdim, width, kb, col: ((kdim, width), (kb, col)))
    t0, c0 = row0 // tm, col0 // tn
    gm, gn = rows // tm, n_cols // tn
    wn = tn // n_w if epi == "cat" else tn
    in_specs, args = [], []
    for a in a_list:
        in_specs.append(pl.BlockSpec((tm, a.shape[1]), lambda j, i: (i + t0, 0)))
        args.append(a)
    for k, ent in enumerate(w_list):
        w, kb = ent[0], ent[1]
        colfn = ent[2] if len(ent) > 2 else (lambda j: j + c0)
        kdim = a_list[min(k, n_a - 1)].shape[1]
        shape = wblock(kdim, wn, 0, 0)[0]
        in_specs.append(pl.BlockSpec(shape, lambda j, i, kb=kb, colfn=colfn: wblock(0, 0, kb, colfn(j))[1]))
        args.append(w)
    if lane_off:
        for k, ent in enumerate(w_list):
            kdim = a_list[min(k, n_a - 1)].shape[1]
            shape = wblock(kdim, LANE, 0, 0)[0]
            in_specs.append(pl.BlockSpec(shape, lambda j, i, kb=ent[1], tailfn=ent[3]: wblock(0, 0, kb, tailfn(j))[1]))
            args.append(ent[0])
    np_tiles = 0
    if epi == "res2":
        xp, xs = epi_args
        np_tiles = xp.shape[0] // tm
        in_specs.append(pl.BlockSpec((tm, tn), lambda j, i: (jnp.minimum(i, np_tiles - 1), j)))
        in_specs.append(pl.BlockSpec((tm, tn), lambda j, i: (jnp.maximum(i - np_tiles, 0), j)))
        args += [xp, xs]
    elif epi == "ple":
        h, p, wp = epi_args
        in_specs.append(pl.BlockSpec((tm, tn), lambda j, i: (i, j)))
        in_specs.append(pl.BlockSpec((tm, p.shape[1]), lambda j, i: (i, 0)))
        in_specs.append(pl.BlockSpec((wp.shape[0], tn), lambda j, i: (0, j)))
        args += [h, p, wp]
    scratch = [pltpu.VMEM(wblock(a_list[min(k, n_a - 1)].shape[1], wn, 0, 0)[0], BF)
               for k in range(n_w)] if cast else []
    return pl.pallas_call(
        functools.partial(_mm_kernel, n_a=n_a, n_w=n_w, cast=cast, epi=epi, scale=scale, np_tiles=np_tiles,
                          lane_off=lane_off, wt=wt),
        grid=(gn, gm),
        in_specs=in_specs,
        out_specs=pl.BlockSpec((tm, tn), lambda j, i: (i, j)),
        out_shape=jax.ShapeDtypeStruct((rows, n_cols), out_dtype),
        scratch_shapes=scratch,
        compiler_params=_cparams(("arbitrary", "arbitrary")),
        name=name,
    )(*args)


def _bucket_np(dist):
    n = np.maximum(dist, 0)
    max_exact = REL_BUCKETS // 2
    nf = np.maximum(n, 1).astype(np.float32)
    log_b = max_exact + (np.log(nf / np.float32(max_exact)) / np.float32(math.log(REL_MAX_DIST / max_exact))
                         * np.float32(REL_BUCKETS - max_exact)).astype(np.int32)
    return np.where(n < max_exact, n, np.minimum(log_b, REL_BUCKETS - 1)).astype(np.int32)


def _bucket_edges():
    b = _bucket_np(np.arange(0, 4 * REL_MAX_DIST))
    return [(k, int(np.nonzero(b == k)[0].max())) for k in range(REL_BUCKETS - 1) if (b == k).any()]


def _pattern_kernel(tbl_ref, o_ref, *, h0, base, col_step, v_lo, v_hi, c_lim, sub_far, edges):
    h = pl.program_id(0) + h0
    shape = o_ref.shape
    row = lax.broadcasted_iota(jnp.int32, shape, 0) + pl.program_id(1) * shape[0]
    col = lax.broadcasted_iota(jnp.int32, shape, 1)
    dist = row + base - col_step * col
    far = tbl_ref[REL_BUCKETS - 1, h]
    b = jnp.full(shape, far, F32)
    for k, hi in reversed(edges):
        b = jnp.where(dist <= hi, tbl_ref[k, h], b)
    if sub_far:
        b = b - far
    valid = (dist >= v_lo) & (dist <= v_hi) & (col < c_lim)
    o_ref[...] = jnp.where(valid, b, NEG)


def _rel_pattern(tbl, h0, nh, nrows, ncols, base, v_lo, v_hi, sub_far, col_step=1, c_lim=None):
    tr = _pick(nrows, (512, 256, 128, 64, 32, 16, 8))
    return pl.pallas_call(
        functools.partial(_pattern_kernel, h0=h0, base=base, col_step=col_step, v_lo=v_lo, v_hi=v_hi,
                          c_lim=ncols if c_lim is None else c_lim, sub_far=sub_far, edges=_bucket_edges()),
        grid=(nh, nrows // tr),
        in_specs=[pl.BlockSpec(memory_space=pltpu.SMEM)],
        out_specs=pl.BlockSpec((None, tr, ncols), lambda h, r: (h, r, 0)),
        out_shape=jax.ShapeDtypeStruct((nh, nrows, ncols), F32),
        compiler_params=_cparams(("arbitrary", "arbitrary")),
        name="rel_bias",
    )(tbl)


def _pair_sum_matrix(nc):
    n = np.arange(nc)[:, None]
    b = np.arange(LANE)[None, :]
    return jnp.asarray((n // (SELB // CMP) == b).astype(np.float32), dtype=BF)


def _expand_matrix(pad, n_keys):
    l = np.arange(LANE)[:, None]
    c = np.arange(pad + n_keys)[None, :]
    return jnp.asarray(((c >= pad) & ((c - pad) // SELB == l)).astype(np.float32), dtype=BF)


def _split3(x):
    hi = x.astype(BF)
    r = x - hi.astype(F32)
    mid = r.astype(BF)
    lo = (r - mid.astype(F32)).astype(BF)
    return hi, mid, lo


def _select_blocks(psum, s_mat, qpos, ns):
    hi, mid, lo = _split3(psum)
    imp = _dot(hi, s_mat) + _dot(mid, s_mat) + _dot(lo, s_mat)
    shape = imp.shape
    lane = lax.broadcasted_iota(jnp.int32, shape, 1)
    valid = lane * SELB <= qpos
    cur = jnp.right_shift(qpos, 6)
    forced = (lane == 0) | (lane == cur) | (lane == cur - 1)
    score = jnp.where(valid, imp + jnp.where(forced, FORCE, 0.0), NEG)
    score = jnp.where(lane < ns, score, -3e38)
    cnt = jnp.zeros(shape, F32)
    for i in range(ns):
        ci = score[:, i:i + 1]
        cnt = cnt + jnp.where(lane > i, jnp.where(ci >= score, 1.0, 0.0), jnp.where(ci > score, 1.0, 0.0))
    sel = (cnt < float(min(TOPK, ns))) & (lane < ns)
    return jnp.where(sel, 1.0, 0.0).astype(BF)


def _softmax_rows(s, valid=None):
    m = jnp.max(s, axis=-1, keepdims=True)
    p = jnp.exp(s - m)
    if valid is not None:
        p = jnp.where(valid, p, 0.0)
    l = jnp.sum(p, axis=-1, keepdims=True)
    return p / jnp.where(l > 0.0, l, 1.0)


def _lanes(x, n):
    return x if n == LANE else jnp.concatenate([x] * (n // LANE), axis=1)


def _online(carry, s, vt, ones_in_v=False):
    m, l, acc = carry
    dv = acc.shape[1]
    m_new = jnp.maximum(m, jnp.max(s, axis=-1, keepdims=True))
    p = jnp.exp(s - _lanes(m_new, s.shape[1]))
    alpha = jnp.exp(m - m_new)
    pv = _dot(p.astype(BF), vt)
    if ones_in_v:
        l = alpha * l + pv[:, dv:dv + LANE]
        pv = pv[:, 0:dv]
    else:
        l = alpha * l + jnp.sum(p, axis=-1, keepdims=True)
    return m_new, l, _lanes(alpha, dv) * acc + pv


def _online_init(rows, dv):
    return jnp.full((rows, LANE), NEG, F32), jnp.zeros((rows, LANE), F32), jnp.zeros((rows, dv), F32)


def _flash_step(q, kt, vt, bias, m_ref, l_ref, acc_ref, rows, ones_in_v=False):
    s = _dot_nt(q, kt)
    if bias is not None:
        s = s + bias
    m_new, l, acc = _online((m_ref[rows, :], l_ref[rows, :], acc_ref[rows, :]), s, vt, ones_in_v)
    m_ref[rows, :] = m_new
    l_ref[rows, :] = l
    acc_ref[rows, :] = acc


def _nsa_prompt_kernel(q_ref, gate_ref, kc_ref, vc_ref, ks_ref, vs_ref, kw_ref, vw_ref,
                       bc_ref, pn_ref, pw_ref, e_ref, s_ref, o_ref,
                       ks_s, vs_s, kw_s, vw_s, m_s, l_s, acc_s, o_s, *, seq, nc, ns):
    i = pl.program_id(2)
    s0 = i * QB_NSA
    rows = NSA_G * QB_NSA

    @pl.when(i == 0)
    def _():
        ks_s[0:PAD_SEL, :] = jnp.zeros((PAD_SEL, HD), BF)
        vs_s[0:PAD_SEL, :] = jnp.zeros((PAD_SEL, 2 * HD), BF)
        kw_s[0:PAD_WIN, :] = jnp.zeros((PAD_WIN, HD), BF)
        vw_s[0:PAD_WIN, :] = jnp.zeros((PAD_WIN, 2 * HD), BF)
        ch = 512
        ones = jnp.ones((ch, HD), BF)

        def cp(c, carry):
            r = pl.multiple_of(c * ch, ch)
            ks_s[pl.ds(PAD_SEL + r, ch), :] = ks_ref[pl.ds(r, ch), :].astype(BF)
            vs_s[pl.ds(PAD_SEL + r, ch), 0:HD] = vs_ref[pl.ds(r, ch), :].astype(BF)
            vs_s[pl.ds(PAD_SEL + r, ch), HD:2 * HD] = ones
            kw_s[pl.ds(PAD_WIN + r, ch), :] = kw_ref[pl.ds(r, ch), :].astype(BF)
            vw_s[pl.ds(PAD_WIN + r, ch), 0:HD] = vw_ref[pl.ds(r, ch), :].astype(BF)
            vw_s[pl.ds(PAD_WIN + r, ch), HD:2 * HD] = ones
            return carry

        lax.fori_loop(0, seq // ch, cp, 0)

    gt = jax.nn.sigmoid(gate_ref[...])
    head = lambda g: slice(g * HD, (g + 1) * HD)
    hrows = lambda g: slice(g * QB_NSA, (g + 1) * QB_NSA)

    kc = kc_ref[...]
    vc = vc_ref[...]
    psum = jnp.zeros((QB_NSA, nc), F32)
    for g in range(NSA_G):
        bc = bc_ref[g]
        pc = _softmax_rows(_dot_nt(q_ref[:, head(g)], kc) + bc, bc > 0.5 * NEG)
        psum = psum + pc
        o_s[:, head(g)] = gt[:, g:g + 1] * _dot(pc.astype(BF), vc)
    qpos = s0 + lax.broadcasted_iota(jnp.int32, (QB_NSA, LANE), 0)
    selb = _select_blocks(psum, s_ref[...], qpos, ns)

    nch = jnp.maximum(i - 1, 0) // 2
    far_keys = nch * TK
    m_s[...] = jnp.full((rows, LANE), NEG, F32)
    l_s[...] = jnp.zeros((rows, LANE), F32)
    acc_s[...] = jnp.zeros((rows, HD), F32)

    def sel_chunk(r, bias_of):
        kt = ks_s[pl.ds(r, TK), :]
        vt = vs_s[pl.ds(r, TK), :]
        madd = (_dot(selb, e_ref[:, pl.ds(r, TK)]) - 1.0) * (-NEG)
        for g in range(NSA_G):
            _flash_step(q_ref[:, head(g)], kt, vt, bias_of(g, madd), m_s, l_s, acc_s, hrows(g), ones_in_v=True)

    def far(c, carry):
        sel_chunk(pl.multiple_of(PAD_SEL + c * TK, LANE), lambda g, madd: madd)
        return carry

    lax.fori_loop(0, nch, far, 0)
    for kh in range(NEAR_SEL // TK):
        col = lax.broadcasted_iota(jnp.int32, (QB_NSA, TK), 1) + kh * TK
        cut = jnp.where(col < far_keys - s0 + PAD_SEL, NEG, 0.0)
        sel_chunk(pl.multiple_of(s0 + kh * TK, LANE),
                  lambda g, madd, kh=kh, cut=cut: pn_ref[g, :, kh * TK:(kh + 1) * TK] + madd + cut)
    for g in range(NSA_G):
        osel = acc_s[hrows(g), :] / l_s[hrows(g), :]
        o_s[:, head(g)] = o_s[:, head(g)] + gt[:, NSA_G + g:NSA_G + g + 1] * osel

    for g in range(NSA_G):
        st = _online_init(QB_NSA, HD)
        for c0 in range(0, BAND_WIN, TK):
            w = min(TK, BAND_WIN - c0)
            r = pl.multiple_of(s0 + c0, LANE)
            colw = lax.broadcasted_iota(jnp.int32, (QB_NSA, w), 1) + c0
            bias = pw_ref[g, :, c0:c0 + w] + jnp.where(colw < PAD_WIN - s0, NEG, 0.0)
            st = _online(st, _dot_nt(q_ref[:, head(g)], kw_s[pl.ds(r, w), :]) + bias, vw_s[pl.ds(r, w), :],
                         ones_in_v=True)
        ow = st[2] / st[1]
        o_ref[:, head(g)] = (o_s[:, head(g)] + gt[:, 2 * NSA_G + g:2 * NSA_G + g + 1] * ow).astype(o_ref.dtype)


def _nsa_prompt(q_nsa, gates, kc, vc, kv4_p, win_p, tbl_t, batch, seq, m_total):
    nc = seq // CMP
    ns = -(-seq // SELB)
    nqb = seq // QB_NSA
    big = 1 << 30
    pn = _rel_pattern(tbl_t, 0, NSA_H, QB_NSA, NEAR_SEL, PAD_SEL, 0, big, True)
    pw = _rel_pattern(tbl_t, 0, NSA_H, QB_NSA, BAND_WIN, PAD_WIN, 0, WIN, False)
    bc = _rel_pattern(tbl_t, 0, NSA_H, seq, nc, -(CMP - 1), 0, big, False, col_step=CMP)
    e_mat = _expand_matrix(PAD_SEL, seq)
    s_mat = _pair_sum_matrix(nc)
    kv_spec = lambda col: pl.BlockSpec((seq, HD), lambda b, h, i, col=col: (b, col + h))
    return pl.pallas_call(
        functools.partial(_nsa_prompt_kernel, seq=seq, nc=nc, ns=ns),
        grid=(batch, NSA_KV, nqb),
        in_specs=[
            pl.BlockSpec((QB_NSA, NSA_G * HD), lambda b, h, i: (b * nqb + i, h)),
            pl.BlockSpec((QB_NSA, LANE), lambda b, h, i: (b * nqb + i, h)),
            pl.BlockSpec((None, None, nc, HD), lambda b, h, i: (h, b, 0, 0)),
            pl.BlockSpec((None, None, nc, HD), lambda b, h, i: (h, b, 0, 0)),
            kv_spec(2 * NSA_KV), kv_spec(3 * NSA_KV),
            pl.BlockSpec((seq, HD), lambda b, h, i: (b, h)),
            pl.BlockSpec((seq, HD), lambda b, h, i: (b, NSA_KV + h)),
            pl.BlockSpec((NSA_G, QB_NSA, nc), lambda b, h, i: (h, i, 0)),
            pl.BlockSpec((NSA_G, QB_NSA, NEAR_SEL), lambda b, h, i: (h, 0, 0)),
            pl.BlockSpec((NSA_G, QB_NSA, BAND_WIN), lambda b, h, i: (h, 0, 0)),
            pl.BlockSpec((LANE, PAD_SEL + seq), lambda b, h, i: (0, 0)),
            pl.BlockSpec((nc, LANE), lambda b, h, i: (0, 0)),
        ],
        out_specs=pl.BlockSpec((QB_NSA, NSA_G * HD), lambda b, h, i: (b * nqb + i, h)),
        out_shape=jax.ShapeDtypeStruct((m_total, NSA_H * HD), BF),
        scratch_shapes=[pltpu.VMEM((PAD_SEL + seq, HD), BF), pltpu.VMEM((PAD_SEL + seq, 2 * HD), BF),
                        pltpu.VMEM((PAD_WIN + seq, HD), BF), pltpu.VMEM((PAD_WIN + seq, 2 * HD), BF),
                        pltpu.VMEM((NSA_G * QB_NSA, LANE), F32), pltpu.VMEM((NSA_G * QB_NSA, LANE), F32),
                        pltpu.VMEM((NSA_G * QB_NSA, HD), F32), pltpu.VMEM((QB_NSA, NSA_G * HD), F32)],
        compiler_params=_cparams(("arbitrary", "arbitrary", "arbitrary")),
        name="nsa_prompt",
    )(q_nsa, gates, kc, vc, kv4_p, kv4_p, win_p, win_p, bc, pn, pw, e_mat, s_mat)


def _diff_lambda(dl, lam_init):
    a = jnp.sum(dl[0:1] * dl[1:2], axis=-1, keepdims=True)
    b = jnp.sum(dl[2:3] * dl[3:4], axis=-1, keepdims=True)
    return jnp.exp(a) - jnp.exp(b) + lam_init


def _diff_finish(a, sub, lam_init):
    ms = jnp.mean(a * a, axis=-1, keepdims=True)
    return a * lax.rsqrt(ms + EPS) * sub * (1.0 - lam_init)


def _diff_prompt_kernel(q_ref, kv_ref, pn_ref, dl_ref, sub_ref, o_ref, kv_s, m_s, l_s, acc_s, *, seq, lam_init):
    i = pl.program_id(2)
    s0 = i * QB_DF
    rows = DF_G * QB_DF

    @pl.when(i == 0)
    def _():
        kv_s[0:PAD_DF, :] = jnp.zeros((PAD_DF, 4 * HD), BF)
        ch = 256

        def cp(c, carry):
            r = pl.multiple_of(c * ch, ch)
            kv_s[pl.ds(PAD_DF + r, ch), :] = kv_ref[pl.ds(r, ch), :].astype(BF)
            return carry

        lax.fori_loop(0, seq // ch, cp, 0)

    lam = _diff_lambda(dl_ref[...], lam_init)
    nfar = jnp.maximum(i - 1, 0)
    sub_rows = 128
    n_sub = QB_DF // sub_rows
    streams = [(m, g, j) for m in range(2) for g in range(DF_G) for j in range(n_sub)]
    srows = lambda k: slice(k * sub_rows, (k + 1) * sub_rows)
    m_s[...] = jnp.full((2 * rows, LANE), NEG, F32)
    l_s[...] = jnp.zeros((2 * rows, LANE), F32)
    acc_s[...] = jnp.zeros((2 * rows, DF_DV), F32)

    def chunk(r, bias_of):
        vt = kv_s[pl.ds(r, TK), 2 * HD:4 * HD]
        for k, (m, g, j) in enumerate(streams):
            q = q_ref[j * sub_rows:(j + 1) * sub_rows, (g * 2 + m) * HD:(g * 2 + m + 1) * HD]
            kt = kv_s[pl.ds(r, TK), m * HD:(m + 1) * HD]
            _flash_step(q, kt, vt, bias_of(g, j), m_s, l_s, acc_s, srows(k))

    def far(c, carry):
        chunk(pl.multiple_of(PAD_DF + c * TK, TK), lambda g, j: None)
        return carry

    lax.fori_loop(0, nfar, far, 0)
    for kh in range(NEAR_DF // TK):
        col = lax.broadcasted_iota(jnp.int32, (sub_rows, TK), 1) + kh * TK
        cut = jnp.where(col < nfar * TK - s0 + PAD_DF, NEG, 0.0)
        chunk(pl.multiple_of(s0 + kh * TK, TK),
              lambda g, j, kh=kh, cut=cut: pn_ref[g, j * sub_rows:(j + 1) * sub_rows, kh * TK:(kh + 1) * TK] + cut)
    half = len(streams) // 2
    for k, (_, g, j) in enumerate(streams[:half]):
        o1 = acc_s[srows(k), :] / _lanes(l_s[srows(k), :], DF_DV)
        o2 = acc_s[srows(half + k), :] / _lanes(l_s[srows(half + k), :], DF_DV)
        out = _diff_finish(o1 - lam * o2, sub_ref[...], lam_init)
        o_ref[j * sub_rows:(j + 1) * sub_rows, g * DF_DV:(g + 1) * DF_DV] = out.astype(o_ref.dtype)


def _diff_prompt(qd, dkv_p, tbl_t, dl, sub, batch, seq, m_total, lam_init):
    nqb = seq // QB_DF
    pn = _rel_pattern(tbl_t, NSA_H, DF_H, QB_DF, NEAR_DF, PAD_DF, 0, 1 << 30, True)
    width = DF_G * 2 * HD
    return pl.pallas_call(
        functools.partial(_diff_prompt_kernel, seq=seq, lam_init=lam_init),
        grid=(batch, DF_KV, nqb),
        in_specs=[
            pl.BlockSpec((QB_DF, width), lambda b, h, i: (b * nqb + i, h)),
            pl.BlockSpec((seq, 4 * HD), lambda b, h, i: (b, h)),
            pl.BlockSpec((DF_G, QB_DF, NEAR_DF), lambda b, h, i: (h, 0, 0)),
            pl.BlockSpec((4, HD), lambda b, h, i: (0, 0)),
            pl.BlockSpec((1, DF_DV), lambda b, h, i: (0, 0)),
        ],
        out_specs=pl.BlockSpec((QB_DF, DF_G * DF_DV), lambda b, h, i: (b * nqb + i, h)),
        out_shape=jax.ShapeDtypeStruct((m_total, DF_H * DF_DV), BF),
        scratch_shapes=[pltpu.VMEM((PAD_DF + seq, 4 * HD), BF),
                        pltpu.VMEM((2 * DF_G * QB_DF, LANE), F32), pltpu.VMEM((2 * DF_G * QB_DF, LANE), F32),
                        pltpu.VMEM((2 * DF_G * QB_DF, DF_DV), F32)],
        compiler_params=_cparams(("arbitrary", "arbitrary", "arbitrary")),
        name="diff_prompt",
    )(qd, dkv_p, pn, dl, sub.reshape(1, DF_DV))


def _tail_tile(new, width):
    t = new.shape[0]
    return jnp.concatenate([new, jnp.zeros((LANE - t, width), F32)], axis=0).astype(BF)


def _nsa_sample_kernel(pt_ref, *refs, n_pages, page, past, t_new, ncs, ns, wb):
    pages = refs[:n_pages]
    (q_ref, gate_ref, kvn_ref, wn_ref, st_ref, wc_ref, bc_ref, bs_ref, bw_ref, e_ref, s_ref,
     o_ref, kcmp_s, ksel_s, kwin_s) = refs[n_pages:]
    del pt_ref
    rows = NSA_G * t_new
    n_kinds = 4

    for p in range(n_pages):
        for kind in range(n_kinds):
            for h in range(NSA_KV):
                blk = pages[p][pl.ds(kind * NSA_KV + h, page, stride=n_kinds * NSA_KV), :]
                if kind < 2:
                    for nb in range(page // CMP):
                        r0 = (p * (page // CMP) + nb) * CMP_PITCH
                        kcmp_s[kind, h, r0:r0 + CMP, :] = blk[nb * CMP:(nb + 1) * CMP]
                else:
                    ksel_s[kind - 2, h, p * page:(p + 1) * page, :] = blk.astype(BF)
    kvn = kvn_ref[...]
    wn = wn_ref[...]
    for kind in range(2):
        for h in range(NSA_KV):
            c0 = ((kind + 2) * NSA_KV + h) * HD
            ksel_s[kind, h, past:past + LANE, :] = _tail_tile(kvn[:, c0:c0 + HD], HD)
            kwin_s[kind, h, 0:wb, :] = st_ref[pl.ds(kind * NSA_KV + h, wb, stride=2 * NSA_KV), :].astype(BF)
            c0 = (kind * NSA_KV + h) * HD
            kwin_s[kind, h, wb:wb + LANE, :] = _tail_tile(wn[:, c0:c0 + HD], HD)

    q = q_ref[...]
    gt = jax.nn.sigmoid(gate_ref[...])
    qpos = past + lax.broadcasted_iota(jnp.int32, (t_new, LANE), 0)
    for h in range(NSA_KV):
        cmp = []
        for kind in range(2):
            acc = jnp.zeros((ncs, HD), F32)
            for j in range(CMP):
                kj = kcmp_s[kind, h, pl.ds(j, ncs, stride=CMP_PITCH), :]
                acc = acc + _dot(kj.astype(BF), wc_ref[kind, j * HD:(j + 1) * HD, :])
            cmp.append(acc.astype(BF))
        kc, vc = cmp
        qs = jnp.concatenate([q[:, (h * NSA_G + g) * HD:(h * NSA_G + g + 1) * HD] for g in range(NSA_G)],
                             axis=0).astype(BF)
        bc = bc_ref[h * NSA_G:(h + 1) * NSA_G].reshape(rows, ncs)
        pc = _softmax_rows(_dot_nt(qs, kc) + bc, bc > 0.5 * NEG)
        oc = _dot(pc.astype(BF), vc)
        psum = pc[0:t_new]
        for g in range(1, NSA_G):
            psum = psum + pc[g * t_new:(g + 1) * t_new]
        selb = _select_blocks(psum, s_ref[...], qpos, ns)
        lk = past + LANE
        madd = (_dot(selb, e_ref[...]) - 1.0) * (-NEG)
        s = _dot_nt(qs, ksel_s[0, h])
        s = (s.reshape(NSA_G, t_new, lk) + bs_ref[h * NSA_G:(h + 1) * NSA_G] + madd[None]).reshape(rows, lk)
        osel = _dot(_softmax_rows(s).astype(BF), ksel_s[1, h])
        lw = wb + LANE
        s = _dot_nt(qs, kwin_s[0, h]) + bw_ref[h * NSA_G:(h + 1) * NSA_G].reshape(rows, lw)
        ow = _dot(_softmax_rows(s).astype(BF), kwin_s[1, h])
        for g in range(NSA_G):
            sl = slice(g * t_new, (g + 1) * t_new)
            gl = h * LANE + g
            o = (gt[:, gl:gl + 1] * oc[sl] + gt[:, gl + NSA_G:gl + NSA_G + 1] * osel[sl]
                 + gt[:, gl + 2 * NSA_G:gl + 2 * NSA_G + 1] * ow[sl])
            o_ref[:, (h * NSA_G + g) * HD:(h * NSA_G + g + 1) * HD] = o


def _nsa_sample(q_s, gates_s, kv4_s, win_s, cache, state, page_table, w_cmp, tbl_t, t_new):
    db, n_pages = page_table.shape
    n_phys, page = cache.shape[1], cache.shape[2]
    past = n_pages * page
    wb = state.shape[2]
    assert (past + t_new) // CMP * CMP <= past and past % SELB == 0 and wb == min(WIN, past)
    ncs = (past + t_new) // CMP
    ns = -(-(past + t_new) // SELB)
    lk, lw = past + LANE, wb + LANE
    rows_pp = page * 4 * NSA_KV
    cache2 = cache.reshape(cache.shape[0], n_phys, rows_pp, HD)
    state2 = state.reshape(state.shape[0], db, wb * 2 * NSA_KV, HD)
    wc = w_cmp.reshape(2, CMP * HD, HD).astype(BF)
    big = 1 << 30
    bc = _rel_pattern(tbl_t, 0, NSA_H, t_new, ncs, past - (CMP - 1), 0, big, False, col_step=CMP)
    bs = _rel_pattern(tbl_t, 0, NSA_H, t_new, lk, past, 0, big, False, c_lim=past + t_new)
    bw = _rel_pattern(tbl_t, 0, NSA_H, t_new, lw, wb, 0, WIN, False, c_lim=wb + t_new)
    e_mat = _expand_matrix(0, lk)
    s_mat = _pair_sum_matrix(ncs)
    full = lambda shape: pl.BlockSpec(shape, lambda b, pt: (0,) * len(shape))
    page_specs = [pl.BlockSpec((None, None, rows_pp, HD), lambda b, pt, p=p: (0, pt[b, p], 0, 0))
                  for p in range(n_pages)]
    in_specs = page_specs + [
        pl.BlockSpec((t_new, NSA_H * HD), lambda b, pt: (b, 0)),
        pl.BlockSpec((t_new, NSA_KV * LANE), lambda b, pt: (b, 0)),
        pl.BlockSpec((t_new, 4 * NSA_KV * HD), lambda b, pt: (b, 0)),
        pl.BlockSpec((t_new, 2 * NSA_KV * HD), lambda b, pt: (b, 0)),
        pl.BlockSpec((None, None, wb * 2 * NSA_KV, HD), lambda b, pt: (0, b, 0, 0)),
        full((2, CMP * HD, HD)), full((NSA_H, t_new, ncs)), full((NSA_H, t_new, lk)),
        full((NSA_H, t_new, lw)), full((LANE, lk)), full((ncs, LANE)),
    ]
    return pl.pallas_call(
        functools.partial(_nsa_sample_kernel, n_pages=n_pages, page=page, past=past, t_new=t_new,
                          ncs=ncs, ns=ns, wb=wb),
        grid_spec=pltpu.PrefetchScalarGridSpec(
            num_scalar_prefetch=1, grid=(db,), in_specs=in_specs,
            out_specs=pl.BlockSpec((t_new, NSA_H * HD), lambda b, pt: (b, 0)),
            scratch_shapes=[pltpu.VMEM((2, NSA_KV, ncs * CMP_PITCH, HD), F32),
                            pltpu.VMEM((2, NSA_KV, lk, HD), BF),
                            pltpu.VMEM((2, NSA_KV, lw, HD), BF)]),
        out_shape=jax.ShapeDtypeStruct((db * t_new, NSA_H * HD), F32),
        compiler_params=_cparams(("arbitrary",)),
        name="nsa_sample",
    )(page_table, *([cache2] * n_pages), q_s, gates_s, kv4_s, win_s, state2, wc, bc, bs, bw, e_mat, s_mat)


def _diff_sample_kernel(pt_ref, *refs, n_pages, page, past, t_new, lam_init):
    pages = refs[:n_pages]
    q_ref, kvn_ref, b_ref, dl_ref, sub_ref, o_ref, kv_s, stage_s = refs[n_pages:]
    del pt_ref
    rows = DF_G * t_new
    lk = past + LANE
    for p in range(n_pages):
        for h in range(DF_KV):
            stage_s[h] = pages[p][:, h, :]
            kv_s[h, p * page:(p + 1) * page, :] = stage_s[h].astype(BF)
    kvn = kvn_ref[...]
    for h in range(DF_KV):
        kv_s[h, past:past + LANE, :] = _tail_tile(kvn[:, h * 4 * HD:(h + 1) * 4 * HD], 4 * HD)
    lam = _diff_lambda(dl_ref[...], lam_init)
    q = q_ref[...]
    for h in range(DF_KV):
        bias = b_ref[h * DF_G:(h + 1) * DF_G].reshape(rows, lk)
        ps = []
        for m in range(2):
            qm = jnp.concatenate(
                [q[:, ((h * DF_G + g) * 2 + m) * HD:((h * DF_G + g) * 2 + m + 1) * HD] for g in range(DF_G)],
                axis=0).astype(BF)
            ps.append(_softmax_rows(_dot_nt(qm, kv_s[h, :, m * HD:(m + 1) * HD]) + bias))
        a = ps[0] - lam * ps[1]
        out = _diff_finish(_dot(a.astype(BF), kv_s[h, :, 2 * HD:4 * HD]), sub_ref[...], lam_init)
        for g in range(DF_G):
            o_ref[:, (h * DF_G + g) * DF_DV:(h * DF_G + g + 1) * DF_DV] = out[g * t_new:(g + 1) * t_new]


def _diff_sample(qd_s, dkv_s, cache, page_table, tbl_t, dl, sub, t_new, lam_init):
    db, n_pages = page_table.shape
    page = cache.shape[2]
    past = n_pages * page
    lk = past + LANE
    bias = _rel_pattern(tbl_t, NSA_H, DF_H, t_new, lk, past, 0, 1 << 30, False, c_lim=past + t_new)
    full = lambda shape: pl.BlockSpec(shape, lambda b, pt: (0,) * len(shape))
    page_specs = [pl.BlockSpec((None, None, page, DF_KV, 4 * HD), lambda b, pt, p=p: (0, pt[b, p], 0, 0, 0))
                  for p in range(n_pages)]
    in_specs = page_specs + [
        pl.BlockSpec((t_new, DF_H * 2 * HD), lambda b, pt: (b, 0)),
        pl.BlockSpec((t_new, DF_KV * 4 * HD), lambda b, pt: (b, 0)),
        full((DF_H, t_new, lk)), full((4, HD)), full((1, DF_DV)),
    ]
    return pl.pallas_call(
        functools.partial(_diff_sample_kernel, n_pages=n_pages, page=page, past=past, t_new=t_new,
                          lam_init=lam_init),
        grid_spec=pltpu.PrefetchScalarGridSpec(
            num_scalar_prefetch=1, grid=(db,), in_specs=in_specs,
            out_specs=pl.BlockSpec((t_new, DF_H * DF_DV), lambda b, pt: (b, 0)),
            scratch_shapes=[pltpu.VMEM((DF_KV, lk, 4 * HD), BF), pltpu.VMEM((DF_KV, page, 4 * HD), F32)]),
        out_shape=jax.ShapeDtypeStruct((db * t_new, DF_H * DF_DV), F32),
        compiler_params=_cparams(("arbitrary",)),
        name="diff_sample",
    )(page_table, *([cache] * n_pages), qd_s, dkv_s, bias, dl, sub.reshape(1, DF_DV))


def _router_kernel(h_ref, g_ref, wr_ref, br_ref, xn_ref, ids_ref, wts_ref):
    x = h_ref[...]
    ms = jnp.mean(x * x, axis=-1, keepdims=True)
    xn = x * lax.rsqrt(ms + EPS) * g_ref[...]
    xh = xn.astype(BF)
    xn_ref[...] = xh
    xl = (xn - xh.astype(F32)).astype(BF)
    wr = wr_ref[...]
    wh = wr.astype(BF)
    wl = (wr - wh.astype(F32)).astype(BF)
    lg = _dot(xh, wh) + _dot(xl, wh) + _dot(xh, wl) + br_ref[...]
    lane_i = lax.broadcasted_iota(jnp.int32, lg.shape, 1)
    lane = lane_i.astype(F32)
    big = 1000.0
    isg = lane_i < N_GROUPS
    gmax = jnp.max(jnp.where(isg, lg, -3e38), axis=-1, keepdims=True)
    gsel = jnp.min(jnp.where(isg & (lg == gmax), lane, big), axis=-1, keepdims=True)
    gw = 1.0 / jnp.sum(jnp.where(isg, jnp.exp(lg - gmax), 0.0), axis=-1, keepdims=True)
    lo = N_GROUPS + gsel * EPG
    ing = (lane >= lo) & (lane < lo + EPG)
    emax = jnp.max(jnp.where(ing, lg, -3e38), axis=-1, keepdims=True)
    pe = jnp.where(ing, jnp.exp(lg - emax), 0.0)
    pr = jnp.where(ing, pe / jnp.sum(pe, axis=-1, keepdims=True), -1.0)
    v1 = jnp.max(pr, axis=-1, keepdims=True)
    i1 = jnp.min(jnp.where(pr == v1, lane, big), axis=-1, keepdims=True)
    pr2 = jnp.where(lane == i1, -1.0, pr)
    v2 = jnp.max(pr2, axis=-1, keepdims=True)
    i2 = jnp.min(jnp.where(pr2 == v2, lane, big), axis=-1, keepdims=True)
    den = v1 + v2
    e12 = jnp.where(lane_i == 0, i1 - N_GROUPS, jnp.where(lane_i == 1, i2 - N_GROUPS, 0.0))
    ids_ref[...] = e12.astype(jnp.int32)
    wts_ref[...] = jnp.where(lane_i == 0, v1 / den * gw, jnp.where(lane_i == 1, v2 / den * gw, 0.0))


def _router(h, g, wr, br, tm):
    m, d = h.shape
    return pl.pallas_call(
        _router_kernel,
        grid=(m // tm,),
        in_specs=[pl.BlockSpec((tm, d), lambda i: (i, 0)), pl.BlockSpec((1, d), lambda i: (0, 0)),
                  pl.BlockSpec((d, LANE), lambda i: (0, 0)), pl.BlockSpec((1, LANE), lambda i: (0, 0))],
        out_specs=[pl.BlockSpec((tm, d), lambda i: (i, 0)), pl.BlockSpec((tm, LANE), lambda i: (i, 0)),
                   pl.BlockSpec((tm, LANE), lambda i: (i, 0))],
        out_shape=[jax.ShapeDtypeStruct((m, d), BF), jax.ShapeDtypeStruct((m, LANE), jnp.int32),
                   jax.ShapeDtypeStruct((m, LANE), F32)],
        compiler_params=_cparams(("arbitrary",)),
        name="router",
    )(h, g.reshape(1, d), wr, br)


def _split_dot(x_ref, w_refs):
    rows = w_refs[0].shape[0]
    acc = _dot(x_ref[:, 0:rows], w_refs[0][...].astype(BF))
    for s in range(1, len(w_refs)):
        acc = acc + _dot(x_ref[:, s * rows:(s + 1) * rows], w_refs[s][...].astype(BF))
    return acc


def _moe_up_kernel(te_ref, txb_ref, nt_ref, *refs, nk, n_split, nxb):
    xb = refs[:nxb]
    wg = refs[nxb:nxb + n_split]
    wu = refs[nxb + n_split:nxb + 2 * n_split]
    o_ref, x_s, a_s, u_s = refs[nxb + 2 * n_split:]
    k = pl.program_id(1)
    xg = xb[0].shape[0]

    @pl.when(pl.program_id(0) < nt_ref[0])
    def _():
        @pl.when(k == 0)
        def _():
            a_s[...] = jnp.zeros(a_s.shape, F32)
            u_s[...] = jnp.zeros(u_s.shape, F32)

        for j in range(nxb):
            x_s[j * xg:(j + 1) * xg, :] = xb[j][...]
        a_s[...] += _split_dot(x_s, wg)
        u_s[...] += _split_dot(x_s, wu)

        @pl.when(k == nk - 1)
        def _():
            g = a_s[...]
            o_ref[...] = (g * jax.nn.sigmoid(g) * u_s[...]).astype(o_ref.dtype)


def _moe_down_kernel(te_ref, nt_ref, h_ref, *refs, n_split):
    o_ref = refs[n_split]

    @pl.when(pl.program_id(0) < nt_ref[0])
    def _():
        o_ref[...] = _split_dot(h_ref, refs[:n_split]).astype(o_ref.dtype)


def _moe_sizes(m, n_exp):
    n_pairs = 2 * m
    return n_exp + n_pairs // MOE_TM, n_exp + n_pairs // MOE_XBLK


def _moe_dispatch(xn, ids, n_exp):
    m, d = xn.shape
    tm, xg = MOE_TM, MOE_XBLK
    n_pairs = 2 * m
    nxb = tm // xg
    n_tiles, n_xblk = _moe_sizes(m, n_exp)
    flat_e = ids[:, :2].reshape(-1)
    onehot = (flat_e[:, None] == jnp.arange(n_exp, dtype=jnp.int32)[None, :]).astype(jnp.int32)
    counts = jnp.sum(onehot, axis=0)
    tiles_per = (counts + tm - 1) // tm
    tile_end = jnp.cumsum(tiles_per)
    tile_start = tile_end - tiles_per
    xblk_per = (counts + xg - 1) // xg
    xblk_start = jnp.cumsum(xblk_per) - xblk_per
    rank = jnp.sum(onehot * (jnp.cumsum(onehot, axis=0) - 1), axis=1)
    pos = (jnp.sum(onehot * (tile_start * tm)[None, :], axis=1) + rank).reshape(m, 2)
    xpos = jnp.sum(onehot * (xblk_start * xg)[None, :], axis=1) + rank
    nt = tile_end[-1]
    tix = jnp.minimum(jnp.arange(n_tiles, dtype=jnp.int32), nt - 1)
    tile_e = jnp.minimum(jnp.searchsorted(tile_end, tix, side="right"), n_exp - 1).astype(jnp.int32)
    tile_xb = (xblk_start[tile_e] + (tix - tile_start[tile_e]) * nxb).astype(jnp.int32)
    row_tok = (jnp.arange(n_xblk * xg, dtype=jnp.int32) % m).at[xpos].set(
        jnp.arange(n_pairs, dtype=jnp.int32) // 2, mode="promise_in_bounds", unique_indices=True)
    xs = xn.at[row_tok].get(mode="promise_in_bounds")
    return xs, pos, tile_e, tile_xb, nt.reshape(1).astype(jnp.int32)


def _moe_experts(xs, tile_e, tile_xb, nt_arr, w_gate, w_up, w_down, m):
    d = xs.shape[1]
    n_exp, _, ff = w_gate.shape
    tm, xg = MOE_TM, MOE_XBLK
    nxb = tm // xg
    n_tiles, n_xblk = _moe_sizes(m, n_exp)
    n_rows = n_tiles * tm
    tk = _pick(d, (1024, 512, 256, 128))
    nk = d // tk
    live = lambda t, ntr: t < ntr[0]
    clamp = lambda t, ntr: jnp.minimum(t, ntr[0] - 1)
    kidx = lambda t, k, ntr: jnp.where(live(t, ntr), k, nk - 1)
    ns_up = MOE_DMA_SPLIT if tk % (MOE_DMA_SPLIT * LANE) == 0 else 1
    xspecs = [pl.BlockSpec((xg, tk), lambda t, k, te, txb, ntr, j=j: (jnp.minimum(txb[t] + j, n_xblk - 1),
                                                                     kidx(t, k, ntr))) for j in range(nxb)]
    wspecs = [pl.BlockSpec((None, tk // ns_up, ff),
                           lambda t, k, te, txb, ntr, s=s: (te[t], kidx(t, k, ntr) * ns_up + s, 0))
              for s in range(ns_up)]
    hdn = pl.pallas_call(
        functools.partial(_moe_up_kernel, nk=nk, n_split=ns_up, nxb=nxb),
        grid_spec=pltpu.PrefetchScalarGridSpec(
            num_scalar_prefetch=3, grid=(n_tiles, nk),
            in_specs=xspecs + wspecs + wspecs,
            out_specs=pl.BlockSpec((tm, ff), lambda t, k, te, txb, ntr: (clamp(t, ntr), 0)),
            scratch_shapes=[pltpu.VMEM((tm, tk), BF), pltpu.VMEM((tm, ff), F32), pltpu.VMEM((tm, ff), F32)]),
        out_shape=jax.ShapeDtypeStruct((n_rows, ff), BF),
        compiler_params=_cparams(("arbitrary", "arbitrary")),
        name="moe_up",
    )(tile_e, tile_xb, nt_arr, *([xs] * nxb), *([w_gate] * ns_up), *([w_up] * ns_up))

    tn = _pick(d, (2048, 1024, 512, 256, 128))
    nn = d // tn
    ns_dn = MOE_DMA_SPLIT if ff % (MOE_DMA_SPLIT * LANE) == 0 else 1
    jidx = lambda t, j, ntr: jnp.where(live(t, ntr), j, nn - 1)
    y = pl.pallas_call(
        functools.partial(_moe_down_kernel, n_split=ns_dn),
        grid_spec=pltpu.PrefetchScalarGridSpec(
            num_scalar_prefetch=2, grid=(n_tiles, nn),
            in_specs=[pl.BlockSpec((tm, ff), lambda t, j, te, ntr: (clamp(t, ntr), 0))]
            + [pl.BlockSpec((None, ff // ns_dn, tn), lambda t, j, te, ntr, s=s: (te[t], s, jidx(t, j, ntr)))
               for s in range(ns_dn)],
            out_specs=pl.BlockSpec((tm, tn), lambda t, j, te, ntr: (clamp(t, ntr), jidx(t, j, ntr)))),
        out_shape=jax.ShapeDtypeStruct((n_rows, d), BF),
        compiler_params=_cparams(("arbitrary", "arbitrary")),
        name="moe_down",
    )(tile_e, nt_arr, hdn, *([w_down] * ns_dn))
    return y


def _split_cols_kernel(x_ref, o_ref, *, dims, width):
    for i in range(dims[0]):
        if len(dims) == 1:
            o_ref[:, i, :] = x_ref[:, i * width:(i + 1) * width]
        else:
            for j in range(dims[1]):
                c = (i * dims[1] + j) * width
                o_ref[:, i, j, :] = x_ref[:, c:c + width]


def _split_cols(x, batch, rows_per_batch, row0, dims, width, tb=256):
    seq = x.shape[0] // batch
    tb = _pick(math.gcd(rows_per_batch, math.gcd(row0, seq)) if row0 else math.gcd(rows_per_batch, seq),
               (tb, 128, 64, 32, 16, 8))
    nb = rows_per_batch // tb
    ncol = x.shape[1]
    zeros = (0,) * (len(dims) + 1)
    return pl.pallas_call(
        functools.partial(_split_cols_kernel, dims=dims, width=width),
        grid=(batch, nb),
        in_specs=[pl.BlockSpec((tb, ncol), lambda b, i: ((b * seq + row0) // tb + i, 0))],
        out_specs=pl.BlockSpec((None, tb) + tuple(dims) + (width,), lambda b, i: (b, i) + zeros),
        out_shape=jax.ShapeDtypeStruct((batch, rows_per_batch) + tuple(dims) + (width,), x.dtype),
        compiler_params=_cparams(("arbitrary", "arbitrary")),
        name="split_cols",
    )(x)


def _shift_window_kernel(s_ref, w_ref, o_ref, *, t_new, n_kh):
    nbb, rows, _ = s_ref.shape
    keep = rows - t_new * n_kh
    o_ref[:, 0:keep, :] = s_ref[:, t_new * n_kh:rows, :]
    w = w_ref[...]
    for bb in range(nbb):
        for k in range(n_kh):
            o_ref[bb, pl.ds(keep + k, t_new, stride=n_kh), :] = w[bb * t_new:(bb + 1) * t_new, k * HD:(k + 1) * HD]


def _shift_window(state, win_s, t_new):
    _, db, wb, n_kind, n_head, hd = state.shape
    n_kh = n_kind * n_head
    rows = wb * n_kh
    nbb = _pick(db, (4, 2, 1))
    out = pl.pallas_call(
        functools.partial(_shift_window_kernel, t_new=t_new, n_kh=n_kh),
        grid=(db // nbb,),
        in_specs=[pl.BlockSpec((None, nbb, rows, hd), lambda i: (0, i, 0, 0)),
                  pl.BlockSpec((nbb * t_new, n_kh * hd), lambda i: (i, 0))],
        out_specs=pl.BlockSpec((nbb, rows, hd), lambda i: (i, 0, 0)),
        out_shape=jax.ShapeDtypeStruct((db, rows, hd), state.dtype),
        compiler_params=_cparams(("arbitrary",)),
        name="shift_window",
    )(state.reshape(state.shape[0], db, rows, hd), win_s)
    return out.reshape(1, db, wb, n_kind, n_head, hd)


O_KV = NSA_H * HD
O_GATE = O_KV + 6 * NSA_KV * HD
O_QD = O_GATE + 3 * NSA_H
W_QD = DF_H * 2 * HD
W_KD = DF_KV * 2 * HD


def _gate_lanes(raw):
    m = raw.shape[0]
    g = raw[:, :3 * NSA_H].reshape(m, 3, NSA_KV, NSA_G).transpose(0, 2, 1, 3).reshape(m, NSA_KV, 3 * NSA_G)
    return jnp.pad(g, ((0, 0), (0, 0), (0, LANE - 3 * NSA_G))).reshape(m, NSA_KV * LANE)


def kernel(x_prompt, x_sample, cache_nsa_kv, cache_diff_kv, state_nsa_win, page_table, p_prompt, p_sample,
           rel_bias_table, norm_mix, w_in, w_cmp, diff_lambda, diff_subln, w_out, norm_ffn, w_router_group,
           b_router_group, w_router_expert, b_router_expert, w_exp_gate, w_exp_up, w_exp_down, norm_ple,
           w_ple_gate, w_ple_proj, final_norm):
    assert norm_mix.shape[0] == 1, "single-layer trunk"
    batch, seq, d = x_prompt.shape
    db, t_new, _ = x_sample.shape
    mp, ms = batch * seq, db * t_new
    m = mp + ms
    lam_init = 0.8 - 0.6 * math.exp(-0.3 * 0)
    tm = _pick(math.gcd(mp, ms), (1024, 512, 256, 128))
    tm_s = _pick(math.gcd(mp, ms), (256, 128))
    xp = x_prompt.reshape(mp, d)
    xs = x_sample.reshape(ms, d)
    tbl_t = rel_bias_table

    xn = _rms2(xp, xs, norm_mix[0], tm_s)
    w0 = jnp.swapaxes(w_in[0], 0, 1)
    off = O_QD - O_GATE
    scale = HD ** -0.5
    tn = 512
    wide = functools.partial(_matmul, [xn], tm=tm, tn=tn, wt=True)
    q_nsa = wide([(w0, 0)], rows=m, n_cols=O_KV, out_dtype=BF, scale=scale, name="proj_qn")
    shifted = lambda base, width: (w0, 0, lambda j: base // width + j,
                                   lambda j: (base + width) // LANE + j * (width // LANE))
    qd = wide([shifted(O_GATE, tn)], rows=m, n_cols=W_QD, out_dtype=BF, scale=scale, lane_off=off, name="proj_qd")
    gates = _gate_lanes(_matmul([xn], [(w0, 0, lambda j: O_GATE // LANE)], rows=m, n_cols=LANE, tm=tm, tn=LANE,
                                out_dtype=F32, wt=True, name="proj_gate"))
    kv_w = [(w0, 0, lambda j: j + O_KV // tn)]
    win_w = [(w0, 0, lambda j: j + (O_KV + 4 * NSA_KV * HD) // tn)]
    dkv_w = [shifted(O_GATE + W_QD, DF_DV), shifted(O_GATE + W_QD + W_KD, DF_DV)]
    dkv = functools.partial(_matmul, [xn], dkv_w, tm=tm, tn=2 * DF_DV, n_cols=DF_KV * 4 * HD, out_dtype=F32,
                            epi="cat", lane_off=off, wt=True)
    kv4_p = wide(kv_w, rows=mp, n_cols=4 * NSA_KV * HD, out_dtype=F32, name="proj_kv_p")
    kv4_s = wide(kv_w, rows=ms, row0=mp, n_cols=4 * NSA_KV * HD, out_dtype=F32, name="proj_kv_s")
    win_p = wide(win_w, rows=mp, n_cols=2 * NSA_KV * HD, out_dtype=F32, name="proj_win_p")
    win_s = wide(win_w, rows=ms, row0=mp, n_cols=2 * NSA_KV * HD, out_dtype=F32, name="proj_win_s")
    dkv_p = dkv(rows=mp, name="proj_dkv_p")
    dkv_s = dkv(rows=ms, row0=mp, name="proj_dkv_s")

    nc = seq // CMP
    kvr = kv4_p.reshape(batch, seq, 4, NSA_KV, HD)
    cmp_out = []
    for kind in range(2):
        a = kvr[:, :, kind].transpose(2, 0, 1, 3).reshape(NSA_KV * batch * nc, CMP * HD).astype(BF)
        r = a.shape[0]
        cmp_out.append(_matmul([a], [(w_cmp[0, kind].reshape(CMP * HD, HD), 0)], rows=r, n_cols=HD,
                               tm=_pick(r, (512, 256, 128, 64, 32, 16)), tn=HD, out_dtype=BF,
                               name="compress").reshape(NSA_KV, batch, nc, HD))
    o_n = _nsa_prompt(q_nsa, gates, cmp_out[0], cmp_out[1], kv4_p, win_p, tbl_t, batch, seq, m)
    o_d = _diff_prompt(qd, dkv_p, tbl_t, diff_lambda[0], diff_subln[0], batch, seq, m, lam_init)

    o_n_s = _nsa_sample(q_nsa[mp:].astype(F32), gates[mp:], kv4_s, win_s, cache_nsa_kv, state_nsa_win,
                        page_table, w_cmp[0], tbl_t, t_new)
    o_d_s = _diff_sample(qd[mp:].astype(F32), dkv_s, cache_diff_kv, page_table, tbl_t, diff_lambda[0],
                         diff_subln[0], t_new, lam_init)
    o_n = lax.dynamic_update_slice(o_n, o_n_s.astype(BF), (mp, 0))
    o_d = lax.dynamic_update_slice(o_d, o_d_s.astype(BF), (mp, 0))

    h1 = _matmul([o_n, o_d], [(w_out[0], 0), (w_out[0], 1)], rows=m, n_cols=d, tm=tm,
                 tn=_pick(d, (512, 256, 128)), out_dtype=F32, epi="res2", epi_args=(xp, xs), name="out_proj")

    wr = jnp.concatenate([w_router_group[0], w_router_expert[0],
                          jnp.zeros((d, LANE - N_GROUPS - N_EXP), F32)], axis=1)
    br = jnp.concatenate([b_router_group[0], b_router_expert[0],
                          jnp.zeros((LANE - N_GROUPS - N_EXP,), F32)]).reshape(1, LANE)
    xn2, ids, wts = _router(h1, norm_ffn[0], wr, br, tm_s)
    xs_moe, pos, tile_e, tile_xb, nt_arr = _moe_dispatch(xn2, ids, N_EXP)
    wk = min(WIN, seq)
    nsa_kv_p = _split_cols(kv4_p, batch, seq, 0, (4, NSA_KV), HD)
    diff_kv_p = _split_cols(dkv_p, batch, seq, 0, (DF_KV,), 4 * HD)
    win_p_out = _split_cols(win_p, batch, wk, seq - wk, (2, NSA_KV), HD)
    new_win = _shift_window(state_nsa_win, win_s, t_new)
    y = _moe_experts(xs_moe, tile_e, tile_xb, nt_arr, w_exp_gate[0], w_exp_up[0], w_exp_down[0], m)
    h2, xn3 = _combine_rms(h1, y.at[pos[:, 0]].get(mode="promise_in_bounds"),
                           y.at[pos[:, 1]].get(mode="promise_in_bounds"), wts, norm_ple[0], tm_s)

    p_all = jnp.concatenate([p_prompt[0].reshape(mp, -1), p_sample[0].reshape(ms, -1)], axis=0).astype(BF)
    h3 = _matmul([xn3], [(w_ple_gate[0], 0)], rows=m, n_cols=d, tm=tm, tn=_pick(d, (512, 256, 128)),
                 out_dtype=F32, epi="ple", epi_args=(h2, p_all, w_ple_proj[0]), name="ple")

    y_p = _rms(h3, final_norm, tm_s, F32, row0=0, rows=mp).reshape(batch, seq, d)
    y_s = _rms(h3, final_norm, tm_s, F32, row0=mp, rows=ms).reshape(db, t_new, d)
    return (y_p, y_s,
            nsa_kv_p[None], kv4_s.reshape(1, db, t_new, 4, NSA_KV, HD),
            diff_kv_p[None], dkv_s.reshape(1, db, t_new, DF_KV, 4 * HD),
            win_p_out[None], new_win)
```
